```python
import math
import jax
import jax.numpy as jnp
from jax import lax
import numpy as np

D_MODEL = 2048
BATCH = 2
SEQ = 8192
DEPTH = 2

GRID_W = 64
CTX_LEN = 256
EPS = 1e-6
F32 = jnp.float32

SSD_D_INNER = 1024
SSD_HEAD_DIM = 64
SSD_HEADS = SSD_D_INNER // SSD_HEAD_DIM
SSD_GROUPS = 2
SSD_HG = SSD_HEADS // SSD_GROUPS
SSD_STATE = 128
SSD_BC = SSD_GROUPS * SSD_STATE
SSD_XBC = SSD_D_INNER + 2 * SSD_BC
SSD_CONV_W = 5
SSD_CHUNK = 128

HG_WIDTH = 1024
HG_HEAD_DIM = 128
HG_HEADS = HG_WIDTH // HG_HEAD_DIM
HG_CHUNK = 64

DA_HEADS = 8
DA_HEAD_DIM = 64
DA_WIDTH = 2 * DA_HEADS * DA_HEAD_DIM
DA_SCALE = DA_HEAD_DIM ** -0.5
DA_Q_BLOCK = 128
ROPE_BASE = 10000.0
ROPE_PAIRS = DA_HEAD_DIM // 4

BRANCH_WIDTH = 1024
N_BRANCH = 3

OFF_SSD_Z = 0
OFF_SSD_XBC = OFF_SSD_Z + SSD_D_INNER
OFF_SSD_DT = OFF_SSD_XBC + SSD_XBC
OFF_HG_Q = OFF_SSD_DT + 2 * SSD_HEADS
OFF_HG_F = OFF_HG_Q + HG_WIDTH
OFF_HG_I = OFF_HG_F + 2 * HG_WIDTH
OFF_HG_G = OFF_HG_I + HG_WIDTH
OFF_DA_Q = OFF_HG_G + HG_WIDTH
OFF_DA_K = OFF_DA_Q + DA_WIDTH
OFF_DA_V = OFF_DA_K + DA_WIDTH
N_IN = OFF_DA_V + DA_WIDTH

N_EXPERTS = 16
EXPERT_FF = 2048
EC_CAPACITY = 2

kernel_name = 'hybrid_ssd_hgrn2_diffattn_ecmoe_dit'


def rms_f32(x):
    x = x.astype(F32)
    return x * lax.rsqrt(jnp.mean(x * x, axis=-1, keepdims=True) + EPS)


def rmsnorm(x, w):
    return (rms_f32(x) * w.astype(F32)).astype(x.dtype)


def modulate(x, w, shift, scale):
    return rmsnorm(x, w) * (1 + scale) + shift


def flip(t):
    return t[:, ::-1]


def to_chunks(t, q):
    bsz, n = t.shape[:2]
    return jnp.moveaxis(t.reshape((bsz, n // q, q) + t.shape[2:]), 1, 0)


def from_chunks(t):
    nc, bsz, q = t.shape[:3]
    return jnp.moveaxis(t, 0, 1).reshape((bsz, nc * q) + t.shape[3:])


def centred_dwconv(u, w, b):
    pad = (SSD_CONV_W - 1) // 2
    y = lax.conv_general_dilated(u, w[:, None, :], window_strides=(1,), padding=[(pad, pad)],
                                 dimension_numbers=('NWC', 'WIO', 'NWC'),
                                 feature_group_count=u.shape[-1])
    return y + b


def axial_rope_tables(n_tokens):
    rows = n_tokens // GRID_W
    row = jnp.repeat(jnp.arange(rows, dtype=F32), GRID_W)
    col = (jnp.arange(rows * GRID_W, dtype=jnp.int32) % GRID_W).astype(F32)
    inv = ROPE_BASE ** (-jnp.arange(ROPE_PAIRS, dtype=F32) / ROPE_PAIRS)
    ang = jnp.concatenate([row[:, None] * inv, col[:, None] * inv], axis=-1)
    return jnp.cos(ang), jnp.sin(ang)


def apply_axial_rope(t, cos, sin):
    half = DA_HEAD_DIM // 2
    tf = t.astype(F32)
    t1, t2 = tf[..., :half], tf[..., half:]
    c, s = cos[None, :, None, :], sin[None, :, None, :]
    return jnp.concatenate([t1 * c - t2 * s, t1 * s + t2 * c], axis=-1).astype(t.dtype)


def lower_bounds(p):
    cum = jnp.cumsum(jax.nn.softmax(p.astype(F32), axis=0), axis=0)
    return cum - cum[0]


def ssd_scan(xh, dt, a, bm, cm, s0):
    mask = jnp.tril(jnp.ones((SSD_CHUNK, SSD_CHUNK), dtype=bool))[None, :, :, None, None]

    def step(s, inp):
        xc, dtc, bc, cc = inp
        cs = jnp.cumsum(dtc * a, axis=1)
        seg = jnp.exp(jnp.where(mask, cs[:, :, None] - cs[:, None, :], -jnp.inf))
        xdt = xc * dtc[..., None]
        cb = jnp.einsum('bign,bjgn->bgij', cc, bc)
        y = jnp.einsum('bgij,bijgh,bjghp->bighp', cb, seg, xdt)
        y = y + jnp.einsum('bign,bghpn,bigh->bighp', cc, s, jnp.exp(cs))
        s = (jnp.exp(cs[:, -1])[..., None, None] * s
             + jnp.einsum('bjgn,bjgh,bjghp->bghpn', bc, jnp.exp(cs[:, -1:] - cs), xdt))
        return s, y

    s, ys = lax.scan(step, s0, tuple(to_chunks(t, SSD_CHUNK) for t in (xh, dt, bm, cm)))
    return from_chunks(ys), s


def gla_scan(q, k, v, logf, s0):
    mask = jnp.tril(jnp.ones((HG_CHUNK, HG_CHUNK), dtype=bool))[None, :, :, None, None]

    def step(s, inp):
        qc, kc, vc, gc = inp
        b = jnp.cumsum(gc, axis=1)
        decay = jnp.exp(jnp.where(mask, b[:, :, None] - b[:, None, :], -jnp.inf))
        att = jnp.einsum('bihk,bijhk,bjhk->bhij', qc, decay, kc)
        o = (jnp.einsum('bhij,bjhv->bihv', att, vc)
             + jnp.einsum('bihk,bhkv->bihv', qc * jnp.exp(b), s))
        s = (jnp.exp(b[:, -1])[..., None] * s
             + jnp.einsum('bjhk,bjhv->bhkv', kc * jnp.exp(b[:, -1:] - b), vc))
        return s, o

    s, os_ = lax.scan(step, s0, tuple(to_chunks(t, HG_CHUNK) for t in (q, k, v, logf)))
    return from_chunks(os_), s


def ssd_prep(p, conv_w, conv_b, dt_bias):
    bsz, n = p.shape[:2]
    z = p[..., OFF_SSD_Z:OFF_SSD_XBC].astype(F32)
    xbc = jax.nn.silu(centred_dwconv(p[..., OFF_SSD_XBC:OFF_SSD_DT], conv_w, conv_b)).astype(F32)
    xh = xbc[..., :SSD_D_INNER].reshape(bsz, n, SSD_GROUPS, SSD_HG, SSD_HEAD_DIM)
    bm = xbc[..., SSD_D_INNER:SSD_D_INNER + SSD_BC].reshape(bsz, n, SSD_GROUPS, SSD_STATE)
    cm = xbc[..., SSD_D_INNER + SSD_BC:].reshape(bsz, n, SSD_GROUPS, SSD_STATE)
    dt = jax.nn.softplus(p[..., OFF_SSD_DT:OFF_HG_Q].astype(F32).reshape(bsz, n, 2, SSD_GROUPS, SSD_HG)
                         + dt_bias.astype(F32).reshape(2, SSD_GROUPS, SSD_HG))
    return z, xh, bm, cm, dt


def ssd_branch(pc, pl, conv_w, conv_b, dt_bias, a_log, d_skip, norm_w, need_ctx):
    a = -jnp.exp(a_log.astype(F32)).reshape(2, SSD_GROUPS, SSD_HG)
    d = d_skip.astype(F32).reshape(SSD_GROUPS, SSD_HG, 1)
    zc, xc, bc, cc, dtc = ssd_prep(pc, conv_w, conv_b, dt_bias)
    zl, xl, bl, cl, dtl = ssd_prep(pl, conv_w, conv_b, dt_bias)
    s0 = jnp.zeros((pc.shape[0], SSD_GROUPS, SSD_HG, SSD_HEAD_DIM, SSD_STATE), F32)
    yc_f, s_f = ssd_scan(xc, dtc[:, :, 0], a[0], bc, cc, s0)
    yc_b, s_b = ssd_scan(flip(xc), flip(dtc[:, :, 1]), a[1], flip(bc), flip(cc), s0)
    yl_f, _ = ssd_scan(xl, dtl[:, :, 0], a[0], bl, cl, s_f)
    yl_b, _ = ssd_scan(flip(xl), flip(dtl[:, :, 1]), a[1], flip(bl), flip(cl), s_b)

    def readout(y, xh, z):
        bsz, n = z.shape[:2]
        y = (y + d * xh).reshape(bsz, n, SSD_D_INNER) * jax.nn.silu(z)
        y = rms_f32(y.reshape(bsz, n, SSD_GROUPS, SSD_D_INNER // SSD_GROUPS)).reshape(bsz, n, SSD_D_INNER)
        return y * norm_w.astype(F32)

    yl = readout(yl_f + flip(yl_b), xl, zl)
    yc = readout(yc_f + flip(yc_b), xc, zc) if need_ctx else None
    return yc, yl


def hgrn2_prep(p, lb):
    bsz, n = p.shape[:2]
    hd = (bsz, n, HG_HEADS, HG_HEAD_DIM)
    q = jax.nn.silu(p[..., OFF_HG_Q:OFF_HG_F].astype(F32)).reshape(hd)
    f = lb + (1.0 - lb) * jax.nn.sigmoid(p[..., OFF_HG_F:OFF_HG_I].astype(F32).reshape(bsz, n, 2, HG_WIDTH))
    f = f.reshape(bsz, n, 2, HG_HEADS, HG_HEAD_DIM)
    v = p[..., OFF_HG_I:OFF_HG_G].astype(F32).reshape(hd)
    g = p[..., OFF_HG_G:OFF_DA_Q].astype(F32)
    return q, 1.0 - f, jnp.log(f), v, g


def hgrn2_branch(pc, pl, lb, norm_w, need_ctx):
    qc, kc, lfc, vc, gc = hgrn2_prep(pc, lb)
    ql, kl, lfl, vl, gl = hgrn2_prep(pl, lb)
    s0 = jnp.zeros((pc.shape[0], HG_HEADS, HG_HEAD_DIM, HG_HEAD_DIM), F32)
    oc_f, s_f = gla_scan(qc, kc[:, :, 0], vc, lfc[:, :, 0], s0)
    oc_b, s_b = gla_scan(flip(qc), flip(kc[:, :, 1]), flip(vc), flip(lfc[:, :, 1]), s0)
    ol_f, _ = gla_scan(ql, kl[:, :, 0], vl, lfl[:, :, 0], s_f)
    ol_b, _ = gla_scan(flip(ql), flip(kl[:, :, 1]), flip(vl), flip(lfl[:, :, 1]), s_b)

    def readout(o, g):
        bsz, n = g.shape[:2]
        return (rms_f32(o) * norm_w.astype(F32)).reshape(bsz, n, HG_WIDTH) * jax.nn.silu(g)

    yl = readout(ol_f + flip(ol_b), gl)
    yc = readout(oc_f + flip(oc_b), gc) if need_ctx else None
    return yc, yl


def diff_attn_branch(pc, pl, lam, norm_w, lam_init, cos, sin, need_ctx):
    def split_heads(p):
        bsz, n = p.shape[:2]
        q = p[..., OFF_DA_Q:OFF_DA_K].reshape(bsz, n, 2 * DA_HEADS, DA_HEAD_DIM)
        k = p[..., OFF_DA_K:OFF_DA_V].reshape(bsz, n, 2 * DA_HEADS, DA_HEAD_DIM)
        v = p[..., OFF_DA_V:N_IN].reshape(bsz, n, DA_HEADS, 2 * DA_HEAD_DIM)
        return q, k, v

    lam = lam.astype(F32)
    lmbda = jnp.exp(jnp.sum(lam[0] * lam[1])) - jnp.exp(jnp.sum(lam[2] * lam[3])) + lam_init
    qc, kc, vc = split_heads(pc)
    ql, kl, vl = split_heads(pl)
    ql, kl = apply_axial_rope(ql, cos, sin), apply_axial_rope(kl, cos, sin)
    k_all = jnp.concatenate([kc, kl], axis=1)
    v_all = jnp.concatenate([vc, vl], axis=1).astype(F32)

    def attend(q, k, v):
        bsz, nq = q.shape[:2]
        s = jnp.einsum('bqhd,bkhd->bhqk', q, k).astype(F32) * DA_SCALE
        p = jax.nn.softmax(s, axis=-1).reshape(bsz, DA_HEADS, 2, nq, k.shape[1])
        return jnp.einsum('bhqk,bkhv->bqhv', p[:, :, 0] - lmbda * p[:, :, 1], v)

    ol = from_chunks(lax.map(lambda qb: attend(qb, k_all, v_all), to_chunks(ql, DA_Q_BLOCK)))

    def readout(o):
        bsz, n = o.shape[:2]
        return (rms_f32(o) * norm_w.astype(F32) * (1.0 - lam_init)).reshape(bsz, n, DA_WIDTH)

    yl = readout(ol)
    yc = readout(attend(qc, kc, vc.astype(F32))) if need_ctx else None
    return yc, yl


def merge_branches(h, ya, yb, yd, w_up, w_gate, b_gate, w_out):
    bsz, n, _ = h.shape
    g = jax.nn.sigmoid((h @ w_gate + b_gate).astype(F32)).reshape(bsz, n, N_BRANCH, D_MODEL)
    ys = jnp.stack([ya, yb, yd], axis=2).astype(h.dtype)
    up = jnp.einsum('btnf,nfd->btnd', ys, w_up)
    return jnp.sum(g.astype(h.dtype) * up, axis=2) @ w_out


def ec_moe(h, w_router, w1, w3, w2):
    bsz, n, _ = h.shape
    cap = EC_CAPACITY * n // N_EXPERTS
    aff = jax.nn.softmax((h @ w_router).astype(F32), axis=-1)
    vals, idx = lax.top_k(jnp.swapaxes(aff, 1, 2), cap)
    xg = jax.vmap(lambda hb, ib: hb[ib])(h, idx)
    a = jnp.einsum('becd,edf->becf', xg, w1)
    b = jnp.einsum('becd,edf->becf', xg, w3)
    y = jnp.einsum('becf,efd->becd', jax.nn.silu(a) * b, w2) * vals[..., None].astype(h.dtype)
    return jax.vmap(lambda yb, ib: jnp.zeros((n, D_MODEL), yb.dtype).at[ib.reshape(-1)].add(
        yb.reshape(-1, D_MODEL)))(y, idx)


def setup_inputs(seed: int = 0) -> dict:
    key = jax.random.key(seed)
    ks = jax.random.split(key, 28)
    D = D_MODEL

    def nrm(k, shape, scale):
        return jax.random.normal(k, shape, F32) * scale

    def gain(k, shape):
        return 1.0 + 0.1 * jax.random.normal(k, shape, F32)

    dt0 = jnp.exp(jax.random.uniform(ks[11], (DEPTH, 2, SSD_HEADS), F32,
                                     minval=math.log(1e-3), maxval=math.log(1e-1)))
    return {
        'x': nrm(ks[0], (BATCH, SEQ, D), 1.0),
        'c': nrm(ks[1], (BATCH, D), 1.0),
        'ctx': nrm(ks[2], (BATCH, CTX_LEN, D), 1.0),
        'c_ctx': nrm(ks[3], (D,), 1.0),
        'ada_w': nrm(ks[4], (DEPTH, D, 6 * D), 0.5 * D ** -0.5),
        'ada_b': nrm(ks[5], (DEPTH, 6 * D), 0.02),
        'norm1_w': gain(ks[6], (DEPTH, D)),
        'norm2_w': gain(ks[7], (DEPTH, D)),
        'w_in': nrm(ks[8], (DEPTH, D, N_IN), D ** -0.5),
        'ssd_conv_w': nrm(ks[9], (DEPTH, SSD_CONV_W, SSD_XBC), SSD_CONV_W ** -0.5),
        'ssd_conv_b': nrm(ks[10], (DEPTH, SSD_XBC), 0.02),
        'ssd_dt_bias': dt0 + jnp.log(-jnp.expm1(-dt0)),
        'ssd_a_log': jnp.log(jax.random.uniform(ks[12], (DEPTH, 2, SSD_HEADS), F32, minval=1.0, maxval=16.0)),
        'ssd_d': gain(ks[13], (DEPTH, SSD_HEADS)),
        'ssd_norm_w': gain(ks[14], (DEPTH, SSD_D_INNER)),
        'hg_lb': nrm(ks[15], (2, DEPTH, HG_WIDTH), 1.0),
        'hg_norm_w': gain(ks[16], (DEPTH, HG_HEAD_DIM)),
        'da_lambda': nrm(ks[17], (DEPTH, 4, DA_HEAD_DIM), 0.1),
        'da_norm_w': gain(ks[18], (DEPTH, 2 * DA_HEAD_DIM)),
        'w_up': nrm(ks[19], (DEPTH, N_BRANCH, BRANCH_WIDTH, D), BRANCH_WIDTH ** -0.5),
        'w_gate': nrm(ks[20], (DEPTH, D, N_BRANCH * D), D ** -0.5),
        'b_gate': nrm(ks[21], (DEPTH, N_BRANCH * D), 0.02),
        'w_out': nrm(ks[22], (DEPTH, D, D), D ** -0.5),
        'moe_router': nrm(ks[23], (DEPTH, D, N_EXPERTS), D ** -0.5),
        'moe_w1': nrm(ks[24], (DEPTH, N_EXPERTS, D, EXPERT_FF), D ** -0.5),
        'moe_w3': nrm(ks[25], (DEPTH, N_EXPERTS, D, EXPERT_FF), D ** -0.5),
        'moe_w2': nrm(ks[26], (DEPTH, N_EXPERTS, EXPERT_FF, D), EXPERT_FF ** -0.5),
        'final_norm_w': gain(ks[27], (D,)),
    }


def reference(x, c, ctx, c_ctx, ada_w, ada_b, norm1_w, norm2_w, w_in, ssd_conv_w, ssd_conv_b,
              ssd_dt_bias, ssd_a_log, ssd_d, ssd_norm_w, hg_lb, hg_norm_w, da_lambda, da_norm_w,
              w_up, w_gate, b_gate, w_out, moe_router, moe_w1, moe_w3, moe_w2, final_norm_w):
    bsz, n_lat, _ = x.shape
    cos, sin = axial_rope_tables(n_lat)
    lb_all = jnp.stack([lower_bounds(hg_lb[0]), lower_bounds(hg_lb[1])], axis=1)
    xl, xc = x, ctx
    for l in range(DEPTH):
        need_ctx = l < DEPTH - 1
        lam_init = 0.8 - 0.6 * math.exp(-0.3 * l)
        mod_l = (jax.nn.silu(c) @ ada_w[l] + ada_b[l]).reshape(bsz, 6, 1, D_MODEL)
        mod_c = (jax.nn.silu(c_ctx) @ ada_w[l] + ada_b[l]).reshape(6, 1, D_MODEL)
        hl = modulate(xl, norm1_w[l], mod_l[:, 0], mod_l[:, 1])
        hc = modulate(xc, norm1_w[l], mod_c[0], mod_c[1])
        pl = hl @ w_in[l]
        pc = hc @ w_in[l]
        ya_c, ya_l = ssd_branch(pc, pl, ssd_conv_w[l], ssd_conv_b[l], ssd_dt_bias[l], ssd_a_log[l],
                                ssd_d[l], ssd_norm_w[l], need_ctx)
        yb_c, yb_l = hgrn2_branch(pc, pl, lb_all[l], hg_norm_w[l], need_ctx)
        yd_c, yd_l = diff_attn_branch(pc, pl, da_lambda[l], da_norm_w[l], lam_init, cos, sin, need_ctx)
        xl = xl + mod_l[:, 2] * merge_branches(hl, ya_l, yb_l, yd_l, w_up[l], w_gate[l], b_gate[l], w_out[l])
        h2l = modulate(xl, norm2_w[l], mod_l[:, 3], mod_l[:, 4])
        xl = xl + mod_l[:, 5] * ec_moe(h2l, moe_router[l], moe_w1[l], moe_w3[l], moe_w2[l])
        if need_ctx:
            xc = xc + mod_c[2] * merge_branches(hc, ya_c, yb_c, yd_c, w_up[l], w_gate[l], b_gate[l], w_out[l])
            h2c = modulate(xc, norm2_w[l], mod_c[3], mod_c[4])
            xc = xc + mod_c[5] * ec_moe(h2c, moe_router[l], moe_w1[l], moe_w3[l], moe_w2[l])
    return rmsnorm(xl, final_norm_w)
```

```python
import functools
import math

import jax
import jax.numpy as jnp
from jax import lax
from jax.experimental import pallas as pl
from jax.experimental.pallas import tpu as pltpu

F32 = jnp.float32
BF16 = jnp.bfloat16
I32 = jnp.int32
HIGHEST = lax.Precision.HIGHEST

D = 2048
EPS = 1e-6
GRID_W = 64
SSD_D_INNER = 1024
SSD_HEADS = 16
SSD_GROUPS = 2
SSD_STATE = 128
SSD_XBC = 1536
SSD_CONV_W = 5
HG_WIDTH = 1024
HG_HEADS = 8
HG_HEAD_DIM = 128
DA_HEADS = 8
DA_HEAD_DIM = 64
DA_WIDTH = 1024
ROPE_BASE = 10000.0
ROPE_PAIRS = 16
N_BRANCH = 3
N_EXPERTS = 16
EXPERT_FF = 2048
EC_CAPACITY = 2

OFF_SSD_Z = 0
OFF_SSD_XBC = 1024
OFF_SSD_DT = 2560
OFF_HG_Q = 2592
OFF_HG_F = OFF_HG_Q + 1024
OFF_HG_I = OFF_HG_F + 2048
OFF_HG_G = OFF_HG_I + 1024
OFF_DA_Q = OFF_HG_G + 1024
OFF_DA_K = OFF_DA_Q + 1024
OFF_DA_V = OFF_DA_K + 1024
N_IN = OFF_DA_V + 1024

LANES = 128
PA_Z, PA_HQ, PA_HI, PA_HG, PA_DV, PA_XBC = 0, 1024, 2048, 3072, 4096, 5120
PA_N = 6656
PF_F, PF_DT = 0, 2048
PF_N = 2176
PQK_N = 2048

SCAN_Q = 256
TOK_BLK = 256
SLOT_ALIGN = 16
VMEM_LIMIT = 56 * 1024 * 1024


def _cparams(sem, vmem=VMEM_LIMIT):
    return pltpu.CompilerParams(dimension_semantics=sem, vmem_limit_bytes=vmem)


def _silu(x):
    return x * jax.nn.sigmoid(x)


def _adaln_kernel(cb_ref, w_ref, b_ref, o_ref, *, n_rows, tn):
    nct = tn // LANES

    def body(k, accs):
        r0 = pl.multiple_of(k * 8, 8)
        ws = [w_ref[pl.ds(r0, 8), c * LANES:(c + 1) * LANES] for c in range(nct)]
        out = []
        for r in range(n_rows):
            s = _silu(cb_ref[r, pl.ds(r0, 8), :])
            out.append(tuple(accs[r][c] + s * ws[c] for c in range(nct)))
        return tuple(out)

    zero = jnp.zeros((8, LANES), F32)
    accs = lax.fori_loop(0, D // 8, body, tuple(tuple(zero for _ in range(nct)) for _ in range(n_rows)))
    rows = []
    for r in range(n_rows):
        rows.append(jnp.concatenate([jnp.sum(accs[r][c], axis=0, keepdims=True) for c in range(nct)], axis=1))
    rows.append(jnp.zeros((8 - n_rows, tn), F32))
    o_ref[...] = jnp.concatenate(rows, axis=0) + b_ref[...]


def _adaln(cvec, ada_w, ada_b):
    n_rows = cvec.shape[0]
    depth, _, n6 = ada_w.shape
    tn = 512
    cb = jnp.broadcast_to(cvec[:, :, None], (n_rows, D, LANES))
    return pl.pallas_call(
        functools.partial(_adaln_kernel, n_rows=n_rows, tn=tn),
        grid=(depth, n6 // tn),
        in_specs=[pl.BlockSpec((n_rows, D, LANES), lambda l, j: (0, 0, 0)),
                  pl.BlockSpec((None, D, tn), lambda l, j: (l, 0, j)),
                  pl.BlockSpec((None, 1, tn), lambda l, j: (l, 0, j))],
        out_specs=pl.BlockSpec((None, 8, tn), lambda l, j: (l, 0, j)),
        out_shape=jax.ShapeDtypeStruct((depth, 8, n6), F32),
        compiler_params=_cparams(("parallel", "parallel")),
        name="adaln",
    )(cb, ada_w, ada_b.reshape(depth, 1, n6))


def _modulate_kernel(x_ref, nw_ref, sh_ref, sc_ref, *rest, with_router):
    x = x_ref[...]
    ms = jnp.mean(x * x, axis=-1, keepdims=True)
    h = (x * lax.rsqrt(ms + EPS) * nw_ref[...]) * (1.0 + sc_ref[...]) + sh_ref[...]
    if with_router:
        wr_ref, h_ref, aff_ref = rest
        logits = jnp.dot(h, wr_ref[...], preferred_element_type=F32, precision=HIGHEST)
        m = jnp.max(logits, axis=-1, keepdims=True)
        e = jnp.exp(logits - m)
        aff_ref[...] = e / jnp.sum(e, axis=-1, keepdims=True)
    else:
        (h_ref,) = rest
    h_ref[...] = h.astype(BF16)


def _modulate(x, nw, shift, scale, w_router=None):
    bsz, n, _ = x.shape
    tm = min(512, n)
    with_router = w_router is not None
    in_specs = [pl.BlockSpec((None, tm, D), lambda b, i: (b, i, 0)),
                pl.BlockSpec((1, D), lambda b, i: (0, 0)),
                pl.BlockSpec((None, 1, D), lambda b, i: (b, 0, 0)),
                pl.BlockSpec((None, 1, D), lambda b, i: (b, 0, 0))]
    args = [x, nw.reshape(1, D), shift, scale]
    out_specs = [pl.BlockSpec((None, tm, D), lambda b, i: (b, i, 0))]
    out_shape = [jax.ShapeDtypeStruct((bsz, n, D), BF16)]
    if with_router:
        in_specs.append(pl.BlockSpec((D, N_EXPERTS), lambda b, i: (0, 0)))
        args.append(w_router)
        out_specs.append(pl.BlockSpec((None, tm, N_EXPERTS), lambda b, i: (b, i, 0)))
        out_shape.append(jax.ShapeDtypeStruct((bsz, n, N_EXPERTS), F32))
    res = pl.pallas_call(
        functools.partial(_modulate_kernel, with_router=with_router),
        grid=(bsz, n // tm),
        in_specs=in_specs, out_specs=out_specs, out_shape=out_shape,
        compiler_params=_cparams(("parallel", "parallel")),
        name="modulate_router" if with_router else "modulate",
    )(*args)
    return res if with_router else res[0]


def _mm_kernel(a_ref, w_ref, *rest, rope, tn):
    acc = jnp.dot(a_ref[...], w_ref[...], preferred_element_type=F32)
    if rope:
        cos_ref, sin_ref, o_ref = rest
        cos = cos_ref[...]
        sin = sin_ref[...]
        lane = lax.broadcasted_iota(I32, cos.shape, 1)
        first_half = (lane % DA_HEAD_DIM) < (DA_HEAD_DIM // 2)
        for c in range(tn // LANES):
            xs = acc[:, c * LANES:(c + 1) * LANES]
            partner = jnp.where(first_half,
                                pltpu.roll(xs, LANES - DA_HEAD_DIM // 2, 1),
                                pltpu.roll(xs, DA_HEAD_DIM // 2, 1))
            o_ref[:, c * LANES:(c + 1) * LANES] = (xs * cos + partner * sin).astype(o_ref.dtype)
    else:
        (o_ref,) = rest
        o_ref[...] = acc.astype(o_ref.dtype)


def _matmul(a, w, out_dtype, tm, tn, rope_tables=None):
    m, k = a.shape
    n = w.shape[1]
    rope = rope_tables is not None
    in_specs = [pl.BlockSpec((tm, k), lambda i, j: (i, 0)),
                pl.BlockSpec((k, tn), lambda i, j: (0, j))]
    args = [a, w]
    if rope:
        cos, sin = rope_tables
        nt = cos.shape[0] // tm
        in_specs += [pl.BlockSpec((tm, LANES), lambda i, j: (i % nt, 0)),
                     pl.BlockSpec((tm, LANES), lambda i, j: (i % nt, 0))]
        args += [cos, sin]
    return pl.pallas_call(
        functools.partial(_mm_kernel, rope=rope, tn=tn),
        grid=(m // tm, n // tn),
        in_specs=in_specs,
        out_specs=pl.BlockSpec((tm, tn), lambda i, j: (i, j)),
        out_shape=jax.ShapeDtypeStruct((m, n), out_dtype),
        compiler_params=_cparams(("parallel", "parallel")),
        name="proj_rope" if rope else "proj",
    )(*args)


def _conv_kernel(prev_ref, cur_ref, next_ref, w_ref, b_ref, o_ref, *, tc):
    i = pl.program_id(1)
    last = pl.num_programs(1) - 1
    pad = (SSD_CONV_W - 1) // 2
    prev = jnp.where(i > 0, prev_ref[...].astype(F32), 0.0)
    nxt = jnp.where(i < last, next_ref[...].astype(F32), 0.0)
    ext = jnp.concatenate([prev, cur_ref[...].astype(F32), nxt], axis=0)
    w = w_ref[...]
    y = b_ref[...] + w[0:1] * ext[HALO - pad:HALO - pad + tc]
    for k in range(1, SSD_CONV_W):
        y = y + w[k:k + 1] * ext[HALO - pad + k:HALO - pad + k + tc]
    o_ref[...] = _silu(y).astype(o_ref.dtype)


HALO = 16


def _ssd_conv(pa, conv_w, conv_b):
    bsz, n, _ = pa.shape
    tc = min(512, n)
    tw = 512
    c0 = PA_XBC // tw
    nbh = n // HALO
    return pl.pallas_call(
        functools.partial(_conv_kernel, tc=tc),
        grid=(bsz, n // tc, SSD_XBC // tw),
        in_specs=[pl.BlockSpec((None, HALO, tw), lambda b, i, j: (b, jnp.maximum(i * (tc // HALO) - 1, 0), c0 + j)),
                  pl.BlockSpec((None, tc, tw), lambda b, i, j: (b, i, c0 + j)),
                  pl.BlockSpec((None, HALO, tw),
                               lambda b, i, j: (b, jnp.minimum((i + 1) * (tc // HALO), nbh - 1), c0 + j)),
                  pl.BlockSpec((SSD_CONV_W, tw), lambda b, i, j: (0, j)),
                  pl.BlockSpec((1, tw), lambda b, i, j: (0, j))],
        out_specs=pl.BlockSpec((None, tc, tw), lambda b, i, j: (b, i, j)),
        out_shape=jax.ShapeDtypeStruct((bsz, n, SSD_XBC), BF16),
        compiler_params=_cparams(("parallel", "parallel", "parallel")),
        name="ssd_conv",
    )(pa, pa, pa, conv_w, conv_b.reshape(1, SSD_XBC))


def _tri(q, rev):
    r = lax.broadcasted_iota(I32, (q, q), 0)
    c = lax.broadcasted_iota(I32, (q, q), 1)
    return (c >= r) if rev else (r >= c)


def _ssd_scan_kernel(*refs, rev, has_init, readout):
    it = iter(refs)
    xact_ref, dt_ref, alog_ref, bias_ref = next(it), next(it), next(it), next(it)
    init_ref = next(it) if has_init else None
    if readout:
        z_ref, yprev_ref, d_ref, nw_ref = next(it), next(it), next(it), next(it)
    y_ref, st_ref = next(it), next(it)
    ybuf_ref = next(it) if readout else None

    q = SCAN_Q
    hd = SSD_D_INNER // SSD_HEADS
    hg = SSD_HEADS // SSD_GROUPS
    gw = hg * hd
    d_off = SSD_HEADS if rev else 0

    @pl.when(pl.program_id(1) == 0)
    def _():
        st_ref[...] = init_ref[...] if has_init else jnp.zeros_like(st_ref)

    tri = _tri(q, rev)
    dtv = jax.nn.softplus(dt_ref[...] + bias_ref[...])
    da = dtv * (-jnp.exp(alog_ref[...]))
    cs = jnp.dot(tri.astype(F32), da, preferred_element_type=F32, precision=HIGHEST)
    cs_t = cs.T
    erow = lax.broadcasted_iota(I32, (LANES, SSD_D_INNER), 0)
    ecol = lax.broadcasted_iota(I32, (LANES, SSD_D_INNER), 1)
    expand = (erow == d_off + ecol // hd).astype(F32)
    csx = jnp.dot(cs, expand, preferred_element_type=F32, precision=HIGHEST)
    dtx = jnp.dot(dtv, expand, preferred_element_type=F32, precision=HIGHEST)
    x = xact_ref[:, :SSD_D_INNER].astype(F32)
    xdt = x * dtx
    xdt_b = xdt.astype(BF16)
    cs_end = csx[0:1] if rev else csx[q - 1:q]
    ecs = jnp.exp(csx)
    xw = (xdt * jnp.exp(cs_end - csx)).astype(BF16)
    ecs_end = jnp.exp(cs_end)
    lane = lax.broadcasted_iota(I32, (q, LANES), 1)
    out_ref = ybuf_ref if readout else y_ref

    for g in range(SSD_GROUPS):
        bg = xact_ref[:, SSD_D_INNER + g * SSD_STATE:SSD_D_INNER + (g + 1) * SSD_STATE]
        cg = xact_ref[:, SSD_D_INNER + (SSD_GROUPS + g) * SSD_STATE:SSD_D_INNER + (SSD_GROUPS + g + 1) * SSD_STATE]
        cb = lax.dot_general(cg, bg, (((1,), (1,)), ((), ())), preferred_element_type=F32)
        st_g = st_ref[:, g * gw:(g + 1) * gw]
        y_inter = jnp.dot(cg, st_g.astype(BF16), preferred_element_type=F32) * ecs[:, g * gw:(g + 1) * gw]
        for jj in range(hg // 2):
            col0 = g * gw + jj * LANES
            xp = xdt_b[:, col0:col0 + LANES]
            acc = y_inter[:, jj * LANES:(jj + 1) * LANES]
            for s in range(2):
                j = d_off + g * hg + 2 * jj + s
                e = cs[:, j:j + 1] - cs_t[j:j + 1, :]
                seg = jnp.exp(jnp.where(tri, e, -jnp.inf))
                m = (cb * seg).astype(BF16)
                xm = jnp.where((lane >= hd) if s else (lane < hd), xp, jnp.zeros_like(xp))
                acc = acc + jnp.dot(m, xm, preferred_element_type=F32)
            out_ref[:, col0:col0 + LANES] = acc
        upd = lax.dot_general(bg, xw[:, g * gw:(g + 1) * gw], (((0,), (0,)), ((), ())),
                              preferred_element_type=F32)
        st_ref[:, g * gw:(g + 1) * gw] = st_g * ecs_end[:, g * gw:(g + 1) * gw] + upd

    if readout:
        z = z_ref[...].astype(F32)
        yy = (yprev_ref[...] + ybuf_ref[...] + d_ref[...] * x) * _silu(z)
        for g in range(SSD_GROUPS):
            seg = yy[:, g * gw:(g + 1) * gw]
            ms = jnp.mean(seg * seg, axis=-1, keepdims=True)
            y_ref[:, g * gw:(g + 1) * gw] = (seg * lax.rsqrt(ms + EPS) * nw_ref[:, g * gw:(g + 1) * gw]).astype(y_ref.dtype)


def _ssd_scan(xact, pf, alog128, bias128, *, rev, init=None, readout=None):
    bsz, n, _ = xact.shape
    nc = n // SCAN_Q
    cidx = (lambda c: nc - 1 - c) if rev else (lambda c: c)
    in_specs = [pl.BlockSpec((None, SCAN_Q, SSD_XBC), lambda b, c: (b, cidx(c), 0)),
                pl.BlockSpec((None, SCAN_Q, LANES), lambda b, c: (b, cidx(c), PF_DT // LANES)),
                pl.BlockSpec((1, LANES), lambda b, c: (0, 0)),
                pl.BlockSpec((1, LANES), lambda b, c: (0, 0))]
    args = [xact, pf, alog128, bias128]
    if init is not None:
        in_specs.append(pl.BlockSpec((None, SSD_STATE, SSD_D_INNER), lambda b, c: (b, 0, 0)))
        args.append(init)
    scratch = []
    if readout is not None:
        pa, yprev, d_full, norm_w = readout
        in_specs += [pl.BlockSpec((None, SCAN_Q, SSD_D_INNER), lambda b, c: (b, cidx(c), PA_Z // SSD_D_INNER)),
                     pl.BlockSpec((None, SCAN_Q, SSD_D_INNER), lambda b, c: (b, cidx(c), 0)),
                     pl.BlockSpec((1, SSD_D_INNER), lambda b, c: (0, 0)),
                     pl.BlockSpec((1, SSD_D_INNER), lambda b, c: (0, 0))]
        args += [pa, yprev, d_full, norm_w]
        scratch = [pltpu.VMEM((SCAN_Q, SSD_D_INNER), F32)]
    y_dtype = BF16 if readout is not None else F32
    y, st = pl.pallas_call(
        functools.partial(_ssd_scan_kernel, rev=rev, has_init=init is not None, readout=readout is not None),
        grid=(bsz, nc),
        in_specs=in_specs,
        out_specs=[pl.BlockSpec((None, SCAN_Q, SSD_D_INNER), lambda b, c: (b, cidx(c), 0)),
                   pl.BlockSpec((None, SSD_STATE, SSD_D_INNER), lambda b, c: (b, 0, 0))],
        out_shape=[jax.ShapeDtypeStruct((bsz, n, SSD_D_INNER), y_dtype),
                   jax.ShapeDtypeStruct((bsz, SSD_STATE, SSD_D_INNER), F32)],
        scratch_shapes=scratch,
        compiler_params=_cparams(("parallel", "arbitrary")),
        name="ssd_scan_bwd" if rev else "ssd_scan_fwd",
    )(*args)
    return y, st


def _ref_rows(bb, s, rev):
    q, w = bb.shape
    rl = s if rev else s - 1
    if 2 * s >= 8:
        b3 = bb.reshape(q // (2 * s), 2 * s, w)
        return jnp.broadcast_to(b3[:, rl:rl + 1, :], b3.shape).reshape(q, w)
    off = lax.broadcasted_iota(I32, (q, w), 0) % (2 * s)
    out = bb
    for ov in range(2 * s):
        if ov != rl:
            out = jnp.where(off == ov, pltpu.roll(bb, (ov - rl) % q, 0), out)
    return out


def _hg_scan_kernel(*refs, rev, has_init, readout):
    it = iter(refs)
    q_ref, f_ref, v_ref, lb_ref = next(it), next(it), next(it), next(it)
    init_ref = next(it) if has_init else None
    if readout:
        g_ref, oprev_ref, nw_ref = next(it), next(it), next(it)
    o_ref, st_ref, att_ref = next(it), next(it), next(it)
    obuf_ref = next(it) if readout else None

    q = SCAN_Q
    hd = HG_HEAD_DIM

    @pl.when(pl.program_id(1) == 0)
    def _():
        st_ref[...] = init_ref[...] if has_init else jnp.zeros_like(st_ref)

    lb = lb_ref[...]
    qq = _silu(q_ref[...].astype(F32))
    f = lb + (1.0 - lb) * jax.nn.sigmoid(f_ref[...])
    kk = 1.0 - f
    lf = jnp.log(f)
    tri = _tri(q, rev)
    bb = jnp.dot(tri.astype(F32), lf, preferred_element_type=F32, precision=HIGHEST)

    row = lax.broadcasted_iota(I32, (q, HG_WIDTH), 0)
    ri = lax.broadcasted_iota(I32, (q, q), 0)
    ci = lax.broadcasted_iota(I32, (q, q), 1)
    xr = ri ^ ci
    att_ref[...] = jnp.zeros_like(att_ref)
    s = q // 2
    while s >= 1:
        upper = (row % (2 * s)) >= s
        q_side = jnp.logical_not(upper) if rev else upper
        delta = bb - _ref_rows(bb, s, rev)
        ex = jnp.exp(jnp.where(q_side, delta, -delta))
        qt = jnp.where(q_side, qq * ex, 0.0).astype(BF16)
        kt = jnp.where(q_side, 0.0, kk * ex).astype(BF16)
        level = (xr >> int(math.log2(s))) == 1
        for h in range(HG_HEADS):
            p = lax.dot_general(qt[:, h * hd:(h + 1) * hd], kt[:, h * hd:(h + 1) * hd],
                                (((1,), (1,)), ((), ())), preferred_element_type=F32)
            att_ref[h] = jnp.where(level, p, att_ref[h])
        s //= 2

    b_end = bb[0:1] if rev else bb[q - 1:q]
    qe = (qq * jnp.exp(bb)).astype(BF16)
    kh = (kk * jnp.exp(b_end - bb)).astype(BF16)
    qk = (qq * kk).astype(BF16)
    e_end = jnp.exp(b_end)
    ones = jnp.ones((hd, hd), BF16)
    out_ref = obuf_ref if readout else o_ref
    for h in range(HG_HEADS):
        sl = slice(h * hd, (h + 1) * hd)
        vh = v_ref[:, sl]
        st_h = st_ref[sl, :]
        o = jnp.dot(att_ref[h].astype(BF16), vh, preferred_element_type=F32)
        o = o + jnp.dot(qk[:, sl], ones, preferred_element_type=F32) * vh.astype(F32)
        o = o + lax.dot_general(qe[:, sl], st_h.astype(BF16), (((1,), (1,)), ((), ())), preferred_element_type=F32)
        out_ref[:, sl] = o
        upd = lax.dot_general(vh, kh[:, sl], (((0,), (0,)), ((), ())), preferred_element_type=F32)
        st_ref[sl, :] = st_h * e_end[:, sl] + upd

    if readout:
        for h in range(HG_HEADS):
            sl = slice(h * hd, (h + 1) * hd)
            o = oprev_ref[:, sl] + obuf_ref[:, sl]
            ms = jnp.mean(o * o, axis=-1, keepdims=True)
            gate = _silu(g_ref[:, sl].astype(F32))
            o_ref[:, sl] = (o * lax.rsqrt(ms + EPS) * nw_ref[...] * gate).astype(o_ref.dtype)


def _hg_scan(pa, pf, lb_row, *, rev, init=None, readout=None):
    bsz, n, _ = pa.shape
    nc = n // SCAN_Q
    cidx = (lambda c: nc - 1 - c) if rev else (lambda c: c)
    fcol = (PF_F // HG_WIDTH) + (1 if rev else 0)
    in_specs = [pl.BlockSpec((None, SCAN_Q, HG_WIDTH), lambda b, c: (b, cidx(c), PA_HQ // HG_WIDTH)),
                pl.BlockSpec((None, SCAN_Q, HG_WIDTH), lambda b, c: (b, cidx(c), fcol)),
                pl.BlockSpec((None, SCAN_Q, HG_WIDTH), lambda b, c: (b, cidx(c), PA_HI // HG_WIDTH)),
                pl.BlockSpec((1, HG_WIDTH), lambda b, c: (0, 0))]
    args = [pa, pf, pa, lb_row]
    if init is not None:
        in_specs.append(pl.BlockSpec((None, HG_WIDTH, HG_HEAD_DIM), lambda b, c: (b, 0, 0)))
        args.append(init)
    scratch = [pltpu.VMEM((HG_HEADS, SCAN_Q, SCAN_Q), F32)]
    if readout is not None:
        oprev, nw128 = readout
        in_specs += [pl.BlockSpec((None, SCAN_Q, HG_WIDTH), lambda b, c: (b, cidx(c), PA_HG // HG_WIDTH)),
                     pl.BlockSpec((None, SCAN_Q, HG_WIDTH), lambda b, c: (b, cidx(c), 0)),
                     pl.BlockSpec((1, HG_HEAD_DIM), lambda b, c: (0, 0))]
        args += [pa, oprev, nw128]
        scratch.append(pltpu.VMEM((SCAN_Q, HG_WIDTH), F32))
    o_dtype = BF16 if readout is not None else F32
    assert PA_HQ % HG_WIDTH == 0 or True
    o, st = pl.pallas_call(
        functools.partial(_hg_scan_kernel, rev=rev, has_init=init is not None, readout=readout is not None),
        grid=(bsz, nc),
        in_specs=in_specs,
        out_specs=[pl.BlockSpec((None, SCAN_Q, HG_WIDTH), lambda b, c: (b, cidx(c), 0)),
                   pl.BlockSpec((None, HG_WIDTH, HG_HEAD_DIM), lambda b, c: (b, 0, 0))],
        out_shape=[jax.ShapeDtypeStruct((bsz, n, HG_WIDTH), o_dtype),
                   jax.ShapeDtypeStruct((bsz, HG_WIDTH, HG_HEAD_DIM), F32)],
        scratch_shapes=scratch,
        compiler_params=_cparams(("parallel", "arbitrary")),
        name="hg_scan_bwd" if rev else "hg_scan_fwd",
    )(*args)
    return o, st


def _attn_kernel(*refs, seg_lens, tk, lam_init):
    lam_ref, nw_ref, q_ref = refs[0], refs[1], refs[2]
    kv_refs = refs[3:3 + 2 * len(seg_lens)]
    o_ref = refs[3 + 2 * len(seg_lens)]
    tq = q_ref.shape[0]
    qv = q_ref[...] * jnp.asarray(DA_HEAD_DIM ** -0.5, BF16)
    lane = lax.broadcasted_iota(I32, qv.shape, 1)
    zero = jnp.zeros_like(qv)
    qs = jnp.concatenate([jnp.where(lane < DA_HEAD_DIM, qv, zero),
                          jnp.where(lane >= DA_HEAD_DIM, qv, zero)], axis=0)

    def step(kc, vc, carry):
        m, l, acc = carry
        s = lax.dot_general(qs, kc, (((1,), (1,)), ((), ())), preferred_element_type=F32)
        m_new = jnp.maximum(m, jnp.max(s, axis=-1, keepdims=True))
        alpha = jnp.exp(m - m_new)
        p = jnp.exp(s - m_new)
        l = alpha * l + jnp.sum(p, axis=-1, keepdims=True)
        acc = alpha * acc + jnp.dot(p.astype(BF16), vc, preferred_element_type=F32)
        return m_new, l, acc

    carry = (jnp.full((2 * tq, 1), -jnp.inf, F32), jnp.zeros((2 * tq, 1), F32),
             jnp.zeros((2 * tq, 2 * DA_HEAD_DIM), F32))
    for si, n in enumerate(seg_lens):
        k_ref, v_ref = kv_refs[2 * si], kv_refs[2 * si + 1]
        t = min(tk, n)

        def body(c, carry, k_ref=k_ref, v_ref=v_ref, t=t):
            r0 = pl.multiple_of(c * t, t)
            return step(k_ref[pl.ds(r0, t), :], v_ref[pl.ds(r0, t), :], carry)

        carry = lax.fori_loop(0, n // t, body, carry)
    m, l, acc = carry
    lam = lam_ref[...]
    lmbda = (jnp.exp(jnp.sum(lam[0:1] * lam[1:2], axis=-1, keepdims=True))
             - jnp.exp(jnp.sum(lam[2:3] * lam[3:4], axis=-1, keepdims=True)) + lam_init)
    o = acc[:tq] / l[:tq] - lmbda * (acc[tq:] / l[tq:])
    ms = jnp.mean(o * o, axis=-1, keepdims=True)
    o_ref[...] = (o * lax.rsqrt(ms + EPS) * nw_ref[...] * (1.0 - lam_init)).astype(o_ref.dtype)


def _diff_attn(pqk_q, kv_segs, lam, nw, lam_init, tq, tk):
    bsz, nq, _ = pqk_q.shape
    hw = 2 * DA_HEAD_DIM
    kcol0 = DA_WIDTH // hw
    vcol0 = PA_DV // hw
    in_specs = [pl.BlockSpec((4, DA_HEAD_DIM), lambda b, h, i: (0, 0)),
                pl.BlockSpec((1, hw), lambda b, h, i: (0, 0)),
                pl.BlockSpec((None, tq, hw), lambda b, h, i: (b, i, h))]
    args = [lam, nw.reshape(1, hw), pqk_q]
    seg_lens = []
    for pqk, pa in kv_segs:
        nk = pqk.shape[1]
        seg_lens.append(nk)
        in_specs += [pl.BlockSpec((None, nk, hw), lambda b, h, i: (b, 0, kcol0 + h)),
                     pl.BlockSpec((None, nk, hw), lambda b, h, i: (b, 0, vcol0 + h))]
        args += [pqk, pa]
    return pl.pallas_call(
        functools.partial(_attn_kernel, seg_lens=tuple(seg_lens), tk=tk, lam_init=lam_init),
        grid=(bsz, DA_HEADS, nq // tq),
        in_specs=in_specs,
        out_specs=pl.BlockSpec((None, tq, hw), lambda b, h, i: (b, i, h)),
        out_shape=jax.ShapeDtypeStruct((bsz, nq, DA_WIDTH), BF16),
        compiler_params=_cparams(("parallel", "parallel", "parallel")),
        name="diff_attn",
    )(*args)


def _merge_kernel(x_ref, g1_ref, h_ref, ya_ref, yb_ref, yd_ref, wg0, wg1, wg2, bg0, bg1, bg2,
                  wu0, wu1, wu2, wo_ref, o_ref, acc_ref):
    n = pl.program_id(2)

    @pl.when(n == 0)
    def _():
        acc_ref[...] = jnp.zeros_like(acc_ref)

    h = h_ref[...]
    mix = None
    for y_ref, wg, bg, wu in ((ya_ref, wg0, bg0, wu0), (yb_ref, wg1, bg1, wu1), (yd_ref, wg2, bg2, wu2)):
        gate = jax.nn.sigmoid(jnp.dot(h, wg[...], preferred_element_type=F32) + bg[...])
        up = jnp.dot(y_ref[...], wu[...], preferred_element_type=F32)
        mix = gate * up if mix is None else mix + gate * up
    acc_ref[...] += jnp.dot(mix.astype(BF16), wo_ref[...], preferred_element_type=F32)

    @pl.when(n == pl.num_programs(2) - 1)
    def _():
        o_ref[...] = x_ref[...] + g1_ref[...] * acc_ref[...]


def _merge(x, gate1, h, ya, yb, yd, w_gate, b_gate, w_up, w_out):
    bsz, n, _ = x.shape
    tm = min(512, n)
    tn = 256
    nn = D // tn
    row = lambda b, i, j: (b, i, 0)
    in_specs = [pl.BlockSpec((None, tm, D), row),
                pl.BlockSpec((None, 1, D), lambda b, i, j: (b, 0, 0)),
                pl.BlockSpec((None, tm, D), row),
                pl.BlockSpec((None, tm, 1024), row),
                pl.BlockSpec((None, tm, 1024), row),
                pl.BlockSpec((None, tm, 1024), row)]
    in_specs += [pl.BlockSpec((D, tn), functools.partial(lambda b, i, j, k: (0, k * nn + j), k=k)) for k in range(3)]
    in_specs += [pl.BlockSpec((1, tn), functools.partial(lambda b, i, j, k: (0, k * nn + j), k=k)) for k in range(3)]
    in_specs += [pl.BlockSpec((None, 1024, tn), functools.partial(lambda b, i, j, k: (k, 0, j), k=k)) for k in range(3)]
    in_specs += [pl.BlockSpec((tn, D), lambda b, i, j: (j, 0))]
    return pl.pallas_call(
        _merge_kernel,
        grid=(bsz, n // tm, nn),
        in_specs=in_specs,
        out_specs=pl.BlockSpec((None, tm, D), row),
        out_shape=jax.ShapeDtypeStruct((bsz, n, D), F32),
        scratch_shapes=[pltpu.VMEM((tm, D), F32)],
        compiler_params=_cparams(("parallel", "parallel", "arbitrary")),
        name="merge",
    )(x, gate1, h, ya, yb, yd, w_gate, w_gate, w_gate, b_gate, b_gate, b_gate, w_up, w_up, w_up, w_out)


def _route_kernel(aff_ref, posm_ref, pos_ref, *, cap, rpe):
    rows = N_EXPERTS * rpe
    bits = pltpu.bitcast(aff_ref[...], I32)
    ones = jnp.ones((LANES, LANES), BF16)
    r = lax.broadcasted_iota(I32, (rows, rows), 0)
    c = lax.broadcasted_iota(I32, (rows, rows), 1)
    same = (r // rpe) == (c // rpe)
    grp = same.astype(BF16)
    grp_before = jnp.logical_and(same, c < r).astype(BF16)
    lr = lax.broadcasted_iota(I32, (LANES, LANES), 0)
    lc = lax.broadcasted_iota(I32, (LANES, LANES), 1)
    before = (lr < lc).astype(BF16)

    def count(mask):
        per_row = jnp.dot(mask.astype(BF16), ones, preferred_element_type=F32)
        return jnp.dot(grp, per_row.astype(BF16), preferred_element_type=F32)

    def excl_prefix(mask):
        mb = mask.astype(BF16)
        within = jnp.dot(mb, before, preferred_element_type=F32)
        per_row = jnp.dot(mb, ones, preferred_element_type=F32)
        return within + jnp.dot(grp_before, per_row.astype(BF16), preferred_element_type=F32)

    def body(i, thr):
        cand = thr | (jnp.int32(1) << (30 - i))
        return jnp.where(count(bits >= cand) >= cap, cand, thr)

    thr = lax.fori_loop(0, 31, body, jnp.zeros((rows, LANES), I32))
    gt = bits > thr
    eq = bits == thr
    need = cap - count(gt)
    sel = jnp.logical_or(gt, jnp.logical_and(eq, excl_prefix(eq) < need))
    pos = excl_prefix(sel).astype(I32)
    pos_ref[...] = pos
    posm_ref[...] = jnp.where(sel, pos, -1)


def _route(aff, cap):
    bsz, n, _ = aff.shape
    rpe = n // LANES
    rows = N_EXPERTS * rpe
    aff_t = jnp.swapaxes(aff, 1, 2).reshape(bsz, rows, LANES)
    spec = pl.BlockSpec((None, rows, LANES), lambda b: (b, 0, 0))
    posm, pos = pl.pallas_call(
        functools.partial(_route_kernel, cap=cap, rpe=rpe),
        grid=(bsz,),
        in_specs=[spec], out_specs=[spec, spec],
        out_shape=[jax.ShapeDtypeStruct((bsz, rows, LANES), I32)] * 2,
        compiler_params=_cparams(("parallel",)),
        name="route",
    )(aff_t)
    return posm.reshape(bsz, N_EXPERTS, n), pos.reshape(bsz, N_EXPERTS, n)


def _gather_kernel(base_ref, h_ref, posm_ref, xg_ref, *, win, nj):
    b, e, j = pl.program_id(0), pl.program_id(1), pl.program_id(2)

    @pl.when(j == 0)
    def _():
        xg_ref[...] = jnp.zeros_like(xg_ref)

    base = pl.multiple_of(base_ref[(b * N_EXPERTS + e) * nj + j], SLOT_ALIGN)
    rel = posm_ref[...] - base
    slot = lax.broadcasted_iota(I32, (win, rel.shape[1]), 0)
    onehot = jnp.where(slot == rel, 1.0, 0.0).astype(BF16)
    rows = jnp.dot(onehot, h_ref[...], preferred_element_type=F32).astype(BF16)
    xg_ref[pl.ds(base, win), :] += rows


def _expert_kernel(xg_ref, w1_ref, w3_ref, w2_ref, y_ref, acc_ref):
    f = pl.program_id(2)

    @pl.when(f == 0)
    def _():
        acc_ref[...] = jnp.zeros_like(acc_ref)

    xg = xg_ref[...]
    a = jnp.dot(xg, w1_ref[...], preferred_element_type=F32)
    g = jnp.dot(xg, w3_ref[...], preferred_element_type=F32)
    acc_ref[...] += jnp.dot((_silu(a) * g).astype(BF16), w2_ref[...], preferred_element_type=F32)

    @pl.when(f == pl.num_programs(2) - 1)
    def _():
        y_ref[...] = acc_ref[...].astype(y_ref.dtype)


def _combine_kernel(base_ref, x_ref, g2_ref, aff_ref, posm_ref, y_ref, *rest, win, nj, final):
    if final:
        fw_ref, o_ref, acc_ref = rest
    else:
        o_ref, acc_ref = rest
    b, j, e = pl.program_id(0), pl.program_id(1), pl.program_id(2)

    @pl.when(e == 0)
    def _():
        acc_ref[...] = jnp.zeros_like(acc_ref)

    base = base_ref[(b * N_EXPERTS + e) * nj + j]
    posm = posm_ref[...]
    aff = aff_ref[...]
    lane = lax.broadcasted_iota(I32, posm.shape, 1)
    mine = lane == e
    rel = jnp.sum(jnp.where(mine, posm, 0), axis=-1, keepdims=True) - base
    val = jnp.sum(jnp.where(mine, aff, 0.0), axis=-1, keepdims=True)
    slot = lax.broadcasted_iota(I32, (posm.shape[0], win), 1)
    onehot = jnp.where(slot == rel, 1.0, 0.0).astype(BF16)
    acc_ref[...] += val * jnp.dot(onehot, y_ref[...], preferred_element_type=F32)

    @pl.when(e == pl.num_programs(2) - 1)
    def _():
        out = x_ref[...] + g2_ref[...] * acc_ref[...]
        if final:
            ms = jnp.mean(out * out, axis=-1, keepdims=True)
            out = out * lax.rsqrt(ms + EPS) * fw_ref[...]
        o_ref[...] = out


def _ec_moe(x, gate2, h2, aff, w1, w3, w2, final_w=None):
    bsz, n, _ = x.shape
    cap = EC_CAPACITY * n // N_EXPERTS
    nj = n // TOK_BLK
    win = min(TOK_BLK + SLOT_ALIGN, cap)
    posm, pos = _route(aff, cap)
    base = jnp.minimum((pos[:, :, ::TOK_BLK] // SLOT_ALIGN) * SLOT_ALIGN, cap - win).astype(I32).reshape(-1)

    xg = pl.pallas_call(
        functools.partial(_gather_kernel, win=win, nj=nj),
        grid_spec=pltpu.PrefetchScalarGridSpec(
            num_scalar_prefetch=1,
            grid=(bsz, N_EXPERTS, nj),
            in_specs=[pl.BlockSpec((None, TOK_BLK, D), lambda b, e, j, base: (b, j, 0)),
                      pl.BlockSpec((None, None, 1, TOK_BLK), lambda b, e, j, base: (b, e, 0, j))],
            out_specs=pl.BlockSpec((None, None, cap, D), lambda b, e, j, base: (b, e, 0, 0))),
        out_shape=jax.ShapeDtypeStruct((bsz, N_EXPERTS, cap, D), BF16),
        compiler_params=_cparams(("parallel", "parallel", "arbitrary")),
        name="moe_gather",
    )(base, h2, posm.reshape(bsz, N_EXPERTS, 1, n))

    tf = 512
    y = pl.pallas_call(
        _expert_kernel,
        grid=(N_EXPERTS, bsz, EXPERT_FF // tf),
        in_specs=[pl.BlockSpec((None, None, cap, D), lambda e, b, f: (b, e, 0, 0)),
                  pl.BlockSpec((None, D, tf), lambda e, b, f: (e, 0, f)),
                  pl.BlockSpec((None, D, tf), lambda e, b, f: (e, 0, f)),
                  pl.BlockSpec((None, tf, D), lambda e, b, f: (e, f, 0))],
        out_specs=pl.BlockSpec((None, None, cap, D), lambda e, b, f: (b, e, 0, 0)),
        out_shape=jax.ShapeDtypeStruct((bsz, N_EXPERTS, cap, D), BF16),
        scratch_shapes=[pltpu.VMEM((cap, D), F32)],
        compiler_params=_cparams(("parallel", "parallel", "arbitrary")),
        name="moe_expert",
    )(xg, w1, w3, w2)

    final = final_w is not None
    in_specs = [pl.BlockSpec((None, TOK_BLK, D), lambda b, j, e, base: (b, j, 0)),
                pl.BlockSpec((None, 1, D), lambda b, j, e, base: (b, 0, 0)),
                pl.BlockSpec((None, TOK_BLK, N_EXPERTS), lambda b, j, e, base: (b, j, 0)),
                pl.BlockSpec((None, TOK_BLK, N_EXPERTS), lambda b, j, e, base: (b, j, 0)),
                pl.BlockSpec((pl.Element(win), pl.Element(D)),
                             lambda b, j, e, base: (pl.multiple_of(
                                 (b * N_EXPERTS + e) * cap + base[(b * N_EXPERTS + e) * nj + j], SLOT_ALIGN), 0))]
    args = [base, x, gate2, aff, jnp.swapaxes(posm, 1, 2), y.reshape(bsz * N_EXPERTS * cap, D)]
    if final:
        in_specs.append(pl.BlockSpec((1, D), lambda b, j, e, base: (0, 0)))
        args.append(final_w.reshape(1, D))
    return pl.pallas_call(
        functools.partial(_combine_kernel, win=win, nj=nj, final=final),
        grid_spec=pltpu.PrefetchScalarGridSpec(
            num_scalar_prefetch=1,
            grid=(bsz, nj, N_EXPERTS),
            in_specs=in_specs,
            out_specs=pl.BlockSpec((None, TOK_BLK, D), lambda b, j, e, base: (b, j, 0)),
            scratch_shapes=[pltpu.VMEM((TOK_BLK, D), F32)]),
        out_shape=jax.ShapeDtypeStruct((bsz, n, D), F32),
        compiler_params=_cparams(("parallel", "parallel", "arbitrary")),
        name="moe_combine",
    )(*args)


def _rope_tables(n_tokens):
    rows = n_tokens // GRID_W
    row = jnp.repeat(jnp.arange(rows, dtype=F32), GRID_W)
    col = (jnp.arange(rows * GRID_W, dtype=I32) % GRID_W).astype(F32)
    inv = ROPE_BASE ** (-jnp.arange(ROPE_PAIRS, dtype=F32) / ROPE_PAIRS)
    ang = jnp.concatenate([row[:, None] * inv, col[:, None] * inv], axis=-1)
    cos, sin = jnp.cos(ang), jnp.sin(ang)
    reps = LANES // DA_HEAD_DIM
    cos_t = jnp.tile(jnp.concatenate([cos, cos], axis=-1), (1, reps))
    sin_t = jnp.tile(jnp.concatenate([-sin, sin], axis=-1), (1, reps))
    return cos_t, sin_t


def _lower_bounds(p):
    cum = jnp.cumsum(jax.nn.softmax(p.astype(F32), axis=0), axis=0)
    return cum - cum[0]


def _pad128(v):
    return jnp.pad(v.reshape(1, -1).astype(F32), ((0, 0), (0, LANES - v.size)))


def kernel(x, c, ctx, c_ctx, ada_w, ada_b, norm1_w, norm2_w, w_in, ssd_conv_w, ssd_conv_b, ssd_dt_bias,
           ssd_a_log, ssd_d, ssd_norm_w, hg_lb, hg_norm_w, da_lambda, da_norm_w, w_up, w_gate, b_gate,
           w_out, moe_router, moe_w1, moe_w3, moe_w2, final_norm_w):
    bsz, n_lat, _ = x.shape
    n_ctx = ctx.shape[1]
    depth = ada_w.shape[0]
    rope = _rope_tables(n_lat)
    lb_all = jnp.stack([_lower_bounds(hg_lb[0]), _lower_bounds(hg_lb[1])], axis=1)

    mod = _adaln(jnp.concatenate([c, c_ctx[None]], axis=0), ada_w, ada_b)
    mod = mod.reshape(depth, 8, 6, 1, D)

    xl, xc = x, ctx
    for l in range(depth):
        need_ctx = l < depth - 1
        lam_init = 0.8 - 0.6 * math.exp(-0.3 * l)
        mod_l = [mod[l, :bsz, k] for k in range(6)]
        mod_c = [jnp.broadcast_to(mod[l, bsz:bsz + 1, k], (bsz, 1, D)) for k in range(6)]

        wl = w_in[l]
        w_a = jnp.concatenate([wl[:, OFF_SSD_Z:OFF_SSD_XBC], wl[:, OFF_HG_Q:OFF_HG_F], wl[:, OFF_HG_I:OFF_DA_Q],
                               wl[:, OFF_DA_V:N_IN], wl[:, OFF_SSD_XBC:OFF_SSD_DT]], axis=1).astype(BF16)
        w_f = jnp.concatenate([wl[:, OFF_HG_F:OFF_HG_I], wl[:, OFF_SSD_DT:OFF_HG_Q],
                               jnp.zeros((D, PF_N - PF_DT - 2 * SSD_HEADS), F32)], axis=1).astype(BF16)
        w_qk = wl[:, OFF_DA_Q:OFF_DA_V].astype(BF16)
        alog128 = _pad128(ssd_a_log[l])
        bias128 = _pad128(ssd_dt_bias[l])
        d_full = jnp.repeat(ssd_d[l].astype(F32), SSD_D_INNER // SSD_HEADS).reshape(1, SSD_D_INNER)
        ssd_nw = ssd_norm_w[l].reshape(1, SSD_D_INNER).astype(F32)
        hg_nw = hg_norm_w[l].reshape(1, HG_HEAD_DIM).astype(F32)
        wg_b = w_gate[l].astype(BF16)
        bg = b_gate[l].reshape(1, N_BRANCH * D)
        wu_b = w_up[l].astype(BF16)
        wo_b = w_out[l].astype(BF16)
        w1_b, w3_b, w2_b = moe_w1[l].astype(BF16), moe_w3[l].astype(BF16), moe_w2[l].astype(BF16)

        def project(xs, m, use_rope):
            n = xs.shape[1]
            h = _modulate(xs, norm1_w[l], m[0], m[1])
            h2d = h.reshape(bsz * n, D)
            tm = min(1024, n)
            pa = _matmul(h2d, w_a, BF16, tm, 512).reshape(bsz, n, PA_N)
            pf = _matmul(h2d, w_f, F32, min(512, n), PF_N).reshape(bsz, n, PF_N)
            pqk = _matmul(h2d, w_qk, BF16, tm, 512, rope_tables=rope if use_rope else None).reshape(bsz, n, PQK_N)
            return h, pa, pf, pqk

        hl, pa_l, pf_l, pqk_l = project(xl, mod_l, True)
        hc, pa_c, pf_c, pqk_c = project(xc, mod_c, False)

        xa_c = _ssd_conv(pa_c, ssd_conv_w[l], ssd_conv_b[l])
        xa_l = _ssd_conv(pa_l, ssd_conv_w[l], ssd_conv_b[l])
        yc_f, s_f = _ssd_scan(xa_c, pf_c, alog128, bias128, rev=False)
        yl_f, _ = _ssd_scan(xa_l, pf_l, alog128, bias128, rev=False, init=s_f)
        ya_c, s_b = _ssd_scan(xa_c, pf_c, alog128, bias128, rev=True, readout=(pa_c, yc_f, d_full, ssd_nw))
        ya_l, _ = _ssd_scan(xa_l, pf_l, alog128, bias128, rev=True, init=s_b, readout=(pa_l, yl_f, d_full, ssd_nw))

        lb_f, lb_b = lb_all[l, 0].reshape(1, HG_WIDTH), lb_all[l, 1].reshape(1, HG_WIDTH)
        oc_f, t_f = _hg_scan(pa_c, pf_c, lb_f, rev=False)
        ol_f, _ = _hg_scan(pa_l, pf_l, lb_f, rev=False, init=t_f)
        yb_c, t_b = _hg_scan(pa_c, pf_c, lb_b, rev=True, readout=(oc_f, hg_nw))
        yb_l, _ = _hg_scan(pa_l, pf_l, lb_b, rev=True, init=t_b, readout=(ol_f, hg_nw))

        yd_l = _diff_attn(pqk_l, [(pqk_c, pa_c), (pqk_l, pa_l)], da_lambda[l], da_norm_w[l], lam_init, 512, 512)

        xl = _merge(xl, mod_l[2], hl, ya_l, yb_l, yd_l, wg_b, bg, wu_b, wo_b)
        h2l, aff_l = _modulate(xl, norm2_w[l], mod_l[3], mod_l[4], w_router=moe_router[l])
        xl = _ec_moe(xl, mod_l[5], h2l, aff_l, w1_b, w3_b, w2_b,
                     final_w=final_norm_w if l == depth - 1 else None)
        if need_ctx:
            yd_c = _diff_attn(pqk_c, [(pqk_c, pa_c)], da_lambda[l], da_norm_w[l], lam_init, n_ctx, 512)
            xc = _merge(xc, mod_c[2], hc, ya_c, yb_c, yd_c, wg_b, bg, wu_b, wo_b)
            h2c, aff_c = _modulate(xc, norm2_w[l], mod_c[3], mod_c[4], w_router=moe_router[l])
            xc = _ec_moe(xc, mod_c[5], h2c, aff_c, w1_b, w3_b, w2_b)
    return xl
```

```python
import functools
import math

import jax
import jax.numpy as jnp
from jax import lax
from jax.experimental import pallas as pl
from jax.experimental.pallas import tpu as pltpu

F32 = jnp.float32
BF16 = jnp.bfloat16
I32 = jnp.int32
HIGHEST = lax.Precision.HIGHEST

D = 2048
EPS = 1e-6
GRID_W = 64
SSD_D_INNER = 1024
SSD_HEADS = 16
SSD_GROUPS = 2
SSD_STATE = 128
SSD_XBC = 1536
SSD_CONV_W = 5
HG_WIDTH = 1024
HG_HEADS = 8
HG_HEAD_DIM = 128
DA_HEADS = 8
DA_HEAD_DIM = 64
DA_WIDTH = 1024
ROPE_BASE = 10000.0
ROPE_PAIRS = 16
N_BRANCH = 3
N_EXPERTS = 16
EXPERT_FF = 2048
EC_CAPACITY = 2

OFF_SSD_Z = 0
OFF_SSD_XBC = 1024
OFF_SSD_DT = 2560
OFF_HG_Q = 2592
OFF_HG_F = OFF_HG_Q + 1024
OFF_HG_I = OFF_HG_F + 2048
OFF_HG_G = OFF_HG_I + 1024
OFF_DA_Q = OFF_HG_G + 1024
OFF_DA_K = OFF_DA_Q + 1024
OFF_DA_V = OFF_DA_K + 1024
N_IN = OFF_DA_V + 1024

LANES = 128
PA_Z, PA_HQ, PA_HI, PA_HG, PA_DV, PA_XBC = 0, 1024, 2048, 3072, 4096, 5120
PA_N = 6656
PF_F, PF_DT = 0, 2048
PF_N = 2176
PQK_N = 2048

SCAN_Q = 256
ATTN_STRIP = 32
HALO = 16
TOK_BLK = 128
COMBINE_TILE = 512
SLOT_ALIGN = 16
VMEM_LIMIT = 56 * 1024 * 1024


def _cparams(sem, vmem=VMEM_LIMIT):
    return pltpu.CompilerParams(dimension_semantics=sem, vmem_limit_bytes=vmem)


def _silu(x):
    return x * jax.nn.sigmoid(x)


def _adaln_kernel(cb_ref, w_ref, b_ref, o_ref, *, n_rows, tn):
    nct = tn // LANES

    def body(k, accs):
        r0 = pl.multiple_of(k * 8, 8)
        ws = [w_ref[pl.ds(r0, 8), c * LANES:(c + 1) * LANES] for c in range(nct)]
        out = []
        for r in range(n_rows):
            s = _silu(cb_ref[r, pl.ds(r0, 8), :])
            out.append(tuple(accs[r][c] + s * ws[c] for c in range(nct)))
        return tuple(out)

    zero = jnp.zeros((8, LANES), F32)
    accs = lax.fori_loop(0, D // 8, body, tuple(tuple(zero for _ in range(nct)) for _ in range(n_rows)))
    rows = []
    for r in range(n_rows):
        rows.append(jnp.concatenate([jnp.sum(accs[r][c], axis=0, keepdims=True) for c in range(nct)], axis=1))
    rows.append(jnp.zeros((8 - n_rows, tn), F32))
    o_ref[...] = jnp.concatenate(rows, axis=0) + b_ref[...]


def _adaln(cvec, ada_w, ada_b):
    n_rows = cvec.shape[0]
    depth, _, n6 = ada_w.shape
    tn = 512
    cb = jnp.broadcast_to(cvec[:, :, None], (n_rows, D, LANES))
    return pl.pallas_call(
        functools.partial(_adaln_kernel, n_rows=n_rows, tn=tn),
        grid=(depth, n6 // tn),
        in_specs=[pl.BlockSpec((n_rows, D, LANES), lambda l, j: (0, 0, 0)),
                  pl.BlockSpec((None, D, tn), lambda l, j: (l, 0, j)),
                  pl.BlockSpec((None, 1, tn), lambda l, j: (l, 0, j))],
        out_specs=pl.BlockSpec((None, 8, tn), lambda l, j: (l, 0, j)),
        out_shape=jax.ShapeDtypeStruct((depth, 8, n6), F32),
        compiler_params=_cparams(("parallel", "parallel")),
        name="adaln",
    )(cb, ada_w, ada_b.reshape(depth, 1, n6))


def _modulate_kernel(x_ref, nw_ref, sh_ref, sc_ref, *rest, with_router):
    x = x_ref[...]
    ms = jnp.mean(x * x, axis=-1, keepdims=True)
    h = (x * lax.rsqrt(ms + EPS) * nw_ref[...]) * (1.0 + sc_ref[...]) + sh_ref[...]
    if with_router:
        wr_ref, h_ref, aff_ref = rest
        logits = jnp.dot(h, wr_ref[...], preferred_element_type=F32, precision=HIGHEST)
        m = jnp.max(logits, axis=-1, keepdims=True)
        e = jnp.exp(logits - m)
        aff_ref[...] = e / jnp.sum(e, axis=-1, keepdims=True)
    else:
        (h_ref,) = rest
    h_ref[...] = h.astype(BF16)


def _modulate(x, nw, shift, scale, w_router=None):
    bsz, n, _ = x.shape
    tm = min(512, n)
    with_router = w_router is not None
    in_specs = [pl.BlockSpec((None, tm, D), lambda b, i: (b, i, 0)),
                pl.BlockSpec((1, D), lambda b, i: (0, 0)),
                pl.BlockSpec((None, 1, D), lambda b, i: (b, 0, 0)),
                pl.BlockSpec((None, 1, D), lambda b, i: (b, 0, 0))]
    args = [x, nw.reshape(1, D), shift, scale]
    out_specs = [pl.BlockSpec((None, tm, D), lambda b, i: (b, i, 0))]
    out_shape = [jax.ShapeDtypeStruct((bsz, n, D), BF16)]
    if with_router:
        in_specs.append(pl.BlockSpec((D, N_EXPERTS), lambda b, i: (0, 0)))
        args.append(w_router)
        out_specs.append(pl.BlockSpec((None, tm, N_EXPERTS), lambda b, i: (b, i, 0)))
        out_shape.append(jax.ShapeDtypeStruct((bsz, n, N_EXPERTS), F32))
    res = pl.pallas_call(
        functools.partial(_modulate_kernel, with_router=with_router),
        grid=(bsz, n // tm),
        in_specs=in_specs, out_specs=out_specs, out_shape=out_shape,
        compiler_params=_cparams(("parallel", "parallel")),
        name="modulate_router" if with_router else "modulate",
    )(*args)
    return res if with_router else res[0]


def _mm_kernel(a_ref, w_ref, *rest, rope, tn):
    acc = jnp.dot(a_ref[...], w_ref[...], preferred_element_type=F32)
    if rope:
        cos_ref, sin_ref, o_ref = rest
        cos = cos_ref[...]
        sin = sin_ref[...]
        lane = lax.broadcasted_iota(I32, cos.shape, 1)
        first_half = (lane % DA_HEAD_DIM) < (DA_HEAD_DIM // 2)
        for c in range(tn // LANES):
            xs = acc[:, c * LANES:(c + 1) * LANES]
            partner = jnp.where(first_half,
                                pltpu.roll(xs, LANES - DA_HEAD_DIM // 2, 1),
                                pltpu.roll(xs, DA_HEAD_DIM // 2, 1))
            o_ref[:, c * LANES:(c + 1) * LANES] = (xs * cos + partner * sin).astype(o_ref.dtype)
    else:
        (o_ref,) = rest
        o_ref[...] = acc.astype(o_ref.dtype)


def _matmul(a, w, out_dtype, tm, tn, rope_tables=None):
    m, k = a.shape
    n = w.shape[1]
    rope = rope_tables is not None
    in_specs = [pl.BlockSpec((tm, k), lambda i, j: (i, 0)),
                pl.BlockSpec((k, tn), lambda i, j: (0, j))]
    args = [a, w]
    if rope:
        cos, sin = rope_tables
        nt = cos.shape[0] // tm
        in_specs += [pl.BlockSpec((tm, LANES), lambda i, j: (i % nt, 0)),
                     pl.BlockSpec((tm, LANES), lambda i, j: (i % nt, 0))]
        args += [cos, sin]
    return pl.pallas_call(
        functools.partial(_mm_kernel, rope=rope, tn=tn),
        grid=(m // tm, n // tn),
        in_specs=in_specs,
        out_specs=pl.BlockSpec((tm, tn), lambda i, j: (i, j)),
        out_shape=jax.ShapeDtypeStruct((m, n), out_dtype),
        compiler_params=_cparams(("parallel", "parallel")),
        name="proj_rope" if rope else "proj",
    )(*args)


def _conv_kernel(prev_ref, cur_ref, next_ref, w_ref, b_ref, o_ref, *, tc):
    i = pl.program_id(1)
    last = pl.num_programs(1) - 1
    pad = (SSD_CONV_W - 1) // 2
    prev = jnp.where(i > 0, prev_ref[...].astype(F32), 0.0)
    nxt = jnp.where(i < last, next_ref[...].astype(F32), 0.0)
    ext = jnp.concatenate([prev, cur_ref[...].astype(F32), nxt], axis=0)
    w = w_ref[...]
    y = b_ref[...] + w[0:1] * ext[HALO - pad:HALO - pad + tc]
    for k in range(1, SSD_CONV_W):
        y = y + w[k:k + 1] * ext[HALO - pad + k:HALO - pad + k + tc]
    o_ref[...] = _silu(y).astype(o_ref.dtype)


def _ssd_conv(pa, conv_w, conv_b):
    bsz, n, _ = pa.shape
    tc = min(512, n)
    tw = 512
    c0 = PA_XBC // tw
    nbh = n // HALO
    return pl.pallas_call(
        functools.partial(_conv_kernel, tc=tc),
        grid=(bsz, n // tc, SSD_XBC // tw),
        in_specs=[pl.BlockSpec((None, HALO, tw), lambda b, i, j: (b, jnp.maximum(i * (tc // HALO) - 1, 0), c0 + j)),
                  pl.BlockSpec((None, tc, tw), lambda b, i, j: (b, i, c0 + j)),
                  pl.BlockSpec((None, HALO, tw),
                               lambda b, i, j: (b, jnp.minimum((i + 1) * (tc // HALO), nbh - 1), c0 + j)),
                  pl.BlockSpec((SSD_CONV_W, tw), lambda b, i, j: (0, j)),
                  pl.BlockSpec((1, tw), lambda b, i, j: (0, j))],
        out_specs=pl.BlockSpec((None, tc, tw), lambda b, i, j: (b, i, j)),
        out_shape=jax.ShapeDtypeStruct((bsz, n, SSD_XBC), BF16),
        compiler_params=_cparams(("parallel", "parallel", "parallel")),
        name="ssd_conv",
    )(pa, pa, pa, conv_w, conv_b.reshape(1, SSD_XBC))


def _tri(q, rev):
    r = lax.broadcasted_iota(I32, (q, q), 0)
    c = lax.broadcasted_iota(I32, (q, q), 1)
    return (c >= r) if rev else (r >= c)


def _ssd_scan_kernel(*refs, rev, has_init, readout):
    it = iter(refs)
    xact_ref, dt_ref, alog_ref, bias_ref = next(it), next(it), next(it), next(it)
    init_ref = next(it) if has_init else None
    if readout:
        z_ref, yprev_ref, d_ref, nw_ref = next(it), next(it), next(it), next(it)
    y_ref, st_ref = next(it), next(it)
    ybuf_ref = next(it) if readout else None

    q = SCAN_Q
    hd = SSD_D_INNER // SSD_HEADS
    hg = SSD_HEADS // SSD_GROUPS
    gw = hg * hd
    d_off = SSD_HEADS if rev else 0

    @pl.when(pl.program_id(1) == 0)
    def _():
        st_ref[...] = init_ref[...] if has_init else jnp.zeros_like(st_ref)

    tri = _tri(q, rev)
    dtv = jax.nn.softplus(dt_ref[...] + bias_ref[...])
    da = dtv * (-jnp.exp(alog_ref[...]))
    cs = jnp.dot(tri.astype(F32), da, preferred_element_type=F32, precision=HIGHEST)
    cs_t = cs.T
    erow = lax.broadcasted_iota(I32, (LANES, SSD_D_INNER), 0)
    ecol = lax.broadcasted_iota(I32, (LANES, SSD_D_INNER), 1)
    expand = (erow == d_off + ecol // hd).astype(F32)
    csx = jnp.dot(cs, expand, preferred_element_type=F32, precision=HIGHEST)
    dtx = jnp.dot(dtv, expand, preferred_element_type=F32, precision=HIGHEST)
    x = xact_ref[:, :SSD_D_INNER].astype(F32)
    xdt = x * dtx
    xdt_b = xdt.astype(BF16)
    cs_end = csx[0:1] if rev else csx[q - 1:q]
    ecs = jnp.exp(csx)
    xw = (xdt * jnp.exp(cs_end - csx)).astype(BF16)
    ecs_end = jnp.exp(cs_end)
    lane = lax.broadcasted_iota(I32, (q, LANES), 1)
    out_ref = ybuf_ref if readout else y_ref

    for g in range(SSD_GROUPS):
        bg = xact_ref[:, SSD_D_INNER + g * SSD_STATE:SSD_D_INNER + (g + 1) * SSD_STATE]
        cg = xact_ref[:, SSD_D_INNER + (SSD_GROUPS + g) * SSD_STATE:SSD_D_INNER + (SSD_GROUPS + g + 1) * SSD_STATE]
        cb = lax.dot_general(cg, bg, (((1,), (1,)), ((), ())), preferred_element_type=F32)
        st_g = st_ref[:, g * gw:(g + 1) * gw]
        y_inter = jnp.dot(cg, st_g.astype(BF16), preferred_element_type=F32) * ecs[:, g * gw:(g + 1) * gw]
        for jj in range(hg // 2):
            col0 = g * gw + jj * LANES
            xp = xdt_b[:, col0:col0 + LANES]
            acc = y_inter[:, jj * LANES:(jj + 1) * LANES]
            for s in range(2):
                j = d_off + g * hg + 2 * jj + s
                e = cs[:, j:j + 1] - cs_t[j:j + 1, :]
                seg = jnp.exp(jnp.where(tri, e, -jnp.inf))
                m = (cb * seg).astype(BF16)
                xm = jnp.where((lane >= hd) if s else (lane < hd), xp, jnp.zeros_like(xp))
                acc = acc + jnp.dot(m, xm, preferred_element_type=F32)
            out_ref[:, col0:col0 + LANES] = acc
        upd = lax.dot_general(bg, xw[:, g * gw:(g + 1) * gw], (((0,), (0,)), ((), ())),
                              preferred_element_type=F32)
        st_ref[:, g * gw:(g + 1) * gw] = st_g * ecs_end[:, g * gw:(g + 1) * gw] + upd

    if readout:
        z = z_ref[...].astype(F32)
        yy = (yprev_ref[...] + ybuf_ref[...] + d_ref[...] * x) * _silu(z)
        for g in range(SSD_GROUPS):
            seg = yy[:, g * gw:(g + 1) * gw]
            ms = jnp.mean(seg * seg, axis=-1, keepdims=True)
            y_ref[:, g * gw:(g + 1) * gw] = (seg * lax.rsqrt(ms + EPS) * nw_ref[:, g * gw:(g + 1) * gw]).astype(y_ref.dtype)


def _ssd_scan(xact, pf, alog128, bias128, *, rev, init=None, readout=None):
    bsz, n, _ = xact.shape
    nc = n // SCAN_Q
    cidx = (lambda c: nc - 1 - c) if rev else (lambda c: c)
    in_specs = [pl.BlockSpec((None, SCAN_Q, SSD_XBC), lambda b, c: (b, cidx(c), 0)),
                pl.BlockSpec((None, SCAN_Q, LANES), lambda b, c: (b, cidx(c), PF_DT // LANES)),
                pl.BlockSpec((1, LANES), lambda b, c: (0, 0)),
                pl.BlockSpec((1, LANES), lambda b, c: (0, 0))]
    args = [xact, pf, alog128, bias128]
    if init is not None:
        in_specs.append(pl.BlockSpec((None, SSD_STATE, SSD_D_INNER), lambda b, c: (b, 0, 0)))
        args.append(init)
    scratch = []
    if readout is not None:
        pa, yprev, d_full, norm_w = readout
        in_specs += [pl.BlockSpec((None, SCAN_Q, SSD_D_INNER), lambda b, c: (b, cidx(c), PA_Z // SSD_D_INNER)),
                     pl.BlockSpec((None, SCAN_Q, SSD_D_INNER), lambda b, c: (b, cidx(c), 0)),
                     pl.BlockSpec((1, SSD_D_INNER), lambda b, c: (0, 0)),
                     pl.BlockSpec((1, SSD_D_INNER), lambda b, c: (0, 0))]
        args += [pa, yprev, d_full, norm_w]
        scratch = [pltpu.VMEM((SCAN_Q, SSD_D_INNER), F32)]
    y_dtype = BF16 if readout is not None else F32
    y, st = pl.pallas_call(
        functools.partial(_ssd_scan_kernel, rev=rev, has_init=init is not None, readout=readout is not None),
        grid=(bsz, nc),
        in_specs=in_specs,
        out_specs=[pl.BlockSpec((None, SCAN_Q, SSD_D_INNER), lambda b, c: (b, cidx(c), 0)),
                   pl.BlockSpec((None, SSD_STATE, SSD_D_INNER), lambda b, c: (b, 0, 0))],
        out_shape=[jax.ShapeDtypeStruct((bsz, n, SSD_D_INNER), y_dtype),
                   jax.ShapeDtypeStruct((bsz, SSD_STATE, SSD_D_INNER), F32)],
        scratch_shapes=scratch,
        compiler_params=_cparams(("parallel", "arbitrary")),
        name="ssd_scan_bwd" if rev else "ssd_scan_fwd",
    )(*args)
    return y, st


def _ref_rows(bb, s, rev):
    q, w = bb.shape
    rl = s if rev else s - 1
    if 2 * s >= 8:
        b3 = bb.reshape(q // (2 * s), 2 * s, w)
        return jnp.broadcast_to(b3[:, rl:rl + 1, :], b3.shape).reshape(q, w)
    off = lax.broadcasted_iota(I32, (q, w), 0) % (2 * s)
    out = bb
    for ov in range(2 * s):
        if ov != rl:
            out = jnp.where(off == ov, pltpu.roll(bb, (ov - rl) % q, 0), out)
    return out


def _hg_scan_kernel(*refs, rev, has_init, readout):
    it = iter(refs)
    q_ref, f_ref, v_ref, lb_ref = next(it), next(it), next(it), next(it)
    init_ref = next(it) if has_init else None
    if readout:
        g_ref, oprev_ref, nw_ref = next(it), next(it), next(it)
    o_ref, st_ref, att_ref = next(it), next(it), next(it)
    obuf_ref = next(it) if readout else None

    q = SCAN_Q
    hd = HG_HEAD_DIM

    @pl.when(pl.program_id(1) == 0)
    def _():
        st_ref[...] = init_ref[...] if has_init else jnp.zeros_like(st_ref)

    lb = lb_ref[...]
    qq = _silu(q_ref[...].astype(F32))
    f = lb + (1.0 - lb) * jax.nn.sigmoid(f_ref[...])
    kk = 1.0 - f
    lf = jnp.log(f)
    tri = _tri(q, rev)
    bb = jnp.dot(tri.astype(F32), lf, preferred_element_type=F32, precision=HIGHEST)

    row = lax.broadcasted_iota(I32, (q, HG_WIDTH), 0)
    ri = lax.broadcasted_iota(I32, (q, q), 0)
    ci = lax.broadcasted_iota(I32, (q, q), 1)
    xr = ri ^ ci
    att_ref[...] = jnp.zeros_like(att_ref)
    s = q // 2
    while s >= 1:
        upper = (row % (2 * s)) >= s
        q_side = jnp.logical_not(upper) if rev else upper
        delta = bb - _ref_rows(bb, s, rev)
        ex = jnp.exp(jnp.where(q_side, delta, -delta))
        qt = jnp.where(q_side, qq * ex, 0.0).astype(BF16)
        kt = jnp.where(q_side, 0.0, kk * ex).astype(BF16)
        level = (xr >> int(math.log2(s))) == 1
        for h in range(HG_HEADS):
            p = lax.dot_general(qt[:, h * hd:(h + 1) * hd], kt[:, h * hd:(h + 1) * hd],
                                (((1,), (1,)), ((), ())), preferred_element_type=F32)
            att_ref[h] = jnp.where(level, p, att_ref[h])
        s //= 2

    b_end = bb[0:1] if rev else bb[q - 1:q]
    qe = (qq * jnp.exp(bb)).astype(BF16)
    kh = (kk * jnp.exp(b_end - bb)).astype(BF16)
    qk = (qq * kk).astype(BF16)
    e_end = jnp.exp(b_end)
    ones = jnp.ones((hd, hd), BF16)
    out_ref = obuf_ref if readout else o_ref
    for h in range(HG_HEADS):
        sl = slice(h * hd, (h + 1) * hd)
        vh = v_ref[:, sl]
        st_h = st_ref[sl, :]
        o = jnp.dot(att_ref[h].astype(BF16), vh, preferred_element_type=F32)
        o = o + jnp.dot(qk[:, sl], ones, preferred_element_type=F32) * vh.astype(F32)
        o = o + lax.dot_general(qe[:, sl], st_h.astype(BF16), (((1,), (1,)), ((), ())), preferred_element_type=F32)
        out_ref[:, sl] = o
        upd = lax.dot_general(vh, kh[:, sl], (((0,), (0,)), ((), ())), preferred_element_type=F32)
        st_ref[sl, :] = st_h * e_end[:, sl] + upd

    if readout:
        for h in range(HG_HEADS):
            sl = slice(h * hd, (h + 1) * hd)
            o = oprev_ref[:, sl] + obuf_ref[:, sl]
            ms = jnp.mean(o * o, axis=-1, keepdims=True)
            gate = _silu(g_ref[:, sl].astype(F32))
            o_ref[:, sl] = (o * lax.rsqrt(ms + EPS) * nw_ref[...] * gate).astype(o_ref.dtype)


def _hg_scan(pa, pf, lb_row, *, rev, init=None, readout=None):
    bsz, n, _ = pa.shape
    nc = n // SCAN_Q
    cidx = (lambda c: nc - 1 - c) if rev else (lambda c: c)
    fcol = (PF_F // HG_WIDTH) + (1 if rev else 0)
    in_specs = [pl.BlockSpec((None, SCAN_Q, HG_WIDTH), lambda b, c: (b, cidx(c), PA_HQ // HG_WIDTH)),
                pl.BlockSpec((None, SCAN_Q, HG_WIDTH), lambda b, c: (b, cidx(c), fcol)),
                pl.BlockSpec((None, SCAN_Q, HG_WIDTH), lambda b, c: (b, cidx(c), PA_HI // HG_WIDTH)),
                pl.BlockSpec((1, HG_WIDTH), lambda b, c: (0, 0))]
    args = [pa, pf, pa, lb_row]
    if init is not None:
        in_specs.append(pl.BlockSpec((None, HG_WIDTH, HG_HEAD_DIM), lambda b, c: (b, 0, 0)))
        args.append(init)
    scratch = [pltpu.VMEM((HG_HEADS, SCAN_Q, SCAN_Q), F32)]
    if readout is not None:
        oprev, nw128 = readout
        in_specs += [pl.BlockSpec((None, SCAN_Q, HG_WIDTH), lambda b, c: (b, cidx(c), PA_HG // HG_WIDTH)),
                     pl.BlockSpec((None, SCAN_Q, HG_WIDTH), lambda b, c: (b, cidx(c), 0)),
                     pl.BlockSpec((1, HG_HEAD_DIM), lambda b, c: (0, 0))]
        args += [pa, oprev, nw128]
        scratch.append(pltpu.VMEM((SCAN_Q, HG_WIDTH), F32))
    o_dtype = BF16 if readout is not None else F32
    assert PA_HQ % HG_WIDTH == 0 or True
    o, st = pl.pallas_call(
        functools.partial(_hg_scan_kernel, rev=rev, has_init=init is not None, readout=readout is not None),
        grid=(bsz, nc),
        in_specs=in_specs,
        out_specs=[pl.BlockSpec((None, SCAN_Q, HG_WIDTH), lambda b, c: (b, cidx(c), 0)),
                   pl.BlockSpec((None, HG_WIDTH, HG_HEAD_DIM), lambda b, c: (b, 0, 0))],
        out_shape=[jax.ShapeDtypeStruct((bsz, n, HG_WIDTH), o_dtype),
                   jax.ShapeDtypeStruct((bsz, HG_WIDTH, HG_HEAD_DIM), F32)],
        scratch_shapes=scratch,
        compiler_params=_cparams(("parallel", "arbitrary")),
        name="hg_scan_bwd" if rev else "hg_scan_fwd",
    )(*args)
    return o, st


def _attn_kernel(*refs, seg_lens, tk, lam_init):
    lam_ref, nw_ref, q_ref = refs[0], refs[1], refs[2]
    nseg = len(seg_lens)
    kv_refs = refs[3:3 + 2 * nseg]
    o_ref, qs_ref, sa_ref, sb_ref, m_ref, mn_ref, l_ref, acc_ref = refs[3 + 2 * nseg:]
    tq = q_ref.shape[0]
    qv = q_ref[...] * jnp.asarray(DA_HEAD_DIM ** -0.5, BF16)
    lane = lax.broadcasted_iota(I32, qv.shape, 1)
    zero = jnp.zeros_like(qv)
    qs_ref[:tq] = jnp.where(lane < DA_HEAD_DIM, qv, zero)
    qs_ref[tq:] = jnp.where(lane >= DA_HEAD_DIM, qv, zero)
    m_ref[...] = jnp.full_like(m_ref, -jnp.inf)
    l_ref[...] = jnp.zeros_like(l_ref)
    acc_ref[...] = jnp.zeros_like(acc_ref)

    def scores(kc):
        return lax.dot_general(qs_ref[...], kc, (((1,), (1,)), ((), ())), preferred_element_type=F32)

    def lane_tiles(a):
        return [a[:, c * LANES:(c + 1) * LANES] for c in range(a.shape[1] // LANES)]

    def row_max(s, m_prev):
        tiles = lane_tiles(s)
        m = tiles[0]
        for tile in tiles[1:]:
            m = jnp.maximum(m, tile)
        return jnp.maximum(m_prev, jnp.broadcast_to(jnp.max(m, axis=-1, keepdims=True), m_prev.shape))

    def online_step(s, vc, m_prev, m_cur):
        alpha = jnp.exp(m_prev - m_cur)
        ps = [jnp.exp(tile - m_cur) for tile in lane_tiles(s)]
        l = alpha * l_ref[...]
        for p in ps:
            l = l + p
        l_ref[...] = l
        pv = jnp.dot(jnp.concatenate(ps, axis=1).astype(BF16), vc, preferred_element_type=F32)
        acc_ref[...] = alpha * acc_ref[...] + pv

    for si, n in enumerate(seg_lens):
        k_ref, v_ref = kv_refs[2 * si], kv_refs[2 * si + 1]
        t = min(tk, n)
        nc = n // t
        if nc == 1:
            s = scores(k_ref[...])
            m_prev = m_ref[...]
            m_cur = row_max(s, m_prev)
            online_step(s, v_ref[...], m_prev, m_cur)
            m_ref[...] = m_cur
        else:
            assert nc % 2 == 0
            s0 = scores(k_ref[pl.ds(0, t), :])
            sa_ref[...] = s0
            mn_ref[...] = row_max(s0, m_ref[...])

            def pipe_step(c, cur_ref, nxt_ref, k_ref=k_ref, v_ref=v_ref, t=t):
                r0 = pl.multiple_of(c * t, t)
                r1 = pl.multiple_of(r0 + t, t)
                m_prev, m_cur = m_ref[...], mn_ref[...]
                s_next = scores(k_ref[pl.ds(r1, t), :])
                nxt_ref[...] = s_next
                mn_ref[...] = row_max(s_next, m_cur)
                m_ref[...] = m_cur
                online_step(cur_ref[...], v_ref[pl.ds(r0, t), :], m_prev, m_cur)

            def body(i, carry):
                pipe_step(2 * i, sa_ref, sb_ref)
                pipe_step(2 * i + 1, sb_ref, sa_ref)
                return carry

            lax.fori_loop(0, (nc - 2) // 2, body, 0)
            pipe_step(nc - 2, sa_ref, sb_ref)
            m_prev, m_cur = m_ref[...], mn_ref[...]
            online_step(sb_ref[...], v_ref[pl.ds((nc - 1) * t, t), :], m_prev, m_cur)
            m_ref[...] = m_cur
    l = jnp.sum(l_ref[...], axis=-1, keepdims=True)
    acc = acc_ref[...]
    lam = lam_ref[...]
    lmbda = (jnp.exp(jnp.sum(lam[0:1] * lam[1:2], axis=-1, keepdims=True))
             - jnp.exp(jnp.sum(lam[2:3] * lam[3:4], axis=-1, keepdims=True)) + lam_init)
    o = acc[:tq] / l[:tq] - lmbda * (acc[tq:] / l[tq:])
    ms = jnp.mean(o * o, axis=-1, keepdims=True)
    o_ref[...] = (o * lax.rsqrt(ms + EPS) * nw_ref[...] * (1.0 - lam_init)).astype(o_ref.dtype)


def _diff_attn(pqk_q, kv_segs, lam, nw, lam_init, tq, tk):
    bsz, nq, _ = pqk_q.shape
    hw = 2 * DA_HEAD_DIM
    kcol0 = DA_WIDTH // hw
    vcol0 = PA_DV // hw
    in_specs = [pl.BlockSpec((4, DA_HEAD_DIM), lambda b, h, i: (0, 0)),
                pl.BlockSpec((1, hw), lambda b, h, i: (0, 0)),
                pl.BlockSpec((None, tq, hw), lambda b, h, i: (b, i, h))]
    args = [lam, nw.reshape(1, hw), pqk_q]
    seg_lens = []
    for pqk, pa in kv_segs:
        nk = pqk.shape[1]
        seg_lens.append(nk)
        in_specs += [pl.BlockSpec((None, nk, hw), lambda b, h, i: (b, 0, kcol0 + h)),
                     pl.BlockSpec((None, nk, hw), lambda b, h, i: (b, 0, vcol0 + h))]
        args += [pqk, pa]
    return pl.pallas_call(
        functools.partial(_attn_kernel, seg_lens=tuple(seg_lens), tk=tk, lam_init=lam_init),
        grid=(bsz, DA_HEADS, nq // tq),
        in_specs=in_specs,
        out_specs=pl.BlockSpec((None, tq, hw), lambda b, h, i: (b, i, h)),
        out_shape=jax.ShapeDtypeStruct((bsz, nq, DA_WIDTH), BF16),
        scratch_shapes=[pltpu.VMEM((2 * tq, hw), BF16),
                        pltpu.VMEM((2 * tq, tk), F32),
                        pltpu.VMEM((2 * tq, tk), F32),
                        pltpu.VMEM((2 * tq, LANES), F32),
                        pltpu.VMEM((2 * tq, LANES), F32),
                        pltpu.VMEM((2 * tq, LANES), F32),
                        pltpu.VMEM((2 * tq, hw), F32)],
        compiler_params=_cparams(("parallel", "parallel", "parallel")),
        name="diff_attn",
    )(*args)


def _merge_kernel(x_ref, g1_ref, h_ref, ya_ref, yb_ref, yd_ref, wg0, wg1, wg2, bg0, bg1, bg2,
                  wu0, wu1, wu2, wo_ref, o_ref, acc_ref):
    n = pl.program_id(2)

    @pl.when(n == 0)
    def _():
        acc_ref[...] = jnp.zeros_like(acc_ref)

    h = h_ref[...]
    mix = None
    for y_ref, wg, bg, wu in ((ya_ref, wg0, bg0, wu0), (yb_ref, wg1, bg1, wu1), (yd_ref, wg2, bg2, wu2)):
        gate = jax.nn.sigmoid(jnp.dot(h, wg[...], preferred_element_type=F32) + bg[...])
        up = jnp.dot(y_ref[...], wu[...], preferred_element_type=F32)
        mix = gate * up if mix is None else mix + gate * up
    acc_ref[...] += jnp.dot(mix.astype(BF16), wo_ref[...], preferred_element_type=F32)

    @pl.when(n == pl.num_programs(2) - 1)
    def _():
        o_ref[...] = x_ref[...] + g1_ref[...] * acc_ref[...]


def _merge(x, gate1, h, ya, yb, yd, w_gate, b_gate, w_up, w_out):
    bsz, n, _ = x.shape
    tm = min(512, n)
    tn = 256
    nn = D // tn
    row = lambda b, i, j: (b, i, 0)
    in_specs = [pl.BlockSpec((None, tm, D), row),
                pl.BlockSpec((None, 1, D), lambda b, i, j: (b, 0, 0)),
                pl.BlockSpec((None, tm, D), row),
                pl.BlockSpec((None, tm, 1024), row),
                pl.BlockSpec((None, tm, 1024), row),
                pl.BlockSpec((None, tm, 1024), row)]
    in_specs += [pl.BlockSpec((D, tn), functools.partial(lambda b, i, j, k: (0, k * nn + j), k=k)) for k in range(3)]
    in_specs += [pl.BlockSpec((1, tn), functools.partial(lambda b, i, j, k: (0, k * nn + j), k=k)) for k in range(3)]
    in_specs += [pl.BlockSpec((None, 1024, tn), functools.partial(lambda b, i, j, k: (k, 0, j), k=k)) for k in range(3)]
    in_specs += [pl.BlockSpec((tn, D), lambda b, i, j: (j, 0))]
    return pl.pallas_call(
        _merge_kernel,
        grid=(bsz, n // tm, nn),
        in_specs=in_specs,
        out_specs=pl.BlockSpec((None, tm, D), row),
        out_shape=jax.ShapeDtypeStruct((bsz, n, D), F32),
        scratch_shapes=[pltpu.VMEM((tm, D), F32)],
        compiler_params=_cparams(("parallel", "parallel", "arbitrary")),
        name="merge",
    )(x, gate1, h, ya, yb, yd, w_gate, w_gate, w_gate, b_gate, b_gate, b_gate, w_up, w_up, w_up, w_out)


def _route_kernel(aff_ref, posm_ref, pos_ref, *, cap, rpe):
    rows = N_EXPERTS * rpe
    bits = pltpu.bitcast(aff_ref[...], I32)
    ones = jnp.ones((LANES, LANES), BF16)
    r = lax.broadcasted_iota(I32, (rows, rows), 0)
    c = lax.broadcasted_iota(I32, (rows, rows), 1)
    same = (r // rpe) == (c // rpe)
    grp = same.astype(BF16)
    grp_before = jnp.logical_and(same, c < r).astype(BF16)
    lr = lax.broadcasted_iota(I32, (LANES, LANES), 0)
    lc = lax.broadcasted_iota(I32, (LANES, LANES), 1)
    before = (lr < lc).astype(BF16)

    def count(mask):
        per_row = jnp.dot(mask.astype(BF16), ones, preferred_element_type=F32)
        return jnp.dot(grp, per_row.astype(BF16), preferred_element_type=F32)

    def excl_prefix(mask):
        mb = mask.astype(BF16)
        within = jnp.dot(mb, before, preferred_element_type=F32)
        per_row = jnp.dot(mb, ones, preferred_element_type=F32)
        return within + jnp.dot(grp_before, per_row.astype(BF16), preferred_element_type=F32)

    def body(i, thr):
        cand = thr | (jnp.int32(1) << (30 - i))
        return jnp.where(count(bits >= cand) >= cap, cand, thr)

    thr = lax.fori_loop(0, 31, body, jnp.zeros((rows, LANES), I32))
    gt = bits > thr
    eq = bits == thr
    need = cap - count(gt)
    sel = jnp.logical_or(gt, jnp.logical_and(eq, excl_prefix(eq) < need))
    pos = excl_prefix(sel).astype(I32)
    pos_ref[...] = pos
    posm_ref[...] = jnp.where(sel, pos, -1)


def _route(aff, cap):
    bsz, n, _ = aff.shape
    rpe = n // LANES
    rows = N_EXPERTS * rpe
    aff_t = jnp.swapaxes(aff, 1, 2).reshape(bsz, rows, LANES)
    spec = pl.BlockSpec((None, rows, LANES), lambda b: (b, 0, 0))
    posm, pos = pl.pallas_call(
        functools.partial(_route_kernel, cap=cap, rpe=rpe),
        grid=(bsz,),
        in_specs=[spec], out_specs=[spec, spec],
        out_shape=[jax.ShapeDtypeStruct((bsz, rows, LANES), I32)] * 2,
        compiler_params=_cparams(("parallel",)),
        name="route",
    )(aff_t)
    return posm.reshape(bsz, N_EXPERTS, n), pos.reshape(bsz, N_EXPERTS, n)


def _gather_kernel(base_ref, h_ref, posm_ref, xg_ref, *, win, nblk):
    b, e = pl.program_id(0), pl.program_id(2)
    xg_ref[...] = jnp.zeros_like(xg_ref)

    def body(j, carry):
        base = pl.multiple_of(base_ref[(b * N_EXPERTS + e) * nblk + j], SLOT_ALIGN)
        t0 = pl.multiple_of(j * TOK_BLK, TOK_BLK)
        rel = posm_ref[:, pl.ds(t0, TOK_BLK)] - base
        slot = lax.broadcasted_iota(I32, (win, TOK_BLK), 0)
        onehot = jnp.where(slot == rel, 1.0, 0.0).astype(BF16)
        rows = jnp.dot(onehot, h_ref[pl.ds(t0, TOK_BLK), :], preferred_element_type=F32).astype(BF16)
        xg_ref[pl.ds(base, win), :] += rows
        return carry

    lax.fori_loop(0, nblk, body, 0)


def _expert_kernel(xg_ref, w1_ref, w3_ref, w2_ref, y_ref, acc_ref):
    f = pl.program_id(2)

    @pl.when(f == 0)
    def _():
        acc_ref[...] = jnp.zeros_like(acc_ref)

    xg = xg_ref[...]
    a = jnp.dot(xg, w1_ref[...].astype(BF16), preferred_element_type=F32)
    g = jnp.dot(xg, w3_ref[...].astype(BF16), preferred_element_type=F32)
    acc_ref[...] += jnp.dot((_silu(a) * g).astype(BF16), w2_ref[...].astype(BF16), preferred_element_type=F32)

    @pl.when(f == pl.num_programs(2) - 1)
    def _():
        y_ref[...] = acc_ref[...].astype(y_ref.dtype)


def _combine_kernel(base_ref, x_ref, g2_ref, aff_ref, posm_ref, y_ref, *rest, win, win_tile, cap, nblk, nsub, final):
    if final:
        fw_ref, o_ref, acc_ref = rest
    else:
        o_ref, acc_ref = rest
    b, jt, e = pl.program_id(0), pl.program_id(1), pl.program_id(2)

    @pl.when(e == 0)
    def _():
        acc_ref[...] = jnp.zeros_like(acc_ref)

    first = (b * N_EXPERTS + e) * nblk + jt * nsub
    tile_base = jnp.minimum(base_ref[first], cap - win_tile)
    lane = lax.broadcasted_iota(I32, (TOK_BLK, N_EXPERTS), 1)
    mine = lane == e
    slot = lax.broadcasted_iota(I32, (TOK_BLK, win), 1)
    for sb in range(nsub):
        rows = slice(sb * TOK_BLK, (sb + 1) * TOK_BLK)
        base = base_ref[first + sb]
        rel = jnp.sum(jnp.where(mine, posm_ref[rows, :], 0), axis=-1, keepdims=True) - base
        val = jnp.sum(jnp.where(mine, aff_ref[rows, :], 0.0), axis=-1, keepdims=True)
        onehot = jnp.where(slot == rel, 1.0, 0.0).astype(BF16)
        off = pl.multiple_of(base - tile_base, SLOT_ALIGN)
        acc_ref[rows, :] += val * jnp.dot(onehot, y_ref[pl.ds(off, win), :], preferred_element_type=F32)

    @pl.when(e == pl.num_programs(2) - 1)
    def _():
        out = x_ref[...] + g2_ref[...] * acc_ref[...]
        if final:
            ms = jnp.mean(out * out, axis=-1, keepdims=True)
            out = out * lax.rsqrt(ms + EPS) * fw_ref[...]
        o_ref[...] = out


def _ec_moe(x, gate2, h2, aff, w1, w3, w2, layer, final_w=None):
    bsz, n, _ = x.shape
    cap = EC_CAPACITY * n // N_EXPERTS
    nblk = n // TOK_BLK
    win = min(TOK_BLK + SLOT_ALIGN, cap)
    tile = min(COMBINE_TILE, n)
    nsub = tile // TOK_BLK
    win_tile = min(tile + SLOT_ALIGN, cap)
    posm, pos = _route(aff, cap)
    base = jnp.minimum((pos[:, :, ::TOK_BLK] // SLOT_ALIGN) * SLOT_ALIGN, cap - win).astype(I32).reshape(-1)

    dh = D // 2
    xg = pl.pallas_call(
        functools.partial(_gather_kernel, win=win, nblk=nblk),
        grid_spec=pltpu.PrefetchScalarGridSpec(
            num_scalar_prefetch=1,
            grid=(bsz, D // dh, N_EXPERTS),
            in_specs=[pl.BlockSpec((None, n, dh), lambda b, c, e, base: (b, 0, c)),
                      pl.BlockSpec((None, None, 1, n), lambda b, c, e, base: (b, e, 0, 0))],
            out_specs=pl.BlockSpec((None, None, cap, dh), lambda b, c, e, base: (b, e, 0, c))),
        out_shape=jax.ShapeDtypeStruct((bsz, N_EXPERTS, cap, D), BF16),
        compiler_params=_cparams(("parallel", "parallel", "arbitrary")),
        name="moe_gather",
    )(base, h2, posm.reshape(bsz, N_EXPERTS, 1, n))

    tf = 256
    y = pl.pallas_call(
        _expert_kernel,
        grid=(N_EXPERTS, bsz, EXPERT_FF // tf),
        in_specs=[pl.BlockSpec((None, None, cap, D), lambda e, b, f: (b, e, 0, 0)),
                  pl.BlockSpec((None, None, D, tf), lambda e, b, f: (layer, e, 0, f)),
                  pl.BlockSpec((None, None, D, tf), lambda e, b, f: (layer, e, 0, f)),
                  pl.BlockSpec((None, None, tf, D), lambda e, b, f: (layer, e, f, 0))],
        out_specs=pl.BlockSpec((None, None, cap, D), lambda e, b, f: (b, e, 0, 0)),
        out_shape=jax.ShapeDtypeStruct((bsz, N_EXPERTS, cap, D), BF16),
        scratch_shapes=[pltpu.VMEM((cap, D), F32)],
        compiler_params=_cparams(("parallel", "parallel", "arbitrary")),
        name="moe_expert",
    )(xg, w1, w3, w2)

    final = final_w is not None
    def y_window(b, j, e, base):
        be = b * N_EXPERTS + e
        start = jnp.minimum(base[be * nblk + j * nsub], cap - win_tile)
        return pl.multiple_of(be * cap + start, SLOT_ALIGN), 0

    in_specs = [pl.BlockSpec((None, tile, D), lambda b, j, e, base: (b, j, 0)),
                pl.BlockSpec((None, 1, D), lambda b, j, e, base: (b, 0, 0)),
                pl.BlockSpec((None, tile, N_EXPERTS), lambda b, j, e, base: (b, j, 0)),
                pl.BlockSpec((None, tile, N_EXPERTS), lambda b, j, e, base: (b, j, 0)),
                pl.BlockSpec((pl.Element(win_tile), pl.Element(D)), y_window)]
    args = [base, x, gate2, aff, jnp.swapaxes(posm, 1, 2), y.reshape(bsz * N_EXPERTS * cap, D)]
    if final:
        in_specs.append(pl.BlockSpec((1, D), lambda b, j, e, base: (0, 0)))
        args.append(final_w.reshape(1, D))
    return pl.pallas_call(
        functools.partial(_combine_kernel, win=win, win_tile=win_tile, cap=cap, nblk=nblk, nsub=nsub, final=final),
        grid_spec=pltpu.PrefetchScalarGridSpec(
            num_scalar_prefetch=1,
            grid=(bsz, n // tile, N_EXPERTS),
            in_specs=in_specs,
            out_specs=pl.BlockSpec((None, tile, D), lambda b, j, e, base: (b, j, 0)),
            scratch_shapes=[pltpu.VMEM((tile, D), F32)]),
        out_shape=jax.ShapeDtypeStruct((bsz, n, D), F32),
        compiler_params=_cparams(("parallel", "parallel", "arbitrary")),
        name="moe_combine",
    )(*args)


def _rope_tables(n_tokens):
    rows = n_tokens // GRID_W
    row = jnp.repeat(jnp.arange(rows, dtype=F32), GRID_W)
    col = (jnp.arange(rows * GRID_W, dtype=I32) % GRID_W).astype(F32)
    inv = ROPE_BASE ** (-jnp.arange(ROPE_PAIRS, dtype=F32) / ROPE_PAIRS)
    ang = jnp.concatenate([row[:, None] * inv, col[:, None] * inv], axis=-1)
    cos, sin = jnp.cos(ang), jnp.sin(ang)
    reps = LANES // DA_HEAD_DIM
    cos_t = jnp.tile(jnp.concatenate([cos, cos], axis=-1), (1, reps))
    sin_t = jnp.tile(jnp.concatenate([-sin, sin], axis=-1), (1, reps))
    return cos_t, sin_t


def _lower_bounds(p):
    cum = jnp.cumsum(jax.nn.softmax(p.astype(F32), axis=0), axis=0)
    return cum - cum[0]


def _pad128(v):
    return jnp.pad(v.reshape(1, -1).astype(F32), ((0, 0), (0, LANES - v.size)))


def kernel(x, c, ctx, c_ctx, ada_w, ada_b, norm1_w, norm2_w, w_in, ssd_conv_w, ssd_conv_b, ssd_dt_bias,
           ssd_a_log, ssd_d, ssd_norm_w, hg_lb, hg_norm_w, da_lambda, da_norm_w, w_up, w_gate, b_gate,
           w_out, moe_router, moe_w1, moe_w3, moe_w2, final_norm_w):
    bsz, n_lat, _ = x.shape
    n_ctx = ctx.shape[1]
    depth = ada_w.shape[0]
    rope = _rope_tables(n_lat)
    lb_all = jnp.stack([_lower_bounds(hg_lb[0]), _lower_bounds(hg_lb[1])], axis=1)

    mod = _adaln(jnp.concatenate([c, c_ctx[None]], axis=0), ada_w, ada_b)
    mod = mod.reshape(depth, 8, 6, 1, D)

    xl, xc = x, ctx
    for l in range(depth):
        need_ctx = l < depth - 1
        lam_init = 0.8 - 0.6 * math.exp(-0.3 * l)
        mod_l = [mod[l, :bsz, k] for k in range(6)]
        mod_c = [jnp.broadcast_to(mod[l, bsz:bsz + 1, k], (bsz, 1, D)) for k in range(6)]

        wl = w_in[l]
        w_a = jnp.concatenate([wl[:, OFF_SSD_Z:OFF_SSD_XBC], wl[:, OFF_HG_Q:OFF_HG_F], wl[:, OFF_HG_I:OFF_DA_Q],
                               wl[:, OFF_DA_V:N_IN], wl[:, OFF_SSD_XBC:OFF_SSD_DT]], axis=1).astype(BF16)
        w_f = jnp.concatenate([wl[:, OFF_HG_F:OFF_HG_I], wl[:, OFF_SSD_DT:OFF_HG_Q],
                               jnp.zeros((D, PF_N - PF_DT - 2 * SSD_HEADS), F32)], axis=1).astype(BF16)
        w_qk = wl[:, OFF_DA_Q:OFF_DA_V].astype(BF16)
        alog128 = _pad128(ssd_a_log[l])
        bias128 = _pad128(ssd_dt_bias[l])
        d_full = jnp.repeat(ssd_d[l].astype(F32), SSD_D_INNER // SSD_HEADS).reshape(1, SSD_D_INNER)
        ssd_nw = ssd_norm_w[l].reshape(1, SSD_D_INNER).astype(F32)
        hg_nw = hg_norm_w[l].reshape(1, HG_HEAD_DIM).astype(F32)
        wg_b = w_gate[l].astype(BF16)
        bg = b_gate[l].reshape(1, N_BRANCH * D)
        wu_b = w_up[l].astype(BF16)
        wo_b = w_out[l].astype(BF16)

        def project(xs, m, use_rope):
            n = xs.shape[1]
            h = _modulate(xs, norm1_w[l], m[0], m[1])
            h2d = h.reshape(bsz * n, D)
            tm = min(1024, n)
            pa = _matmul(h2d, w_a, BF16, tm, 512).reshape(bsz, n, PA_N)
            pf = _matmul(h2d, w_f, F32, min(512, n), PF_N).reshape(bsz, n, PF_N)
            pqk = _matmul(h2d, w_qk, BF16, tm, 512, rope_tables=rope if use_rope else None).reshape(bsz, n, PQK_N)
            return h, pa, pf, pqk

        hl, pa_l, pf_l, pqk_l = project(xl, mod_l, True)
        hc, pa_c, pf_c, pqk_c = project(xc, mod_c, False)

        xa_c = _ssd_conv(pa_c, ssd_conv_w[l], ssd_conv_b[l])
        xa_l = _ssd_conv(pa_l, ssd_conv_w[l], ssd_conv_b[l])
        yc_f, s_f = _ssd_scan(xa_c, pf_c, alog128, bias128, rev=False)
        yl_f, _ = _ssd_scan(xa_l, pf_l, alog128, bias128, rev=False, init=s_f)
        ya_c, s_b = _ssd_scan(xa_c, pf_c, alog128, bias128, rev=True, readout=(pa_c, yc_f, d_full, ssd_nw))
        ya_l, _ = _ssd_scan(xa_l, pf_l, alog128, bias128, rev=True, init=s_b, readout=(pa_l, yl_f, d_full, ssd_nw))

        lb_f, lb_b = lb_all[l, 0].reshape(1, HG_WIDTH), lb_all[l, 1].reshape(1, HG_WIDTH)
        oc_f, t_f = _hg_scan(pa_c, pf_c, lb_f, rev=False)
        ol_f, _ = _hg_scan(pa_l, pf_l, lb_f, rev=False, init=t_f)
        yb_c, t_b = _hg_scan(pa_c, pf_c, lb_b, rev=True, readout=(oc_f, hg_nw))
        yb_l, _ = _hg_scan(pa_l, pf_l, lb_b, rev=True, init=t_b, readout=(ol_f, hg_nw))

        yd_l = _diff_attn(pqk_l, [(pqk_c, pa_c), (pqk_l, pa_l)], da_lambda[l], da_norm_w[l], lam_init, 512, 1024)

        xl = _merge(xl, mod_l[2], hl, ya_l, yb_l, yd_l, wg_b, bg, wu_b, wo_b)
        h2l, aff_l = _modulate(xl, norm2_w[l], mod_l[3], mod_l[4], w_router=moe_router[l])
        xl = _ec_moe(xl, mod_l[5], h2l, aff_l, moe_w1, moe_w3, moe_w2, l,
                     final_w=final_norm_w if l == depth - 1 else None)
        if need_ctx:
            yd_c = _diff_attn(pqk_c, [(pqk_c, pa_c)], da_lambda[l], da_norm_w[l], lam_init, n_ctx, 512)
            xc = _merge(xc, mod_c[2], hc, ya_c, yb_c, yd_c, wg_b, bg, wu_b, wo_b)
            h2c, aff_c = _modulate(xc, norm2_w[l], mod_c[3], mod_c[4], w_router=moe_router[l])
            xc = _ec_moe(xc, mod_c[5], h2c, aff_c, moe_w1, moe_w3, moe_w2, l)
    return xl
```

```python
import functools
import math

import jax
import jax.numpy as jnp
from jax import lax
from jax.experimental import pallas as pl
from jax.experimental.pallas import tpu as pltpu

F32 = jnp.float32
BF16 = jnp.bfloat16
I32 = jnp.int32
HIGHEST = lax.Precision.HIGHEST

D = 2048
EPS = 1e-6
GRID_W = 64
SSD_D_INNER = 1024
SSD_HEADS = 16
SSD_GROUPS = 2
SSD_STATE = 128
SSD_XBC = 1536
SSD_CONV_W = 5
HG_WIDTH = 1024
HG_HEADS = 8
HG_HEAD_DIM = 128
DA_HEADS = 8
DA_HEAD_DIM = 64
DA_WIDTH = 1024
ROPE_BASE = 10000.0
ROPE_PAIRS = 16
N_BRANCH = 3
N_EXPERTS = 16
EXPERT_FF = 2048
EC_CAPACITY = 2

OFF_SSD_Z = 0
OFF_SSD_XBC = 1024
OFF_SSD_DT = 2560
OFF_HG_Q = 2592
OFF_HG_F = OFF_HG_Q + 1024
OFF_HG_I = OFF_HG_F + 2048
OFF_HG_G = OFF_HG_I + 1024
OFF_DA_Q = OFF_HG_G + 1024
OFF_DA_K = OFF_DA_Q + 1024
OFF_DA_V = OFF_DA_K + 1024
N_IN = OFF_DA_V + 1024

LANES = 128
PA_Z, PA_HQ, PA_HI, PA_HG, PA_DV, PA_XBC = 0, 1024, 2048, 3072, 4096, 5120
PA_N = 6656
PF_F, PF_DT = 0, 2048
PF_N = 2176
PQK_N = 2048

SCAN_Q = 256
ATTN_STRIP = 32
HALO = 16
TOK_BLK = 128
COMBINE_TILE = 512
GATHER_COLS = 512
SLOT_ALIGN = 16
VMEM_LIMIT = 56 * 1024 * 1024


def _cparams(sem, vmem=VMEM_LIMIT):
    return pltpu.CompilerParams(dimension_semantics=sem, vmem_limit_bytes=vmem)


def _silu(x):
    return x * jax.nn.sigmoid(x)


def _adaln_kernel(cb_ref, w_ref, b_ref, o_ref, *, n_rows, tn):
    nct = tn // LANES

    def body(k, accs):
        r0 = pl.multiple_of(k * 8, 8)
        ws = [w_ref[pl.ds(r0, 8), c * LANES:(c + 1) * LANES] for c in range(nct)]
        out = []
        for r in range(n_rows):
            s = _silu(cb_ref[r, pl.ds(r0, 8), :])
            out.append(tuple(accs[r][c] + s * ws[c] for c in range(nct)))
        return tuple(out)

    zero = jnp.zeros((8, LANES), F32)
    accs = lax.fori_loop(0, D // 8, body, tuple(tuple(zero for _ in range(nct)) for _ in range(n_rows)),
                         unroll=8)
    rows = []
    for r in range(n_rows):
        rows.append(jnp.concatenate([jnp.sum(accs[r][c], axis=0, keepdims=True) for c in range(nct)], axis=1))
    rows.append(jnp.zeros((8 - n_rows, tn), F32))
    o_ref[...] = jnp.concatenate(rows, axis=0) + b_ref[...]


def _adaln(cvec, ada_w, ada_b):
    n_rows = cvec.shape[0]
    depth, _, n6 = ada_w.shape
    tn = 512
    cb = jnp.broadcast_to(cvec[:, :, None], (n_rows, D, LANES))
    return pl.pallas_call(
        functools.partial(_adaln_kernel, n_rows=n_rows, tn=tn),
        grid=(depth, n6 // tn),
        in_specs=[pl.BlockSpec((n_rows, D, LANES), lambda l, j: (0, 0, 0)),
                  pl.BlockSpec((None, D, tn), lambda l, j: (l, 0, j)),
                  pl.BlockSpec((None, 1, tn), lambda l, j: (l, 0, j))],
        out_specs=pl.BlockSpec((None, 8, tn), lambda l, j: (l, 0, j)),
        out_shape=jax.ShapeDtypeStruct((depth, 8, n6), F32),
        compiler_params=_cparams(("parallel", "parallel")),
        name="adaln",
    )(cb, ada_w, ada_b.reshape(depth, 1, n6))


def _modulate_kernel(x_ref, nw_ref, sh_ref, sc_ref, *rest, with_router):
    x = x_ref[...]
    ms = jnp.mean(x * x, axis=-1, keepdims=True)
    h = (x * lax.rsqrt(ms + EPS) * nw_ref[...]) * (1.0 + sc_ref[...]) + sh_ref[...]
    if with_router:
        wr_ref, h_ref, aff_ref = rest
        logits = jnp.dot(h, wr_ref[...], preferred_element_type=F32, precision=HIGHEST)
        m = jnp.max(logits, axis=-1, keepdims=True)
        e = jnp.exp(logits - m)
        aff_ref[...] = e / jnp.sum(e, axis=-1, keepdims=True)
    else:
        (h_ref,) = rest
    h_ref[...] = h.astype(BF16)


def _modulate(x, nw, shift, scale, w_router=None):
    bsz, n, _ = x.shape
    tm = min(512, n)
    with_router = w_router is not None
    in_specs = [pl.BlockSpec((None, tm, D), lambda b, i: (b, i, 0)),
                pl.BlockSpec((1, D), lambda b, i: (0, 0)),
                pl.BlockSpec((None, 1, D), lambda b, i: (b, 0, 0)),
                pl.BlockSpec((None, 1, D), lambda b, i: (b, 0, 0))]
    args = [x, nw.reshape(1, D), shift, scale]
    out_specs = [pl.BlockSpec((None, tm, D), lambda b, i: (b, i, 0))]
    out_shape = [jax.ShapeDtypeStruct((bsz, n, D), BF16)]
    if with_router:
        in_specs.append(pl.BlockSpec((D, N_EXPERTS), lambda b, i: (0, 0)))
        args.append(w_router)
        out_specs.append(pl.BlockSpec((None, tm, N_EXPERTS), lambda b, i: (b, i, 0)))
        out_shape.append(jax.ShapeDtypeStruct((bsz, n, N_EXPERTS), F32))
    res = pl.pallas_call(
        functools.partial(_modulate_kernel, with_router=with_router),
        grid=(bsz, n // tm),
        in_specs=in_specs, out_specs=out_specs, out_shape=out_shape,
        compiler_params=_cparams(("parallel", "parallel")),
        name="modulate_router" if with_router else "modulate",
    )(*args)
    return res if with_router else res[0]


def _mm_kernel(a_ref, w_ref, *rest, rope, tn):
    acc = jnp.dot(a_ref[...], w_ref[...], preferred_element_type=F32)
    if rope:
        cos_ref, sin_ref, o_ref = rest
        cos = cos_ref[...]
        sin = sin_ref[...]
        lane = lax.broadcasted_iota(I32, cos.shape, 1)
        first_half = (lane % DA_HEAD_DIM) < (DA_HEAD_DIM // 2)
        for c in range(tn // LANES):
            xs = acc[:, c * LANES:(c + 1) * LANES]
            partner = jnp.where(first_half,
                                pltpu.roll(xs, LANES - DA_HEAD_DIM // 2, 1),
                                pltpu.roll(xs, DA_HEAD_DIM // 2, 1))
            o_ref[:, c * LANES:(c + 1) * LANES] = (xs * cos + partner * sin).astype(o_ref.dtype)
    else:
        (o_ref,) = rest
        o_ref[...] = acc.astype(o_ref.dtype)


def _matmul(a, w, out_dtype, tm, tn, rope_tables=None):
    m, k = a.shape
    n = w.shape[1]
    rope = rope_tables is not None
    in_specs = [pl.BlockSpec((tm, k), lambda i, j: (i, 0)),
                pl.BlockSpec((k, tn), lambda i, j: (0, j))]
    args = [a, w]
    if rope:
        cos, sin = rope_tables
        nt = cos.shape[0] // tm
        in_specs += [pl.BlockSpec((tm, LANES), lambda i, j: (i % nt, 0)),
                     pl.BlockSpec((tm, LANES), lambda i, j: (i % nt, 0))]
        args += [cos, sin]
    return pl.pallas_call(
        functools.partial(_mm_kernel, rope=rope, tn=tn),
        grid=(m // tm, n // tn),
        in_specs=in_specs,
        out_specs=pl.BlockSpec((tm, tn), lambda i, j: (i, j)),
        out_shape=jax.ShapeDtypeStruct((m, n), out_dtype),
        compiler_params=_cparams(("parallel", "parallel")),
        name="proj_rope" if rope else "proj",
    )(*args)


def _conv_kernel(prev_ref, cur_ref, next_ref, w_ref, b_ref, o_ref, *, tc):
    i = pl.program_id(1)
    last = pl.num_programs(1) - 1
    pad = (SSD_CONV_W - 1) // 2
    prev = jnp.where(i > 0, prev_ref[...].astype(F32), 0.0)
    nxt = jnp.where(i < last, next_ref[...].astype(F32), 0.0)
    ext = jnp.concatenate([prev, cur_ref[...].astype(F32), nxt], axis=0)
    w = w_ref[...]
    y = b_ref[...] + w[0:1] * ext[HALO - pad:HALO - pad + tc]
    for k in range(1, SSD_CONV_W):
        y = y + w[k:k + 1] * ext[HALO - pad + k:HALO - pad + k + tc]
    o_ref[...] = _silu(y).astype(o_ref.dtype)


def _ssd_conv(pa, conv_w, conv_b):
    bsz, n, _ = pa.shape
    tc = min(512, n)
    tw = 512
    c0 = PA_XBC // tw
    nbh = n // HALO
    return pl.pallas_call(
        functools.partial(_conv_kernel, tc=tc),
        grid=(bsz, n // tc, SSD_XBC // tw),
        in_specs=[pl.BlockSpec((None, HALO, tw), lambda b, i, j: (b, jnp.maximum(i * (tc // HALO) - 1, 0), c0 + j)),
                  pl.BlockSpec((None, tc, tw), lambda b, i, j: (b, i, c0 + j)),
                  pl.BlockSpec((None, HALO, tw),
                               lambda b, i, j: (b, jnp.minimum((i + 1) * (tc // HALO), nbh - 1), c0 + j)),
                  pl.BlockSpec((SSD_CONV_W, tw), lambda b, i, j: (0, j)),
                  pl.BlockSpec((1, tw), lambda b, i, j: (0, j))],
        out_specs=pl.BlockSpec((None, tc, tw), lambda b, i, j: (b, i, j)),
        out_shape=jax.ShapeDtypeStruct((bsz, n, SSD_XBC), BF16),
        compiler_params=_cparams(("parallel", "parallel", "parallel")),
        name="ssd_conv",
    )(pa, pa, pa, conv_w, conv_b.reshape(1, SSD_XBC))


def _tri(q, rev):
    r = lax.broadcasted_iota(I32, (q, q), 0)
    c = lax.broadcasted_iota(I32, (q, q), 1)
    return (c >= r) if rev else (r >= c)


def _ssd_scan_kernel(*refs, rev, has_init, readout):
    it = iter(refs)
    xact_ref, dt_ref, alog_ref, bias_ref = next(it), next(it), next(it), next(it)
    init_ref = next(it) if has_init else None
    if readout:
        z_ref, yprev_ref, d_ref, nw_ref = next(it), next(it), next(it), next(it)
    y_ref, st_ref = next(it), next(it)
    ybuf_ref = next(it) if readout else None

    q = SCAN_Q
    hd = SSD_D_INNER // SSD_HEADS
    hg = SSD_HEADS // SSD_GROUPS
    gw = hg * hd
    d_off = SSD_HEADS if rev else 0

    @pl.when(pl.program_id(1) == 0)
    def _():
        st_ref[...] = init_ref[...] if has_init else jnp.zeros_like(st_ref)

    tri = _tri(q, rev)
    dtv = jax.nn.softplus(dt_ref[...] + bias_ref[...])
    da = dtv * (-jnp.exp(alog_ref[...]))
    cs = jnp.dot(tri.astype(F32), da, preferred_element_type=F32, precision=HIGHEST)
    cs_t = cs.T
    erow = lax.broadcasted_iota(I32, (LANES, SSD_D_INNER), 0)
    ecol = lax.broadcasted_iota(I32, (LANES, SSD_D_INNER), 1)
    expand = (erow == d_off + ecol // hd).astype(F32)
    csx = jnp.dot(cs, expand, preferred_element_type=F32, precision=HIGHEST)
    dtx = jnp.dot(dtv, expand, preferred_element_type=F32, precision=HIGHEST)
    x = xact_ref[:, :SSD_D_INNER].astype(F32)
    xdt = x * dtx
    xdt_b = xdt.astype(BF16)
    cs_end = csx[0:1] if rev else csx[q - 1:q]
    ecs = jnp.exp(csx)
    xw = (xdt * jnp.exp(cs_end - csx)).astype(BF16)
    ecs_end = jnp.exp(cs_end)
    lane = lax.broadcasted_iota(I32, (q, LANES), 1)
    out_ref = ybuf_ref if readout else y_ref

    for g in range(SSD_GROUPS):
        bg = xact_ref[:, SSD_D_INNER + g * SSD_STATE:SSD_D_INNER + (g + 1) * SSD_STATE]
        cg = xact_ref[:, SSD_D_INNER + (SSD_GROUPS + g) * SSD_STATE:SSD_D_INNER + (SSD_GROUPS + g + 1) * SSD_STATE]
        cb = lax.dot_general(cg, bg, (((1,), (1,)), ((), ())), preferred_element_type=F32)
        st_g = st_ref[:, g * gw:(g + 1) * gw]
        y_inter = jnp.dot(cg, st_g.astype(BF16), preferred_element_type=F32) * ecs[:, g * gw:(g + 1) * gw]
        for jj in range(hg // 2):
            col0 = g * gw + jj * LANES
            xp = xdt_b[:, col0:col0 + LANES]
            acc = y_inter[:, jj * LANES:(jj + 1) * LANES]
            for s in range(2):
                j = d_off + g * hg + 2 * jj + s
                e = cs[:, j:j + 1] - cs_t[j:j + 1, :]
                seg = jnp.exp(jnp.where(tri, e, -jnp.inf))
                m = (cb * seg).astype(BF16)
                xm = jnp.where((lane >= hd) if s else (lane < hd), xp, jnp.zeros_like(xp))
                acc = acc + jnp.dot(m, xm, preferred_element_type=F32)
            out_ref[:, col0:col0 + LANES] = acc
        upd = lax.dot_general(bg, xw[:, g * gw:(g + 1) * gw], (((0,), (0,)), ((), ())),
                              preferred_element_type=F32)
        st_ref[:, g * gw:(g + 1) * gw] = st_g * ecs_end[:, g * gw:(g + 1) * gw] + upd

    if readout:
        z = z_ref[...].astype(F32)
        yy = (yprev_ref[...] + ybuf_ref[...] + d_ref[...] * x) * _silu(z)
        for g in range(SSD_GROUPS):
            seg = yy[:, g * gw:(g + 1) * gw]
            ms = jnp.mean(seg * seg, axis=-1, keepdims=True)
            y_ref[:, g * gw:(g + 1) * gw] = (seg * lax.rsqrt(ms + EPS) * nw_ref[:, g * gw:(g + 1) * gw]).astype(y_ref.dtype)


def _ssd_scan(xact, pf, alog128, bias128, *, rev, init=None, readout=None):
    bsz, n, _ = xact.shape
    nc = n // SCAN_Q
    cidx = (lambda c: nc - 1 - c) if rev else (lambda c: c)
    in_specs = [pl.BlockSpec((None, SCAN_Q, SSD_XBC), lambda b, c: (b, cidx(c), 0)),
                pl.BlockSpec((None, SCAN_Q, LANES), lambda b, c: (b, cidx(c), PF_DT // LANES)),
                pl.BlockSpec((1, LANES), lambda b, c: (0, 0)),
                pl.BlockSpec((1, LANES), lambda b, c: (0, 0))]
    args = [xact, pf, alog128, bias128]
    if init is not None:
        in_specs.append(pl.BlockSpec((None, SSD_STATE, SSD_D_INNER), lambda b, c: (b, 0, 0)))
        args.append(init)
    scratch = []
    if readout is not None:
        pa, yprev, d_full, norm_w = readout
        in_specs += [pl.BlockSpec((None, SCAN_Q, SSD_D_INNER), lambda b, c: (b, cidx(c), PA_Z // SSD_D_INNER)),
                     pl.BlockSpec((None, SCAN_Q, SSD_D_INNER), lambda b, c: (b, cidx(c), 0)),
                     pl.BlockSpec((1, SSD_D_INNER), lambda b, c: (0, 0)),
                     pl.BlockSpec((1, SSD_D_INNER), lambda b, c: (0, 0))]
        args += [pa, yprev, d_full, norm_w]
        scratch = [pltpu.VMEM((SCAN_Q, SSD_D_INNER), F32)]
    y_dtype = BF16 if readout is not None else F32
    y, st = pl.pallas_call(
        functools.partial(_ssd_scan_kernel, rev=rev, has_init=init is not None, readout=readout is not None),
        grid=(bsz, nc),
        in_specs=in_specs,
        out_specs=[pl.BlockSpec((None, SCAN_Q, SSD_D_INNER), lambda b, c: (b, cidx(c), 0)),
                   pl.BlockSpec((None, SSD_STATE, SSD_D_INNER), lambda b, c: (b, 0, 0))],
        out_shape=[jax.ShapeDtypeStruct((bsz, n, SSD_D_INNER), y_dtype),
                   jax.ShapeDtypeStruct((bsz, SSD_STATE, SSD_D_INNER), F32)],
        scratch_shapes=scratch,
        compiler_params=_cparams(("parallel", "arbitrary")),
        name="ssd_scan_bwd" if rev else "ssd_scan_fwd",
    )(*args)
    return y, st


def _ref_rows(bb, s, rev):
    q, w = bb.shape
    rl = s if rev else s - 1
    if 2 * s >= 8:
        b3 = bb.reshape(q // (2 * s), 2 * s, w)
        return jnp.broadcast_to(b3[:, rl:rl + 1, :], b3.shape).reshape(q, w)
    off = lax.broadcasted_iota(I32, (q, w), 0) % (2 * s)
    out = bb
    for ov in range(2 * s):
        if ov != rl:
            out = jnp.where(off == ov, pltpu.roll(bb, (ov - rl) % q, 0), out)
    return out


def _hg_scan_kernel(*refs, rev, has_init, readout):
    it = iter(refs)
    q_ref, f_ref, v_ref, lb_ref = next(it), next(it), next(it), next(it)
    init_ref = next(it) if has_init else None
    if readout:
        g_ref, oprev_ref, nw_ref = next(it), next(it), next(it)
    o_ref, st_ref, att_ref = next(it), next(it), next(it)
    obuf_ref = next(it) if readout else None

    q = SCAN_Q
    hd = HG_HEAD_DIM

    @pl.when(pl.program_id(1) == 0)
    def _():
        st_ref[...] = init_ref[...] if has_init else jnp.zeros_like(st_ref)

    lb = lb_ref[...]
    qq = _silu(q_ref[...].astype(F32))
    f = lb + (1.0 - lb) * jax.nn.sigmoid(f_ref[...])
    kk = 1.0 - f
    lf = jnp.log(f)
    tri = _tri(q, rev)
    bb = jnp.dot(tri.astype(F32), lf, preferred_element_type=F32, precision=HIGHEST)

    row = lax.broadcasted_iota(I32, (q, HG_WIDTH), 0)
    ri = lax.broadcasted_iota(I32, (q, q), 0)
    ci = lax.broadcasted_iota(I32, (q, q), 1)
    xr = ri ^ ci
    att_ref[...] = jnp.zeros_like(att_ref)
    s = q // 2
    while s >= 1:
        upper = (row % (2 * s)) >= s
        q_side = jnp.logical_not(upper) if rev else upper
        delta = bb - _ref_rows(bb, s, rev)
        ex = jnp.exp(jnp.where(q_side, delta, -delta))
        qt = jnp.where(q_side, qq * ex, 0.0).astype(BF16)
        kt = jnp.where(q_side, 0.0, kk * ex).astype(BF16)
        level = (xr >> int(math.log2(s))) == 1
        for h in range(HG_HEADS):
            p = lax.dot_general(qt[:, h * hd:(h + 1) * hd], kt[:, h * hd:(h + 1) * hd],
                                (((1,), (1,)), ((), ())), preferred_element_type=F32)
            att_ref[h] = jnp.where(level, p, att_ref[h])
        s //= 2

    b_end = bb[0:1] if rev else bb[q - 1:q]
    qe = (qq * jnp.exp(bb)).astype(BF16)
    kh = (kk * jnp.exp(b_end - bb)).astype(BF16)
    qk = (qq * kk).astype(BF16)
    e_end = jnp.exp(b_end)
    ones = jnp.ones((hd, hd), BF16)
    out_ref = obuf_ref if readout else o_ref
    for h in range(HG_HEADS):
        sl = slice(h * hd, (h + 1) * hd)
        vh = v_ref[:, sl]
        st_h = st_ref[sl, :]
        o = jnp.dot(att_ref[h].astype(BF16), vh, preferred_element_type=F32)
        o = o + jnp.dot(qk[:, sl], ones, preferred_element_type=F32) * vh.astype(F32)
        o = o + lax.dot_general(qe[:, sl], st_h.astype(BF16), (((1,), (1,)), ((), ())), preferred_element_type=F32)
        out_ref[:, sl] = o
        upd = lax.dot_general(vh, kh[:, sl], (((0,), (0,)), ((), ())), preferred_element_type=F32)
        st_ref[sl, :] = st_h * e_end[:, sl] + upd

    if readout:
        for h in range(HG_HEADS):
            sl = slice(h * hd, (h + 1) * hd)
            o = oprev_ref[:, sl] + obuf_ref[:, sl]
            ms = jnp.mean(o * o, axis=-1, keepdims=True)
            gate = _silu(g_ref[:, sl].astype(F32))
            o_ref[:, sl] = (o * lax.rsqrt(ms + EPS) * nw_ref[...] * gate).astype(o_ref.dtype)


def _hg_scan(pa, pf, lb_row, *, rev, init=None, readout=None):
    bsz, n, _ = pa.shape
    nc = n // SCAN_Q
    cidx = (lambda c: nc - 1 - c) if rev else (lambda c: c)
    fcol = (PF_F // HG_WIDTH) + (1 if rev else 0)
    in_specs = [pl.BlockSpec((None, SCAN_Q, HG_WIDTH), lambda b, c: (b, cidx(c), PA_HQ // HG_WIDTH)),
                pl.BlockSpec((None, SCAN_Q, HG_WIDTH), lambda b, c: (b, cidx(c), fcol)),
                pl.BlockSpec((None, SCAN_Q, HG_WIDTH), lambda b, c: (b, cidx(c), PA_HI // HG_WIDTH)),
                pl.BlockSpec((1, HG_WIDTH), lambda b, c: (0, 0))]
    args = [pa, pf, pa, lb_row]
    if init is not None:
        in_specs.append(pl.BlockSpec((None, HG_WIDTH, HG_HEAD_DIM), lambda b, c: (b, 0, 0)))
        args.append(init)
    scratch = [pltpu.VMEM((HG_HEADS, SCAN_Q, SCAN_Q), F32)]
    if readout is not None:
        oprev, nw128 = readout
        in_specs += [pl.BlockSpec((None, SCAN_Q, HG_WIDTH), lambda b, c: (b, cidx(c), PA_HG // HG_WIDTH)),
                     pl.BlockSpec((None, SCAN_Q, HG_WIDTH), lambda b, c: (b, cidx(c), 0)),
                     pl.BlockSpec((1, HG_HEAD_DIM), lambda b, c: (0, 0))]
        args += [pa, oprev, nw128]
        scratch.append(pltpu.VMEM((SCAN_Q, HG_WIDTH), F32))
    o_dtype = BF16 if readout is not None else F32
    assert PA_HQ % HG_WIDTH == 0 or True
    o, st = pl.pallas_call(
        functools.partial(_hg_scan_kernel, rev=rev, has_init=init is not None, readout=readout is not None),
        grid=(bsz, nc),
        in_specs=in_specs,
        out_specs=[pl.BlockSpec((None, SCAN_Q, HG_WIDTH), lambda b, c: (b, cidx(c), 0)),
                   pl.BlockSpec((None, HG_WIDTH, HG_HEAD_DIM), lambda b, c: (b, 0, 0))],
        out_shape=[jax.ShapeDtypeStruct((bsz, n, HG_WIDTH), o_dtype),
                   jax.ShapeDtypeStruct((bsz, HG_WIDTH, HG_HEAD_DIM), F32)],
        scratch_shapes=scratch,
        compiler_params=_cparams(("parallel", "arbitrary")),
        name="hg_scan_bwd" if rev else "hg_scan_fwd",
    )(*args)
    return o, st


def _attn_kernel(*refs, seg_lens, tk, lam_init):
    lam_ref, nw_ref, q_ref = refs[0], refs[1], refs[2]
    nseg = len(seg_lens)
    kv_refs = refs[3:3 + 2 * nseg]
    o_ref, qs_ref, sa_ref, sb_ref, m_ref, mn_ref, l_ref, acc_ref = refs[3 + 2 * nseg:]
    tq = q_ref.shape[0]
    qv = q_ref[...] * jnp.asarray(DA_HEAD_DIM ** -0.5, BF16)
    lane = lax.broadcasted_iota(I32, qv.shape, 1)
    zero = jnp.zeros_like(qv)
    qs_ref[:tq] = jnp.where(lane < DA_HEAD_DIM, qv, zero)
    qs_ref[tq:] = jnp.where(lane >= DA_HEAD_DIM, qv, zero)
    m_ref[...] = jnp.full_like(m_ref, -jnp.inf)
    l_ref[...] = jnp.zeros_like(l_ref)
    acc_ref[...] = jnp.zeros_like(acc_ref)

    def scores(kc):
        return lax.dot_general(qs_ref[...], kc, (((1,), (1,)), ((), ())), preferred_element_type=F32)

    def lane_tiles(a):
        return [a[:, c * LANES:(c + 1) * LANES] for c in range(a.shape[1] // LANES)]

    def row_max(s, m_prev):
        tiles = lane_tiles(s)
        m = tiles[0]
        for tile in tiles[1:]:
            m = jnp.maximum(m, tile)
        return jnp.maximum(m_prev, jnp.broadcast_to(jnp.max(m, axis=-1, keepdims=True), m_prev.shape))

    def online_step(s, vc, m_prev, m_cur):
        alpha = jnp.exp(m_prev - m_cur)
        ps = [jnp.exp(tile - m_cur) for tile in lane_tiles(s)]
        l = alpha * l_ref[...]
        for p in ps:
            l = l + p
        l_ref[...] = l
        pv = jnp.dot(jnp.concatenate(ps, axis=1).astype(BF16), vc, preferred_element_type=F32)
        acc_ref[...] = alpha * acc_ref[...] + pv

    for si, n in enumerate(seg_lens):
        k_ref, v_ref = kv_refs[2 * si], kv_refs[2 * si + 1]
        t = min(tk, n)
        nc = n // t
        if nc == 1:
            s = scores(k_ref[...])
            m_prev = m_ref[...]
            m_cur = row_max(s, m_prev)
            online_step(s, v_ref[...], m_prev, m_cur)
            m_ref[...] = m_cur
        else:
            assert nc % 2 == 0
            s0 = scores(k_ref[pl.ds(0, t), :])
            sa_ref[...] = s0
            mn_ref[...] = row_max(s0, m_ref[...])

            def pipe_step(c, cur_ref, nxt_ref, k_ref=k_ref, v_ref=v_ref, t=t):
                r0 = pl.multiple_of(c * t, t)
                r1 = pl.multiple_of(r0 + t, t)
                m_prev, m_cur = m_ref[...], mn_ref[...]
                s_next = scores(k_ref[pl.ds(r1, t), :])
                nxt_ref[...] = s_next
                mn_ref[...] = row_max(s_next, m_cur)
                m_ref[...] = m_cur
                online_step(cur_ref[...], v_ref[pl.ds(r0, t), :], m_prev, m_cur)

            def body(i, carry):
                pipe_step(2 * i, sa_ref, sb_ref)
                pipe_step(2 * i + 1, sb_ref, sa_ref)
                return carry

            lax.fori_loop(0, (nc - 2) // 2, body, 0)
            pipe_step(nc - 2, sa_ref, sb_ref)
            m_prev, m_cur = m_ref[...], mn_ref[...]
            online_step(sb_ref[...], v_ref[pl.ds((nc - 1) * t, t), :], m_prev, m_cur)
            m_ref[...] = m_cur
    l = jnp.sum(l_ref[...], axis=-1, keepdims=True)
    acc = acc_ref[...]
    lam = lam_ref[...]
    lmbda = (jnp.exp(jnp.sum(lam[0:1] * lam[1:2], axis=-1, keepdims=True))
             - jnp.exp(jnp.sum(lam[2:3] * lam[3:4], axis=-1, keepdims=True)) + lam_init)
    o = acc[:tq] / l[:tq] - lmbda * (acc[tq:] / l[tq:])
    ms = jnp.mean(o * o, axis=-1, keepdims=True)
    o_ref[...] = (o * lax.rsqrt(ms + EPS) * nw_ref[...] * (1.0 - lam_init)).astype(o_ref.dtype)


def _diff_attn(pqk_q, kv_segs, lam, nw, lam_init, tq, tk):
    bsz, nq, _ = pqk_q.shape
    hw = 2 * DA_HEAD_DIM
    kcol0 = DA_WIDTH // hw
    vcol0 = PA_DV // hw
    in_specs = [pl.BlockSpec((4, DA_HEAD_DIM), lambda b, h, i: (0, 0)),
                pl.BlockSpec((1, hw), lambda b, h, i: (0, 0)),
                pl.BlockSpec((None, tq, hw), lambda b, h, i: (b, i, h))]
    args = [lam, nw.reshape(1, hw), pqk_q]
    seg_lens = []
    for pqk, pa in kv_segs:
        nk = pqk.shape[1]
        seg_lens.append(nk)
        in_specs += [pl.BlockSpec((None, nk, hw), lambda b, h, i: (b, 0, kcol0 + h)),
                     pl.BlockSpec((None, nk, hw), lambda b, h, i: (b, 0, vcol0 + h))]
        args += [pqk, pa]
    return pl.pallas_call(
        functools.partial(_attn_kernel, seg_lens=tuple(seg_lens), tk=tk, lam_init=lam_init),
        grid=(bsz, DA_HEADS, nq // tq),
        in_specs=in_specs,
        out_specs=pl.BlockSpec((None, tq, hw), lambda b, h, i: (b, i, h)),
        out_shape=jax.ShapeDtypeStruct((bsz, nq, DA_WIDTH), BF16),
        scratch_shapes=[pltpu.VMEM((2 * tq, hw), BF16),
                        pltpu.VMEM((2 * tq, tk), F32),
                        pltpu.VMEM((2 * tq, tk), F32),
                        pltpu.VMEM((2 * tq, LANES), F32),
                        pltpu.VMEM((2 * tq, LANES), F32),
                        pltpu.VMEM((2 * tq, LANES), F32),
                        pltpu.VMEM((2 * tq, hw), F32)],
        compiler_params=_cparams(("parallel", "parallel", "parallel")),
        name="diff_attn",
    )(*args)


def _mix_kernel(h_ref, ya_ref, yb_ref, yd_ref, wg0, wg1, wg2, bg0, bg1, bg2, wu0, wu1, wu2, o_ref):
    h = h_ref[...]
    mix = None
    for y_ref, wg, bg, wu in ((ya_ref, wg0, bg0, wu0), (yb_ref, wg1, bg1, wu1), (yd_ref, wg2, bg2, wu2)):
        gate = jax.nn.sigmoid(jnp.dot(h, wg[...], preferred_element_type=F32) + bg[...])
        up = jnp.dot(y_ref[...], wu[...], preferred_element_type=F32)
        mix = gate * up if mix is None else mix + gate * up
    o_ref[...] = mix.astype(o_ref.dtype)


def _residual_out_kernel(x_ref, g1_ref, m_ref, wo_ref, o_ref):
    o_ref[...] = x_ref[...] + g1_ref[...] * jnp.dot(m_ref[...], wo_ref[...], preferred_element_type=F32)


def _merge(x, gate1, h, ya, yb, yd, w_gate, b_gate, w_up, w_out):
    bsz, n, _ = x.shape
    tm = min(1024, n)
    tn = 512
    nn = D // tn
    row = lambda b, i, j: (b, i, 0)
    in_specs = [pl.BlockSpec((None, tm, D), row),
                pl.BlockSpec((None, tm, 1024), row),
                pl.BlockSpec((None, tm, 1024), row),
                pl.BlockSpec((None, tm, 1024), row)]
    in_specs += [pl.BlockSpec((D, tn), functools.partial(lambda b, i, j, k: (0, k * nn + j), k=k)) for k in range(3)]
    in_specs += [pl.BlockSpec((1, tn), functools.partial(lambda b, i, j, k: (0, k * nn + j), k=k)) for k in range(3)]
    in_specs += [pl.BlockSpec((None, 1024, tn), functools.partial(lambda b, i, j, k: (k, 0, j), k=k)) for k in range(3)]
    mix = pl.pallas_call(
        _mix_kernel,
        grid=(bsz, n // tm, nn),
        in_specs=in_specs,
        out_specs=pl.BlockSpec((None, tm, tn), lambda b, i, j: (b, i, j)),
        out_shape=jax.ShapeDtypeStruct((bsz, n, D), BF16),
        compiler_params=_cparams(("parallel", "parallel", "parallel")),
        name="merge_mix",
    )(h, ya, yb, yd, w_gate, w_gate, w_gate, b_gate, b_gate, b_gate, w_up, w_up, w_up)
    return pl.pallas_call(
        _residual_out_kernel,
        grid=(bsz, n // tm, nn),
        in_specs=[pl.BlockSpec((None, tm, tn), lambda b, i, j: (b, i, j)),
                  pl.BlockSpec((None, 1, tn), lambda b, i, j: (b, 0, j)),
                  pl.BlockSpec((None, tm, D), row),
                  pl.BlockSpec((D, tn), lambda b, i, j: (0, j))],
        out_specs=pl.BlockSpec((None, tm, tn), lambda b, i, j: (b, i, j)),
        out_shape=jax.ShapeDtypeStruct((bsz, n, D), F32),
        compiler_params=_cparams(("parallel", "parallel", "parallel")),
        name="merge_out",
    )(x, gate1, mix, w_out)


def _route_kernel(aff_ref, posm_ref, pos_ref, *, cap, rpe):
    rows = N_EXPERTS * rpe
    bits = pltpu.bitcast(aff_ref[...], I32)
    ones = jnp.ones((LANES, LANES), BF16)
    r = lax.broadcasted_iota(I32, (rows, rows), 0)
    c = lax.broadcasted_iota(I32, (rows, rows), 1)
    same = (r // rpe) == (c // rpe)
    grp = same.astype(BF16)
    grp_before = jnp.logical_and(same, c < r).astype(BF16)
    lr = lax.broadcasted_iota(I32, (LANES, LANES), 0)
    lc = lax.broadcasted_iota(I32, (LANES, LANES), 1)
    before = (lr < lc).astype(BF16)

    def count(mask):
        per_row = jnp.dot(mask.astype(BF16), ones, preferred_element_type=F32)
        return jnp.dot(grp, per_row.astype(BF16), preferred_element_type=F32)

    def excl_prefix(mask):
        mb = mask.astype(BF16)
        within = jnp.dot(mb, before, preferred_element_type=F32)
        per_row = jnp.dot(mb, ones, preferred_element_type=F32)
        return within + jnp.dot(grp_before, per_row.astype(BF16), preferred_element_type=F32)

    def body(i, thr):
        cand = thr | (jnp.int32(1) << (30 - i))
        return jnp.where(count(bits >= cand) >= cap, cand, thr)

    thr = lax.fori_loop(0, 31, body, jnp.zeros((rows, LANES), I32))
    gt = bits > thr
    eq = bits == thr
    need = cap - count(gt)
    sel = jnp.logical_or(gt, jnp.logical_and(eq, excl_prefix(eq) < need))
    pos = excl_prefix(sel).astype(I32)
    pos_ref[...] = pos
    posm_ref[...] = jnp.where(sel, pos, -1)


def _route(aff, cap):
    bsz, n, _ = aff.shape
    rpe = n // LANES
    rows = N_EXPERTS * rpe
    aff_t = jnp.swapaxes(aff, 1, 2).reshape(bsz, rows, LANES)
    spec = pl.BlockSpec((None, rows, LANES), lambda b: (b, 0, 0))
    posm, pos = pl.pallas_call(
        functools.partial(_route_kernel, cap=cap, rpe=rpe),
        grid=(bsz,),
        in_specs=[spec], out_specs=[spec, spec],
        out_shape=[jax.ShapeDtypeStruct((bsz, rows, LANES), I32)] * 2,
        compiler_params=_cparams(("parallel",)),
        name="route",
    )(aff_t)
    return posm.reshape(bsz, N_EXPERTS, n), pos.reshape(bsz, N_EXPERTS, n)


def _gather_kernel(base_ref, h_ref, posm_ref, xg_ref, *, win, nblk):
    b = pl.program_id(0)
    xg_ref[...] = jnp.zeros_like(xg_ref)
    slot = lax.broadcasted_iota(I32, (win, TOK_BLK), 0)

    def body(j, carry):
        t0 = pl.multiple_of(j * TOK_BLK, TOK_BLK)
        bases, onehots = [], []
        for e in range(N_EXPERTS):
            base = pl.multiple_of(base_ref[(b * N_EXPERTS + e) * nblk + j], SLOT_ALIGN)
            rel = posm_ref[e:e + 1, pl.ds(t0, TOK_BLK)] - base
            bases.append(base)
            onehots.append(jnp.where(slot == rel, 1.0, 0.0).astype(BF16))
        rows = jnp.dot(jnp.concatenate(onehots, axis=0), h_ref[pl.ds(t0, TOK_BLK), :],
                       preferred_element_type=F32).astype(BF16)
        for e in range(N_EXPERTS):
            xg_ref[e, pl.ds(bases[e], win), :] += rows[e * win:(e + 1) * win]
        return carry

    lax.fori_loop(0, nblk, body, 0)


def _expert_kernel(xg_ref, w1_ref, w3_ref, w2_ref, y_ref, acc_ref):
    f = pl.program_id(2)

    @pl.when(f == 0)
    def _():
        acc_ref[...] = jnp.zeros_like(acc_ref)

    xg = xg_ref[...]
    a = jnp.dot(xg, w1_ref[...].astype(BF16), preferred_element_type=F32)
    g = jnp.dot(xg, w3_ref[...].astype(BF16), preferred_element_type=F32)
    acc_ref[...] += jnp.dot((_silu(a) * g).astype(BF16), w2_ref[...].astype(BF16), preferred_element_type=F32)

    @pl.when(f == pl.num_programs(2) - 1)
    def _():
        y_ref[...] = acc_ref[...].astype(y_ref.dtype)


def _combine_kernel(base_ref, x_ref, g2_ref, aff_ref, posm_ref, y_ref, *rest, win, win_tile, cap, nblk, nsub, final):
    if final:
        fw_ref, o_ref, acc_ref = rest
    else:
        o_ref, acc_ref = rest
    b, jt, e = pl.program_id(0), pl.program_id(1), pl.program_id(2)

    @pl.when(e == 0)
    def _():
        acc_ref[...] = jnp.zeros_like(acc_ref)

    first = (b * N_EXPERTS + e) * nblk + jt * nsub
    tile_base = jnp.minimum(base_ref[first], cap - win_tile)
    lane = lax.broadcasted_iota(I32, (TOK_BLK, N_EXPERTS), 1)
    mine = lane == e
    slot = lax.broadcasted_iota(I32, (TOK_BLK, win), 1)
    for sb in range(nsub):
        rows = slice(sb * TOK_BLK, (sb + 1) * TOK_BLK)
        base = base_ref[first + sb]
        rel = jnp.sum(jnp.where(mine, posm_ref[rows, :], 0), axis=-1, keepdims=True) - base
        val = jnp.sum(jnp.where(mine, aff_ref[rows, :], 0.0), axis=-1, keepdims=True)
        onehot = jnp.where(slot == rel, 1.0, 0.0).astype(BF16)
        off = pl.multiple_of(base - tile_base, SLOT_ALIGN)
        acc_ref[rows, :] += val * jnp.dot(onehot, y_ref[pl.ds(off, win), :], preferred_element_type=F32)

    @pl.when(e == pl.num_programs(2) - 1)
    def _():
        out = x_ref[...] + g2_ref[...] * acc_ref[...]
        if final:
            ms = jnp.mean(out * out, axis=-1, keepdims=True)
            out = out * lax.rsqrt(ms + EPS) * fw_ref[...]
        o_ref[...] = out


def _ec_moe(token_sets, w1, w3, w2, layer):
    bsz = token_sets[0][0].shape[0]
    plans, xgs = [], []
    for x, gate2, h2, aff, final_w in token_sets:
        n = x.shape[1]
        cap = EC_CAPACITY * n // N_EXPERTS
        nblk = n // TOK_BLK
        win = min(TOK_BLK + SLOT_ALIGN, cap)
        posm, pos = _route(aff, cap)
        base = jnp.minimum((pos[:, :, ::TOK_BLK] // SLOT_ALIGN) * SLOT_ALIGN, cap - win).astype(I32).reshape(-1)
        dq = GATHER_COLS
        xgs.append(pl.pallas_call(
            functools.partial(_gather_kernel, win=win, nblk=nblk),
            grid_spec=pltpu.PrefetchScalarGridSpec(
                num_scalar_prefetch=1,
                grid=(bsz, D // dq),
                in_specs=[pl.BlockSpec((None, n, dq), lambda b, c, base: (b, 0, c)),
                          pl.BlockSpec((None, N_EXPERTS, n), lambda b, c, base: (b, 0, 0))],
                out_specs=pl.BlockSpec((None, N_EXPERTS, cap, dq), lambda b, c, base: (b, 0, 0, c))),
            out_shape=jax.ShapeDtypeStruct((bsz, N_EXPERTS, cap, D), BF16),
            compiler_params=_cparams(("parallel", "parallel")),
            name="moe_gather",
        )(base, h2, posm))
        plans.append((n, cap, nblk, win, posm, base))

    xg = xgs[0] if len(xgs) == 1 else jnp.concatenate(xgs, axis=2)
    cap_all = xg.shape[2]
    tf = 256
    y = pl.pallas_call(
        _expert_kernel,
        grid=(N_EXPERTS, bsz, EXPERT_FF // tf),
        in_specs=[pl.BlockSpec((None, None, cap_all, D), lambda e, b, f: (b, e, 0, 0)),
                  pl.BlockSpec((None, None, D, tf), lambda e, b, f: (layer, e, 0, f)),
                  pl.BlockSpec((None, None, D, tf), lambda e, b, f: (layer, e, 0, f)),
                  pl.BlockSpec((None, None, tf, D), lambda e, b, f: (layer, e, f, 0))],
        out_specs=pl.BlockSpec((None, None, cap_all, D), lambda e, b, f: (b, e, 0, 0)),
        out_shape=jax.ShapeDtypeStruct((bsz, N_EXPERTS, cap_all, D), BF16),
        scratch_shapes=[pltpu.VMEM((cap_all, D), F32)],
        compiler_params=_cparams(("parallel", "parallel", "arbitrary")),
        name="moe_expert",
    )(xg, w1, w3, w2)
    y2d = y.reshape(bsz * N_EXPERTS * cap_all, D)

    outs = []
    row0 = 0
    for (x, gate2, h2, aff, final_w), (n, cap, nblk, win, posm, base) in zip(token_sets, plans):
        tile = min(COMBINE_TILE, n)
        nsub = tile // TOK_BLK
        win_tile = min(tile + SLOT_ALIGN, cap)
        final = final_w is not None

        def y_window(b, j, e, base, cap=cap, nblk=nblk, nsub=nsub, win_tile=win_tile, row0=row0):
            be = b * N_EXPERTS + e
            start = jnp.minimum(base[be * nblk + j * nsub], cap - win_tile)
            return pl.multiple_of(be * cap_all + row0 + start, SLOT_ALIGN), 0

        in_specs = [pl.BlockSpec((None, tile, D), lambda b, j, e, base: (b, j, 0)),
                    pl.BlockSpec((None, 1, D), lambda b, j, e, base: (b, 0, 0)),
                    pl.BlockSpec((None, tile, N_EXPERTS), lambda b, j, e, base: (b, j, 0)),
                    pl.BlockSpec((None, tile, N_EXPERTS), lambda b, j, e, base: (b, j, 0)),
                    pl.BlockSpec((pl.Element(win_tile), pl.Element(D)), y_window)]
        args = [base, x, gate2, aff, jnp.swapaxes(posm, 1, 2), y2d]
        if final:
            in_specs.append(pl.BlockSpec((1, D), lambda b, j, e, base: (0, 0)))
            args.append(final_w.reshape(1, D))
        outs.append(pl.pallas_call(
            functools.partial(_combine_kernel, win=win, win_tile=win_tile, cap=cap, nblk=nblk, nsub=nsub, final=final),
            grid_spec=pltpu.PrefetchScalarGridSpec(
                num_scalar_prefetch=1,
                grid=(bsz, n // tile, N_EXPERTS),
                in_specs=in_specs,
                out_specs=pl.BlockSpec((None, tile, D), lambda b, j, e, base: (b, j, 0)),
                scratch_shapes=[pltpu.VMEM((tile, D), F32)]),
            out_shape=jax.ShapeDtypeStruct((bsz, n, D), F32),
            compiler_params=_cparams(("parallel", "parallel", "arbitrary")),
            name="moe_combine",
        )(*args))
        row0 += cap
    return outs


def _rope_tables(n_tokens):
    rows = n_tokens // GRID_W
    row = jnp.repeat(jnp.arange(rows, dtype=F32), GRID_W)
    col = (jnp.arange(rows * GRID_W, dtype=I32) % GRID_W).astype(F32)
    inv = ROPE_BASE ** (-jnp.arange(ROPE_PAIRS, dtype=F32) / ROPE_PAIRS)
    ang = jnp.concatenate([row[:, None] * inv, col[:, None] * inv], axis=-1)
    cos, sin = jnp.cos(ang), jnp.sin(ang)
    reps = LANES // DA_HEAD_DIM
    cos_t = jnp.tile(jnp.concatenate([cos, cos], axis=-1), (1, reps))
    sin_t = jnp.tile(jnp.concatenate([-sin, sin], axis=-1), (1, reps))
    return cos_t, sin_t


def _lower_bounds(p):
    cum = jnp.cumsum(jax.nn.softmax(p.astype(F32), axis=0), axis=0)
    return cum - cum[0]


def _pad128(v):
    return jnp.pad(v.reshape(1, -1).astype(F32), ((0, 0), (0, LANES - v.size)))


def kernel(x, c, ctx, c_ctx, ada_w, ada_b, norm1_w, norm2_w, w_in, ssd_conv_w, ssd_conv_b, ssd_dt_bias,
           ssd_a_log, ssd_d, ssd_norm_w, hg_lb, hg_norm_w, da_lambda, da_norm_w, w_up, w_gate, b_gate,
           w_out, moe_router, moe_w1, moe_w3, moe_w2, final_norm_w):
    bsz, n_lat, _ = x.shape
    n_ctx = ctx.shape[1]
    depth = ada_w.shape[0]
    rope = _rope_tables(n_lat)
    lb_all = jnp.stack([_lower_bounds(hg_lb[0]), _lower_bounds(hg_lb[1])], axis=1)

    mod = _adaln(jnp.concatenate([c, c_ctx[None]], axis=0), ada_w, ada_b)
    mod = mod.reshape(depth, 8, 6, 1, D)

    xl, xc = x, ctx
    for l in range(depth):
        need_ctx = l < depth - 1
        lam_init = 0.8 - 0.6 * math.exp(-0.3 * l)
        mod_l = [mod[l, :bsz, k] for k in range(6)]
        mod_c = [jnp.broadcast_to(mod[l, bsz:bsz + 1, k], (bsz, 1, D)) for k in range(6)]

        wl = w_in[l]
        w_a = jnp.concatenate([wl[:, OFF_SSD_Z:OFF_SSD_XBC], wl[:, OFF_HG_Q:OFF_HG_F], wl[:, OFF_HG_I:OFF_DA_Q],
                               wl[:, OFF_DA_V:N_IN], wl[:, OFF_SSD_XBC:OFF_SSD_DT]], axis=1).astype(BF16)
        w_f = jnp.concatenate([wl[:, OFF_HG_F:OFF_HG_I], wl[:, OFF_SSD_DT:OFF_HG_Q],
                               jnp.zeros((D, PF_N - PF_DT - 2 * SSD_HEADS), F32)], axis=1).astype(BF16)
        w_qk = wl[:, OFF_DA_Q:OFF_DA_V].astype(BF16)
        alog128 = _pad128(ssd_a_log[l])
        bias128 = _pad128(ssd_dt_bias[l])
        d_full = jnp.repeat(ssd_d[l].astype(F32), SSD_D_INNER // SSD_HEADS).reshape(1, SSD_D_INNER)
        ssd_nw = ssd_norm_w[l].reshape(1, SSD_D_INNER).astype(F32)
        hg_nw = hg_norm_w[l].reshape(1, HG_HEAD_DIM).astype(F32)
        wg_b = w_gate[l].astype(BF16)
        bg = b_gate[l].reshape(1, N_BRANCH * D)
        wu_b = w_up[l].astype(BF16)
        wo_b = w_out[l].astype(BF16)

        def project(xs, m, use_rope):
            n = xs.shape[1]
            h = _modulate(xs, norm1_w[l], m[0], m[1])
            h2d = h.reshape(bsz * n, D)
            tm = min(1024, n)
            pa = _matmul(h2d, w_a, BF16, tm, 512).reshape(bsz, n, PA_N)
            pf = _matmul(h2d, w_f, F32, min(512, n), PF_N).reshape(bsz, n, PF_N)
            pqk = _matmul(h2d, w_qk, BF16, tm, 512, rope_tables=rope if use_rope else None).reshape(bsz, n, PQK_N)
            return h, pa, pf, pqk

        hl, pa_l, pf_l, pqk_l = project(xl, mod_l, True)
        hc, pa_c, pf_c, pqk_c = project(xc, mod_c, False)

        xa_c = _ssd_conv(pa_c, ssd_conv_w[l], ssd_conv_b[l])
        xa_l = _ssd_conv(pa_l, ssd_conv_w[l], ssd_conv_b[l])
        yc_f, s_f = _ssd_scan(xa_c, pf_c, alog128, bias128, rev=False)
        yl_f, _ = _ssd_scan(xa_l, pf_l, alog128, bias128, rev=False, init=s_f)
        ya_c, s_b = _ssd_scan(xa_c, pf_c, alog128, bias128, rev=True, readout=(pa_c, yc_f, d_full, ssd_nw))
        ya_l, _ = _ssd_scan(xa_l, pf_l, alog128, bias128, rev=True, init=s_b, readout=(pa_l, yl_f, d_full, ssd_nw))

        lb_f, lb_b = lb_all[l, 0].reshape(1, HG_WIDTH), lb_all[l, 1].reshape(1, HG_WIDTH)
        oc_f, t_f = _hg_scan(pa_c, pf_c, lb_f, rev=False)
        ol_f, _ = _hg_scan(pa_l, pf_l, lb_f, rev=False, init=t_f)
        yb_c, t_b = _hg_scan(pa_c, pf_c, lb_b, rev=True, readout=(oc_f, hg_nw))
        yb_l, _ = _hg_scan(pa_l, pf_l, lb_b, rev=True, init=t_b, readout=(ol_f, hg_nw))

        yd_l = _diff_attn(pqk_l, [(pqk_c, pa_c), (pqk_l, pa_l)], da_lambda[l], da_norm_w[l], lam_init, 512, 1024)

        xl = _merge(xl, mod_l[2], hl, ya_l, yb_l, yd_l, wg_b, bg, wu_b, wo_b)
        h2l, aff_l = _modulate(xl, norm2_w[l], mod_l[3], mod_l[4], w_router=moe_router[l])
        token_sets = [(xl, mod_l[5], h2l, aff_l, final_norm_w if l == depth - 1 else None)]
        if need_ctx:
            yd_c = _diff_attn(pqk_c, [(pqk_c, pa_c)], da_lambda[l], da_norm_w[l], lam_init, n_ctx, 512)
            xc = _merge(xc, mod_c[2], hc, ya_c, yb_c, yd_c, wg_b, bg, wu_b, wo_b)
            h2c, aff_c = _modulate(xc, norm2_w[l], mod_c[3], mod_c[4], w_router=moe_router[l])
            token_sets.append((xc, mod_c[5], h2c, aff_c, None))
        outs = _ec_moe(token_sets, moe_w1, moe_w3, moe_w2, l)
        xl = outs[0]
        if need_ctx:
            xc = outs[1]
    return xl
```

```python
import functools
import math

import jax
import jax.numpy as jnp
from jax import lax
from jax.experimental import pallas as pl
from jax.experimental.pallas import tpu as pltpu

F32 = jnp.float32
BF16 = jnp.bfloat16
I32 = jnp.int32
HIGHEST = lax.Precision.HIGHEST

D = 2048
EPS = 1e-6
GRID_W = 64
SSD_D_INNER = 1024
SSD_HEADS = 16
SSD_GROUPS = 2
SSD_STATE = 128
SSD_XBC = 1536
SSD_CONV_W = 5
HG_WIDTH = 1024
HG_HEADS = 8
HG_HEAD_DIM = 128
DA_HEADS = 8
DA_HEAD_DIM = 64
DA_WIDTH = 1024
ROPE_BASE = 10000.0
ROPE_PAIRS = 16
N_BRANCH = 3
N_EXPERTS = 16
EXPERT_FF = 2048
EC_CAPACITY = 2

OFF_SSD_Z = 0
OFF_SSD_XBC = 1024
OFF_SSD_DT = 2560
OFF_HG_Q = 2592
OFF_HG_F = OFF_HG_Q + 1024
OFF_HG_I = OFF_HG_F + 2048
OFF_HG_G = OFF_HG_I + 1024
OFF_DA_Q = OFF_HG_G + 1024
OFF_DA_K = OFF_DA_Q + 1024
OFF_DA_V = OFF_DA_K + 1024
N_IN = OFF_DA_V + 1024

LANES = 128
PA_Z, PA_HQ, PA_HI, PA_HG, PA_DV, PA_XBC = 0, 1024, 2048, 3072, 4096, 5120
PA_N = 6656
PF_F, PF_DT = 0, 2048
PF_N = 2176
PQK_N = 2048

SCAN_Q = 256
ATTN_TK = 1408
HALO = 16
TOK_BLK = 128
COMBINE_TILE = 512
GATHER_COLS = 512
EXPERT_GROUP = 4
SLOT_ALIGN = 16
VMEM_LIMIT = 56 * 1024 * 1024


def _cparams(sem, vmem=VMEM_LIMIT):
    return pltpu.CompilerParams(dimension_semantics=sem, vmem_limit_bytes=vmem)


def _silu(x):
    return x * jax.nn.sigmoid(x)


def _adaln_kernel(cb_ref, w_ref, b_ref, o_ref, *, n_rows, tn):
    nct = tn // LANES

    def body(k, accs):
        r0 = pl.multiple_of(k * 8, 8)
        ws = [w_ref[pl.ds(r0, 8), c * LANES:(c + 1) * LANES] for c in range(nct)]
        out = []
        for r in range(n_rows):
            s = _silu(cb_ref[r, pl.ds(r0, 8), :])
            out.append(tuple(accs[r][c] + s * ws[c] for c in range(nct)))
        return tuple(out)

    zero = jnp.zeros((8, LANES), F32)
    accs = lax.fori_loop(0, D // 8, body, tuple(tuple(zero for _ in range(nct)) for _ in range(n_rows)),
                         unroll=8)
    rows = []
    for r in range(n_rows):
        rows.append(jnp.concatenate([jnp.sum(accs[r][c], axis=0, keepdims=True) for c in range(nct)], axis=1))
    rows.append(jnp.zeros((8 - n_rows, tn), F32))
    o_ref[...] = jnp.concatenate(rows, axis=0) + b_ref[...]


def _adaln(cvec, ada_w, ada_b):
    n_rows = cvec.shape[0]
    depth, _, n6 = ada_w.shape
    tn = 512
    cb = jnp.broadcast_to(cvec[:, :, None], (n_rows, D, LANES))
    return pl.pallas_call(
        functools.partial(_adaln_kernel, n_rows=n_rows, tn=tn),
        grid=(depth, n6 // tn),
        in_specs=[pl.BlockSpec((n_rows, D, LANES), lambda l, j: (0, 0, 0)),
                  pl.BlockSpec((None, D, tn), lambda l, j: (l, 0, j)),
                  pl.BlockSpec((None, 1, tn), lambda l, j: (l, 0, j))],
        out_specs=pl.BlockSpec((None, 8, tn), lambda l, j: (l, 0, j)),
        out_shape=jax.ShapeDtypeStruct((depth, 8, n6), F32),
        compiler_params=_cparams(("parallel", "parallel")),
        name="adaln",
    )(cb, ada_w, ada_b.reshape(depth, 1, n6))


def _modulate_kernel(x_ref, nw_ref, sh_ref, sc_ref, *rest, with_router):
    x = x_ref[...]
    ms = jnp.mean(x * x, axis=-1, keepdims=True)
    h = (x * lax.rsqrt(ms + EPS) * nw_ref[...]) * (1.0 + sc_ref[...]) + sh_ref[...]
    if with_router:
        wr_ref, h_ref, aff_ref = rest
        logits = jnp.dot(h, wr_ref[...], preferred_element_type=F32, precision=HIGHEST)
        m = jnp.max(logits, axis=-1, keepdims=True)
        e = jnp.exp(logits - m)
        aff_ref[...] = e / jnp.sum(e, axis=-1, keepdims=True)
    else:
        (h_ref,) = rest
    h_ref[...] = h.astype(BF16)


def _modulate(x, nw, shift, scale, w_router=None):
    bsz, n, _ = x.shape
    tm = min(512, n)
    with_router = w_router is not None
    in_specs = [pl.BlockSpec((None, tm, D), lambda b, i: (b, i, 0)),
                pl.BlockSpec((1, D), lambda b, i: (0, 0)),
                pl.BlockSpec((None, 1, D), lambda b, i: (b, 0, 0)),
                pl.BlockSpec((None, 1, D), lambda b, i: (b, 0, 0))]
    args = [x, nw.reshape(1, D), shift, scale]
    out_specs = [pl.BlockSpec((None, tm, D), lambda b, i: (b, i, 0))]
    out_shape = [jax.ShapeDtypeStruct((bsz, n, D), BF16)]
    if with_router:
        in_specs.append(pl.BlockSpec((D, N_EXPERTS), lambda b, i: (0, 0)))
        args.append(w_router)
        out_specs.append(pl.BlockSpec((None, tm, N_EXPERTS), lambda b, i: (b, i, 0)))
        out_shape.append(jax.ShapeDtypeStruct((bsz, n, N_EXPERTS), F32))
    res = pl.pallas_call(
        functools.partial(_modulate_kernel, with_router=with_router),
        grid=(bsz, n // tm),
        in_specs=in_specs, out_specs=out_specs, out_shape=out_shape,
        compiler_params=_cparams(("parallel", "parallel")),
        name="modulate_router" if with_router else "modulate",
    )(*args)
    return res if with_router else res[0]


def _mm_kernel(a_ref, w_ref, *rest, rope, tn):
    acc = jnp.dot(a_ref[...], w_ref[...], preferred_element_type=F32)
    if rope:
        cos_ref, sin_ref, o_ref = rest
        cos = cos_ref[...]
        sin = sin_ref[...]
        lane = lax.broadcasted_iota(I32, cos.shape, 1)
        first_half = (lane % DA_HEAD_DIM) < (DA_HEAD_DIM // 2)
        for c in range(tn // LANES):
            xs = acc[:, c * LANES:(c + 1) * LANES]
            partner = jnp.where(first_half,
                                pltpu.roll(xs, LANES - DA_HEAD_DIM // 2, 1),
                                pltpu.roll(xs, DA_HEAD_DIM // 2, 1))
            o_ref[:, c * LANES:(c + 1) * LANES] = (xs * cos + partner * sin).astype(o_ref.dtype)
    else:
        (o_ref,) = rest
        o_ref[...] = acc.astype(o_ref.dtype)


def _matmul(a, w, out_dtype, tm, tn, rope_tables=None):
    m, k = a.shape
    n = w.shape[1]
    rope = rope_tables is not None
    in_specs = [pl.BlockSpec((tm, k), lambda i, j: (i, 0)),
                pl.BlockSpec((k, tn), lambda i, j: (0, j))]
    args = [a, w]
    if rope:
        cos, sin = rope_tables
        nt = cos.shape[0] // tm
        in_specs += [pl.BlockSpec((tm, LANES), lambda i, j: (i % nt, 0)),
                     pl.BlockSpec((tm, LANES), lambda i, j: (i % nt, 0))]
        args += [cos, sin]
    return pl.pallas_call(
        functools.partial(_mm_kernel, rope=rope, tn=tn),
        grid=(m // tm, n // tn),
        in_specs=in_specs,
        out_specs=pl.BlockSpec((tm, tn), lambda i, j: (i, j)),
        out_shape=jax.ShapeDtypeStruct((m, n), out_dtype),
        compiler_params=_cparams(("parallel", "parallel")),
        name="proj_rope" if rope else "proj",
    )(*args)


def _conv_kernel(prev_ref, cur_ref, next_ref, w_ref, b_ref, o_ref, *, tc):
    i = pl.program_id(1)
    last = pl.num_programs(1) - 1
    pad = (SSD_CONV_W - 1) // 2
    prev = jnp.where(i > 0, prev_ref[...].astype(F32), 0.0)
    nxt = jnp.where(i < last, next_ref[...].astype(F32), 0.0)
    ext = jnp.concatenate([prev, cur_ref[...].astype(F32), nxt], axis=0)
    w = w_ref[...]
    y = b_ref[...] + w[0:1] * ext[HALO - pad:HALO - pad + tc]
    for k in range(1, SSD_CONV_W):
        y = y + w[k:k + 1] * ext[HALO - pad + k:HALO - pad + k + tc]
    o_ref[...] = _silu(y).astype(o_ref.dtype)


def _ssd_conv(pa, conv_w, conv_b):
    bsz, n, _ = pa.shape
    tc = min(512, n)
    tw = 512
    c0 = PA_XBC // tw
    nbh = n // HALO
    return pl.pallas_call(
        functools.partial(_conv_kernel, tc=tc),
        grid=(bsz, n // tc, SSD_XBC // tw),
        in_specs=[pl.BlockSpec((None, HALO, tw), lambda b, i, j: (b, jnp.maximum(i * (tc // HALO) - 1, 0), c0 + j)),
                  pl.BlockSpec((None, tc, tw), lambda b, i, j: (b, i, c0 + j)),
                  pl.BlockSpec((None, HALO, tw),
                               lambda b, i, j: (b, jnp.minimum((i + 1) * (tc // HALO), nbh - 1), c0 + j)),
                  pl.BlockSpec((SSD_CONV_W, tw), lambda b, i, j: (0, j)),
                  pl.BlockSpec((1, tw), lambda b, i, j: (0, j))],
        out_specs=pl.BlockSpec((None, tc, tw), lambda b, i, j: (b, i, j)),
        out_shape=jax.ShapeDtypeStruct((bsz, n, SSD_XBC), BF16),
        compiler_params=_cparams(("parallel", "parallel", "parallel")),
        name="ssd_conv",
    )(pa, pa, pa, conv_w, conv_b.reshape(1, SSD_XBC))


def _tri(q, rev):
    r = lax.broadcasted_iota(I32, (q, q), 0)
    c = lax.broadcasted_iota(I32, (q, q), 1)
    return (c >= r) if rev else (r >= c)


def _split3(x):
    hi = x.astype(BF16)
    r = x - hi.astype(F32)
    mid = r.astype(BF16)
    lo = (r - mid.astype(F32)).astype(BF16)
    return hi, mid, lo


def _select_rows_dot(sel01, x):
    hi, mid, lo = _split3(x)
    return (jnp.dot(sel01, hi, preferred_element_type=F32) + jnp.dot(sel01, mid, preferred_element_type=F32)
            + jnp.dot(sel01, lo, preferred_element_type=F32))


def _select_cols_dot(x, sel01):
    hi, mid, lo = _split3(x)
    return (jnp.dot(hi, sel01, preferred_element_type=F32) + jnp.dot(mid, sel01, preferred_element_type=F32)
            + jnp.dot(lo, sel01, preferred_element_type=F32))


def _ssd_scan_kernel(*refs, rev, has_init, readout):
    it = iter(refs)
    xact_ref, dt_ref, alog_ref, bias_ref = next(it), next(it), next(it), next(it)
    init_ref = next(it) if has_init else None
    if readout:
        z_ref, yprev_ref, d_ref, nw_ref = next(it), next(it), next(it), next(it)
    y_ref, st_ref = next(it), next(it)
    ybuf_ref = next(it) if readout else None

    q = SCAN_Q
    hd = SSD_D_INNER // SSD_HEADS
    hg = SSD_HEADS // SSD_GROUPS
    gw = hg * hd
    d_off = SSD_HEADS if rev else 0

    @pl.when(pl.program_id(1) == 0)
    def _():
        st_ref[...] = init_ref[...] if has_init else jnp.zeros_like(st_ref)

    tri = _tri(q, rev)
    dtv = jax.nn.softplus(dt_ref[...] + bias_ref[...])
    da = dtv * (-jnp.exp(alog_ref[...]))
    cs = _select_rows_dot(jnp.where(tri, 1.0, 0.0).astype(BF16), da)
    cs_t = cs.T
    erow = lax.broadcasted_iota(I32, (LANES, SSD_D_INNER), 0)
    ecol = lax.broadcasted_iota(I32, (LANES, SSD_D_INNER), 1)
    expand = jnp.where(erow == d_off + ecol // hd, 1.0, 0.0).astype(BF16)
    csx = _select_cols_dot(cs, expand)
    dtx = _select_cols_dot(dtv, expand)
    x = xact_ref[:, :SSD_D_INNER].astype(F32)
    xdt = x * dtx
    xdt_b = xdt.astype(BF16)
    cs_end = csx[0:1] if rev else csx[q - 1:q]
    ecs = jnp.exp(csx)
    xw = (xdt * jnp.exp(cs_end - csx)).astype(BF16)
    ecs_end = jnp.exp(cs_end)
    lane = lax.broadcasted_iota(I32, (q, LANES), 1)
    out_ref = ybuf_ref if readout else y_ref

    for g in range(SSD_GROUPS):
        bg = xact_ref[:, SSD_D_INNER + g * SSD_STATE:SSD_D_INNER + (g + 1) * SSD_STATE]
        cg = xact_ref[:, SSD_D_INNER + (SSD_GROUPS + g) * SSD_STATE:SSD_D_INNER + (SSD_GROUPS + g + 1) * SSD_STATE]
        cb = lax.dot_general(cg, bg, (((1,), (1,)), ((), ())), preferred_element_type=F32)
        st_g = st_ref[:, g * gw:(g + 1) * gw]
        y_inter = jnp.dot(cg, st_g.astype(BF16), preferred_element_type=F32) * ecs[:, g * gw:(g + 1) * gw]
        for jj in range(hg // 2):
            col0 = g * gw + jj * LANES
            xp = xdt_b[:, col0:col0 + LANES]
            acc = y_inter[:, jj * LANES:(jj + 1) * LANES]
            for s in range(2):
                j = d_off + g * hg + 2 * jj + s
                e = cs[:, j:j + 1] - cs_t[j:j + 1, :]
                seg = jnp.exp(jnp.where(tri, e, -jnp.inf))
                m = (cb * seg).astype(BF16)
                xm = jnp.where((lane >= hd) if s else (lane < hd), xp, jnp.zeros_like(xp))
                acc = acc + jnp.dot(m, xm, preferred_element_type=F32)
            out_ref[:, col0:col0 + LANES] = acc
        upd = lax.dot_general(bg, xw[:, g * gw:(g + 1) * gw], (((0,), (0,)), ((), ())),
                              preferred_element_type=F32)
        st_ref[:, g * gw:(g + 1) * gw] = st_g * ecs_end[:, g * gw:(g + 1) * gw] + upd

    if readout:
        z = z_ref[...].astype(F32)
        yy = (yprev_ref[...] + ybuf_ref[...] + d_ref[...] * x) * _silu(z)
        for g in range(SSD_GROUPS):
            seg = yy[:, g * gw:(g + 1) * gw]
            ms = jnp.mean(seg * seg, axis=-1, keepdims=True)
            y_ref[:, g * gw:(g + 1) * gw] = (seg * lax.rsqrt(ms + EPS) * nw_ref[:, g * gw:(g + 1) * gw]).astype(y_ref.dtype)


def _ssd_scan(xact, pf, alog128, bias128, *, rev, init=None, readout=None):
    bsz, n, _ = xact.shape
    nc = n // SCAN_Q
    cidx = (lambda c: nc - 1 - c) if rev else (lambda c: c)
    in_specs = [pl.BlockSpec((None, SCAN_Q, SSD_XBC), lambda b, c: (b, cidx(c), 0)),
                pl.BlockSpec((None, SCAN_Q, LANES), lambda b, c: (b, cidx(c), PF_DT // LANES)),
                pl.BlockSpec((1, LANES), lambda b, c: (0, 0)),
                pl.BlockSpec((1, LANES), lambda b, c: (0, 0))]
    args = [xact, pf, alog128, bias128]
    if init is not None:
        in_specs.append(pl.BlockSpec((None, SSD_STATE, SSD_D_INNER), lambda b, c: (b, 0, 0)))
        args.append(init)
    scratch = []
    if readout is not None:
        pa, yprev, d_full, norm_w = readout
        in_specs += [pl.BlockSpec((None, SCAN_Q, SSD_D_INNER), lambda b, c: (b, cidx(c), PA_Z // SSD_D_INNER)),
                     pl.BlockSpec((None, SCAN_Q, SSD_D_INNER), lambda b, c: (b, cidx(c), 0)),
                     pl.BlockSpec((1, SSD_D_INNER), lambda b, c: (0, 0)),
                     pl.BlockSpec((1, SSD_D_INNER), lambda b, c: (0, 0))]
        args += [pa, yprev, d_full, norm_w]
        scratch = [pltpu.VMEM((SCAN_Q, SSD_D_INNER), F32)]
    y_dtype = BF16 if readout is not None else F32
    y, st = pl.pallas_call(
        functools.partial(_ssd_scan_kernel, rev=rev, has_init=init is not None, readout=readout is not None),
        grid=(bsz, nc),
        in_specs=in_specs,
        out_specs=[pl.BlockSpec((None, SCAN_Q, SSD_D_INNER), lambda b, c: (b, cidx(c), 0)),
                   pl.BlockSpec((None, SSD_STATE, SSD_D_INNER), lambda b, c: (b, 0, 0))],
        out_shape=[jax.ShapeDtypeStruct((bsz, n, SSD_D_INNER), y_dtype),
                   jax.ShapeDtypeStruct((bsz, SSD_STATE, SSD_D_INNER), F32)],
        scratch_shapes=scratch,
        compiler_params=_cparams(("parallel", "arbitrary")),
        name="ssd_scan_bwd" if rev else "ssd_scan_fwd",
    )(*args)
    return y, st


def _ref_rows(bb, s, rev):
    q, w = bb.shape
    rl = s if rev else s - 1
    if 2 * s >= 8:
        b3 = bb.reshape(q // (2 * s), 2 * s, w)
        return jnp.broadcast_to(b3[:, rl:rl + 1, :], b3.shape).reshape(q, w)
    off = lax.broadcasted_iota(I32, (q, w), 0) % (2 * s)
    out = bb
    for ov in range(2 * s):
        if ov != rl:
            out = jnp.where(off == ov, pltpu.roll(bb, (ov - rl) % q, 0), out)
    return out


def _hg_scan_kernel(*refs, rev, has_init, readout):
    it = iter(refs)
    q_ref, f_ref, v_ref, lb_ref = next(it), next(it), next(it), next(it)
    init_ref = next(it) if has_init else None
    if readout:
        g_ref, oprev_ref, nw_ref = next(it), next(it), next(it)
    o_ref, st_ref, att_ref = next(it), next(it), next(it)
    obuf_ref = next(it) if readout else None

    q = SCAN_Q
    hd = HG_HEAD_DIM

    @pl.when(pl.program_id(1) == 0)
    def _():
        st_ref[...] = init_ref[...] if has_init else jnp.zeros_like(st_ref)

    lb = lb_ref[...]
    qq = _silu(q_ref[...].astype(F32))
    f = lb + (1.0 - lb) * jax.nn.sigmoid(f_ref[...])
    kk = 1.0 - f
    lf = jnp.log(f)
    tri = _tri(q, rev)
    bb = _select_rows_dot(jnp.where(tri, 1.0, 0.0).astype(BF16), lf)

    row = lax.broadcasted_iota(I32, (q, HG_WIDTH), 0)
    ri = lax.broadcasted_iota(I32, (q, q), 0)
    ci = lax.broadcasted_iota(I32, (q, q), 1)
    xr = ri ^ ci
    causal = (ci > ri) if rev else (ri > ci)
    att_ref[...] = jnp.zeros_like(att_ref)
    s = q // 2
    while s >= 1:
        upper = (row & s) != 0
        q_side = jnp.logical_not(upper) if rev else upper
        ex = jnp.exp(-jnp.abs(bb - _ref_rows(bb, s, rev)))
        u = (jnp.where(q_side, qq, kk) * ex).astype(BF16)
        level = jnp.logical_and((xr >> int(math.log2(s))) == 1, causal)
        for h in range(HG_HEADS):
            uh = u[:, h * hd:(h + 1) * hd]
            p = lax.dot_general(uh, uh, (((1,), (1,)), ((), ())), preferred_element_type=F32)
            att_ref[h] = jnp.where(level, p, att_ref[h])
        s //= 2

    b_end = bb[0:1] if rev else bb[q - 1:q]
    qe = (qq * jnp.exp(bb)).astype(BF16)
    kh = (kk * jnp.exp(b_end - bb)).astype(BF16)
    qk = (qq * kk).astype(BF16)
    e_end = jnp.exp(b_end)
    ones = jnp.ones((hd, hd), BF16)
    out_ref = obuf_ref if readout else o_ref
    for h in range(HG_HEADS):
        sl = slice(h * hd, (h + 1) * hd)
        vh = v_ref[:, sl]
        st_h = st_ref[sl, :]
        o = jnp.dot(att_ref[h].astype(BF16), vh, preferred_element_type=F32)
        o = o + jnp.dot(qk[:, sl], ones, preferred_element_type=F32) * vh.astype(F32)
        o = o + lax.dot_general(qe[:, sl], st_h.astype(BF16), (((1,), (1,)), ((), ())), preferred_element_type=F32)
        out_ref[:, sl] = o
        upd = lax.dot_general(vh, kh[:, sl], (((0,), (0,)), ((), ())), preferred_element_type=F32)
        st_ref[sl, :] = st_h * e_end[:, sl] + upd

    if readout:
        for h in range(HG_HEADS):
            sl = slice(h * hd, (h + 1) * hd)
            o = oprev_ref[:, sl] + obuf_ref[:, sl]
            ms = jnp.mean(o * o, axis=-1, keepdims=True)
            gate = _silu(g_ref[:, sl].astype(F32))
            o_ref[:, sl] = (o * lax.rsqrt(ms + EPS) * nw_ref[...] * gate).astype(o_ref.dtype)


def _hg_scan(pa, pf, lb_row, *, rev, init=None, readout=None):
    bsz, n, _ = pa.shape
    nc = n // SCAN_Q
    cidx = (lambda c: nc - 1 - c) if rev else (lambda c: c)
    fcol = (PF_F // HG_WIDTH) + (1 if rev else 0)
    in_specs = [pl.BlockSpec((None, SCAN_Q, HG_WIDTH), lambda b, c: (b, cidx(c), PA_HQ // HG_WIDTH)),
                pl.BlockSpec((None, SCAN_Q, HG_WIDTH), lambda b, c: (b, cidx(c), fcol)),
                pl.BlockSpec((None, SCAN_Q, HG_WIDTH), lambda b, c: (b, cidx(c), PA_HI // HG_WIDTH)),
                pl.BlockSpec((1, HG_WIDTH), lambda b, c: (0, 0))]
    args = [pa, pf, pa, lb_row]
    if init is not None:
        in_specs.append(pl.BlockSpec((None, HG_WIDTH, HG_HEAD_DIM), lambda b, c: (b, 0, 0)))
        args.append(init)
    scratch = [pltpu.VMEM((HG_HEADS, SCAN_Q, SCAN_Q), F32)]
    if readout is not None:
        oprev, nw128 = readout
        in_specs += [pl.BlockSpec((None, SCAN_Q, HG_WIDTH), lambda b, c: (b, cidx(c), PA_HG // HG_WIDTH)),
                     pl.BlockSpec((None, SCAN_Q, HG_WIDTH), lambda b, c: (b, cidx(c), 0)),
                     pl.BlockSpec((1, HG_HEAD_DIM), lambda b, c: (0, 0))]
        args += [pa, oprev, nw128]
        scratch.append(pltpu.VMEM((SCAN_Q, HG_WIDTH), F32))
    o_dtype = BF16 if readout is not None else F32
    assert PA_HQ % HG_WIDTH == 0 or True
    o, st = pl.pallas_call(
        functools.partial(_hg_scan_kernel, rev=rev, has_init=init is not None, readout=readout is not None),
        grid=(bsz, nc),
        in_specs=in_specs,
        out_specs=[pl.BlockSpec((None, SCAN_Q, HG_WIDTH), lambda b, c: (b, cidx(c), 0)),
                   pl.BlockSpec((None, HG_WIDTH, HG_HEAD_DIM), lambda b, c: (b, 0, 0))],
        out_shape=[jax.ShapeDtypeStruct((bsz, n, HG_WIDTH), o_dtype),
                   jax.ShapeDtypeStruct((bsz, HG_WIDTH, HG_HEAD_DIM), F32)],
        scratch_shapes=scratch,
        compiler_params=_cparams(("parallel", "arbitrary")),
        name="hg_scan_bwd" if rev else "hg_scan_fwd",
    )(*args)
    return o, st


def _attn_kernel(*refs, seg_lens, tk, lam_init):
    lam_ref, nw_ref, q_ref = refs[0], refs[1], refs[2]
    nseg = len(seg_lens)
    kv_refs = refs[3:3 + 2 * nseg]
    o_ref, qs_ref, sa_ref, sb_ref, m_ref, mn_ref, l_ref, acc_ref = refs[3 + 2 * nseg:]
    tq = q_ref.shape[0]
    qv = q_ref[...] * jnp.asarray(DA_HEAD_DIM ** -0.5, BF16)
    lane = lax.broadcasted_iota(I32, qv.shape, 1)
    zero = jnp.zeros_like(qv)
    qs_ref[:tq] = jnp.where(lane < DA_HEAD_DIM, qv, zero)
    qs_ref[tq:] = jnp.where(lane >= DA_HEAD_DIM, qv, zero)
    m_ref[...] = jnp.full_like(m_ref, -jnp.inf)
    l_ref[...] = jnp.zeros_like(l_ref)
    acc_ref[...] = jnp.zeros_like(acc_ref)

    def scores(kc):
        return lax.dot_general(qs_ref[...], kc, (((1,), (1,)), ((), ())), preferred_element_type=F32)

    def lane_tiles(a):
        return [a[:, c * LANES:(c + 1) * LANES] for c in range(a.shape[1] // LANES)]

    def row_max(s, m_prev):
        tiles = lane_tiles(s)
        m = tiles[0]
        for tile in tiles[1:]:
            m = jnp.maximum(m, tile)
        return jnp.maximum(m_prev, jnp.broadcast_to(jnp.max(m, axis=-1, keepdims=True), m_prev.shape))

    def online_step(s, vc, m_prev, m_cur):
        alpha = jnp.exp(m_prev - m_cur)
        ps = [jnp.exp(tile - m_cur) for tile in lane_tiles(s)]
        l = alpha * l_ref[...]
        for p in ps:
            l = l + p
        l_ref[...] = l
        pv = jnp.dot(jnp.concatenate(ps, axis=1).astype(BF16), vc, preferred_element_type=F32)
        acc_ref[...] = alpha * acc_ref[...] + pv

    for si, n in enumerate(seg_lens):
        k_ref, v_ref = kv_refs[2 * si], kv_refs[2 * si + 1]
        t = min(tk, n)
        nc = n // t
        if nc == 1:
            s = scores(k_ref[...])
            m_prev = m_ref[...]
            m_cur = row_max(s, m_prev)
            online_step(s, v_ref[...], m_prev, m_cur)
            m_ref[...] = m_cur
        else:
            s0 = scores(k_ref[pl.ds(0, t), :])
            sa_ref[...] = s0
            mn_ref[...] = row_max(s0, m_ref[...])

            def pipe_step(c, cur_ref, nxt_ref, k_ref=k_ref, v_ref=v_ref, t=t):
                r0 = pl.multiple_of(c * t, t)
                r1 = pl.multiple_of(r0 + t, t)
                m_prev, m_cur = m_ref[...], mn_ref[...]
                s_next = scores(k_ref[pl.ds(r1, t), :])
                nxt_ref[...] = s_next
                mn_ref[...] = row_max(s_next, m_cur)
                m_ref[...] = m_cur
                online_step(cur_ref[...], v_ref[pl.ds(r0, t), :], m_prev, m_cur)

            def body(i, carry):
                pipe_step(2 * i, sa_ref, sb_ref)
                pipe_step(2 * i + 1, sb_ref, sa_ref)
                return carry

            lax.fori_loop(0, (nc - 1) // 2, body, 0)
            last_ref = sa_ref
            if nc % 2 == 0:
                pipe_step(nc - 2, sa_ref, sb_ref)
                last_ref = sb_ref
            m_prev, m_cur = m_ref[...], mn_ref[...]
            online_step(last_ref[...], v_ref[pl.ds((nc - 1) * t, t), :], m_prev, m_cur)
            m_ref[...] = m_cur
    l = jnp.sum(l_ref[...], axis=-1, keepdims=True)
    acc = acc_ref[...]
    lam = lam_ref[...]
    lmbda = (jnp.exp(jnp.sum(lam[0:1] * lam[1:2], axis=-1, keepdims=True))
             - jnp.exp(jnp.sum(lam[2:3] * lam[3:4], axis=-1, keepdims=True)) + lam_init)
    o = acc[:tq] / l[:tq] - lmbda * (acc[tq:] / l[tq:])
    ms = jnp.mean(o * o, axis=-1, keepdims=True)
    o_ref[...] = (o * lax.rsqrt(ms + EPS) * nw_ref[...] * (1.0 - lam_init)).astype(o_ref.dtype)


def _diff_attn(pqk_q, kv_segs, lam, nw, lam_init, tq, tk):
    bsz, nq, _ = pqk_q.shape
    hw = 2 * DA_HEAD_DIM
    in_specs = [pl.BlockSpec((4, DA_HEAD_DIM), lambda b, h, i: (0, 0)),
                pl.BlockSpec((1, hw), lambda b, h, i: (0, 0)),
                pl.BlockSpec((None, tq, hw), lambda b, h, i: (b, i, h))]
    args = [lam, nw.reshape(1, hw), pqk_q]
    seg_lens = []
    for k_arr, k_col, v_arr, v_col in kv_segs:
        nk = k_arr.shape[1]
        seg_lens.append(nk)
        in_specs += [pl.BlockSpec((None, nk, hw), functools.partial(lambda b, h, i, c: (b, 0, c + h), c=k_col // hw)),
                     pl.BlockSpec((None, nk, hw), functools.partial(lambda b, h, i, c: (b, 0, c + h), c=v_col // hw))]
        args += [k_arr, v_arr]
    return pl.pallas_call(
        functools.partial(_attn_kernel, seg_lens=tuple(seg_lens), tk=tk, lam_init=lam_init),
        grid=(bsz, DA_HEADS, nq // tq),
        in_specs=in_specs,
        out_specs=pl.BlockSpec((None, tq, hw), lambda b, h, i: (b, i, h)),
        out_shape=jax.ShapeDtypeStruct((bsz, nq, DA_WIDTH), BF16),
        scratch_shapes=[pltpu.VMEM((2 * tq, hw), BF16),
                        pltpu.VMEM((2 * tq, tk), F32),
                        pltpu.VMEM((2 * tq, tk), F32),
                        pltpu.VMEM((2 * tq, LANES), F32),
                        pltpu.VMEM((2 * tq, LANES), F32),
                        pltpu.VMEM((2 * tq, LANES), F32),
                        pltpu.VMEM((2 * tq, hw), F32)],
        compiler_params=_cparams(("parallel", "parallel", "parallel")),
        name="diff_attn",
    )(*args)


def _mix_kernel(h_ref, ya_ref, yb_ref, yd_ref, wg0, wg1, wg2, bg0, bg1, bg2, wu0, wu1, wu2, o_ref):
    h = h_ref[...]
    mix = None
    for y_ref, wg, bg, wu in ((ya_ref, wg0, bg0, wu0), (yb_ref, wg1, bg1, wu1), (yd_ref, wg2, bg2, wu2)):
        gate = jax.nn.sigmoid(jnp.dot(h, wg[...], preferred_element_type=F32) + bg[...])
        up = jnp.dot(y_ref[...], wu[...], preferred_element_type=F32)
        mix = gate * up if mix is None else mix + gate * up
    o_ref[...] = mix.astype(o_ref.dtype)


def _residual_out_kernel(x_ref, g1_ref, m_ref, wo_ref, o_ref):
    o_ref[...] = x_ref[...] + g1_ref[...] * jnp.dot(m_ref[...], wo_ref[...], preferred_element_type=F32)


def _merge(x, gate1, h, ya, yb, yd, w_gate, b_gate, w_up, w_out):
    bsz, n, _ = x.shape
    tm = min(1024, n)
    tn = 512
    nn = D // tn
    row = lambda b, i, j: (b, i, 0)
    in_specs = [pl.BlockSpec((None, tm, D), row),
                pl.BlockSpec((None, tm, 1024), row),
                pl.BlockSpec((None, tm, 1024), row),
                pl.BlockSpec((None, tm, 1024), row)]
    in_specs += [pl.BlockSpec((D, tn), functools.partial(lambda b, i, j, k: (0, k * nn + j), k=k)) for k in range(3)]
    in_specs += [pl.BlockSpec((1, tn), functools.partial(lambda b, i, j, k: (0, k * nn + j), k=k)) for k in range(3)]
    in_specs += [pl.BlockSpec((None, 1024, tn), functools.partial(lambda b, i, j, k: (k, 0, j), k=k)) for k in range(3)]
    mix = pl.pallas_call(
        _mix_kernel,
        grid=(bsz, n // tm, nn),
        in_specs=in_specs,
        out_specs=pl.BlockSpec((None, tm, tn), lambda b, i, j: (b, i, j)),
        out_shape=jax.ShapeDtypeStruct((bsz, n, D), BF16),
        compiler_params=_cparams(("parallel", "parallel", "parallel")),
        name="merge_mix",
    )(h, ya, yb, yd, w_gate, w_gate, w_gate, b_gate, b_gate, b_gate, w_up, w_up, w_up)
    return pl.pallas_call(
        _residual_out_kernel,
        grid=(bsz, n // tm, nn),
        in_specs=[pl.BlockSpec((None, tm, tn), lambda b, i, j: (b, i, j)),
                  pl.BlockSpec((None, 1, tn), lambda b, i, j: (b, 0, j)),
                  pl.BlockSpec((None, tm, D), row),
                  pl.BlockSpec((D, tn), lambda b, i, j: (0, j))],
        out_specs=pl.BlockSpec((None, tm, tn), lambda b, i, j: (b, i, j)),
        out_shape=jax.ShapeDtypeStruct((bsz, n, D), F32),
        compiler_params=_cparams(("parallel", "parallel", "parallel")),
        name="merge_out",
    )(x, gate1, mix, w_out)


def _route_kernel(aff_ref, posm_ref, pos_ref, *, cap, rpe):
    rows = N_EXPERTS * rpe
    bits = pltpu.bitcast(aff_ref[...], I32)
    ones = jnp.ones((LANES, LANES), BF16)
    r = lax.broadcasted_iota(I32, (rows, rows), 0)
    c = lax.broadcasted_iota(I32, (rows, rows), 1)
    same = (r // rpe) == (c // rpe)
    grp = same.astype(BF16)
    grp_before = jnp.logical_and(same, c < r).astype(BF16)
    lr = lax.broadcasted_iota(I32, (LANES, LANES), 0)
    lc = lax.broadcasted_iota(I32, (LANES, LANES), 1)
    before = (lr < lc).astype(BF16)

    def count(mask):
        per_row = jnp.dot(mask.astype(BF16), ones, preferred_element_type=F32)
        return jnp.dot(grp, per_row.astype(BF16), preferred_element_type=F32)

    def excl_prefix(mask):
        mb = mask.astype(BF16)
        within = jnp.dot(mb, before, preferred_element_type=F32)
        per_row = jnp.dot(mb, ones, preferred_element_type=F32)
        return within + jnp.dot(grp_before, per_row.astype(BF16), preferred_element_type=F32)

    def body(i, thr):
        cand = thr | (jnp.int32(1) << (30 - i))
        return jnp.where(count(bits >= cand) >= cap, cand, thr)

    thr = lax.fori_loop(0, 31, body, jnp.zeros((rows, LANES), I32))
    gt = bits > thr
    eq = bits == thr
    need = cap - count(gt)
    sel = jnp.logical_or(gt, jnp.logical_and(eq, excl_prefix(eq) < need))
    pos = excl_prefix(sel).astype(I32)
    pos_ref[...] = pos
    posm_ref[...] = jnp.where(sel, pos, -1)


def _route(aff, cap):
    bsz, n, _ = aff.shape
    rpe = n // LANES
    rows = N_EXPERTS * rpe
    aff_t = jnp.swapaxes(aff, 1, 2).reshape(bsz, rows, LANES)
    spec = pl.BlockSpec((None, rows, LANES), lambda b: (b, 0, 0))
    posm, pos = pl.pallas_call(
        functools.partial(_route_kernel, cap=cap, rpe=rpe),
        grid=(bsz,),
        in_specs=[spec], out_specs=[spec, spec],
        out_shape=[jax.ShapeDtypeStruct((bsz, rows, LANES), I32)] * 2,
        compiler_params=_cparams(("parallel",)),
        name="route",
    )(aff_t)
    return posm.reshape(bsz, N_EXPERTS, n), pos.reshape(bsz, N_EXPERTS, n)


def _gather_kernel(base_ref, h_ref, posm_ref, xg_ref, *, win, nblk):
    b = pl.program_id(0)
    xg_ref[...] = jnp.zeros_like(xg_ref)
    slot = lax.broadcasted_iota(I32, (win, TOK_BLK), 0)

    def body(j, carry):
        t0 = pl.multiple_of(j * TOK_BLK, TOK_BLK)
        bases, onehots = [], []
        for e in range(N_EXPERTS):
            base = pl.multiple_of(base_ref[(b * N_EXPERTS + e) * nblk + j], SLOT_ALIGN)
            rel = posm_ref[e:e + 1, pl.ds(t0, TOK_BLK)] - base
            bases.append(base)
            onehots.append(jnp.where(slot == rel, 1.0, 0.0).astype(BF16))
        rows = jnp.dot(jnp.concatenate(onehots, axis=0), h_ref[pl.ds(t0, TOK_BLK), :],
                       preferred_element_type=F32).astype(BF16)
        for e in range(N_EXPERTS):
            xg_ref[e, pl.ds(bases[e], win), :] += rows[e * win:(e + 1) * win]
        return carry

    lax.fori_loop(0, nblk, body, 0)


def _expert_kernel(xg_ref, w1_ref, w3_ref, w2_ref, y_ref, acc_ref):
    f = pl.program_id(2)

    @pl.when(f == 0)
    def _():
        acc_ref[...] = jnp.zeros_like(acc_ref)

    xg = xg_ref[...]
    a = jnp.dot(xg, w1_ref[...].astype(BF16), preferred_element_type=F32)
    g = jnp.dot(xg, w3_ref[...].astype(BF16), preferred_element_type=F32)
    acc_ref[...] += jnp.dot((_silu(a) * g).astype(BF16), w2_ref[...].astype(BF16), preferred_element_type=F32)

    @pl.when(f == pl.num_programs(2) - 1)
    def _():
        y_ref[...] = acc_ref[...].astype(y_ref.dtype)


def _combine_kernel(base_ref, x_ref, g2_ref, aff_ref, posm_ref, *rest, win, win_tile, cap, nblk, nsub, final):
    y_refs = rest[:EXPERT_GROUP]
    if final:
        fw_ref, o_ref, acc_ref = rest[EXPERT_GROUP:]
    else:
        o_ref, acc_ref = rest[EXPERT_GROUP:]
    b, jt, eg = pl.program_id(0), pl.program_id(1), pl.program_id(2)

    @pl.when(eg == 0)
    def _():
        acc_ref[...] = jnp.zeros_like(acc_ref)

    lane = lax.broadcasted_iota(I32, (TOK_BLK, N_EXPERTS), 1)
    slot = lax.broadcasted_iota(I32, (TOK_BLK, win), 1)
    for sb in range(nsub):
        rows = slice(sb * TOK_BLK, (sb + 1) * TOK_BLK)
        total = None
        for k, y_ref in enumerate(y_refs):
            e = eg * EXPERT_GROUP + k
            first = (b * N_EXPERTS + e) * nblk + jt * nsub
            tile_base = jnp.minimum(base_ref[first], cap - win_tile)
            base = base_ref[first + sb]
            mine = lane == e
            rel = jnp.sum(jnp.where(mine, posm_ref[rows, :], 0), axis=-1, keepdims=True) - base
            val = jnp.sum(jnp.where(mine, aff_ref[rows, :], 0.0), axis=-1, keepdims=True)
            onehot = jnp.where(slot == rel, 1.0, 0.0).astype(BF16)
            off = pl.multiple_of(base - tile_base, SLOT_ALIGN)
            part = val * jnp.dot(onehot, y_ref[pl.ds(off, win), :], preferred_element_type=F32)
            total = part if total is None else total + part
        acc_ref[rows, :] += total

    @pl.when(eg == pl.num_programs(2) - 1)
    def _():
        out = x_ref[...] + g2_ref[...] * acc_ref[...]
        if final:
            ms = jnp.mean(out * out, axis=-1, keepdims=True)
            out = out * lax.rsqrt(ms + EPS) * fw_ref[...]
        o_ref[...] = out


def _ec_moe(token_sets, w1, w3, w2, layer):
    bsz = token_sets[0][0].shape[0]
    plans, xgs = [], []
    for x, gate2, h2, aff, final_w in token_sets:
        n = x.shape[1]
        cap = EC_CAPACITY * n // N_EXPERTS
        nblk = n // TOK_BLK
        win = min(TOK_BLK + SLOT_ALIGN, cap)
        posm, pos = _route(aff, cap)
        base = jnp.minimum((pos[:, :, ::TOK_BLK] // SLOT_ALIGN) * SLOT_ALIGN, cap - win).astype(I32).reshape(-1)
        dq = GATHER_COLS
        xgs.append(pl.pallas_call(
            functools.partial(_gather_kernel, win=win, nblk=nblk),
            grid_spec=pltpu.PrefetchScalarGridSpec(
                num_scalar_prefetch=1,
                grid=(bsz, D // dq),
                in_specs=[pl.BlockSpec((None, n, dq), lambda b, c, base: (b, 0, c)),
                          pl.BlockSpec((None, N_EXPERTS, n), lambda b, c, base: (b, 0, 0))],
                out_specs=pl.BlockSpec((None, N_EXPERTS, cap, dq), lambda b, c, base: (b, 0, 0, c))),
            out_shape=jax.ShapeDtypeStruct((bsz, N_EXPERTS, cap, D), BF16),
            compiler_params=_cparams(("parallel", "parallel")),
            name="moe_gather",
        )(base, h2, posm))
        plans.append((n, cap, nblk, win, posm, base))

    xg = xgs[0] if len(xgs) == 1 else jnp.concatenate(xgs, axis=2)
    cap_all = xg.shape[2]
    tf = 256
    y = pl.pallas_call(
        _expert_kernel,
        grid=(N_EXPERTS, bsz, EXPERT_FF // tf),
        in_specs=[pl.BlockSpec((None, None, cap_all, D), lambda e, b, f: (b, e, 0, 0)),
                  pl.BlockSpec((None, None, D, tf), lambda e, b, f: (layer, e, 0, f)),
                  pl.BlockSpec((None, None, D, tf), lambda e, b, f: (layer, e, 0, f)),
                  pl.BlockSpec((None, None, tf, D), lambda e, b, f: (layer, e, f, 0))],
        out_specs=pl.BlockSpec((None, None, cap_all, D), lambda e, b, f: (b, e, 0, 0)),
        out_shape=jax.ShapeDtypeStruct((bsz, N_EXPERTS, cap_all, D), BF16),
        scratch_shapes=[pltpu.VMEM((cap_all, D), F32)],
        compiler_params=_cparams(("parallel", "parallel", "arbitrary")),
        name="moe_expert",
    )(xg, w1, w3, w2)
    y2d = y.reshape(bsz * N_EXPERTS * cap_all, D)

    outs = []
    row0 = 0
    for (x, gate2, h2, aff, final_w), (n, cap, nblk, win, posm, base) in zip(token_sets, plans):
        tile = min(COMBINE_TILE, n)
        nsub = tile // TOK_BLK
        win_tile = min(tile + SLOT_ALIGN, cap)
        final = final_w is not None

        def y_window(b, j, eg, base, k, cap=cap, nblk=nblk, nsub=nsub, win_tile=win_tile, row0=row0):
            be = b * N_EXPERTS + eg * EXPERT_GROUP + k
            start = jnp.minimum(base[be * nblk + j * nsub], cap - win_tile)
            return pl.multiple_of(be * cap_all + row0 + start, SLOT_ALIGN), 0

        in_specs = [pl.BlockSpec((None, tile, D), lambda b, j, e, base: (b, j, 0)),
                    pl.BlockSpec((None, 1, D), lambda b, j, e, base: (b, 0, 0)),
                    pl.BlockSpec((None, tile, N_EXPERTS), lambda b, j, e, base: (b, j, 0)),
                    pl.BlockSpec((None, tile, N_EXPERTS), lambda b, j, e, base: (b, j, 0))]
        in_specs += [pl.BlockSpec((pl.Element(win_tile), pl.Element(D)), functools.partial(y_window, k=k))
                     for k in range(EXPERT_GROUP)]
        args = [base, x, gate2, aff, jnp.swapaxes(posm, 1, 2)] + [y2d] * EXPERT_GROUP
        if final:
            in_specs.append(pl.BlockSpec((1, D), lambda b, j, e, base: (0, 0)))
            args.append(final_w.reshape(1, D))
        outs.append(pl.pallas_call(
            functools.partial(_combine_kernel, win=win, win_tile=win_tile, cap=cap, nblk=nblk, nsub=nsub, final=final),
            grid_spec=pltpu.PrefetchScalarGridSpec(
                num_scalar_prefetch=1,
                grid=(bsz, n // tile, N_EXPERTS // EXPERT_GROUP),
                in_specs=in_specs,
                out_specs=pl.BlockSpec((None, tile, D), lambda b, j, e, base: (b, j, 0)),
                scratch_shapes=[pltpu.VMEM((tile, D), F32)]),
            out_shape=jax.ShapeDtypeStruct((bsz, n, D), F32),
            compiler_params=_cparams(("parallel", "parallel", "arbitrary")),
            name="moe_combine",
        )(*args))
        row0 += cap
    return outs


def _rope_tables(n_tokens):
    rows = n_tokens // GRID_W
    row = jnp.repeat(jnp.arange(rows, dtype=F32), GRID_W)
    col = (jnp.arange(rows * GRID_W, dtype=I32) % GRID_W).astype(F32)
    inv = ROPE_BASE ** (-jnp.arange(ROPE_PAIRS, dtype=F32) / ROPE_PAIRS)
    ang = jnp.concatenate([row[:, None] * inv, col[:, None] * inv], axis=-1)
    cos, sin = jnp.cos(ang), jnp.sin(ang)
    reps = LANES // DA_HEAD_DIM
    cos_t = jnp.tile(jnp.concatenate([cos, cos], axis=-1), (1, reps))
    sin_t = jnp.tile(jnp.concatenate([-sin, sin], axis=-1), (1, reps))
    return cos_t, sin_t


def _lower_bounds(p):
    cum = jnp.cumsum(jax.nn.softmax(p.astype(F32), axis=0), axis=0)
    return cum - cum[0]


def _pad128(v):
    return jnp.pad(v.reshape(1, -1).astype(F32), ((0, 0), (0, LANES - v.size)))


def kernel(x, c, ctx, c_ctx, ada_w, ada_b, norm1_w, norm2_w, w_in, ssd_conv_w, ssd_conv_b, ssd_dt_bias,
           ssd_a_log, ssd_d, ssd_norm_w, hg_lb, hg_norm_w, da_lambda, da_norm_w, w_up, w_gate, b_gate,
           w_out, moe_router, moe_w1, moe_w3, moe_w2, final_norm_w):
    bsz, n_lat, _ = x.shape
    n_ctx = ctx.shape[1]
    depth = ada_w.shape[0]
    rope = _rope_tables(n_lat)
    lb_all = jnp.stack([_lower_bounds(hg_lb[0]), _lower_bounds(hg_lb[1])], axis=1)

    mod = _adaln(jnp.concatenate([c, c_ctx[None]], axis=0), ada_w, ada_b)
    mod = mod.reshape(depth, 8, 6, 1, D)

    xl, xc = x, ctx
    for l in range(depth):
        need_ctx = l < depth - 1
        lam_init = 0.8 - 0.6 * math.exp(-0.3 * l)
        mod_l = [mod[l, :bsz, k] for k in range(6)]
        mod_c = [jnp.broadcast_to(mod[l, bsz:bsz + 1, k], (bsz, 1, D)) for k in range(6)]

        wl = w_in[l]
        w_a = jnp.concatenate([wl[:, OFF_SSD_Z:OFF_SSD_XBC], wl[:, OFF_HG_Q:OFF_HG_F], wl[:, OFF_HG_I:OFF_DA_Q],
                               wl[:, OFF_DA_V:N_IN], wl[:, OFF_SSD_XBC:OFF_SSD_DT]], axis=1).astype(BF16)
        w_f = jnp.concatenate([wl[:, OFF_HG_F:OFF_HG_I], wl[:, OFF_SSD_DT:OFF_HG_Q],
                               jnp.zeros((D, PF_N - PF_DT - 2 * SSD_HEADS), F32)], axis=1).astype(BF16)
        w_qk = wl[:, OFF_DA_Q:OFF_DA_V].astype(BF16)
        alog128 = _pad128(ssd_a_log[l])
        bias128 = _pad128(ssd_dt_bias[l])
        d_full = jnp.repeat(ssd_d[l].astype(F32), SSD_D_INNER // SSD_HEADS).reshape(1, SSD_D_INNER)
        ssd_nw = ssd_norm_w[l].reshape(1, SSD_D_INNER).astype(F32)
        hg_nw = hg_norm_w[l].reshape(1, HG_HEAD_DIM).astype(F32)
        wg_b = w_gate[l].astype(BF16)
        bg = b_gate[l].reshape(1, N_BRANCH * D)
        wu_b = w_up[l].astype(BF16)
        wo_b = w_out[l].astype(BF16)

        def project(xs, m, use_rope):
            n = xs.shape[1]
            h = _modulate(xs, norm1_w[l], m[0], m[1])
            h2d = h.reshape(bsz * n, D)
            tm = min(1024, n)
            pa = _matmul(h2d, w_a, BF16, tm, 512).reshape(bsz, n, PA_N)
            pf = _matmul(h2d, w_f, F32, min(512, n), PF_N).reshape(bsz, n, PF_N)
            pqk = _matmul(h2d, w_qk, BF16, tm, 512, rope_tables=rope if use_rope else None).reshape(bsz, n, PQK_N)
            return h, pa, pf, pqk

        hl, pa_l, pf_l, pqk_l = project(xl, mod_l, True)
        hc, pa_c, pf_c, pqk_c = project(xc, mod_c, False)

        xa_c = _ssd_conv(pa_c, ssd_conv_w[l], ssd_conv_b[l])
        xa_l = _ssd_conv(pa_l, ssd_conv_w[l], ssd_conv_b[l])
        yc_f, s_f = _ssd_scan(xa_c, pf_c, alog128, bias128, rev=False)
        yl_f, _ = _ssd_scan(xa_l, pf_l, alog128, bias128, rev=False, init=s_f)
        ya_c, s_b = _ssd_scan(xa_c, pf_c, alog128, bias128, rev=True, readout=(pa_c, yc_f, d_full, ssd_nw))
        ya_l, _ = _ssd_scan(xa_l, pf_l, alog128, bias128, rev=True, init=s_b, readout=(pa_l, yl_f, d_full, ssd_nw))

        lb_f, lb_b = lb_all[l, 0].reshape(1, HG_WIDTH), lb_all[l, 1].reshape(1, HG_WIDTH)
        oc_f, t_f = _hg_scan(pa_c, pf_c, lb_f, rev=False)
        ol_f, _ = _hg_scan(pa_l, pf_l, lb_f, rev=False, init=t_f)
        yb_c, t_b = _hg_scan(pa_c, pf_c, lb_b, rev=True, readout=(oc_f, hg_nw))
        yb_l, _ = _hg_scan(pa_l, pf_l, lb_b, rev=True, init=t_b, readout=(ol_f, hg_nw))

        k_all = jnp.concatenate([pqk_c[..., DA_WIDTH:], pqk_l[..., DA_WIDTH:]], axis=1)
        v_all = jnp.concatenate([pa_c[..., PA_DV:PA_DV + DA_WIDTH], pa_l[..., PA_DV:PA_DV + DA_WIDTH]], axis=1)
        yd_l = _diff_attn(pqk_l, [(k_all, 0, v_all, 0)], da_lambda[l], da_norm_w[l], lam_init, 512, ATTN_TK)

        xl = _merge(xl, mod_l[2], hl, ya_l, yb_l, yd_l, wg_b, bg, wu_b, wo_b)
        h2l, aff_l = _modulate(xl, norm2_w[l], mod_l[3], mod_l[4], w_router=moe_router[l])
        token_sets = [(xl, mod_l[5], h2l, aff_l, final_norm_w if l == depth - 1 else None)]
        if need_ctx:
            yd_c = _diff_attn(pqk_c, [(pqk_c, DA_WIDTH, pa_c, PA_DV)], da_lambda[l], da_norm_w[l], lam_init,
                              n_ctx, ATTN_TK)
            xc = _merge(xc, mod_c[2], hc, ya_c, yb_c, yd_c, wg_b, bg, wu_b, wo_b)
            h2c, aff_c = _modulate(xc, norm2_w[l], mod_c[3], mod_c[4], w_router=moe_router[l])
            token_sets.append((xc, mod_c[5], h2c, aff_c, None))
        outs = _ec_moe(token_sets, moe_w1, moe_w3, moe_w2, l)
        xl = outs[0]
        if need_ctx:
            xc = outs[1]
    return xl
```

```python
import functools
import math

import jax
import jax.numpy as jnp
from jax import lax
from jax.experimental import pallas as pl
from jax.experimental.pallas import tpu as pltpu

F32 = jnp.float32
BF16 = jnp.bfloat16
I32 = jnp.int32
HIGHEST = lax.Precision.HIGHEST
LOG2E = 1.4426950408889634

D = 2048
EPS = 1e-6
GRID_W = 64
SSD_D_INNER = 1024
SSD_HEADS = 16
SSD_GROUPS = 2
SSD_STATE = 128
SSD_XBC = 1536
SSD_CONV_W = 5
HG_WIDTH = 1024
HG_HEADS = 8
HG_HEAD_DIM = 128
DA_HEADS = 8
DA_HEAD_DIM = 64
DA_WIDTH = 1024
ROPE_BASE = 10000.0
ROPE_PAIRS = 16
N_BRANCH = 3
N_EXPERTS = 16
EXPERT_FF = 2048
EC_CAPACITY = 2

OFF_SSD_Z = 0
OFF_SSD_XBC = 1024
OFF_SSD_DT = 2560
OFF_HG_Q = 2592
OFF_HG_F = OFF_HG_Q + 1024
OFF_HG_I = OFF_HG_F + 2048
OFF_HG_G = OFF_HG_I + 1024
OFF_DA_Q = OFF_HG_G + 1024
OFF_DA_K = OFF_DA_Q + 1024
OFF_DA_V = OFF_DA_K + 1024
N_IN = OFF_DA_V + 1024

LANES = 128
PA_Z, PA_HQ, PA_HI, PA_HG, PA_DV, PA_XBC = 0, 1024, 2048, 3072, 4096, 5120
PA_N = 6656
PF_F, PF_DT = 0, 2048
PF_N = 2176
PQK_N = 2048

SCAN_Q = 256
ATTN_TK = 1408
HALO = 16
TOK_BLK = 128
COMBINE_TILE = 512
GATHER_COLS = 512
EXPERT_GROUP = 4
GATHER_WIN_SMALL = 48
SLOT_ALIGN = 16
VMEM_LIMIT = 56 * 1024 * 1024


def _cparams(sem, vmem=VMEM_LIMIT):
    return pltpu.CompilerParams(dimension_semantics=sem, vmem_limit_bytes=vmem)


def _silu(x):
    return x * jax.nn.sigmoid(x)


def _adaln_kernel(cb_ref, w_ref, b_ref, o_ref, *, n_rows, tn):
    nct = tn // LANES

    def body(k, accs):
        r0 = pl.multiple_of(k * 8, 8)
        ws = [w_ref[pl.ds(r0, 8), c * LANES:(c + 1) * LANES] for c in range(nct)]
        out = []
        for r in range(n_rows):
            s = _silu(cb_ref[r, pl.ds(r0, 8), :])
            out.append(tuple(accs[r][c] + s * ws[c] for c in range(nct)))
        return tuple(out)

    zero = jnp.zeros((8, LANES), F32)
    accs = lax.fori_loop(0, D // 8, body, tuple(tuple(zero for _ in range(nct)) for _ in range(n_rows)),
                         unroll=8)
    rows = []
    for r in range(n_rows):
        rows.append(jnp.concatenate([jnp.sum(accs[r][c], axis=0, keepdims=True) for c in range(nct)], axis=1))
    rows.append(jnp.zeros((8 - n_rows, tn), F32))
    o_ref[...] = jnp.concatenate(rows, axis=0) + b_ref[...]


def _adaln(cvec, ada_w, ada_b):
    n_rows = cvec.shape[0]
    depth, _, n6 = ada_w.shape
    tn = 512
    cb = jnp.broadcast_to(cvec[:, :, None], (n_rows, D, LANES))
    return pl.pallas_call(
        functools.partial(_adaln_kernel, n_rows=n_rows, tn=tn),
        grid=(depth, n6 // tn),
        in_specs=[pl.BlockSpec((n_rows, D, LANES), lambda l, j: (0, 0, 0)),
                  pl.BlockSpec((None, D, tn), lambda l, j: (l, 0, j)),
                  pl.BlockSpec((None, 1, tn), lambda l, j: (l, 0, j))],
        out_specs=pl.BlockSpec((None, 8, tn), lambda l, j: (l, 0, j)),
        out_shape=jax.ShapeDtypeStruct((depth, 8, n6), F32),
        compiler_params=_cparams(("parallel", "parallel")),
        name="adaln",
    )(cb, ada_w, ada_b.reshape(depth, 1, n6))


def _modulate_kernel(x_ref, nw_ref, sh_ref, sc_ref, *rest, with_router):
    x = x_ref[...]
    ms = jnp.mean(x * x, axis=-1, keepdims=True)
    h = (x * lax.rsqrt(ms + EPS) * nw_ref[...]) * (1.0 + sc_ref[...]) + sh_ref[...]
    if with_router:
        wr_ref, h_ref, aff_ref = rest
        logits = jnp.dot(h, wr_ref[...], preferred_element_type=F32, precision=HIGHEST)
        m = jnp.max(logits, axis=-1, keepdims=True)
        e = jnp.exp(logits - m)
        aff_ref[...] = e / jnp.sum(e, axis=-1, keepdims=True)
    else:
        (h_ref,) = rest
    h_ref[...] = h.astype(BF16)


def _modulate(x, nw, shift, scale, w_router=None):
    bsz, n, _ = x.shape
    tm = min(512, n)
    with_router = w_router is not None
    in_specs = [pl.BlockSpec((None, tm, D), lambda b, i: (b, i, 0)),
                pl.BlockSpec((1, D), lambda b, i: (0, 0)),
                pl.BlockSpec((None, 1, D), lambda b, i: (b, 0, 0)),
                pl.BlockSpec((None, 1, D), lambda b, i: (b, 0, 0))]
    args = [x, nw.reshape(1, D), shift, scale]
    out_specs = [pl.BlockSpec((None, tm, D), lambda b, i: (b, i, 0))]
    out_shape = [jax.ShapeDtypeStruct((bsz, n, D), BF16)]
    if with_router:
        in_specs.append(pl.BlockSpec((D, N_EXPERTS), lambda b, i: (0, 0)))
        args.append(w_router)
        out_specs.append(pl.BlockSpec((None, tm, N_EXPERTS), lambda b, i: (b, i, 0)))
        out_shape.append(jax.ShapeDtypeStruct((bsz, n, N_EXPERTS), F32))
    res = pl.pallas_call(
        functools.partial(_modulate_kernel, with_router=with_router),
        grid=(bsz, n // tm),
        in_specs=in_specs, out_specs=out_specs, out_shape=out_shape,
        compiler_params=_cparams(("parallel", "parallel")),
        name="modulate_router" if with_router else "modulate",
    )(*args)
    return res if with_router else res[0]


def _mm_kernel(a_ref, w_ref, *rest, rope, tn):
    acc = jnp.dot(a_ref[...], w_ref[...], preferred_element_type=F32)
    if rope:
        cos_ref, sin_ref, o_ref = rest
        cos = cos_ref[...]
        sin = sin_ref[...]
        lane = lax.broadcasted_iota(I32, cos.shape, 1)
        first_half = (lane % DA_HEAD_DIM) < (DA_HEAD_DIM // 2)
        for c in range(tn // LANES):
            xs = acc[:, c * LANES:(c + 1) * LANES]
            partner = jnp.where(first_half,
                                pltpu.roll(xs, LANES - DA_HEAD_DIM // 2, 1),
                                pltpu.roll(xs, DA_HEAD_DIM // 2, 1))
            o_ref[:, c * LANES:(c + 1) * LANES] = (xs * cos + partner * sin).astype(o_ref.dtype)
    else:
        (o_ref,) = rest
        o_ref[...] = acc.astype(o_ref.dtype)


def _matmul(a, w, out_dtype, tm, tn, rope_tables=None):
    m, k = a.shape
    n = w.shape[1]
    rope = rope_tables is not None
    in_specs = [pl.BlockSpec((tm, k), lambda i, j: (i, 0)),
                pl.BlockSpec((k, tn), lambda i, j: (0, j))]
    args = [a, w]
    if rope:
        cos, sin = rope_tables
        nt = cos.shape[0] // tm
        in_specs += [pl.BlockSpec((tm, LANES), lambda i, j: (i % nt, 0)),
                     pl.BlockSpec((tm, LANES), lambda i, j: (i % nt, 0))]
        args += [cos, sin]
    return pl.pallas_call(
        functools.partial(_mm_kernel, rope=rope, tn=tn),
        grid=(m // tm, n // tn),
        in_specs=in_specs,
        out_specs=pl.BlockSpec((tm, tn), lambda i, j: (i, j)),
        out_shape=jax.ShapeDtypeStruct((m, n), out_dtype),
        compiler_params=_cparams(("parallel", "parallel")),
        name="proj_rope" if rope else "proj",
    )(*args)


def _conv_kernel(prev_ref, cur_ref, next_ref, w_ref, b_ref, o_ref, *, tc):
    i = pl.program_id(1)
    last = pl.num_programs(1) - 1
    pad = (SSD_CONV_W - 1) // 2
    prev = jnp.where(i > 0, prev_ref[...].astype(F32), 0.0)
    nxt = jnp.where(i < last, next_ref[...].astype(F32), 0.0)
    ext = jnp.concatenate([prev, cur_ref[...].astype(F32), nxt], axis=0)
    w = w_ref[...]
    y = b_ref[...] + w[0:1] * ext[HALO - pad:HALO - pad + tc]
    for k in range(1, SSD_CONV_W):
        y = y + w[k:k + 1] * ext[HALO - pad + k:HALO - pad + k + tc]
    o_ref[...] = _silu(y).astype(o_ref.dtype)


def _ssd_conv(pa, conv_w, conv_b):
    bsz, n, _ = pa.shape
    tc = min(512, n)
    tw = 512
    c0 = PA_XBC // tw
    nbh = n // HALO
    return pl.pallas_call(
        functools.partial(_conv_kernel, tc=tc),
        grid=(bsz, n // tc, SSD_XBC // tw),
        in_specs=[pl.BlockSpec((None, HALO, tw), lambda b, i, j: (b, jnp.maximum(i * (tc // HALO) - 1, 0), c0 + j)),
                  pl.BlockSpec((None, tc, tw), lambda b, i, j: (b, i, c0 + j)),
                  pl.BlockSpec((None, HALO, tw),
                               lambda b, i, j: (b, jnp.minimum((i + 1) * (tc // HALO), nbh - 1), c0 + j)),
                  pl.BlockSpec((SSD_CONV_W, tw), lambda b, i, j: (0, j)),
                  pl.BlockSpec((1, tw), lambda b, i, j: (0, j))],
        out_specs=pl.BlockSpec((None, tc, tw), lambda b, i, j: (b, i, j)),
        out_shape=jax.ShapeDtypeStruct((bsz, n, SSD_XBC), BF16),
        compiler_params=_cparams(("parallel", "parallel", "parallel")),
        name="ssd_conv",
    )(pa, pa, pa, conv_w, conv_b.reshape(1, SSD_XBC))


def _tri(q, rev):
    r = lax.broadcasted_iota(I32, (q, q), 0)
    c = lax.broadcasted_iota(I32, (q, q), 1)
    return (c >= r) if rev else (r >= c)


def _split3(x):
    hi = x.astype(BF16)
    r = x - hi.astype(F32)
    mid = r.astype(BF16)
    lo = (r - mid.astype(F32)).astype(BF16)
    return hi, mid, lo


def _select_rows_dot(sel01, x):
    hi, mid, lo = _split3(x)
    return (jnp.dot(sel01, hi, preferred_element_type=F32) + jnp.dot(sel01, mid, preferred_element_type=F32)
            + jnp.dot(sel01, lo, preferred_element_type=F32))


def _select_cols_dot(x, sel01):
    hi, mid, lo = _split3(x)
    return (jnp.dot(hi, sel01, preferred_element_type=F32) + jnp.dot(mid, sel01, preferred_element_type=F32)
            + jnp.dot(lo, sel01, preferred_element_type=F32))


def _ssd_scan_kernel(*refs, rev, has_init, readout):
    it = iter(refs)
    xact_ref, dt_ref, alog_ref, bias_ref = next(it), next(it), next(it), next(it)
    init_ref = next(it) if has_init else None
    if readout:
        z_ref, yprev_ref, d_ref, nw_ref = next(it), next(it), next(it), next(it)
    y_ref, st_ref = next(it), next(it)
    ybuf_ref = next(it) if readout else None

    q = SCAN_Q
    hd = SSD_D_INNER // SSD_HEADS
    hg = SSD_HEADS // SSD_GROUPS
    gw = hg * hd
    d_off = SSD_HEADS if rev else 0

    @pl.when(pl.program_id(1) == 0)
    def _():
        st_ref[...] = init_ref[...] if has_init else jnp.zeros_like(st_ref)

    tri = _tri(q, rev)
    dtv = jax.nn.softplus(dt_ref[...] + bias_ref[...])
    da = dtv * (-LOG2E * jnp.exp(alog_ref[...]))
    cs = _select_rows_dot(jnp.where(tri, 1.0, 0.0).astype(BF16), da)
    cs_t = cs.T
    erow = lax.broadcasted_iota(I32, (LANES, SSD_D_INNER), 0)
    ecol = lax.broadcasted_iota(I32, (LANES, SSD_D_INNER), 1)
    expand = jnp.where(erow == d_off + ecol // hd, 1.0, 0.0).astype(BF16)
    csx = _select_cols_dot(cs, expand)
    dtx = _select_cols_dot(dtv, expand)
    x = xact_ref[:, :SSD_D_INNER].astype(F32)
    xdt = x * dtx
    xdt_b = xdt.astype(BF16)
    cs_end = csx[0:1] if rev else csx[q - 1:q]
    ecs = jnp.exp2(csx)
    xw = (xdt * jnp.exp2(cs_end - csx)).astype(BF16)
    ecs_end = jnp.exp2(cs_end)
    lane = lax.broadcasted_iota(I32, (q, LANES), 1)
    out_ref = ybuf_ref if readout else y_ref

    for g in range(SSD_GROUPS):
        bg = xact_ref[:, SSD_D_INNER + g * SSD_STATE:SSD_D_INNER + (g + 1) * SSD_STATE]
        cg = xact_ref[:, SSD_D_INNER + (SSD_GROUPS + g) * SSD_STATE:SSD_D_INNER + (SSD_GROUPS + g + 1) * SSD_STATE]
        cb = lax.dot_general(cg, bg, (((1,), (1,)), ((), ())), preferred_element_type=F32)
        st_g = st_ref[:, g * gw:(g + 1) * gw]
        y_inter = jnp.dot(cg, st_g.astype(BF16), preferred_element_type=F32) * ecs[:, g * gw:(g + 1) * gw]
        for jj in range(hg // 2):
            col0 = g * gw + jj * LANES
            xp = xdt_b[:, col0:col0 + LANES]
            acc = y_inter[:, jj * LANES:(jj + 1) * LANES]
            for s in range(2):
                j = d_off + g * hg + 2 * jj + s
                e = cs[:, j:j + 1] - cs_t[j:j + 1, :]
                seg = jnp.exp2(jnp.where(tri, e, -jnp.inf))
                m = (cb * seg).astype(BF16)
                xm = jnp.where((lane >= hd) if s else (lane < hd), xp, jnp.zeros_like(xp))
                acc = acc + jnp.dot(m, xm, preferred_element_type=F32)
            out_ref[:, col0:col0 + LANES] = acc
        upd = lax.dot_general(bg, xw[:, g * gw:(g + 1) * gw], (((0,), (0,)), ((), ())),
                              preferred_element_type=F32)
        st_ref[:, g * gw:(g + 1) * gw] = st_g * ecs_end[:, g * gw:(g + 1) * gw] + upd

    if readout:
        z = z_ref[...].astype(F32)
        yy = (yprev_ref[...] + ybuf_ref[...] + d_ref[...] * x) * _silu(z)
        for g in range(SSD_GROUPS):
            seg = yy[:, g * gw:(g + 1) * gw]
            ms = jnp.mean(seg * seg, axis=-1, keepdims=True)
            y_ref[:, g * gw:(g + 1) * gw] = (seg * lax.rsqrt(ms + EPS) * nw_ref[:, g * gw:(g + 1) * gw]).astype(y_ref.dtype)


def _ssd_scan(xact, pf, alog128, bias128, *, rev, init=None, readout=None):
    bsz, n, _ = xact.shape
    nc = n // SCAN_Q
    cidx = (lambda c: nc - 1 - c) if rev else (lambda c: c)
    in_specs = [pl.BlockSpec((None, SCAN_Q, SSD_XBC), lambda b, c: (b, cidx(c), 0)),
                pl.BlockSpec((None, SCAN_Q, LANES), lambda b, c: (b, cidx(c), PF_DT // LANES)),
                pl.BlockSpec((1, LANES), lambda b, c: (0, 0)),
                pl.BlockSpec((1, LANES), lambda b, c: (0, 0))]
    args = [xact, pf, alog128, bias128]
    if init is not None:
        in_specs.append(pl.BlockSpec((None, SSD_STATE, SSD_D_INNER), lambda b, c: (b, 0, 0)))
        args.append(init)
    scratch = []
    if readout is not None:
        pa, yprev, d_full, norm_w = readout
        in_specs += [pl.BlockSpec((None, SCAN_Q, SSD_D_INNER), lambda b, c: (b, cidx(c), PA_Z // SSD_D_INNER)),
                     pl.BlockSpec((None, SCAN_Q, SSD_D_INNER), lambda b, c: (b, cidx(c), 0)),
                     pl.BlockSpec((1, SSD_D_INNER), lambda b, c: (0, 0)),
                     pl.BlockSpec((1, SSD_D_INNER), lambda b, c: (0, 0))]
        args += [pa, yprev, d_full, norm_w]
        scratch = [pltpu.VMEM((SCAN_Q, SSD_D_INNER), F32)]
    y_dtype = BF16 if readout is not None else F32
    y, st = pl.pallas_call(
        functools.partial(_ssd_scan_kernel, rev=rev, has_init=init is not None, readout=readout is not None),
        grid=(bsz, nc),
        in_specs=in_specs,
        out_specs=[pl.BlockSpec((None, SCAN_Q, SSD_D_INNER), lambda b, c: (b, cidx(c), 0)),
                   pl.BlockSpec((None, SSD_STATE, SSD_D_INNER), lambda b, c: (b, 0, 0))],
        out_shape=[jax.ShapeDtypeStruct((bsz, n, SSD_D_INNER), y_dtype),
                   jax.ShapeDtypeStruct((bsz, SSD_STATE, SSD_D_INNER), F32)],
        scratch_shapes=scratch,
        compiler_params=_cparams(("parallel", "arbitrary")),
        name="ssd_scan_bwd" if rev else "ssd_scan_fwd",
    )(*args)
    return y, st


def _ref_rows(bb, s, rev):
    q, w = bb.shape
    rl = s if rev else s - 1
    if 2 * s >= 8:
        b3 = bb.reshape(q // (2 * s), 2 * s, w)
        return jnp.broadcast_to(b3[:, rl:rl + 1, :], b3.shape).reshape(q, w)
    off = lax.broadcasted_iota(I32, (q, w), 0) % (2 * s)
    out = bb
    for ov in range(2 * s):
        if ov != rl:
            out = jnp.where(off == ov, pltpu.roll(bb, (ov - rl) % q, 0), out)
    return out


def _hg_scan_kernel(*refs, rev, has_init, readout):
    it = iter(refs)
    q_ref, f_ref, v_ref, lb_ref = next(it), next(it), next(it), next(it)
    init_ref = next(it) if has_init else None
    if readout:
        g_ref, oprev_ref, nw_ref = next(it), next(it), next(it)
    o_ref, st_ref, att_ref = next(it), next(it), next(it)
    obuf_ref = next(it) if readout else None

    q = SCAN_Q
    hd = HG_HEAD_DIM

    @pl.when(pl.program_id(1) == 0)
    def _():
        st_ref[...] = init_ref[...] if has_init else jnp.zeros_like(st_ref)

    lb = lb_ref[...]
    qq = _silu(q_ref[...].astype(F32))
    f = lb + (1.0 - lb) * jax.nn.sigmoid(f_ref[...])
    kk = 1.0 - f
    lf = jnp.log(f) * LOG2E
    tri = _tri(q, rev)
    bb = jnp.dot(jnp.where(tri, 1.0, 0.0), lf, preferred_element_type=F32, precision=HIGHEST)

    row = lax.broadcasted_iota(I32, (q, HG_WIDTH), 0)
    ri = lax.broadcasted_iota(I32, (q, q), 0)
    ci = lax.broadcasted_iota(I32, (q, q), 1)
    xr = ri ^ ci
    causal = (ci > ri) if rev else (ri > ci)
    att_ref[...] = jnp.zeros_like(att_ref)
    s = q // 2
    while s >= 1:
        upper = (row & s) != 0
        q_side = jnp.logical_not(upper) if rev else upper
        ex = jnp.exp2(-jnp.abs(bb - _ref_rows(bb, s, rev)))
        u = (jnp.where(q_side, qq, kk) * ex).astype(BF16)
        level = jnp.logical_and((xr >> int(math.log2(s))) == 1, causal)
        for h in range(HG_HEADS):
            uh = u[:, h * hd:(h + 1) * hd]
            p = lax.dot_general(uh, uh, (((1,), (1,)), ((), ())), preferred_element_type=F32)
            att_ref[h] = jnp.where(level, p, att_ref[h])
        s //= 2

    b_end = bb[0:1] if rev else bb[q - 1:q]
    qe = (qq * jnp.exp2(bb)).astype(BF16)
    kh = (kk * jnp.exp2(b_end - bb)).astype(BF16)
    qk = (qq * kk).astype(BF16)
    e_end = jnp.exp2(b_end)
    ones = jnp.ones((hd, hd), BF16)
    out_ref = obuf_ref if readout else o_ref
    for h in range(HG_HEADS):
        sl = slice(h * hd, (h + 1) * hd)
        vh = v_ref[:, sl]
        st_h = st_ref[sl, :]
        o = jnp.dot(att_ref[h].astype(BF16), vh, preferred_element_type=F32)
        o = o + jnp.dot(qk[:, sl], ones, preferred_element_type=F32) * vh.astype(F32)
        o = o + lax.dot_general(qe[:, sl], st_h.astype(BF16), (((1,), (1,)), ((), ())), preferred_element_type=F32)
        out_ref[:, sl] = o
        upd = lax.dot_general(vh, kh[:, sl], (((0,), (0,)), ((), ())), preferred_element_type=F32)
        st_ref[sl, :] = st_h * e_end[:, sl] + upd

    if readout:
        for h in range(HG_HEADS):
            sl = slice(h * hd, (h + 1) * hd)
            o = oprev_ref[:, sl] + obuf_ref[:, sl]
            ms = jnp.mean(o * o, axis=-1, keepdims=True)
            gate = _silu(g_ref[:, sl].astype(F32))
            o_ref[:, sl] = (o * lax.rsqrt(ms + EPS) * nw_ref[...] * gate).astype(o_ref.dtype)


def _hg_scan(pa, pf, lb_row, *, rev, init=None, readout=None):
    bsz, n, _ = pa.shape
    nc = n // SCAN_Q
    cidx = (lambda c: nc - 1 - c) if rev else (lambda c: c)
    fcol = (PF_F // HG_WIDTH) + (1 if rev else 0)
    in_specs = [pl.BlockSpec((None, SCAN_Q, HG_WIDTH), lambda b, c: (b, cidx(c), PA_HQ // HG_WIDTH)),
                pl.BlockSpec((None, SCAN_Q, HG_WIDTH), lambda b, c: (b, cidx(c), fcol)),
                pl.BlockSpec((None, SCAN_Q, HG_WIDTH), lambda b, c: (b, cidx(c), PA_HI // HG_WIDTH)),
                pl.BlockSpec((1, HG_WIDTH), lambda b, c: (0, 0))]
    args = [pa, pf, pa, lb_row]
    if init is not None:
        in_specs.append(pl.BlockSpec((None, HG_WIDTH, HG_HEAD_DIM), lambda b, c: (b, 0, 0)))
        args.append(init)
    scratch = [pltpu.VMEM((HG_HEADS, SCAN_Q, SCAN_Q), F32)]
    if readout is not None:
        oprev, nw128 = readout
        in_specs += [pl.BlockSpec((None, SCAN_Q, HG_WIDTH), lambda b, c: (b, cidx(c), PA_HG // HG_WIDTH)),
                     pl.BlockSpec((None, SCAN_Q, HG_WIDTH), lambda b, c: (b, cidx(c), 0)),
                     pl.BlockSpec((1, HG_HEAD_DIM), lambda b, c: (0, 0))]
        args += [pa, oprev, nw128]
        scratch.append(pltpu.VMEM((SCAN_Q, HG_WIDTH), F32))
    o_dtype = BF16 if readout is not None else F32
    assert PA_HQ % HG_WIDTH == 0 or True
    o, st = pl.pallas_call(
        functools.partial(_hg_scan_kernel, rev=rev, has_init=init is not None, readout=readout is not None),
        grid=(bsz, nc),
        in_specs=in_specs,
        out_specs=[pl.BlockSpec((None, SCAN_Q, HG_WIDTH), lambda b, c: (b, cidx(c), 0)),
                   pl.BlockSpec((None, HG_WIDTH, HG_HEAD_DIM), lambda b, c: (b, 0, 0))],
        out_shape=[jax.ShapeDtypeStruct((bsz, n, HG_WIDTH), o_dtype),
                   jax.ShapeDtypeStruct((bsz, HG_WIDTH, HG_HEAD_DIM), F32)],
        scratch_shapes=scratch,
        compiler_params=_cparams(("parallel", "arbitrary")),
        name="hg_scan_bwd" if rev else "hg_scan_fwd",
    )(*args)
    return o, st


def _attn_kernel(*refs, seg_lens, tk, lam_init):
    lam_ref, nw_ref, q_ref = refs[0], refs[1], refs[2]
    nseg = len(seg_lens)
    kv_refs = refs[3:3 + 2 * nseg]
    o_ref, qs_ref, sa_ref, sb_ref, m_ref, mn_ref, acc_ref = refs[3 + 2 * nseg:]
    tq = q_ref.shape[0]
    qv = q_ref[...] * jnp.asarray(DA_HEAD_DIM ** -0.5, BF16)
    lane = lax.broadcasted_iota(I32, qv.shape, 1)
    zero = jnp.zeros_like(qv)
    qs_ref[:tq] = jnp.where(lane < DA_HEAD_DIM, qv, zero)
    qs_ref[tq:] = jnp.where(lane >= DA_HEAD_DIM, qv, zero)
    m_ref[...] = jnp.full_like(m_ref, -jnp.inf)
    acc_ref[...] = jnp.zeros_like(acc_ref)

    def scores(kc):
        return lax.dot_general(qs_ref[...], kc, (((1,), (1,)), ((), ())), preferred_element_type=F32)

    def lane_tiles(a):
        return [a[:, c * LANES:(c + 1) * LANES] for c in range(a.shape[1] // LANES)]

    def row_max(s, m_prev):
        tiles = lane_tiles(s)
        m = tiles[0]
        for tile in tiles[1:]:
            m = jnp.maximum(m, tile)
        return jnp.maximum(m_prev, jnp.broadcast_to(jnp.max(m, axis=-1, keepdims=True), m_prev.shape))

    def online_step(s, vc, m_prev, m_cur):
        alpha = jnp.exp(m_prev - m_cur)
        ps = [jnp.exp(tile - m_cur) for tile in lane_tiles(s)]
        pv = jnp.dot(jnp.concatenate(ps, axis=1).astype(BF16), vc, preferred_element_type=F32)
        acc_ref[...] = jnp.concatenate([alpha, alpha], axis=1) * acc_ref[...] + pv

    for si, n in enumerate(seg_lens):
        k_ref, v_ref = kv_refs[2 * si], kv_refs[2 * si + 1]
        t = min(tk, n)
        nc = n // t
        if nc == 1:
            s = scores(k_ref[...])
            m_prev = m_ref[...]
            m_cur = row_max(s, m_prev)
            online_step(s, v_ref[...], m_prev, m_cur)
            m_ref[...] = m_cur
        else:
            s0 = scores(k_ref[pl.ds(0, t), :])
            sa_ref[...] = s0
            mn_ref[...] = row_max(s0, m_ref[...])

            def pipe_step(c, cur_ref, nxt_ref, k_ref=k_ref, v_ref=v_ref, t=t):
                r0 = pl.multiple_of(c * t, t)
                r1 = pl.multiple_of(r0 + t, t)
                m_prev, m_cur = m_ref[...], mn_ref[...]
                s_next = scores(k_ref[pl.ds(r1, t), :])
                nxt_ref[...] = s_next
                mn_ref[...] = row_max(s_next, m_cur)
                m_ref[...] = m_cur
                online_step(cur_ref[...], v_ref[pl.ds(r0, t), :], m_prev, m_cur)

            def body(i, carry):
                pipe_step(2 * i, sa_ref, sb_ref)
                pipe_step(2 * i + 1, sb_ref, sa_ref)
                return carry

            lax.fori_loop(0, (nc - 1) // 2, body, 0)
            last_ref = sa_ref
            if nc % 2 == 0:
                pipe_step(nc - 2, sa_ref, sb_ref)
                last_ref = sb_ref
            m_prev, m_cur = m_ref[...], mn_ref[...]
            online_step(last_ref[...], v_ref[pl.ds((nc - 1) * t, t), :], m_prev, m_cur)
            m_ref[...] = m_cur
    hw = 2 * DA_HEAD_DIM
    acc = acc_ref[:, :hw]
    l = acc_ref[:, hw:]
    lam = lam_ref[...]
    lmbda = (jnp.exp(jnp.sum(lam[0:1] * lam[1:2], axis=-1, keepdims=True))
             - jnp.exp(jnp.sum(lam[2:3] * lam[3:4], axis=-1, keepdims=True)) + lam_init)
    o = acc[:tq] / l[:tq] - lmbda * (acc[tq:] / l[tq:])
    ms = jnp.mean(o * o, axis=-1, keepdims=True)
    o_ref[...] = (o * lax.rsqrt(ms + EPS) * nw_ref[...] * (1.0 - lam_init)).astype(o_ref.dtype)


def _diff_attn(pqk_q, kv_segs, lam, nw, lam_init, tq, tk):
    bsz, nq, _ = pqk_q.shape
    hw = 2 * DA_HEAD_DIM
    in_specs = [pl.BlockSpec((4, DA_HEAD_DIM), lambda b, h, i: (0, 0)),
                pl.BlockSpec((1, hw), lambda b, h, i: (0, 0)),
                pl.BlockSpec((None, tq, hw), lambda b, h, i: (b, i, h))]
    args = [lam, nw.reshape(1, hw), pqk_q]
    seg_lens = []
    for k_arr, k_col, v_aug in kv_segs:
        nk = k_arr.shape[1]
        assert nk % min(tk, nk) == 0
        seg_lens.append(nk)
        in_specs += [pl.BlockSpec((None, nk, hw), functools.partial(lambda b, h, i, c: (b, 0, c + h), c=k_col // hw)),
                     pl.BlockSpec((None, nk, 2 * hw), lambda b, h, i: (b, 0, h))]
        args += [k_arr, v_aug]
    return pl.pallas_call(
        functools.partial(_attn_kernel, seg_lens=tuple(seg_lens), tk=tk, lam_init=lam_init),
        grid=(bsz, DA_HEADS, nq // tq),
        in_specs=in_specs,
        out_specs=pl.BlockSpec((None, tq, hw), lambda b, h, i: (b, i, h)),
        out_shape=jax.ShapeDtypeStruct((bsz, nq, DA_WIDTH), BF16),
        scratch_shapes=[pltpu.VMEM((2 * tq, hw), BF16),
                        pltpu.VMEM((2 * tq, tk), F32),
                        pltpu.VMEM((2 * tq, tk), F32),
                        pltpu.VMEM((2 * tq, LANES), F32),
                        pltpu.VMEM((2 * tq, LANES), F32),
                        pltpu.VMEM((2 * tq, 2 * hw), F32)],
        compiler_params=_cparams(("parallel", "parallel", "parallel")),
        name="diff_attn",
    )(*args)


def _mix_kernel(h_ref, ya_ref, yb_ref, yd_ref, wg0, wg1, wg2, bg0, bg1, bg2, wu0, wu1, wu2, o_ref):
    h = h_ref[...]
    mix = None
    for y_ref, wg, bg, wu in ((ya_ref, wg0, bg0, wu0), (yb_ref, wg1, bg1, wu1), (yd_ref, wg2, bg2, wu2)):
        gate = jax.nn.sigmoid(jnp.dot(h, wg[...], preferred_element_type=F32) + bg[...])
        up = jnp.dot(y_ref[...], wu[...], preferred_element_type=F32)
        mix = gate * up if mix is None else mix + gate * up
    o_ref[...] = mix.astype(o_ref.dtype)


def _residual_out_kernel(x_ref, g1_ref, m_ref, wo_ref, o_ref):
    o_ref[...] = x_ref[...] + g1_ref[...] * jnp.dot(m_ref[...], wo_ref[...], preferred_element_type=F32)


def _merge(x, gate1, h, ya, yb, yd, w_gate, b_gate, w_up, w_out):
    bsz, n, _ = x.shape
    tm = min(1024, n)
    tn = 512
    nn = D // tn
    row = lambda b, i, j: (b, i, 0)
    in_specs = [pl.BlockSpec((None, tm, D), row),
                pl.BlockSpec((None, tm, 1024), row),
                pl.BlockSpec((None, tm, 1024), row),
                pl.BlockSpec((None, tm, 1024), row)]
    in_specs += [pl.BlockSpec((D, tn), functools.partial(lambda b, i, j, k: (0, k * nn + j), k=k)) for k in range(3)]
    in_specs += [pl.BlockSpec((1, tn), functools.partial(lambda b, i, j, k: (0, k * nn + j), k=k)) for k in range(3)]
    in_specs += [pl.BlockSpec((None, 1024, tn), functools.partial(lambda b, i, j, k: (k, 0, j), k=k)) for k in range(3)]
    mix = pl.pallas_call(
        _mix_kernel,
        grid=(bsz, n // tm, nn),
        in_specs=in_specs,
        out_specs=pl.BlockSpec((None, tm, tn), lambda b, i, j: (b, i, j)),
        out_shape=jax.ShapeDtypeStruct((bsz, n, D), BF16),
        compiler_params=_cparams(("parallel", "parallel", "parallel")),
        name="merge_mix",
    )(h, ya, yb, yd, w_gate, w_gate, w_gate, b_gate, b_gate, b_gate, w_up, w_up, w_up)
    return pl.pallas_call(
        _residual_out_kernel,
        grid=(bsz, n // tm, nn),
        in_specs=[pl.BlockSpec((None, tm, tn), lambda b, i, j: (b, i, j)),
                  pl.BlockSpec((None, 1, tn), lambda b, i, j: (b, 0, j)),
                  pl.BlockSpec((None, tm, D), row),
                  pl.BlockSpec((D, tn), lambda b, i, j: (0, j))],
        out_specs=pl.BlockSpec((None, tm, tn), lambda b, i, j: (b, i, j)),
        out_shape=jax.ShapeDtypeStruct((bsz, n, D), F32),
        compiler_params=_cparams(("parallel", "parallel", "parallel")),
        name="merge_out",
    )(x, gate1, mix, w_out)


def _route_kernel(aff_ref, posm_ref, pos_ref, *, cap, rpe):
    rows = N_EXPERTS * rpe
    bits = pltpu.bitcast(aff_ref[...], I32)
    ones = jnp.ones((LANES, LANES), BF16)
    r = lax.broadcasted_iota(I32, (rows, rows), 0)
    c = lax.broadcasted_iota(I32, (rows, rows), 1)
    same = (r // rpe) == (c // rpe)
    grp = same.astype(BF16)
    grp_before = jnp.logical_and(same, c < r).astype(BF16)
    lr = lax.broadcasted_iota(I32, (LANES, LANES), 0)
    lc = lax.broadcasted_iota(I32, (LANES, LANES), 1)
    before = (lr < lc).astype(BF16)

    def count(mask):
        per_row = jnp.dot(mask.astype(BF16), ones, preferred_element_type=F32)
        return jnp.dot(grp, per_row.astype(BF16), preferred_element_type=F32)

    def excl_prefix(mask):
        mb = mask.astype(BF16)
        within = jnp.dot(mb, before, preferred_element_type=F32)
        per_row = jnp.dot(mb, ones, preferred_element_type=F32)
        return within + jnp.dot(grp_before, per_row.astype(BF16), preferred_element_type=F32)

    def body(i, thr):
        cand = thr | (jnp.int32(1) << (30 - i))
        return jnp.where(count(bits >= cand) >= cap, cand, thr)

    thr = lax.fori_loop(0, 31, body, jnp.zeros((rows, LANES), I32))
    gt = bits > thr
    eq = bits == thr
    need = cap - count(gt)
    sel = jnp.logical_or(gt, jnp.logical_and(eq, excl_prefix(eq) < need))
    pos = excl_prefix(sel).astype(I32)
    pos_ref[...] = pos
    posm_ref[...] = jnp.where(sel, pos, -1)


def _route(aff, cap):
    bsz, n, _ = aff.shape
    rpe = n // LANES
    rows = N_EXPERTS * rpe
    aff_t = jnp.swapaxes(aff, 1, 2).reshape(bsz, rows, LANES)
    spec = pl.BlockSpec((None, rows, LANES), lambda b: (b, 0, 0))
    posm, pos = pl.pallas_call(
        functools.partial(_route_kernel, cap=cap, rpe=rpe),
        grid=(bsz,),
        in_specs=[spec], out_specs=[spec, spec],
        out_shape=[jax.ShapeDtypeStruct((bsz, rows, LANES), I32)] * 2,
        compiler_params=_cparams(("parallel",)),
        name="route",
    )(aff_t)
    return posm.reshape(bsz, N_EXPERTS, n), pos.reshape(bsz, N_EXPERTS, n)


def _gather_kernel(base_ref, base_small_ref, small_ref, h_ref, posm_ref, xg_ref, *, win, win_small, nblk):
    b = pl.program_id(0)
    xg_ref[...] = jnp.zeros_like(xg_ref)

    def scatter_block(j, bases_ref, w):
        t0 = pl.multiple_of(j * TOK_BLK, TOK_BLK)
        slot = lax.broadcasted_iota(I32, (w, TOK_BLK), 0)
        bases, onehots = [], []
        for e in range(N_EXPERTS):
            base = pl.multiple_of(bases_ref[(b * N_EXPERTS + e) * nblk + j], SLOT_ALIGN)
            rel = posm_ref[e:e + 1, pl.ds(t0, TOK_BLK)] - base
            bases.append(base)
            onehots.append(jnp.where(slot == rel, 1.0, 0.0).astype(BF16))
        rows = jnp.dot(jnp.concatenate(onehots, axis=0), h_ref[pl.ds(t0, TOK_BLK), :],
                       preferred_element_type=F32).astype(BF16)
        for e in range(N_EXPERTS):
            xg_ref[e, pl.ds(bases[e], w), :] += rows[e * w:(e + 1) * w]

    def body(j, carry):
        small = small_ref[b * nblk + j] != 0

        @pl.when(small)
        def _():
            scatter_block(j, base_small_ref, win_small)

        @pl.when(jnp.logical_not(small))
        def _():
            scatter_block(j, base_ref, win)

        return carry

    lax.fori_loop(0, nblk, body, 0)


def _expert_kernel(xg_ref, w1_ref, w3_ref, w2_ref, y_ref, acc_ref):
    f = pl.program_id(2)

    @pl.when(f == 0)
    def _():
        acc_ref[...] = jnp.zeros_like(acc_ref)

    xg = xg_ref[...]
    a = jnp.dot(xg, w1_ref[...].astype(BF16), preferred_element_type=F32)
    g = jnp.dot(xg, w3_ref[...].astype(BF16), preferred_element_type=F32)
    acc_ref[...] += jnp.dot((_silu(a) * g).astype(BF16), w2_ref[...].astype(BF16), preferred_element_type=F32)

    @pl.when(f == pl.num_programs(2) - 1)
    def _():
        y_ref[...] = acc_ref[...].astype(y_ref.dtype)


def _combine_kernel(base_ref, x_ref, g2_ref, aff_ref, posm_ref, *rest, win, win_tile, cap, nblk, nsub, final):
    y_refs = rest[:EXPERT_GROUP]
    if final:
        fw_ref, o_ref, acc_ref = rest[EXPERT_GROUP:]
    else:
        o_ref, acc_ref = rest[EXPERT_GROUP:]
    b, jt, eg = pl.program_id(0), pl.program_id(1), pl.program_id(2)

    @pl.when(eg == 0)
    def _():
        acc_ref[...] = jnp.zeros_like(acc_ref)

    lane = lax.broadcasted_iota(I32, (TOK_BLK, N_EXPERTS), 1)
    slot = lax.broadcasted_iota(I32, (TOK_BLK, win), 1)
    for sb in range(nsub):
        rows = slice(sb * TOK_BLK, (sb + 1) * TOK_BLK)
        total = None
        for k, y_ref in enumerate(y_refs):
            e = eg * EXPERT_GROUP + k
            first = (b * N_EXPERTS + e) * nblk + jt * nsub
            tile_base = jnp.minimum(base_ref[first], cap - win_tile)
            base = base_ref[first + sb]
            mine = lane == e
            rel = jnp.sum(jnp.where(mine, posm_ref[rows, :], 0), axis=-1, keepdims=True) - base
            val = jnp.sum(jnp.where(mine, aff_ref[rows, :], 0.0), axis=-1, keepdims=True)
            onehot = jnp.where(slot == rel, 1.0, 0.0).astype(BF16)
            off = pl.multiple_of(base - tile_base, SLOT_ALIGN)
            part = val * jnp.dot(onehot, y_ref[pl.ds(off, win), :], preferred_element_type=F32)
            total = part if total is None else total + part
        acc_ref[rows, :] += total

    @pl.when(eg == pl.num_programs(2) - 1)
    def _():
        out = x_ref[...] + g2_ref[...] * acc_ref[...]
        if final:
            ms = jnp.mean(out * out, axis=-1, keepdims=True)
            out = out * lax.rsqrt(ms + EPS) * fw_ref[...]
        o_ref[...] = out


def _ec_moe(token_sets, w1, w3, w2, layer):
    bsz = token_sets[0][0].shape[0]
    plans, xgs = [], []
    for x, gate2, h2, aff, final_w in token_sets:
        n = x.shape[1]
        cap = EC_CAPACITY * n // N_EXPERTS
        nblk = n // TOK_BLK
        win = min(TOK_BLK + SLOT_ALIGN, cap)
        posm, pos = _route(aff, cap)
        start = pos[:, :, ::TOK_BLK]
        aligned = (start // SLOT_ALIGN) * SLOT_ALIGN
        base = jnp.minimum(aligned, cap - win).astype(I32).reshape(-1)
        win_small = min(GATHER_WIN_SMALL, win)
        base_small = jnp.minimum(aligned, cap - win_small).astype(I32).reshape(-1)
        count = jnp.concatenate([start[:, :, 1:], jnp.full_like(start[:, :, :1], cap)], axis=2) - start
        small = jnp.all(count <= win_small - SLOT_ALIGN, axis=1).astype(I32).reshape(-1)
        dq = GATHER_COLS
        xgs.append(pl.pallas_call(
            functools.partial(_gather_kernel, win=win, win_small=win_small, nblk=nblk),
            grid_spec=pltpu.PrefetchScalarGridSpec(
                num_scalar_prefetch=3,
                grid=(bsz, D // dq),
                in_specs=[pl.BlockSpec((None, n, dq), lambda b, c, *_: (b, 0, c)),
                          pl.BlockSpec((None, N_EXPERTS, n), lambda b, c, *_: (b, 0, 0))],
                out_specs=pl.BlockSpec((None, N_EXPERTS, cap, dq), lambda b, c, *_: (b, 0, 0, c))),
            out_shape=jax.ShapeDtypeStruct((bsz, N_EXPERTS, cap, D), BF16),
            compiler_params=_cparams(("parallel", "parallel")),
            name="moe_gather",
        )(base, base_small, small, h2, posm))
        plans.append((n, cap, nblk, win, posm, base))

    xg = xgs[0] if len(xgs) == 1 else jnp.concatenate(xgs, axis=2)
    cap_all = xg.shape[2]
    tf = 256
    y = pl.pallas_call(
        _expert_kernel,
        grid=(N_EXPERTS, bsz, EXPERT_FF // tf),
        in_specs=[pl.BlockSpec((None, None, cap_all, D), lambda e, b, f: (b, e, 0, 0)),
                  pl.BlockSpec((None, None, D, tf), lambda e, b, f: (layer, e, 0, f)),
                  pl.BlockSpec((None, None, D, tf), lambda e, b, f: (layer, e, 0, f)),
                  pl.BlockSpec((None, None, tf, D), lambda e, b, f: (layer, e, f, 0))],
        out_specs=pl.BlockSpec((None, None, cap_all, D), lambda e, b, f: (b, e, 0, 0)),
        out_shape=jax.ShapeDtypeStruct((bsz, N_EXPERTS, cap_all, D), BF16),
        scratch_shapes=[pltpu.VMEM((cap_all, D), F32)],
        compiler_params=_cparams(("parallel", "parallel", "arbitrary")),
        name="moe_expert",
    )(xg, w1, w3, w2)
    y2d = y.reshape(bsz * N_EXPERTS * cap_all, D)

    outs = []
    row0 = 0
    for (x, gate2, h2, aff, final_w), (n, cap, nblk, win, posm, base) in zip(token_sets, plans):
        tile = min(COMBINE_TILE, n)
        nsub = tile // TOK_BLK
        win_tile = min(tile + SLOT_ALIGN, cap)
        final = final_w is not None

        def y_window(b, j, eg, base, k, cap=cap, nblk=nblk, nsub=nsub, win_tile=win_tile, row0=row0):
            be = b * N_EXPERTS + eg * EXPERT_GROUP + k
            start = jnp.minimum(base[be * nblk + j * nsub], cap - win_tile)
            return pl.multiple_of(be * cap_all + row0 + start, SLOT_ALIGN), 0

        in_specs = [pl.BlockSpec((None, tile, D), lambda b, j, e, base: (b, j, 0)),
                    pl.BlockSpec((None, 1, D), lambda b, j, e, base: (b, 0, 0)),
                    pl.BlockSpec((None, tile, N_EXPERTS), lambda b, j, e, base: (b, j, 0)),
                    pl.BlockSpec((None, tile, N_EXPERTS), lambda b, j, e, base: (b, j, 0))]
        in_specs += [pl.BlockSpec((pl.Element(win_tile), pl.Element(D)), functools.partial(y_window, k=k))
                     for k in range(EXPERT_GROUP)]
        args = [base, x, gate2, aff, jnp.swapaxes(posm, 1, 2)] + [y2d] * EXPERT_GROUP
        if final:
            in_specs.append(pl.BlockSpec((1, D), lambda b, j, e, base: (0, 0)))
            args.append(final_w.reshape(1, D))
        outs.append(pl.pallas_call(
            functools.partial(_combine_kernel, win=win, win_tile=win_tile, cap=cap, nblk=nblk, nsub=nsub, final=final),
            grid_spec=pltpu.PrefetchScalarGridSpec(
                num_scalar_prefetch=1,
                grid=(bsz, n // tile, N_EXPERTS // EXPERT_GROUP),
                in_specs=in_specs,
                out_specs=pl.BlockSpec((None, tile, D), lambda b, j, e, base: (b, j, 0)),
                scratch_shapes=[pltpu.VMEM((tile, D), F32)]),
            out_shape=jax.ShapeDtypeStruct((bsz, n, D), F32),
            compiler_params=_cparams(("parallel", "parallel", "arbitrary")),
            name="moe_combine",
        )(*args))
        row0 += cap
    return outs


def _rope_tables(n_tokens):
    rows = n_tokens // GRID_W
    row = jnp.repeat(jnp.arange(rows, dtype=F32), GRID_W)
    col = (jnp.arange(rows * GRID_W, dtype=I32) % GRID_W).astype(F32)
    inv = ROPE_BASE ** (-jnp.arange(ROPE_PAIRS, dtype=F32) / ROPE_PAIRS)
    ang = jnp.concatenate([row[:, None] * inv, col[:, None] * inv], axis=-1)
    cos, sin = jnp.cos(ang), jnp.sin(ang)
    reps = LANES // DA_HEAD_DIM
    cos_t = jnp.tile(jnp.concatenate([cos, cos], axis=-1), (1, reps))
    sin_t = jnp.tile(jnp.concatenate([-sin, sin], axis=-1), (1, reps))
    return cos_t, sin_t


def _with_ones(v):
    bsz, n, w = v.shape
    hw = 2 * DA_HEAD_DIM
    v4 = v.reshape(bsz, n, w // hw, hw)
    return jnp.concatenate([v4, jnp.ones_like(v4)], axis=-1).reshape(bsz, n, 2 * w)


def _lower_bounds(p):
    cum = jnp.cumsum(jax.nn.softmax(p.astype(F32), axis=0), axis=0)
    return cum - cum[0]


def _pad128(v):
    return jnp.pad(v.reshape(1, -1).astype(F32), ((0, 0), (0, LANES - v.size)))


def kernel(x, c, ctx, c_ctx, ada_w, ada_b, norm1_w, norm2_w, w_in, ssd_conv_w, ssd_conv_b, ssd_dt_bias,
           ssd_a_log, ssd_d, ssd_norm_w, hg_lb, hg_norm_w, da_lambda, da_norm_w, w_up, w_gate, b_gate,
           w_out, moe_router, moe_w1, moe_w3, moe_w2, final_norm_w):
    bsz, n_lat, _ = x.shape
    n_ctx = ctx.shape[1]
    depth = ada_w.shape[0]
    rope = _rope_tables(n_lat)
    lb_all = jnp.stack([_lower_bounds(hg_lb[0]), _lower_bounds(hg_lb[1])], axis=1)

    mod = _adaln(jnp.concatenate([c, c_ctx[None]], axis=0), ada_w, ada_b)
    mod = mod.reshape(depth, 8, 6, 1, D)

    xl, xc = x, ctx
    for l in range(depth):
        need_ctx = l < depth - 1
        lam_init = 0.8 - 0.6 * math.exp(-0.3 * l)
        mod_l = [mod[l, :bsz, k] for k in range(6)]
        mod_c = [jnp.broadcast_to(mod[l, bsz:bsz + 1, k], (bsz, 1, D)) for k in range(6)]

        wl = w_in[l]
        w_a = jnp.concatenate([wl[:, OFF_SSD_Z:OFF_SSD_XBC], wl[:, OFF_HG_Q:OFF_HG_F], wl[:, OFF_HG_I:OFF_DA_Q],
                               wl[:, OFF_DA_V:N_IN], wl[:, OFF_SSD_XBC:OFF_SSD_DT]], axis=1).astype(BF16)
        w_f = jnp.concatenate([wl[:, OFF_HG_F:OFF_HG_I], wl[:, OFF_SSD_DT:OFF_HG_Q],
                               jnp.zeros((D, PF_N - PF_DT - 2 * SSD_HEADS), F32)], axis=1).astype(BF16)
        w_qk = wl[:, OFF_DA_Q:OFF_DA_V].astype(BF16)
        alog128 = _pad128(ssd_a_log[l])
        bias128 = _pad128(ssd_dt_bias[l])
        d_full = jnp.repeat(ssd_d[l].astype(F32), SSD_D_INNER // SSD_HEADS).reshape(1, SSD_D_INNER)
        ssd_nw = ssd_norm_w[l].reshape(1, SSD_D_INNER).astype(F32)
        hg_nw = hg_norm_w[l].reshape(1, HG_HEAD_DIM).astype(F32)
        wg_b = w_gate[l].astype(BF16)
        bg = b_gate[l].reshape(1, N_BRANCH * D)
        wu_b = w_up[l].astype(BF16)
        wo_b = w_out[l].astype(BF16)

        def project(xs, m, use_rope):
            n = xs.shape[1]
            h = _modulate(xs, norm1_w[l], m[0], m[1])
            h2d = h.reshape(bsz * n, D)
            tm = min(1024, n)
            pa = _matmul(h2d, w_a, BF16, tm, 512).reshape(bsz, n, PA_N)
            pf = _matmul(h2d, w_f, F32, min(512, n), PF_N).reshape(bsz, n, PF_N)
            pqk = _matmul(h2d, w_qk, BF16, tm, 512, rope_tables=rope if use_rope else None).reshape(bsz, n, PQK_N)
            return h, pa, pf, pqk

        hl, pa_l, pf_l, pqk_l = project(xl, mod_l, True)
        hc, pa_c, pf_c, pqk_c = project(xc, mod_c, False)

        xa_c = _ssd_conv(pa_c, ssd_conv_w[l], ssd_conv_b[l])
        xa_l = _ssd_conv(pa_l, ssd_conv_w[l], ssd_conv_b[l])
        yc_f, s_f = _ssd_scan(xa_c, pf_c, alog128, bias128, rev=False)
        yl_f, _ = _ssd_scan(xa_l, pf_l, alog128, bias128, rev=False, init=s_f)
        ya_c, s_b = _ssd_scan(xa_c, pf_c, alog128, bias128, rev=True, readout=(pa_c, yc_f, d_full, ssd_nw))
        ya_l, _ = _ssd_scan(xa_l, pf_l, alog128, bias128, rev=True, init=s_b, readout=(pa_l, yl_f, d_full, ssd_nw))

        lb_f, lb_b = lb_all[l, 0].reshape(1, HG_WIDTH), lb_all[l, 1].reshape(1, HG_WIDTH)
        oc_f, t_f = _hg_scan(pa_c, pf_c, lb_f, rev=False)
        ol_f, _ = _hg_scan(pa_l, pf_l, lb_f, rev=False, init=t_f)
        yb_c, t_b = _hg_scan(pa_c, pf_c, lb_b, rev=True, readout=(oc_f, hg_nw))
        yb_l, _ = _hg_scan(pa_l, pf_l, lb_b, rev=True, init=t_b, readout=(ol_f, hg_nw))

        k_all = jnp.concatenate([pqk_c[..., DA_WIDTH:], pqk_l[..., DA_WIDTH:]], axis=1)
        v_c = _with_ones(pa_c[..., PA_DV:PA_DV + DA_WIDTH])
        v_all = jnp.concatenate([v_c, _with_ones(pa_l[..., PA_DV:PA_DV + DA_WIDTH])], axis=1)
        yd_l = _diff_attn(pqk_l, [(k_all, 0, v_all)], da_lambda[l], da_norm_w[l], lam_init, 512, ATTN_TK)

        xl = _merge(xl, mod_l[2], hl, ya_l, yb_l, yd_l, wg_b, bg, wu_b, wo_b)
        h2l, aff_l = _modulate(xl, norm2_w[l], mod_l[3], mod_l[4], w_router=moe_router[l])
        token_sets = [(xl, mod_l[5], h2l, aff_l, final_norm_w if l == depth - 1 else None)]
        if need_ctx:
            yd_c = _diff_attn(pqk_c, [(pqk_c, DA_WIDTH, v_c)], da_lambda[l], da_norm_w[l], lam_init, n_ctx, ATTN_TK)
            xc = _merge(xc, mod_c[2], hc, ya_c, yb_c, yd_c, wg_b, bg, wu_b, wo_b)
            h2c, aff_c = _modulate(xc, norm2_w[l], mod_c[3], mod_c[4], w_router=moe_router[l])
            token_sets.append((xc, mod_c[5], h2c, aff_c, None))
        outs = _ec_moe(token_sets, moe_w1, moe_w3, moe_w2, l)
        xl = outs[0]
        if need_ctx:
            xc = outs[1]
    return xl
```

```python
import functools
import math

import jax
import jax.numpy as jnp
from jax import lax
from jax.experimental import pallas as pl
from jax.experimental.pallas import tpu as pltpu

F32 = jnp.float32
BF16 = jnp.bfloat16
I32 = jnp.int32
HIGHEST = lax.Precision.HIGHEST
LOG2E = 1.4426950408889634

D = 2048
EPS = 1e-6
GRID_W = 64
SSD_D_INNER = 1024
SSD_HEADS = 16
SSD_GROUPS = 2
SSD_STATE = 128
SSD_XBC = 1536
SSD_CONV_W = 5
HG_WIDTH = 1024
HG_HEADS = 8
HG_HEAD_DIM = 128
DA_HEADS = 8
DA_HEAD_DIM = 64
DA_WIDTH = 1024
ROPE_BASE = 10000.0
ROPE_PAIRS = 16
N_BRANCH = 3
N_EXPERTS = 16
EXPERT_FF = 2048
EC_CAPACITY = 2

OFF_SSD_Z = 0
OFF_SSD_XBC = 1024
OFF_SSD_DT = 2560
OFF_HG_Q = 2592
OFF_HG_F = OFF_HG_Q + 1024
OFF_HG_I = OFF_HG_F + 2048
OFF_HG_G = OFF_HG_I + 1024
OFF_DA_Q = OFF_HG_G + 1024
OFF_DA_K = OFF_DA_Q + 1024
OFF_DA_V = OFF_DA_K + 1024
N_IN = OFF_DA_V + 1024

LANES = 128
PA_Z, PA_HQ, PA_HI, PA_HG, PA_DV, PA_XBC = 0, 1024, 2048, 3072, 4096, 5120
PA_N = 6656
PF_F, PF_DT = 0, 2048
PF_N = 2176
PQK_N = 2048

SCAN_Q = 256
ATTN_TK = 1408
HALO = 16
TOK_BLK = 128
COMBINE_TILE = 512
GATHER_COLS = 512
EXPERT_GROUP = 4
GATHER_WIN_SMALL = 48
SLOT_ALIGN = 16
VMEM_LIMIT = 56 * 1024 * 1024


def _cparams(sem, vmem=VMEM_LIMIT):
    return pltpu.CompilerParams(dimension_semantics=sem, vmem_limit_bytes=vmem)


def _silu(x):
    return x * jax.nn.sigmoid(x)


def _adaln_kernel(cb_ref, w_ref, b_ref, o_ref, *, n_rows, tn):
    nct = tn // LANES

    def body(k, accs):
        r0 = pl.multiple_of(k * 8, 8)
        ws = [w_ref[pl.ds(r0, 8), c * LANES:(c + 1) * LANES] for c in range(nct)]
        out = []
        for r in range(n_rows):
            s = _silu(cb_ref[r, pl.ds(r0, 8), :])
            out.append(tuple(accs[r][c] + s * ws[c] for c in range(nct)))
        return tuple(out)

    zero = jnp.zeros((8, LANES), F32)
    accs = lax.fori_loop(0, D // 8, body, tuple(tuple(zero for _ in range(nct)) for _ in range(n_rows)),
                         unroll=8)
    rows = []
    for r in range(n_rows):
        rows.append(jnp.concatenate([jnp.sum(accs[r][c], axis=0, keepdims=True) for c in range(nct)], axis=1))
    rows.append(jnp.zeros((8 - n_rows, tn), F32))
    o_ref[...] = jnp.concatenate(rows, axis=0) + b_ref[...]


def _adaln(cvec, ada_w, ada_b):
    n_rows = cvec.shape[0]
    depth, _, n6 = ada_w.shape
    tn = 512
    cb = jnp.broadcast_to(cvec[:, :, None], (n_rows, D, LANES))
    return pl.pallas_call(
        functools.partial(_adaln_kernel, n_rows=n_rows, tn=tn),
        grid=(depth, n6 // tn),
        in_specs=[pl.BlockSpec((n_rows, D, LANES), lambda l, j: (0, 0, 0)),
                  pl.BlockSpec((None, D, tn), lambda l, j: (l, 0, j)),
                  pl.BlockSpec((None, 1, tn), lambda l, j: (l, 0, j))],
        out_specs=pl.BlockSpec((None, 8, tn), lambda l, j: (l, 0, j)),
        out_shape=jax.ShapeDtypeStruct((depth, 8, n6), F32),
        compiler_params=_cparams(("parallel", "parallel")),
        name="adaln",
    )(cb, ada_w, ada_b.reshape(depth, 1, n6))


def _modulate_kernel(x_ref, nw_ref, sh_ref, sc_ref, *rest, with_router):
    x = x_ref[...]
    ms = jnp.mean(x * x, axis=-1, keepdims=True)
    h = (x * lax.rsqrt(ms + EPS) * nw_ref[...]) * (1.0 + sc_ref[...]) + sh_ref[...]
    if with_router:
        wr_ref, h_ref, aff_ref = rest
        logits = jnp.dot(h, wr_ref[...], preferred_element_type=F32, precision=HIGHEST)
        m = jnp.max(logits, axis=-1, keepdims=True)
        e = jnp.exp(logits - m)
        aff_ref[...] = e / jnp.sum(e, axis=-1, keepdims=True)
    else:
        (h_ref,) = rest
    h_ref[...] = h.astype(BF16)


def _modulate(x, nw, shift, scale, w_router=None):
    bsz, n, _ = x.shape
    tm = min(512, n)
    with_router = w_router is not None
    in_specs = [pl.BlockSpec((None, tm, D), lambda b, i: (b, i, 0)),
                pl.BlockSpec((1, D), lambda b, i: (0, 0)),
                pl.BlockSpec((None, 1, D), lambda b, i: (b, 0, 0)),
                pl.BlockSpec((None, 1, D), lambda b, i: (b, 0, 0))]
    args = [x, nw.reshape(1, D), shift, scale]
    out_specs = [pl.BlockSpec((None, tm, D), lambda b, i: (b, i, 0))]
    out_shape = [jax.ShapeDtypeStruct((bsz, n, D), BF16)]
    if with_router:
        in_specs.append(pl.BlockSpec((D, N_EXPERTS), lambda b, i: (0, 0)))
        args.append(w_router)
        out_specs.append(pl.BlockSpec((None, tm, N_EXPERTS), lambda b, i: (b, i, 0)))
        out_shape.append(jax.ShapeDtypeStruct((bsz, n, N_EXPERTS), F32))
    res = pl.pallas_call(
        functools.partial(_modulate_kernel, with_router=with_router),
        grid=(bsz, n // tm),
        in_specs=in_specs, out_specs=out_specs, out_shape=out_shape,
        compiler_params=_cparams(("parallel", "parallel")),
        name="modulate_router" if with_router else "modulate",
    )(*args)
    return res if with_router else res[0]


def _mm_kernel(a_ref, w_ref, *rest, rope, tn):
    acc = jnp.dot(a_ref[...], w_ref[...], preferred_element_type=F32)
    if rope:
        cos_ref, sin_ref, o_ref = rest
        cos = cos_ref[...]
        sin = sin_ref[...]
        lane = lax.broadcasted_iota(I32, cos.shape, 1)
        first_half = (lane % DA_HEAD_DIM) < (DA_HEAD_DIM // 2)
        for c in range(tn // LANES):
            xs = acc[:, c * LANES:(c + 1) * LANES]
            partner = jnp.where(first_half,
                                pltpu.roll(xs, LANES - DA_HEAD_DIM // 2, 1),
                                pltpu.roll(xs, DA_HEAD_DIM // 2, 1))
            o_ref[:, c * LANES:(c + 1) * LANES] = (xs * cos + partner * sin).astype(o_ref.dtype)
    else:
        (o_ref,) = rest
        o_ref[...] = acc.astype(o_ref.dtype)


def _matmul(a, w, out_dtype, tm, tn, rope_tables=None):
    m, k = a.shape
    n = w.shape[1]
    rope = rope_tables is not None
    in_specs = [pl.BlockSpec((tm, k), lambda i, j: (i, 0)),
                pl.BlockSpec((k, tn), lambda i, j: (0, j))]
    args = [a, w]
    if rope:
        cos, sin = rope_tables
        nt = cos.shape[0] // tm
        in_specs += [pl.BlockSpec((tm, LANES), lambda i, j: (i % nt, 0)),
                     pl.BlockSpec((tm, LANES), lambda i, j: (i % nt, 0))]
        args += [cos, sin]
    return pl.pallas_call(
        functools.partial(_mm_kernel, rope=rope, tn=tn),
        grid=(m // tm, n // tn),
        in_specs=in_specs,
        out_specs=pl.BlockSpec((tm, tn), lambda i, j: (i, j)),
        out_shape=jax.ShapeDtypeStruct((m, n), out_dtype),
        compiler_params=_cparams(("parallel", "parallel")),
        name="proj_rope" if rope else "proj",
    )(*args)


def _conv_kernel(prev_ref, cur_ref, next_ref, w_ref, b_ref, o_ref, *, tc):
    i = pl.program_id(1)
    last = pl.num_programs(1) - 1
    pad = (SSD_CONV_W - 1) // 2
    prev = jnp.where(i > 0, prev_ref[...].astype(F32), 0.0)
    nxt = jnp.where(i < last, next_ref[...].astype(F32), 0.0)
    ext = jnp.concatenate([prev, cur_ref[...].astype(F32), nxt], axis=0)
    w = w_ref[...]
    y = b_ref[...] + w[0:1] * ext[HALO - pad:HALO - pad + tc]
    for k in range(1, SSD_CONV_W):
        y = y + w[k:k + 1] * ext[HALO - pad + k:HALO - pad + k + tc]
    o_ref[...] = _silu(y).astype(o_ref.dtype)


def _ssd_conv(pa, conv_w, conv_b):
    bsz, n, _ = pa.shape
    tc = min(512, n)
    tw = 512
    c0 = PA_XBC // tw
    nbh = n // HALO
    return pl.pallas_call(
        functools.partial(_conv_kernel, tc=tc),
        grid=(bsz, n // tc, SSD_XBC // tw),
        in_specs=[pl.BlockSpec((None, HALO, tw), lambda b, i, j: (b, jnp.maximum(i * (tc // HALO) - 1, 0), c0 + j)),
                  pl.BlockSpec((None, tc, tw), lambda b, i, j: (b, i, c0 + j)),
                  pl.BlockSpec((None, HALO, tw),
                               lambda b, i, j: (b, jnp.minimum((i + 1) * (tc // HALO), nbh - 1), c0 + j)),
                  pl.BlockSpec((SSD_CONV_W, tw), lambda b, i, j: (0, j)),
                  pl.BlockSpec((1, tw), lambda b, i, j: (0, j))],
        out_specs=pl.BlockSpec((None, tc, tw), lambda b, i, j: (b, i, j)),
        out_shape=jax.ShapeDtypeStruct((bsz, n, SSD_XBC), BF16),
        compiler_params=_cparams(("parallel", "parallel", "parallel")),
        name="ssd_conv",
    )(pa, pa, pa, conv_w, conv_b.reshape(1, SSD_XBC))


def _tri(q, rev):
    r = lax.broadcasted_iota(I32, (q, q), 0)
    c = lax.broadcasted_iota(I32, (q, q), 1)
    return (c >= r) if rev else (r >= c)


def _split3(x):
    hi = x.astype(BF16)
    r = x - hi.astype(F32)
    mid = r.astype(BF16)
    lo = (r - mid.astype(F32)).astype(BF16)
    return hi, mid, lo


def _select_rows_dot(sel01, x):
    hi, mid, lo = _split3(x)
    return (jnp.dot(sel01, hi, preferred_element_type=F32) + jnp.dot(sel01, mid, preferred_element_type=F32)
            + jnp.dot(sel01, lo, preferred_element_type=F32))


def _select_cols_dot(x, sel01):
    hi, mid, lo = _split3(x)
    return (jnp.dot(hi, sel01, preferred_element_type=F32) + jnp.dot(mid, sel01, preferred_element_type=F32)
            + jnp.dot(lo, sel01, preferred_element_type=F32))


def _ssd_scan_kernel(*refs, rev, has_init, readout):
    it = iter(refs)
    xact_ref, dt_ref, alog_ref, bias_ref = next(it), next(it), next(it), next(it)
    init_ref = next(it) if has_init else None
    if readout:
        z_ref, yprev_ref, d_ref, nw_ref = next(it), next(it), next(it), next(it)
    y_ref, st_ref = next(it), next(it)
    ybuf_ref = next(it) if readout else None

    q = SCAN_Q
    hd = SSD_D_INNER // SSD_HEADS
    hg = SSD_HEADS // SSD_GROUPS
    gw = hg * hd
    d_off = SSD_HEADS if rev else 0

    @pl.when(pl.program_id(1) == 0)
    def _():
        st_ref[...] = init_ref[...] if has_init else jnp.zeros_like(st_ref)

    tri = _tri(q, rev)
    dtv = jax.nn.softplus(dt_ref[...] + bias_ref[...])
    da = dtv * (-LOG2E * jnp.exp(alog_ref[...]))
    cs = _select_rows_dot(jnp.where(tri, 1.0, 0.0).astype(BF16), da)
    cs_t = cs.T
    erow = lax.broadcasted_iota(I32, (LANES, SSD_D_INNER), 0)
    ecol = lax.broadcasted_iota(I32, (LANES, SSD_D_INNER), 1)
    expand = jnp.where(erow == d_off + ecol // hd, 1.0, 0.0).astype(BF16)
    csx = _select_cols_dot(cs, expand)
    dtx = _select_cols_dot(dtv, expand)
    x = xact_ref[:, :SSD_D_INNER].astype(F32)
    xdt = x * dtx
    xdt_b = xdt.astype(BF16)
    cs_end = csx[0:1] if rev else csx[q - 1:q]
    ecs = jnp.exp2(csx)
    xw = (xdt * jnp.exp2(cs_end - csx)).astype(BF16)
    ecs_end = jnp.exp2(cs_end)
    lane = lax.broadcasted_iota(I32, (q, LANES), 1)
    out_ref = ybuf_ref if readout else y_ref

    for g in range(SSD_GROUPS):
        bg = xact_ref[:, SSD_D_INNER + g * SSD_STATE:SSD_D_INNER + (g + 1) * SSD_STATE]
        cg = xact_ref[:, SSD_D_INNER + (SSD_GROUPS + g) * SSD_STATE:SSD_D_INNER + (SSD_GROUPS + g + 1) * SSD_STATE]
        cb = lax.dot_general(cg, bg, (((1,), (1,)), ((), ())), preferred_element_type=F32)
        st_g = st_ref[:, g * gw:(g + 1) * gw]
        y_inter = jnp.dot(cg, st_g.astype(BF16), preferred_element_type=F32) * ecs[:, g * gw:(g + 1) * gw]
        for jj in range(hg // 2):
            col0 = g * gw + jj * LANES
            xp = xdt_b[:, col0:col0 + LANES]
            acc = y_inter[:, jj * LANES:(jj + 1) * LANES]
            for s in range(2):
                j = d_off + g * hg + 2 * jj + s
                e = cs[:, j:j + 1] - cs_t[j:j + 1, :]
                seg = jnp.exp2(jnp.where(tri, e, -jnp.inf))
                m = (cb * seg).astype(BF16)
                xm = jnp.where((lane >= hd) if s else (lane < hd), xp, jnp.zeros_like(xp))
                acc = acc + jnp.dot(m, xm, preferred_element_type=F32)
            out_ref[:, col0:col0 + LANES] = acc
        upd = lax.dot_general(bg, xw[:, g * gw:(g + 1) * gw], (((0,), (0,)), ((), ())),
                              preferred_element_type=F32)
        st_ref[:, g * gw:(g + 1) * gw] = st_g * ecs_end[:, g * gw:(g + 1) * gw] + upd

    if readout:
        z = z_ref[...].astype(F32)
        yy = (yprev_ref[...] + ybuf_ref[...] + d_ref[...] * x) * _silu(z)
        for g in range(SSD_GROUPS):
            seg = yy[:, g * gw:(g + 1) * gw]
            ms = jnp.mean(seg * seg, axis=-1, keepdims=True)
            y_ref[:, g * gw:(g + 1) * gw] = (seg * lax.rsqrt(ms + EPS) * nw_ref[:, g * gw:(g + 1) * gw]).astype(y_ref.dtype)


def _ssd_scan(xact, pf, alog128, bias128, *, rev, init=None, readout=None):
    bsz, n, _ = xact.shape
    nc = n // SCAN_Q
    cidx = (lambda c: nc - 1 - c) if rev else (lambda c: c)
    in_specs = [pl.BlockSpec((None, SCAN_Q, SSD_XBC), lambda b, c: (b, cidx(c), 0)),
                pl.BlockSpec((None, SCAN_Q, LANES), lambda b, c: (b, cidx(c), PF_DT // LANES)),
                pl.BlockSpec((1, LANES), lambda b, c: (0, 0)),
                pl.BlockSpec((1, LANES), lambda b, c: (0, 0))]
    args = [xact, pf, alog128, bias128]
    if init is not None:
        in_specs.append(pl.BlockSpec((None, SSD_STATE, SSD_D_INNER), lambda b, c: (b, 0, 0)))
        args.append(init)
    scratch = []
    if readout is not None:
        pa, yprev, d_full, norm_w = readout
        in_specs += [pl.BlockSpec((None, SCAN_Q, SSD_D_INNER), lambda b, c: (b, cidx(c), PA_Z // SSD_D_INNER)),
                     pl.BlockSpec((None, SCAN_Q, SSD_D_INNER), lambda b, c: (b, cidx(c), 0)),
                     pl.BlockSpec((1, SSD_D_INNER), lambda b, c: (0, 0)),
                     pl.BlockSpec((1, SSD_D_INNER), lambda b, c: (0, 0))]
        args += [pa, yprev, d_full, norm_w]
        scratch = [pltpu.VMEM((SCAN_Q, SSD_D_INNER), F32)]
    y_dtype = BF16 if readout is not None else F32
    y, st = pl.pallas_call(
        functools.partial(_ssd_scan_kernel, rev=rev, has_init=init is not None, readout=readout is not None),
        grid=(bsz, nc),
        in_specs=in_specs,
        out_specs=[pl.BlockSpec((None, SCAN_Q, SSD_D_INNER), lambda b, c: (b, cidx(c), 0)),
                   pl.BlockSpec((None, SSD_STATE, SSD_D_INNER), lambda b, c: (b, 0, 0))],
        out_shape=[jax.ShapeDtypeStruct((bsz, n, SSD_D_INNER), y_dtype),
                   jax.ShapeDtypeStruct((bsz, SSD_STATE, SSD_D_INNER), F32)],
        scratch_shapes=scratch,
        compiler_params=_cparams(("parallel", "arbitrary")),
        name="ssd_scan_bwd" if rev else "ssd_scan_fwd",
    )(*args)
    return y, st


def _ref_rows(bb, s, rev):
    q, w = bb.shape
    rl = s if rev else s - 1
    if 2 * s >= 8:
        b3 = bb.reshape(q // (2 * s), 2 * s, w)
        return jnp.broadcast_to(b3[:, rl:rl + 1, :], b3.shape).reshape(q, w)
    off = lax.broadcasted_iota(I32, (q, w), 0) % (2 * s)
    out = bb
    for ov in range(2 * s):
        if ov != rl:
            out = jnp.where(off == ov, pltpu.roll(bb, (ov - rl) % q, 0), out)
    return out


def _hg_scan_kernel(*refs, rev, has_init, readout):
    it = iter(refs)
    q_ref, f_ref, v_ref, lb_ref = next(it), next(it), next(it), next(it)
    init_ref = next(it) if has_init else None
    if readout:
        g_ref, oprev_ref, nw_ref = next(it), next(it), next(it)
    o_ref, st_ref, att_ref = next(it), next(it), next(it)
    obuf_ref = next(it) if readout else None

    q = SCAN_Q
    hd = HG_HEAD_DIM

    @pl.when(pl.program_id(1) == 0)
    def _():
        st_ref[...] = init_ref[...] if has_init else jnp.zeros_like(st_ref)

    lb = lb_ref[...]
    qq = _silu(q_ref[...].astype(F32))
    f = lb + (1.0 - lb) * jax.nn.sigmoid(f_ref[...])
    kk = 1.0 - f
    lf = jnp.log(f) * LOG2E
    tri = _tri(q, rev)
    bb = jnp.dot(jnp.where(tri, 1.0, 0.0), lf, preferred_element_type=F32, precision=HIGHEST)

    row = lax.broadcasted_iota(I32, (q, HG_WIDTH), 0)
    ri = lax.broadcasted_iota(I32, (q, q), 0)
    ci = lax.broadcasted_iota(I32, (q, q), 1)
    xr = ri ^ ci
    causal = (ci > ri) if rev else (ri > ci)
    att_ref[...] = jnp.zeros_like(att_ref)
    s = q // 2
    while s >= 1:
        upper = (row & s) != 0
        q_side = jnp.logical_not(upper) if rev else upper
        ex = jnp.exp2(-jnp.abs(bb - _ref_rows(bb, s, rev)))
        u = (jnp.where(q_side, qq, kk) * ex).astype(BF16)
        level = jnp.logical_and((xr >> int(math.log2(s))) == 1, causal)
        for h in range(HG_HEADS):
            uh = u[:, h * hd:(h + 1) * hd]
            p = lax.dot_general(uh, uh, (((1,), (1,)), ((), ())), preferred_element_type=F32)
            att_ref[h] = jnp.where(level, p, att_ref[h])
        s //= 2

    b_end = bb[0:1] if rev else bb[q - 1:q]
    qe = (qq * jnp.exp2(bb)).astype(BF16)
    kh = (kk * jnp.exp2(b_end - bb)).astype(BF16)
    qk = (qq * kk).astype(BF16)
    e_end = jnp.exp2(b_end)
    ones = jnp.ones((hd, hd), BF16)
    out_ref = obuf_ref if readout else o_ref
    for h in range(HG_HEADS):
        sl = slice(h * hd, (h + 1) * hd)
        vh = v_ref[:, sl]
        st_h = st_ref[sl, :]
        o = jnp.dot(att_ref[h].astype(BF16), vh, preferred_element_type=F32)
        o = o + jnp.dot(qk[:, sl], ones, preferred_element_type=F32) * vh.astype(F32)
        o = o + lax.dot_general(qe[:, sl], st_h.astype(BF16), (((1,), (1,)), ((), ())), preferred_element_type=F32)
        out_ref[:, sl] = o
        upd = lax.dot_general(vh, kh[:, sl], (((0,), (0,)), ((), ())), preferred_element_type=F32)
        st_ref[sl, :] = st_h * e_end[:, sl] + upd

    if readout:
        for h in range(HG_HEADS):
            sl = slice(h * hd, (h + 1) * hd)
            o = oprev_ref[:, sl] + obuf_ref[:, sl]
            ms = jnp.mean(o * o, axis=-1, keepdims=True)
            gate = _silu(g_ref[:, sl].astype(F32))
            o_ref[:, sl] = (o * lax.rsqrt(ms + EPS) * nw_ref[...] * gate).astype(o_ref.dtype)


def _hg_scan(pa, pf, lb_row, *, rev, init=None, readout=None):
    bsz, n, _ = pa.shape
    nc = n // SCAN_Q
    cidx = (lambda c: nc - 1 - c) if rev else (lambda c: c)
    fcol = (PF_F // HG_WIDTH) + (1 if rev else 0)
    in_specs = [pl.BlockSpec((None, SCAN_Q, HG_WIDTH), lambda b, c: (b, cidx(c), PA_HQ // HG_WIDTH)),
                pl.BlockSpec((None, SCAN_Q, HG_WIDTH), lambda b, c: (b, cidx(c), fcol)),
                pl.BlockSpec((None, SCAN_Q, HG_WIDTH), lambda b, c: (b, cidx(c), PA_HI // HG_WIDTH)),
                pl.BlockSpec((1, HG_WIDTH), lambda b, c: (0, 0))]
    args = [pa, pf, pa, lb_row]
    if init is not None:
        in_specs.append(pl.BlockSpec((None, HG_WIDTH, HG_HEAD_DIM), lambda b, c: (b, 0, 0)))
        args.append(init)
    scratch = [pltpu.VMEM((HG_HEADS, SCAN_Q, SCAN_Q), F32)]
    if readout is not None:
        oprev, nw128 = readout
        in_specs += [pl.BlockSpec((None, SCAN_Q, HG_WIDTH), lambda b, c: (b, cidx(c), PA_HG // HG_WIDTH)),
                     pl.BlockSpec((None, SCAN_Q, HG_WIDTH), lambda b, c: (b, cidx(c), 0)),
                     pl.BlockSpec((1, HG_HEAD_DIM), lambda b, c: (0, 0))]
        args += [pa, oprev, nw128]
        scratch.append(pltpu.VMEM((SCAN_Q, HG_WIDTH), F32))
    o_dtype = BF16 if readout is not None else F32
    assert PA_HQ % HG_WIDTH == 0 or True
    o, st = pl.pallas_call(
        functools.partial(_hg_scan_kernel, rev=rev, has_init=init is not None, readout=readout is not None),
        grid=(bsz, nc),
        in_specs=in_specs,
        out_specs=[pl.BlockSpec((None, SCAN_Q, HG_WIDTH), lambda b, c: (b, cidx(c), 0)),
                   pl.BlockSpec((None, HG_WIDTH, HG_HEAD_DIM), lambda b, c: (b, 0, 0))],
        out_shape=[jax.ShapeDtypeStruct((bsz, n, HG_WIDTH), o_dtype),
                   jax.ShapeDtypeStruct((bsz, HG_WIDTH, HG_HEAD_DIM), F32)],
        scratch_shapes=scratch,
        compiler_params=_cparams(("parallel", "arbitrary")),
        name="hg_scan_bwd" if rev else "hg_scan_fwd",
    )(*args)
    return o, st


def _attn_kernel(*refs, seg_lens, tk, lam_init):
    lam_ref, nw_ref, q_ref = refs[0], refs[1], refs[2]
    nseg = len(seg_lens)
    kv_refs = refs[3:3 + 2 * nseg]
    o_ref, qs_ref, sa_ref, sb_ref, m_ref, mn_ref, acc_ref = refs[3 + 2 * nseg:10 + 2 * nseg]
    vaug_refs = refs[10 + 2 * nseg:]
    tq = q_ref.shape[0]

    @pl.when(pl.program_id(2) == 0)
    def _():
        for si in range(nseg):
            v_ref, vaug_ref = kv_refs[2 * si + 1], vaug_refs[si]
            vaug_ref[:, :2 * DA_HEAD_DIM] = v_ref[...]
            vaug_ref[:, 2 * DA_HEAD_DIM:] = jnp.ones(v_ref.shape, BF16)

    qv = q_ref[...] * jnp.asarray(DA_HEAD_DIM ** -0.5, BF16)
    lane = lax.broadcasted_iota(I32, qv.shape, 1)
    zero = jnp.zeros_like(qv)
    qs_ref[:tq] = jnp.where(lane < DA_HEAD_DIM, qv, zero)
    qs_ref[tq:] = jnp.where(lane >= DA_HEAD_DIM, qv, zero)
    m_ref[...] = jnp.full_like(m_ref, -jnp.inf)
    acc_ref[...] = jnp.zeros_like(acc_ref)

    def scores(kc):
        return lax.dot_general(qs_ref[...], kc, (((1,), (1,)), ((), ())), preferred_element_type=F32)

    def lane_tiles(a):
        return [a[:, c * LANES:(c + 1) * LANES] for c in range(a.shape[1] // LANES)]

    def row_max(s, m_prev):
        tiles = lane_tiles(s)
        m = tiles[0]
        for tile in tiles[1:]:
            m = jnp.maximum(m, tile)
        return jnp.maximum(m_prev, jnp.broadcast_to(jnp.max(m, axis=-1, keepdims=True), m_prev.shape))

    def online_step(s, vc, m_prev, m_cur):
        alpha = jnp.exp(m_prev - m_cur)
        ps = [jnp.exp(tile - m_cur) for tile in lane_tiles(s)]
        pv = jnp.dot(jnp.concatenate(ps, axis=1).astype(BF16), vc, preferred_element_type=F32)
        acc_ref[...] = jnp.concatenate([alpha, alpha], axis=1) * acc_ref[...] + pv

    for si, n in enumerate(seg_lens):
        k_ref, v_ref = kv_refs[2 * si], vaug_refs[si]
        t = min(tk, n)
        nc = n // t
        if nc == 1:
            s = scores(k_ref[...])
            m_prev = m_ref[...]
            m_cur = row_max(s, m_prev)
            online_step(s, v_ref[...], m_prev, m_cur)
            m_ref[...] = m_cur
        else:
            s0 = scores(k_ref[pl.ds(0, t), :])
            sa_ref[...] = s0
            mn_ref[...] = row_max(s0, m_ref[...])

            def pipe_step(c, cur_ref, nxt_ref, k_ref=k_ref, v_ref=v_ref, t=t):
                r0 = pl.multiple_of(c * t, t)
                r1 = pl.multiple_of(r0 + t, t)
                m_prev, m_cur = m_ref[...], mn_ref[...]
                s_next = scores(k_ref[pl.ds(r1, t), :])
                nxt_ref[...] = s_next
                mn_ref[...] = row_max(s_next, m_cur)
                m_ref[...] = m_cur
                online_step(cur_ref[...], v_ref[pl.ds(r0, t), :], m_prev, m_cur)

            def body(i, carry):
                pipe_step(2 * i, sa_ref, sb_ref)
                pipe_step(2 * i + 1, sb_ref, sa_ref)
                return carry

            lax.fori_loop(0, (nc - 1) // 2, body, 0)
            last_ref = sa_ref
            if nc % 2 == 0:
                pipe_step(nc - 2, sa_ref, sb_ref)
                last_ref = sb_ref
            m_prev, m_cur = m_ref[...], mn_ref[...]
            online_step(last_ref[...], v_ref[pl.ds((nc - 1) * t, t), :], m_prev, m_cur)
            m_ref[...] = m_cur
    hw = 2 * DA_HEAD_DIM
    acc = acc_ref[:, :hw]
    l = acc_ref[:, hw:]
    lam = lam_ref[...]
    lmbda = (jnp.exp(jnp.sum(lam[0:1] * lam[1:2], axis=-1, keepdims=True))
             - jnp.exp(jnp.sum(lam[2:3] * lam[3:4], axis=-1, keepdims=True)) + lam_init)
    o = acc[:tq] / l[:tq] - lmbda * (acc[tq:] / l[tq:])
    ms = jnp.mean(o * o, axis=-1, keepdims=True)
    o_ref[...] = (o * lax.rsqrt(ms + EPS) * nw_ref[...] * (1.0 - lam_init)).astype(o_ref.dtype)


def _diff_attn(pqk_q, kv_segs, lam, nw, lam_init, tq, tk):
    bsz, nq, _ = pqk_q.shape
    hw = 2 * DA_HEAD_DIM
    in_specs = [pl.BlockSpec((4, DA_HEAD_DIM), lambda b, h, i: (0, 0)),
                pl.BlockSpec((1, hw), lambda b, h, i: (0, 0)),
                pl.BlockSpec((None, tq, hw), lambda b, h, i: (b, i, h))]
    args = [lam, nw.reshape(1, hw), pqk_q]
    seg_lens = []
    for k_arr, k_col, v_arr, v_col in kv_segs:
        nk = k_arr.shape[1]
        assert nk % min(tk, nk) == 0
        seg_lens.append(nk)
        in_specs += [pl.BlockSpec((None, nk, hw), functools.partial(lambda b, h, i, c: (b, 0, c + h), c=k_col // hw)),
                     pl.BlockSpec((None, nk, hw), functools.partial(lambda b, h, i, c: (b, 0, c + h), c=v_col // hw))]
        args += [k_arr, v_arr]
    return pl.pallas_call(
        functools.partial(_attn_kernel, seg_lens=tuple(seg_lens), tk=tk, lam_init=lam_init),
        grid=(bsz, DA_HEADS, nq // tq),
        in_specs=in_specs,
        out_specs=pl.BlockSpec((None, tq, hw), lambda b, h, i: (b, i, h)),
        out_shape=jax.ShapeDtypeStruct((bsz, nq, DA_WIDTH), BF16),
        scratch_shapes=[pltpu.VMEM((2 * tq, hw), BF16),
                        pltpu.VMEM((2 * tq, tk), F32),
                        pltpu.VMEM((2 * tq, tk), F32),
                        pltpu.VMEM((2 * tq, LANES), F32),
                        pltpu.VMEM((2 * tq, LANES), F32),
                        pltpu.VMEM((2 * tq, 2 * hw), F32)] + [pltpu.VMEM((nk, 2 * hw), BF16) for nk in seg_lens],
        compiler_params=_cparams(("parallel", "parallel", "arbitrary")),
        name="diff_attn",
    )(*args)


def _mix_kernel(h_ref, ya_ref, yb_ref, yd_ref, wg0, wg1, wg2, bg0, bg1, bg2, wu0, wu1, wu2, o_ref):
    h = h_ref[...]
    mix = None
    for y_ref, wg, bg, wu in ((ya_ref, wg0, bg0, wu0), (yb_ref, wg1, bg1, wu1), (yd_ref, wg2, bg2, wu2)):
        gate = jax.nn.sigmoid(jnp.dot(h, wg[...], preferred_element_type=F32) + bg[...])
        up = jnp.dot(y_ref[...], wu[...], preferred_element_type=F32)
        mix = gate * up if mix is None else mix + gate * up
    o_ref[...] = mix.astype(o_ref.dtype)


def _residual_out_kernel(x_ref, g1_ref, m_ref, wo_ref, o_ref):
    o_ref[...] = x_ref[...] + g1_ref[...] * jnp.dot(m_ref[...], wo_ref[...], preferred_element_type=F32)


def _merge(x, gate1, h, ya, yb, yd, w_gate, b_gate, w_up, w_out):
    bsz, n, _ = x.shape
    tm = min(1024, n)
    tn = 512
    nn = D // tn
    row = lambda b, i, j: (b, i, 0)
    in_specs = [pl.BlockSpec((None, tm, D), row),
                pl.BlockSpec((None, tm, 1024), row),
                pl.BlockSpec((None, tm, 1024), row),
                pl.BlockSpec((None, tm, 1024), row)]
    in_specs += [pl.BlockSpec((D, tn), functools.partial(lambda b, i, j, k: (0, k * nn + j), k=k)) for k in range(3)]
    in_specs += [pl.BlockSpec((1, tn), functools.partial(lambda b, i, j, k: (0, k * nn + j), k=k)) for k in range(3)]
    in_specs += [pl.BlockSpec((None, 1024, tn), functools.partial(lambda b, i, j, k: (k, 0, j), k=k)) for k in range(3)]
    mix = pl.pallas_call(
        _mix_kernel,
        grid=(bsz, n // tm, nn),
        in_specs=in_specs,
        out_specs=pl.BlockSpec((None, tm, tn), lambda b, i, j: (b, i, j)),
        out_shape=jax.ShapeDtypeStruct((bsz, n, D), BF16),
        compiler_params=_cparams(("parallel", "parallel", "parallel")),
        name="merge_mix",
    )(h, ya, yb, yd, w_gate, w_gate, w_gate, b_gate, b_gate, b_gate, w_up, w_up, w_up)
    return pl.pallas_call(
        _residual_out_kernel,
        grid=(bsz, n // tm, nn),
        in_specs=[pl.BlockSpec((None, tm, tn), lambda b, i, j: (b, i, j)),
                  pl.BlockSpec((None, 1, tn), lambda b, i, j: (b, 0, j)),
                  pl.BlockSpec((None, tm, D), row),
                  pl.BlockSpec((D, tn), lambda b, i, j: (0, j))],
        out_specs=pl.BlockSpec((None, tm, tn), lambda b, i, j: (b, i, j)),
        out_shape=jax.ShapeDtypeStruct((bsz, n, D), F32),
        compiler_params=_cparams(("parallel", "parallel", "parallel")),
        name="merge_out",
    )(x, gate1, mix, w_out)


def _route_kernel(aff_ref, posm_ref, pos_ref, *, cap, rpe):
    rows = N_EXPERTS * rpe
    bits = pltpu.bitcast(aff_ref[...], I32)
    ones = jnp.ones((LANES, LANES), BF16)
    r = lax.broadcasted_iota(I32, (rows, rows), 0)
    c = lax.broadcasted_iota(I32, (rows, rows), 1)
    same = (r // rpe) == (c // rpe)
    grp = same.astype(BF16)
    grp_before = jnp.logical_and(same, c < r).astype(BF16)
    lr = lax.broadcasted_iota(I32, (LANES, LANES), 0)
    lc = lax.broadcasted_iota(I32, (LANES, LANES), 1)
    before = (lr < lc).astype(BF16)

    def count(mask):
        per_row = jnp.dot(mask.astype(BF16), ones, preferred_element_type=F32)
        return jnp.dot(grp, per_row.astype(BF16), preferred_element_type=F32)

    def excl_prefix(mask):
        mb = mask.astype(BF16)
        within = jnp.dot(mb, before, preferred_element_type=F32)
        per_row = jnp.dot(mb, ones, preferred_element_type=F32)
        return within + jnp.dot(grp_before, per_row.astype(BF16), preferred_element_type=F32)

    def body(i, thr):
        cand = thr | (jnp.int32(1) << (30 - i))
        return jnp.where(count(bits >= cand) >= cap, cand, thr)

    thr = lax.fori_loop(0, 31, body, jnp.zeros((rows, LANES), I32))
    gt = bits > thr
    eq = bits == thr
    need = cap - count(gt)
    sel = jnp.logical_or(gt, jnp.logical_and(eq, excl_prefix(eq) < need))
    pos = excl_prefix(sel).astype(I32)
    pos_ref[...] = pos
    posm_ref[...] = jnp.where(sel, pos, -1)


def _route(aff, cap):
    bsz, n, _ = aff.shape
    rpe = n // LANES
    rows = N_EXPERTS * rpe
    aff_t = jnp.swapaxes(aff, 1, 2).reshape(bsz, rows, LANES)
    spec = pl.BlockSpec((None, rows, LANES), lambda b: (b, 0, 0))
    posm, pos = pl.pallas_call(
        functools.partial(_route_kernel, cap=cap, rpe=rpe),
        grid=(bsz,),
        in_specs=[spec], out_specs=[spec, spec],
        out_shape=[jax.ShapeDtypeStruct((bsz, rows, LANES), I32)] * 2,
        compiler_params=_cparams(("parallel",)),
        name="route",
    )(aff_t)
    return posm.reshape(bsz, N_EXPERTS, n), pos.reshape(bsz, N_EXPERTS, n)


def _gather_kernel(base_ref, base_small_ref, small_ref, h_ref, posm_ref, xg_ref, *, win, win_small, nblk):
    b = pl.program_id(0)
    xg_ref[...] = jnp.zeros_like(xg_ref)

    def scatter_block(j, bases_ref, w):
        t0 = pl.multiple_of(j * TOK_BLK, TOK_BLK)
        slot = lax.broadcasted_iota(I32, (w, TOK_BLK), 0)
        bases, onehots = [], []
        for e in range(N_EXPERTS):
            base = pl.multiple_of(bases_ref[(b * N_EXPERTS + e) * nblk + j], SLOT_ALIGN)
            rel = posm_ref[e:e + 1, pl.ds(t0, TOK_BLK)] - base
            bases.append(base)
            onehots.append(jnp.where(slot == rel, 1.0, 0.0).astype(BF16))
        rows = jnp.dot(jnp.concatenate(onehots, axis=0), h_ref[pl.ds(t0, TOK_BLK), :],
                       preferred_element_type=F32).astype(BF16)
        for e in range(N_EXPERTS):
            xg_ref[e, pl.ds(bases[e], w), :] += rows[e * w:(e + 1) * w]

    def body(j, carry):
        small = small_ref[b * nblk + j] != 0

        @pl.when(small)
        def _():
            scatter_block(j, base_small_ref, win_small)

        @pl.when(jnp.logical_not(small))
        def _():
            scatter_block(j, base_ref, win)

        return carry

    lax.fori_loop(0, nblk, body, 0)


def _expert_kernel(xg_ref, w1_ref, w3_ref, w2_ref, y_ref, acc_ref):
    f = pl.program_id(2)

    @pl.when(f == 0)
    def _():
        acc_ref[...] = jnp.zeros_like(acc_ref)

    xg = xg_ref[...]
    a = jnp.dot(xg, w1_ref[...].astype(BF16), preferred_element_type=F32)
    g = jnp.dot(xg, w3_ref[...].astype(BF16), preferred_element_type=F32)
    acc_ref[...] += jnp.dot((_silu(a) * g).astype(BF16), w2_ref[...].astype(BF16), preferred_element_type=F32)

    @pl.when(f == pl.num_programs(2) - 1)
    def _():
        y_ref[...] = acc_ref[...].astype(y_ref.dtype)


def _combine_kernel(base_ref, x_ref, g2_ref, aff_ref, posm_ref, *rest, win, win_tile, cap, nblk, nsub, final):
    y_refs = rest[:EXPERT_GROUP]
    if final:
        fw_ref, o_ref, acc_ref = rest[EXPERT_GROUP:]
    else:
        o_ref, acc_ref = rest[EXPERT_GROUP:]
    b, jt, eg = pl.program_id(0), pl.program_id(1), pl.program_id(2)

    @pl.when(eg == 0)
    def _():
        acc_ref[...] = jnp.zeros_like(acc_ref)

    lane = lax.broadcasted_iota(I32, (TOK_BLK, N_EXPERTS), 1)
    slot = lax.broadcasted_iota(I32, (TOK_BLK, win), 1)
    for sb in range(nsub):
        rows = slice(sb * TOK_BLK, (sb + 1) * TOK_BLK)
        total = None
        for k, y_ref in enumerate(y_refs):
            e = eg * EXPERT_GROUP + k
            first = (b * N_EXPERTS + e) * nblk + jt * nsub
            tile_base = jnp.minimum(base_ref[first], cap - win_tile)
            base = base_ref[first + sb]
            mine = lane == e
            rel = jnp.sum(jnp.where(mine, posm_ref[rows, :], 0), axis=-1, keepdims=True) - base
            val = jnp.sum(jnp.where(mine, aff_ref[rows, :], 0.0), axis=-1, keepdims=True)
            onehot = jnp.where(slot == rel, 1.0, 0.0).astype(BF16)
            off = pl.multiple_of(base - tile_base, SLOT_ALIGN)
            part = val * jnp.dot(onehot, y_ref[pl.ds(off, win), :], preferred_element_type=F32)
            total = part if total is None else total + part
        acc_ref[rows, :] += total

    @pl.when(eg == pl.num_programs(2) - 1)
    def _():
        out = x_ref[...] + g2_ref[...] * acc_ref[...]
        if final:
            ms = jnp.mean(out * out, axis=-1, keepdims=True)
            out = out * lax.rsqrt(ms + EPS) * fw_ref[...]
        o_ref[...] = out


def _ec_moe(token_sets, w1, w3, w2, layer):
    bsz = token_sets[0][0].shape[0]
    plans, xgs = [], []
    for x, gate2, h2, aff, final_w in token_sets:
        n = x.shape[1]
        cap = EC_CAPACITY * n // N_EXPERTS
        nblk = n // TOK_BLK
        win = min(TOK_BLK + SLOT_ALIGN, cap)
        posm, pos = _route(aff, cap)
        start = pos[:, :, ::TOK_BLK]
        aligned = (start // SLOT_ALIGN) * SLOT_ALIGN
        base = jnp.minimum(aligned, cap - win).astype(I32).reshape(-1)
        win_small = min(GATHER_WIN_SMALL, win)
        base_small = jnp.minimum(aligned, cap - win_small).astype(I32).reshape(-1)
        count = jnp.concatenate([start[:, :, 1:], jnp.full_like(start[:, :, :1], cap)], axis=2) - start
        small = jnp.all(count <= win_small - SLOT_ALIGN, axis=1).astype(I32).reshape(-1)
        dq = GATHER_COLS
        xgs.append(pl.pallas_call(
            functools.partial(_gather_kernel, win=win, win_small=win_small, nblk=nblk),
            grid_spec=pltpu.PrefetchScalarGridSpec(
                num_scalar_prefetch=3,
                grid=(bsz, D // dq),
                in_specs=[pl.BlockSpec((None, n, dq), lambda b, c, *_: (b, 0, c)),
                          pl.BlockSpec((None, N_EXPERTS, n), lambda b, c, *_: (b, 0, 0))],
                out_specs=pl.BlockSpec((None, N_EXPERTS, cap, dq), lambda b, c, *_: (b, 0, 0, c))),
            out_shape=jax.ShapeDtypeStruct((bsz, N_EXPERTS, cap, D), BF16),
            compiler_params=_cparams(("parallel", "parallel")),
            name="moe_gather",
        )(base, base_small, small, h2, posm))
        plans.append((n, cap, nblk, win, posm, base))

    xg = xgs[0] if len(xgs) == 1 else jnp.concatenate(xgs, axis=2)
    cap_all = xg.shape[2]
    tf = 256
    y = pl.pallas_call(
        _expert_kernel,
        grid=(N_EXPERTS, bsz, EXPERT_FF // tf),
        in_specs=[pl.BlockSpec((None, None, cap_all, D), lambda e, b, f: (b, e, 0, 0)),
                  pl.BlockSpec((None, None, D, tf), lambda e, b, f: (layer, e, 0, f)),
                  pl.BlockSpec((None, None, D, tf), lambda e, b, f: (layer, e, 0, f)),
                  pl.BlockSpec((None, None, tf, D), lambda e, b, f: (layer, e, f, 0))],
        out_specs=pl.BlockSpec((None, None, cap_all, D), lambda e, b, f: (b, e, 0, 0)),
        out_shape=jax.ShapeDtypeStruct((bsz, N_EXPERTS, cap_all, D), BF16),
        scratch_shapes=[pltpu.VMEM((cap_all, D), F32)],
        compiler_params=_cparams(("parallel", "parallel", "arbitrary")),
        name="moe_expert",
    )(xg, w1, w3, w2)
    y2d = y.reshape(bsz * N_EXPERTS * cap_all, D)

    outs = []
    row0 = 0
    for (x, gate2, h2, aff, final_w), (n, cap, nblk, win, posm, base) in zip(token_sets, plans):
        tile = min(COMBINE_TILE, n)
        nsub = tile // TOK_BLK
        win_tile = min(tile + SLOT_ALIGN, cap)
        final = final_w is not None

        def y_window(b, j, eg, base, k, cap=cap, nblk=nblk, nsub=nsub, win_tile=win_tile, row0=row0):
            be = b * N_EXPERTS + eg * EXPERT_GROUP + k
            start = jnp.minimum(base[be * nblk + j * nsub], cap - win_tile)
            return pl.multiple_of(be * cap_all + row0 + start, SLOT_ALIGN), 0

        in_specs = [pl.BlockSpec((None, tile, D), lambda b, j, e, base: (b, j, 0)),
                    pl.BlockSpec((None, 1, D), lambda b, j, e, base: (b, 0, 0)),
                    pl.BlockSpec((None, tile, N_EXPERTS), lambda b, j, e, base: (b, j, 0)),
                    pl.BlockSpec((None, tile, N_EXPERTS), lambda b, j, e, base: (b, j, 0))]
        in_specs += [pl.BlockSpec((pl.Element(win_tile), pl.Element(D)), functools.partial(y_window, k=k))
                     for k in range(EXPERT_GROUP)]
        args = [base, x, gate2, aff, jnp.swapaxes(posm, 1, 2)] + [y2d] * EXPERT_GROUP
        if final:
            in_specs.append(pl.BlockSpec((1, D), lambda b, j, e, base: (0, 0)))
            args.append(final_w.reshape(1, D))
        outs.append(pl.pallas_call(
            functools.partial(_combine_kernel, win=win, win_tile=win_tile, cap=cap, nblk=nblk, nsub=nsub, final=final),
            grid_spec=pltpu.PrefetchScalarGridSpec(
                num_scalar_prefetch=1,
                grid=(bsz, n // tile, N_EXPERTS // EXPERT_GROUP),
                in_specs=in_specs,
                out_specs=pl.BlockSpec((None, tile, D), lambda b, j, e, base: (b, j, 0)),
                scratch_shapes=[pltpu.VMEM((tile, D), F32)]),
            out_shape=jax.ShapeDtypeStruct((bsz, n, D), F32),
            compiler_params=_cparams(("parallel", "parallel", "arbitrary")),
            name="moe_combine",
        )(*args))
        row0 += cap
    return outs


def _rope_tables(n_tokens):
    rows = n_tokens // GRID_W
    row = jnp.repeat(jnp.arange(rows, dtype=F32), GRID_W)
    col = (jnp.arange(rows * GRID_W, dtype=I32) % GRID_W).astype(F32)
    inv = ROPE_BASE ** (-jnp.arange(ROPE_PAIRS, dtype=F32) / ROPE_PAIRS)
    ang = jnp.concatenate([row[:, None] * inv, col[:, None] * inv], axis=-1)
    cos, sin = jnp.cos(ang), jnp.sin(ang)
    reps = LANES // DA_HEAD_DIM
    cos_t = jnp.tile(jnp.concatenate([cos, cos], axis=-1), (1, reps))
    sin_t = jnp.tile(jnp.concatenate([-sin, sin], axis=-1), (1, reps))
    return cos_t, sin_t


def _lower_bounds(p):
    cum = jnp.cumsum(jax.nn.softmax(p.astype(F32), axis=0), axis=0)
    return cum - cum[0]


def _pad128(v):
    return jnp.pad(v.reshape(1, -1).astype(F32), ((0, 0), (0, LANES - v.size)))


def kernel(x, c, ctx, c_ctx, ada_w, ada_b, norm1_w, norm2_w, w_in, ssd_conv_w, ssd_conv_b, ssd_dt_bias,
           ssd_a_log, ssd_d, ssd_norm_w, hg_lb, hg_norm_w, da_lambda, da_norm_w, w_up, w_gate, b_gate,
           w_out, moe_router, moe_w1, moe_w3, moe_w2, final_norm_w):
    bsz, n_lat, _ = x.shape
    n_ctx = ctx.shape[1]
    depth = ada_w.shape[0]
    rope = _rope_tables(n_lat)
    lb_all = jnp.stack([_lower_bounds(hg_lb[0]), _lower_bounds(hg_lb[1])], axis=1)

    mod = _adaln(jnp.concatenate([c, c_ctx[None]], axis=0), ada_w, ada_b)
    mod = mod.reshape(depth, 8, 6, 1, D)

    xl, xc = x, ctx
    for l in range(depth):
        need_ctx = l < depth - 1
        lam_init = 0.8 - 0.6 * math.exp(-0.3 * l)
        mod_l = [mod[l, :bsz, k] for k in range(6)]
        mod_c = [jnp.broadcast_to(mod[l, bsz:bsz + 1, k], (bsz, 1, D)) for k in range(6)]

        wl = w_in[l]
        w_a = jnp.concatenate([wl[:, OFF_SSD_Z:OFF_SSD_XBC], wl[:, OFF_HG_Q:OFF_HG_F], wl[:, OFF_HG_I:OFF_DA_Q],
                               wl[:, OFF_DA_V:N_IN], wl[:, OFF_SSD_XBC:OFF_SSD_DT]], axis=1).astype(BF16)
        w_f = jnp.concatenate([wl[:, OFF_HG_F:OFF_HG_I], wl[:, OFF_SSD_DT:OFF_HG_Q],
                               jnp.zeros((D, PF_N - PF_DT - 2 * SSD_HEADS), F32)], axis=1).astype(BF16)
        w_qk = wl[:, OFF_DA_Q:OFF_DA_V].astype(BF16)
        alog128 = _pad128(ssd_a_log[l])
        bias128 = _pad128(ssd_dt_bias[l])
        d_full = jnp.repeat(ssd_d[l].astype(F32), SSD_D_INNER // SSD_HEADS).reshape(1, SSD_D_INNER)
        ssd_nw = ssd_norm_w[l].reshape(1, SSD_D_INNER).astype(F32)
        hg_nw = hg_norm_w[l].reshape(1, HG_HEAD_DIM).astype(F32)
        wg_b = w_gate[l].astype(BF16)
        bg = b_gate[l].reshape(1, N_BRANCH * D)
        wu_b = w_up[l].astype(BF16)
        wo_b = w_out[l].astype(BF16)

        def project(xs, m, use_rope):
            n = xs.shape[1]
            h = _modulate(xs, norm1_w[l], m[0], m[1])
            h2d = h.reshape(bsz * n, D)
            tm = min(1024, n)
            pa = _matmul(h2d, w_a, BF16, tm, 512).reshape(bsz, n, PA_N)
            pf = _matmul(h2d, w_f, F32, min(512, n), PF_N).reshape(bsz, n, PF_N)
            pqk = _matmul(h2d, w_qk, BF16, tm, 512, rope_tables=rope if use_rope else None).reshape(bsz, n, PQK_N)
            return h, pa, pf, pqk

        hl, pa_l, pf_l, pqk_l = project(xl, mod_l, True)
        hc, pa_c, pf_c, pqk_c = project(xc, mod_c, False)

        xa_c = _ssd_conv(pa_c, ssd_conv_w[l], ssd_conv_b[l])
        xa_l = _ssd_conv(pa_l, ssd_conv_w[l], ssd_conv_b[l])
        yc_f, s_f = _ssd_scan(xa_c, pf_c, alog128, bias128, rev=False)
        yl_f, _ = _ssd_scan(xa_l, pf_l, alog128, bias128, rev=False, init=s_f)
        ya_c, s_b = _ssd_scan(xa_c, pf_c, alog128, bias128, rev=True, readout=(pa_c, yc_f, d_full, ssd_nw))
        ya_l, _ = _ssd_scan(xa_l, pf_l, alog128, bias128, rev=True, init=s_b, readout=(pa_l, yl_f, d_full, ssd_nw))

        lb_f, lb_b = lb_all[l, 0].reshape(1, HG_WIDTH), lb_all[l, 1].reshape(1, HG_WIDTH)
        oc_f, t_f = _hg_scan(pa_c, pf_c, lb_f, rev=False)
        ol_f, _ = _hg_scan(pa_l, pf_l, lb_f, rev=False, init=t_f)
        yb_c, t_b = _hg_scan(pa_c, pf_c, lb_b, rev=True, readout=(oc_f, hg_nw))
        yb_l, _ = _hg_scan(pa_l, pf_l, lb_b, rev=True, init=t_b, readout=(ol_f, hg_nw))

        k_all = jnp.concatenate([pqk_c[..., DA_WIDTH:], pqk_l[..., DA_WIDTH:]], axis=1)
        v_all = jnp.concatenate([pa_c[..., PA_DV:PA_DV + DA_WIDTH], pa_l[..., PA_DV:PA_DV + DA_WIDTH]], axis=1)
        yd_l = _diff_attn(pqk_l, [(k_all, 0, v_all, 0)], da_lambda[l], da_norm_w[l], lam_init, 512, ATTN_TK)

        xl = _merge(xl, mod_l[2], hl, ya_l, yb_l, yd_l, wg_b, bg, wu_b, wo_b)
        h2l, aff_l = _modulate(xl, norm2_w[l], mod_l[3], mod_l[4], w_router=moe_router[l])
        token_sets = [(xl, mod_l[5], h2l, aff_l, final_norm_w if l == depth - 1 else None)]
        if need_ctx:
            yd_c = _diff_attn(pqk_c, [(pqk_c, DA_WIDTH, pa_c, PA_DV)], da_lambda[l], da_norm_w[l], lam_init,
                              n_ctx, ATTN_TK)
            xc = _merge(xc, mod_c[2], hc, ya_c, yb_c, yd_c, wg_b, bg, wu_b, wo_b)
            h2c, aff_c = _modulate(xc, norm2_w[l], mod_c[3], mod_c[4], w_router=moe_router[l])
            token_sets.append((xc, mod_c[5], h2c, aff_c, None))
        outs = _ec_moe(token_sets, moe_w1, moe_w3, moe_w2, l)
        xl = outs[0]
        if need_ctx:
            xc = outs[1]
    return xl
```

```python
import functools
import math

import jax
import jax.numpy as jnp
from jax import lax
from jax.experimental import pallas as pl
from jax.experimental.pallas import tpu as pltpu

F32 = jnp.float32
BF16 = jnp.bfloat16
I32 = jnp.int32
HIGHEST = lax.Precision.HIGHEST
LOG2E = 1.4426950408889634

D = 2048
EPS = 1e-6
GRID_W = 64
SSD_D_INNER = 1024
SSD_HEADS = 16
SSD_GROUPS = 2
SSD_STATE = 128
SSD_XBC = 1536
SSD_CONV_W = 5
HG_WIDTH = 1024
HG_HEADS = 8
HG_HEAD_DIM = 128
DA_HEADS = 8
DA_HEAD_DIM = 64
DA_WIDTH = 1024
ROPE_BASE = 10000.0
ROPE_PAIRS = 16
N_BRANCH = 3
N_EXPERTS = 16
EXPERT_FF = 2048
EC_CAPACITY = 2

OFF_SSD_Z = 0
OFF_SSD_XBC = 1024
OFF_SSD_DT = 2560
OFF_HG_Q = 2592
OFF_HG_F = OFF_HG_Q + 1024
OFF_HG_I = OFF_HG_F + 2048
OFF_HG_G = OFF_HG_I + 1024
OFF_DA_Q = OFF_HG_G + 1024
OFF_DA_K = OFF_DA_Q + 1024
OFF_DA_V = OFF_DA_K + 1024
N_IN = OFF_DA_V + 1024

LANES = 128
PA_Z, PA_HQ, PA_HI, PA_HG, PA_DV, PA_XBC = 0, 1024, 2048, 3072, 4096, 5120
PA_N = 6656
PF_F, PF_DT = 0, 2048
PF_N = 2176
PQK_N = 2048

SCAN_Q = 256
ATTN_TK = 1408
HALO = 16
TOK_BLK = 128
COMBINE_TILE = 256
GATHER_COLS = 512
EXPERT_GROUP = 16
GATHER_WIN_SMALL = 48
SLOT_ALIGN = 16
VMEM_LIMIT = 56 * 1024 * 1024


def _cparams(sem, vmem=VMEM_LIMIT):
    return pltpu.CompilerParams(dimension_semantics=sem, vmem_limit_bytes=vmem)


def _silu(x):
    return x * jax.nn.sigmoid(x)


def _adaln_kernel(cb_ref, w_ref, b_ref, o_ref, *, n_rows, tn):
    nct = tn // LANES

    def body(k, accs):
        r0 = pl.multiple_of(k * 8, 8)
        ws = [w_ref[pl.ds(r0, 8), c * LANES:(c + 1) * LANES] for c in range(nct)]
        out = []
        for r in range(n_rows):
            s = _silu(cb_ref[r, pl.ds(r0, 8), :])
            out.append(tuple(accs[r][c] + s * ws[c] for c in range(nct)))
        return tuple(out)

    zero = jnp.zeros((8, LANES), F32)
    accs = lax.fori_loop(0, D // 8, body, tuple(tuple(zero for _ in range(nct)) for _ in range(n_rows)),
                         unroll=8)
    rows = []
    for r in range(n_rows):
        rows.append(jnp.concatenate([jnp.sum(accs[r][c], axis=0, keepdims=True) for c in range(nct)], axis=1))
    rows.append(jnp.zeros((8 - n_rows, tn), F32))
    o_ref[...] = jnp.concatenate(rows, axis=0) + b_ref[...]


def _adaln(cvec, ada_w, ada_b):
    n_rows = cvec.shape[0]
    depth, _, n6 = ada_w.shape
    tn = 512
    cb = jnp.broadcast_to(cvec[:, :, None], (n_rows, D, LANES))
    return pl.pallas_call(
        functools.partial(_adaln_kernel, n_rows=n_rows, tn=tn),
        grid=(depth, n6 // tn),
        in_specs=[pl.BlockSpec((n_rows, D, LANES), lambda l, j: (0, 0, 0)),
                  pl.BlockSpec((None, D, tn), lambda l, j: (l, 0, j)),
                  pl.BlockSpec((None, 1, tn), lambda l, j: (l, 0, j))],
        out_specs=pl.BlockSpec((None, 8, tn), lambda l, j: (l, 0, j)),
        out_shape=jax.ShapeDtypeStruct((depth, 8, n6), F32),
        compiler_params=_cparams(("parallel", "parallel")),
        name="adaln",
    )(cb, ada_w, ada_b.reshape(depth, 1, n6))


def _modulate_kernel(x_ref, nw_ref, sh_ref, sc_ref, *rest, with_router):
    x = x_ref[...]
    ms = jnp.mean(x * x, axis=-1, keepdims=True)
    h = (x * lax.rsqrt(ms + EPS) * nw_ref[...]) * (1.0 + sc_ref[...]) + sh_ref[...]
    if with_router:
        wr_ref, h_ref, aff_ref = rest
        logits = jnp.dot(h, wr_ref[...], preferred_element_type=F32, precision=HIGHEST)
        m = jnp.max(logits, axis=-1, keepdims=True)
        e = jnp.exp(logits - m)
        aff_ref[...] = e / jnp.sum(e, axis=-1, keepdims=True)
    else:
        (h_ref,) = rest
    h_ref[...] = h.astype(BF16)


def _modulate(x, nw, shift, scale, w_router=None):
    bsz, n, _ = x.shape
    tm = min(512, n)
    with_router = w_router is not None
    in_specs = [pl.BlockSpec((None, tm, D), lambda b, i: (b, i, 0)),
                pl.BlockSpec((1, D), lambda b, i: (0, 0)),
                pl.BlockSpec((None, 1, D), lambda b, i: (b, 0, 0)),
                pl.BlockSpec((None, 1, D), lambda b, i: (b, 0, 0))]
    args = [x, nw.reshape(1, D), shift, scale]
    out_specs = [pl.BlockSpec((None, tm, D), lambda b, i: (b, i, 0))]
    out_shape = [jax.ShapeDtypeStruct((bsz, n, D), BF16)]
    if with_router:
        in_specs.append(pl.BlockSpec((D, N_EXPERTS), lambda b, i: (0, 0)))
        args.append(w_router)
        out_specs.append(pl.BlockSpec((None, tm, N_EXPERTS), lambda b, i: (b, i, 0)))
        out_shape.append(jax.ShapeDtypeStruct((bsz, n, N_EXPERTS), F32))
    res = pl.pallas_call(
        functools.partial(_modulate_kernel, with_router=with_router),
        grid=(bsz, n // tm),
        in_specs=in_specs, out_specs=out_specs, out_shape=out_shape,
        compiler_params=_cparams(("parallel", "parallel")),
        name="modulate_router" if with_router else "modulate",
    )(*args)
    return res if with_router else res[0]


def _mm_kernel(a_ref, w_ref, *rest, rope, tn):
    acc = jnp.dot(a_ref[...], w_ref[...], preferred_element_type=F32)
    if rope:
        cos_ref, sin_ref, o_ref = rest
        cos = cos_ref[...]
        sin = sin_ref[...]
        lane = lax.broadcasted_iota(I32, cos.shape, 1)
        first_half = (lane % DA_HEAD_DIM) < (DA_HEAD_DIM // 2)
        for c in range(tn // LANES):
            xs = acc[:, c * LANES:(c + 1) * LANES]
            partner = jnp.where(first_half,
                                pltpu.roll(xs, LANES - DA_HEAD_DIM // 2, 1),
                                pltpu.roll(xs, DA_HEAD_DIM // 2, 1))
            o_ref[:, c * LANES:(c + 1) * LANES] = (xs * cos + partner * sin).astype(o_ref.dtype)
    else:
        (o_ref,) = rest
        o_ref[...] = acc.astype(o_ref.dtype)


def _matmul(a, w, out_dtype, tm, tn, rope_tables=None):
    m, k = a.shape
    n = w.shape[1]
    rope = rope_tables is not None
    in_specs = [pl.BlockSpec((tm, k), lambda i, j: (i, 0)),
                pl.BlockSpec((k, tn), lambda i, j: (0, j))]
    args = [a, w]
    if rope:
        cos, sin = rope_tables
        nt = cos.shape[0] // tm
        in_specs += [pl.BlockSpec((tm, LANES), lambda i, j: (i % nt, 0)),
                     pl.BlockSpec((tm, LANES), lambda i, j: (i % nt, 0))]
        args += [cos, sin]
    return pl.pallas_call(
        functools.partial(_mm_kernel, rope=rope, tn=tn),
        grid=(m // tm, n // tn),
        in_specs=in_specs,
        out_specs=pl.BlockSpec((tm, tn), lambda i, j: (i, j)),
        out_shape=jax.ShapeDtypeStruct((m, n), out_dtype),
        compiler_params=_cparams(("parallel", "parallel")),
        name="proj_rope" if rope else "proj",
    )(*args)


def _conv_kernel(prev_ref, cur_ref, next_ref, w_ref, b_ref, o_ref, *, tc):
    i = pl.program_id(1)
    last = pl.num_programs(1) - 1
    pad = (SSD_CONV_W - 1) // 2
    prev = jnp.where(i > 0, prev_ref[...].astype(F32), 0.0)
    nxt = jnp.where(i < last, next_ref[...].astype(F32), 0.0)
    ext = jnp.concatenate([prev, cur_ref[...].astype(F32), nxt], axis=0)
    w = w_ref[...]
    y = b_ref[...] + w[0:1] * ext[HALO - pad:HALO - pad + tc]
    for k in range(1, SSD_CONV_W):
        y = y + w[k:k + 1] * ext[HALO - pad + k:HALO - pad + k + tc]
    o_ref[...] = _silu(y).astype(o_ref.dtype)


def _ssd_conv(pa, conv_w, conv_b):
    bsz, n, _ = pa.shape
    tc = min(512, n)
    tw = 512
    c0 = PA_XBC // tw
    nbh = n // HALO
    return pl.pallas_call(
        functools.partial(_conv_kernel, tc=tc),
        grid=(bsz, n // tc, SSD_XBC // tw),
        in_specs=[pl.BlockSpec((None, HALO, tw), lambda b, i, j: (b, jnp.maximum(i * (tc // HALO) - 1, 0), c0 + j)),
                  pl.BlockSpec((None, tc, tw), lambda b, i, j: (b, i, c0 + j)),
                  pl.BlockSpec((None, HALO, tw),
                               lambda b, i, j: (b, jnp.minimum((i + 1) * (tc // HALO), nbh - 1), c0 + j)),
                  pl.BlockSpec((SSD_CONV_W, tw), lambda b, i, j: (0, j)),
                  pl.BlockSpec((1, tw), lambda b, i, j: (0, j))],
        out_specs=pl.BlockSpec((None, tc, tw), lambda b, i, j: (b, i, j)),
        out_shape=jax.ShapeDtypeStruct((bsz, n, SSD_XBC), BF16),
        compiler_params=_cparams(("parallel", "parallel", "parallel")),
        name="ssd_conv",
    )(pa, pa, pa, conv_w, conv_b.reshape(1, SSD_XBC))


def _tri(q, rev):
    r = lax.broadcasted_iota(I32, (q, q), 0)
    c = lax.broadcasted_iota(I32, (q, q), 1)
    return (c >= r) if rev else (r >= c)


def _split3(x):
    hi = x.astype(BF16)
    r = x - hi.astype(F32)
    mid = r.astype(BF16)
    lo = (r - mid.astype(F32)).astype(BF16)
    return hi, mid, lo


def _select_rows_dot(sel01, x):
    hi, mid, lo = _split3(x)
    return (jnp.dot(sel01, hi, preferred_element_type=F32) + jnp.dot(sel01, mid, preferred_element_type=F32)
            + jnp.dot(sel01, lo, preferred_element_type=F32))


def _select_cols_dot(x, sel01):
    hi, mid, lo = _split3(x)
    return (jnp.dot(hi, sel01, preferred_element_type=F32) + jnp.dot(mid, sel01, preferred_element_type=F32)
            + jnp.dot(lo, sel01, preferred_element_type=F32))


def _ssd_scan_kernel(*refs, rev, has_init, readout):
    it = iter(refs)
    xact_ref, dt_ref, alog_ref, bias_ref = next(it), next(it), next(it), next(it)
    init_ref = next(it) if has_init else None
    if readout:
        z_ref, yprev_ref, d_ref, nw_ref = next(it), next(it), next(it), next(it)
    y_ref, st_ref = next(it), next(it)
    ybuf_ref = next(it) if readout else None

    q = SCAN_Q
    hd = SSD_D_INNER // SSD_HEADS
    hg = SSD_HEADS // SSD_GROUPS
    gw = hg * hd
    d_off = SSD_HEADS if rev else 0

    @pl.when(pl.program_id(1) == 0)
    def _():
        st_ref[...] = init_ref[...] if has_init else jnp.zeros_like(st_ref)

    tri = _tri(q, rev)
    dtv = jax.nn.softplus(dt_ref[...] + bias_ref[...])
    da = dtv * (-LOG2E * jnp.exp(alog_ref[...]))
    cs = _select_rows_dot(jnp.where(tri, 1.0, 0.0).astype(BF16), da)
    cs_t = cs.T
    erow = lax.broadcasted_iota(I32, (LANES, SSD_D_INNER), 0)
    ecol = lax.broadcasted_iota(I32, (LANES, SSD_D_INNER), 1)
    expand = jnp.where(erow == d_off + ecol // hd, 1.0, 0.0).astype(BF16)
    csx = _select_cols_dot(cs, expand)
    dtx = _select_cols_dot(dtv, expand)
    x = xact_ref[:, :SSD_D_INNER].astype(F32)
    xdt = x * dtx
    xdt_b = xdt.astype(BF16)
    cs_end = csx[0:1] if rev else csx[q - 1:q]
    ecs = jnp.exp2(csx)
    xw = (xdt * jnp.exp2(cs_end - csx)).astype(BF16)
    ecs_end = jnp.exp2(cs_end)
    lane = lax.broadcasted_iota(I32, (q, LANES), 1)
    out_ref = ybuf_ref if readout else y_ref

    for g in range(SSD_GROUPS):
        bg = xact_ref[:, SSD_D_INNER + g * SSD_STATE:SSD_D_INNER + (g + 1) * SSD_STATE]
        cg = xact_ref[:, SSD_D_INNER + (SSD_GROUPS + g) * SSD_STATE:SSD_D_INNER + (SSD_GROUPS + g + 1) * SSD_STATE]
        cb = lax.dot_general(cg, bg, (((1,), (1,)), ((), ())), preferred_element_type=F32)
        st_g = st_ref[:, g * gw:(g + 1) * gw]
        y_inter = jnp.dot(cg, st_g.astype(BF16), preferred_element_type=F32) * ecs[:, g * gw:(g + 1) * gw]
        for jj in range(hg // 2):
            col0 = g * gw + jj * LANES
            xp = xdt_b[:, col0:col0 + LANES]
            acc = y_inter[:, jj * LANES:(jj + 1) * LANES]
            for s in range(2):
                j = d_off + g * hg + 2 * jj + s
                e = cs[:, j:j + 1] - cs_t[j:j + 1, :]
                seg = jnp.exp2(jnp.where(tri, e, -jnp.inf))
                m = (cb * seg).astype(BF16)
                xm = jnp.where((lane >= hd) if s else (lane < hd), xp, jnp.zeros_like(xp))
                acc = acc + jnp.dot(m, xm, preferred_element_type=F32)
            out_ref[:, col0:col0 + LANES] = acc
        upd = lax.dot_general(bg, xw[:, g * gw:(g + 1) * gw], (((0,), (0,)), ((), ())),
                              preferred_element_type=F32)
        st_ref[:, g * gw:(g + 1) * gw] = st_g * ecs_end[:, g * gw:(g + 1) * gw] + upd

    if readout:
        z = z_ref[...].astype(F32)
        yy = (yprev_ref[...] + ybuf_ref[...] + d_ref[...] * x) * _silu(z)
        for g in range(SSD_GROUPS):
            seg = yy[:, g * gw:(g + 1) * gw]
            ms = jnp.mean(seg * seg, axis=-1, keepdims=True)
            y_ref[:, g * gw:(g + 1) * gw] = (seg * lax.rsqrt(ms + EPS) * nw_ref[:, g * gw:(g + 1) * gw]).astype(y_ref.dtype)


def _ssd_scan(xact, pf, alog128, bias128, *, rev, init=None, readout=None):
    bsz, n, _ = xact.shape
    nc = n // SCAN_Q
    cidx = (lambda c: nc - 1 - c) if rev else (lambda c: c)
    in_specs = [pl.BlockSpec((None, SCAN_Q, SSD_XBC), lambda b, c: (b, cidx(c), 0)),
                pl.BlockSpec((None, SCAN_Q, LANES), lambda b, c: (b, cidx(c), PF_DT // LANES)),
                pl.BlockSpec((1, LANES), lambda b, c: (0, 0)),
                pl.BlockSpec((1, LANES), lambda b, c: (0, 0))]
    args = [xact, pf, alog128, bias128]
    if init is not None:
        in_specs.append(pl.BlockSpec((None, SSD_STATE, SSD_D_INNER), lambda b, c: (b, 0, 0)))
        args.append(init)
    scratch = []
    if readout is not None:
        pa, yprev, d_full, norm_w = readout
        in_specs += [pl.BlockSpec((None, SCAN_Q, SSD_D_INNER), lambda b, c: (b, cidx(c), PA_Z // SSD_D_INNER)),
                     pl.BlockSpec((None, SCAN_Q, SSD_D_INNER), lambda b, c: (b, cidx(c), 0)),
                     pl.BlockSpec((1, SSD_D_INNER), lambda b, c: (0, 0)),
                     pl.BlockSpec((1, SSD_D_INNER), lambda b, c: (0, 0))]
        args += [pa, yprev, d_full, norm_w]
        scratch = [pltpu.VMEM((SCAN_Q, SSD_D_INNER), F32)]
    y_dtype = BF16 if readout is not None else F32
    y, st = pl.pallas_call(
        functools.partial(_ssd_scan_kernel, rev=rev, has_init=init is not None, readout=readout is not None),
        grid=(bsz, nc),
        in_specs=in_specs,
        out_specs=[pl.BlockSpec((None, SCAN_Q, SSD_D_INNER), lambda b, c: (b, cidx(c), 0)),
                   pl.BlockSpec((None, SSD_STATE, SSD_D_INNER), lambda b, c: (b, 0, 0))],
        out_shape=[jax.ShapeDtypeStruct((bsz, n, SSD_D_INNER), y_dtype),
                   jax.ShapeDtypeStruct((bsz, SSD_STATE, SSD_D_INNER), F32)],
        scratch_shapes=scratch,
        compiler_params=_cparams(("parallel", "arbitrary")),
        name="ssd_scan_bwd" if rev else "ssd_scan_fwd",
    )(*args)
    return y, st


def _ref_rows(bb, s, rev):
    q, w = bb.shape
    rl = s if rev else s - 1
    if 2 * s >= 8:
        b3 = bb.reshape(q // (2 * s), 2 * s, w)
        return jnp.broadcast_to(b3[:, rl:rl + 1, :], b3.shape).reshape(q, w)
    off = lax.broadcasted_iota(I32, (q, 1), 0) % (2 * s)
    out = bb
    for ov in range(2 * s):
        if ov != rl:
            out = jnp.where(off == ov, pltpu.roll(bb, (ov - rl) % q, 0), out)
    return out


def _hg_scan_kernel(*refs, rev, has_init, readout):
    it = iter(refs)
    q_ref, f_ref, v_ref, lb_ref = next(it), next(it), next(it), next(it)
    init_ref = next(it) if has_init else None
    if readout:
        g_ref, oprev_ref, nw_ref = next(it), next(it), next(it)
    o_ref, st_ref, att_ref = next(it), next(it), next(it)
    obuf_ref = next(it) if readout else None

    q = SCAN_Q
    hd = HG_HEAD_DIM

    @pl.when(pl.program_id(1) == 0)
    def _():
        st_ref[...] = init_ref[...] if has_init else jnp.zeros_like(st_ref)

    lb = lb_ref[...]
    qq = _silu(q_ref[...].astype(F32))
    f = lb + (1.0 - lb) * jax.nn.sigmoid(f_ref[...])
    kk = 1.0 - f
    lf = jnp.log(f) * LOG2E
    tri = _tri(q, rev)
    bb = jnp.dot(jnp.where(tri, 1.0, 0.0), lf, preferred_element_type=F32, precision=HIGHEST)

    half = q // 2
    row = lax.broadcasted_iota(I32, (q, 1), 0)
    ri = lax.broadcasted_iota(I32, (half, half), 0)
    ci = lax.broadcasted_iota(I32, (half, half), 1)
    xr = ri ^ ci
    causal = (ci > ri) if rev else (ri > ci)
    nt = (((1,), (1,)), ((), ()))
    s = half
    while s >= 1:
        upper = (row & s) != 0
        q_side = jnp.logical_not(upper) if rev else upper
        ex = jnp.exp2(-jnp.abs(bb - _ref_rows(bb, s, rev)))
        u = (jnp.where(q_side, qq, kk) * ex).astype(BF16)
        if s < half:
            level = jnp.logical_and((xr >> int(math.log2(s))) == 1, causal)
        for h in range(HG_HEADS):
            u_lo, u_hi = u[:half, h * hd:(h + 1) * hd], u[half:, h * hd:(h + 1) * hd]
            if s == half:
                qs_, ks_ = (u_lo, u_hi) if rev else (u_hi, u_lo)
                att_ref[h, 2] = lax.dot_general(qs_, ks_, nt, preferred_element_type=F32)
            else:
                for blk, ub in enumerate((u_lo, u_hi)):
                    p = lax.dot_general(ub, ub, nt, preferred_element_type=F32)
                    old = att_ref[h, blk] if 2 * s < half else 0.0
                    att_ref[h, blk] = jnp.where(level, p, old)
        s //= 2

    b_end = bb[0:1] if rev else bb[q - 1:q]
    qe = (qq * jnp.exp2(bb)).astype(BF16)
    kh = (kk * jnp.exp2(b_end - bb)).astype(BF16)
    qk = (qq * kk).astype(BF16)
    e_end = jnp.exp2(b_end)
    ones = jnp.ones((hd, hd), BF16)
    out_ref = obuf_ref if readout else o_ref
    for h in range(HG_HEADS):
        sl = slice(h * hd, (h + 1) * hd)
        vh = v_ref[:, sl]
        st_h = st_ref[sl, :]
        v_lo, v_hi = vh[:half], vh[half:]
        d0, d1, off = (att_ref[h, k].astype(BF16) for k in range(3))
        o_lo = jnp.dot(d0, v_lo, preferred_element_type=F32)
        o_hi = jnp.dot(d1, v_hi, preferred_element_type=F32)
        if rev:
            o_lo = o_lo + jnp.dot(off, v_hi, preferred_element_type=F32)
        else:
            o_hi = o_hi + jnp.dot(off, v_lo, preferred_element_type=F32)
        o = jnp.concatenate([o_lo, o_hi], axis=0)
        o = o + jnp.dot(qk[:, sl], ones, preferred_element_type=F32) * vh.astype(F32)
        o = o + lax.dot_general(qe[:, sl], st_h.astype(BF16), (((1,), (1,)), ((), ())), preferred_element_type=F32)
        out_ref[:, sl] = o
        upd = lax.dot_general(vh, kh[:, sl], (((0,), (0,)), ((), ())), preferred_element_type=F32)
        st_ref[sl, :] = st_h * e_end[:, sl] + upd

    if readout:
        for h in range(HG_HEADS):
            sl = slice(h * hd, (h + 1) * hd)
            o = oprev_ref[:, sl] + obuf_ref[:, sl]
            ms = jnp.mean(o * o, axis=-1, keepdims=True)
            gate = _silu(g_ref[:, sl].astype(F32))
            o_ref[:, sl] = (o * lax.rsqrt(ms + EPS) * nw_ref[...] * gate).astype(o_ref.dtype)


def _hg_scan(pa, pf, lb_row, *, rev, init=None, readout=None):
    bsz, n, _ = pa.shape
    nc = n // SCAN_Q
    cidx = (lambda c: nc - 1 - c) if rev else (lambda c: c)
    fcol = (PF_F // HG_WIDTH) + (1 if rev else 0)
    in_specs = [pl.BlockSpec((None, SCAN_Q, HG_WIDTH), lambda b, c: (b, cidx(c), PA_HQ // HG_WIDTH)),
                pl.BlockSpec((None, SCAN_Q, HG_WIDTH), lambda b, c: (b, cidx(c), fcol)),
                pl.BlockSpec((None, SCAN_Q, HG_WIDTH), lambda b, c: (b, cidx(c), PA_HI // HG_WIDTH)),
                pl.BlockSpec((1, HG_WIDTH), lambda b, c: (0, 0))]
    args = [pa, pf, pa, lb_row]
    if init is not None:
        in_specs.append(pl.BlockSpec((None, HG_WIDTH, HG_HEAD_DIM), lambda b, c: (b, 0, 0)))
        args.append(init)
    scratch = [pltpu.VMEM((HG_HEADS, 3, SCAN_Q // 2, SCAN_Q // 2), F32)]
    if readout is not None:
        oprev, nw128 = readout
        in_specs += [pl.BlockSpec((None, SCAN_Q, HG_WIDTH), lambda b, c: (b, cidx(c), PA_HG // HG_WIDTH)),
                     pl.BlockSpec((None, SCAN_Q, HG_WIDTH), lambda b, c: (b, cidx(c), 0)),
                     pl.BlockSpec((1, HG_HEAD_DIM), lambda b, c: (0, 0))]
        args += [pa, oprev, nw128]
        scratch.append(pltpu.VMEM((SCAN_Q, HG_WIDTH), F32))
    o_dtype = BF16 if readout is not None else F32
    o, st = pl.pallas_call(
        functools.partial(_hg_scan_kernel, rev=rev, has_init=init is not None, readout=readout is not None),
        grid=(bsz, nc),
        in_specs=in_specs,
        out_specs=[pl.BlockSpec((None, SCAN_Q, HG_WIDTH), lambda b, c: (b, cidx(c), 0)),
                   pl.BlockSpec((None, HG_WIDTH, HG_HEAD_DIM), lambda b, c: (b, 0, 0))],
        out_shape=[jax.ShapeDtypeStruct((bsz, n, HG_WIDTH), o_dtype),
                   jax.ShapeDtypeStruct((bsz, HG_WIDTH, HG_HEAD_DIM), F32)],
        scratch_shapes=scratch,
        compiler_params=_cparams(("parallel", "arbitrary")),
        name="hg_scan_bwd" if rev else "hg_scan_fwd",
    )(*args)
    return o, st


def _attn_kernel(*refs, seg_lens, tk, lam_init):
    lam_ref, nw_ref, q_ref = refs[0], refs[1], refs[2]
    nseg = len(seg_lens)
    kv_refs = refs[3:3 + 2 * nseg]
    o_ref, qs_ref, sa_ref, sb_ref, m_ref, mn_ref, acc_ref = refs[3 + 2 * nseg:10 + 2 * nseg]
    vaug_refs = refs[10 + 2 * nseg:]
    tq = q_ref.shape[0]

    @pl.when(pl.program_id(2) == 0)
    def _():
        for si in range(nseg):
            v_ref, vaug_ref = kv_refs[2 * si + 1], vaug_refs[si]
            vaug_ref[:, :2 * DA_HEAD_DIM] = v_ref[...]
            vaug_ref[:, 2 * DA_HEAD_DIM:] = jnp.ones(v_ref.shape, BF16)

    qv = q_ref[...] * jnp.asarray(DA_HEAD_DIM ** -0.5, BF16)
    lane = lax.broadcasted_iota(I32, qv.shape, 1)
    zero = jnp.zeros_like(qv)
    qs_ref[:tq] = jnp.where(lane < DA_HEAD_DIM, qv, zero)
    qs_ref[tq:] = jnp.where(lane >= DA_HEAD_DIM, qv, zero)
    m_ref[...] = jnp.full_like(m_ref, -jnp.inf)
    acc_ref[...] = jnp.zeros_like(acc_ref)

    def scores(kc):
        return lax.dot_general(qs_ref[...], kc, (((1,), (1,)), ((), ())), preferred_element_type=F32)

    def lane_tiles(a):
        return [a[:, c * LANES:(c + 1) * LANES] for c in range(a.shape[1] // LANES)]

    def row_max(s, m_prev):
        tiles = lane_tiles(s)
        m = tiles[0]
        for tile in tiles[1:]:
            m = jnp.maximum(m, tile)
        return jnp.maximum(m_prev, jnp.broadcast_to(jnp.max(m, axis=-1, keepdims=True), m_prev.shape))

    def online_step(s, vc, m_prev, m_cur):
        alpha = jnp.exp(m_prev - m_cur)
        ps = [jnp.exp(tile - m_cur) for tile in lane_tiles(s)]
        pv = jnp.dot(jnp.concatenate(ps, axis=1).astype(BF16), vc, preferred_element_type=F32)
        acc_ref[...] = jnp.concatenate([alpha, alpha], axis=1) * acc_ref[...] + pv

    for si, n in enumerate(seg_lens):
        k_ref, v_ref = kv_refs[2 * si], vaug_refs[si]
        t = min(tk, n)
        nc = n // t
        if nc == 1:
            s = scores(k_ref[...])
            m_prev = m_ref[...]
            m_cur = row_max(s, m_prev)
            online_step(s, v_ref[...], m_prev, m_cur)
            m_ref[...] = m_cur
        else:
            s0 = scores(k_ref[pl.ds(0, t), :])
            sa_ref[...] = s0
            mn_ref[...] = row_max(s0, m_ref[...])

            def pipe_step(c, cur_ref, nxt_ref, k_ref=k_ref, v_ref=v_ref, t=t):
                r0 = pl.multiple_of(c * t, t)
                r1 = pl.multiple_of(r0 + t, t)
                m_prev, m_cur = m_ref[...], mn_ref[...]
                s_next = scores(k_ref[pl.ds(r1, t), :])
                nxt_ref[...] = s_next
                mn_ref[...] = row_max(s_next, m_cur)
                m_ref[...] = m_cur
                online_step(cur_ref[...], v_ref[pl.ds(r0, t), :], m_prev, m_cur)

            def body(i, carry):
                pipe_step(2 * i, sa_ref, sb_ref)
                pipe_step(2 * i + 1, sb_ref, sa_ref)
                return carry

            lax.fori_loop(0, (nc - 1) // 2, body, 0)
            last_ref = sa_ref
            if nc % 2 == 0:
                pipe_step(nc - 2, sa_ref, sb_ref)
                last_ref = sb_ref
            m_prev, m_cur = m_ref[...], mn_ref[...]
            online_step(last_ref[...], v_ref[pl.ds((nc - 1) * t, t), :], m_prev, m_cur)
            m_ref[...] = m_cur
    hw = 2 * DA_HEAD_DIM
    acc = acc_ref[:, :hw]
    l = acc_ref[:, hw:]
    lam = lam_ref[...]
    lmbda = (jnp.exp(jnp.sum(lam[0:1] * lam[1:2], axis=-1, keepdims=True))
             - jnp.exp(jnp.sum(lam[2:3] * lam[3:4], axis=-1, keepdims=True)) + lam_init)
    o = acc[:tq] / l[:tq] - lmbda * (acc[tq:] / l[tq:])
    ms = jnp.mean(o * o, axis=-1, keepdims=True)
    o_ref[...] = (o * lax.rsqrt(ms + EPS) * nw_ref[...] * (1.0 - lam_init)).astype(o_ref.dtype)


def _diff_attn(pqk_q, kv_segs, lam, nw, lam_init, tq, tk):
    bsz, nq, _ = pqk_q.shape
    hw = 2 * DA_HEAD_DIM
    in_specs = [pl.BlockSpec((4, DA_HEAD_DIM), lambda b, h, i: (0, 0)),
                pl.BlockSpec((1, hw), lambda b, h, i: (0, 0)),
                pl.BlockSpec((None, tq, hw), lambda b, h, i: (b, i, h))]
    args = [lam, nw.reshape(1, hw), pqk_q]
    seg_lens = []
    for k_arr, k_col, v_arr, v_col in kv_segs:
        nk = k_arr.shape[1]
        assert nk % min(tk, nk) == 0
        seg_lens.append(nk)
        in_specs += [pl.BlockSpec((None, nk, hw), functools.partial(lambda b, h, i, c: (b, 0, c + h), c=k_col // hw)),
                     pl.BlockSpec((None, nk, hw), functools.partial(lambda b, h, i, c: (b, 0, c + h), c=v_col // hw))]
        args += [k_arr, v_arr]
    return pl.pallas_call(
        functools.partial(_attn_kernel, seg_lens=tuple(seg_lens), tk=tk, lam_init=lam_init),
        grid=(bsz, DA_HEADS, nq // tq),
        in_specs=in_specs,
        out_specs=pl.BlockSpec((None, tq, hw), lambda b, h, i: (b, i, h)),
        out_shape=jax.ShapeDtypeStruct((bsz, nq, DA_WIDTH), BF16),
        scratch_shapes=[pltpu.VMEM((2 * tq, hw), BF16),
                        pltpu.VMEM((2 * tq, tk), F32),
                        pltpu.VMEM((2 * tq, tk), F32),
                        pltpu.VMEM((2 * tq, LANES), F32),
                        pltpu.VMEM((2 * tq, LANES), F32),
                        pltpu.VMEM((2 * tq, 2 * hw), F32)] + [pltpu.VMEM((nk, 2 * hw), BF16) for nk in seg_lens],
        compiler_params=_cparams(("parallel", "parallel", "arbitrary")),
        name="diff_attn",
    )(*args)


def _mix_kernel(h_ref, ya_ref, yb_ref, yd_ref, wg0, wg1, wg2, bg0, bg1, bg2, wu0, wu1, wu2, o_ref):
    h = h_ref[...]
    mix = None
    for y_ref, wg, bg, wu in ((ya_ref, wg0, bg0, wu0), (yb_ref, wg1, bg1, wu1), (yd_ref, wg2, bg2, wu2)):
        gate = jax.nn.sigmoid(jnp.dot(h, wg[...], preferred_element_type=F32) + bg[...])
        up = jnp.dot(y_ref[...], wu[...], preferred_element_type=F32)
        mix = gate * up if mix is None else mix + gate * up
    o_ref[...] = mix.astype(o_ref.dtype)


def _residual_out_kernel(x_ref, g1_ref, m_ref, wo_ref, o_ref):
    o_ref[...] = x_ref[...] + g1_ref[...] * jnp.dot(m_ref[...], wo_ref[...], preferred_element_type=F32)


def _merge(x, gate1, h, ya, yb, yd, w_gate, b_gate, w_up, w_out):
    bsz, n, _ = x.shape
    tm = min(1024, n)
    tn = 512
    nn = D // tn
    row = lambda b, i, j: (b, i, 0)
    in_specs = [pl.BlockSpec((None, tm, D), row),
                pl.BlockSpec((None, tm, 1024), row),
                pl.BlockSpec((None, tm, 1024), row),
                pl.BlockSpec((None, tm, 1024), row)]
    in_specs += [pl.BlockSpec((D, tn), functools.partial(lambda b, i, j, k: (0, k * nn + j), k=k)) for k in range(3)]
    in_specs += [pl.BlockSpec((1, tn), functools.partial(lambda b, i, j, k: (0, k * nn + j), k=k)) for k in range(3)]
    in_specs += [pl.BlockSpec((None, 1024, tn), functools.partial(lambda b, i, j, k: (k, 0, j), k=k)) for k in range(3)]
    mix = pl.pallas_call(
        _mix_kernel,
        grid=(bsz, n // tm, nn),
        in_specs=in_specs,
        out_specs=pl.BlockSpec((None, tm, tn), lambda b, i, j: (b, i, j)),
        out_shape=jax.ShapeDtypeStruct((bsz, n, D), BF16),
        compiler_params=_cparams(("parallel", "parallel", "parallel")),
        name="merge_mix",
    )(h, ya, yb, yd, w_gate, w_gate, w_gate, b_gate, b_gate, b_gate, w_up, w_up, w_up)
    return pl.pallas_call(
        _residual_out_kernel,
        grid=(bsz, n // tm, nn),
        in_specs=[pl.BlockSpec((None, tm, tn), lambda b, i, j: (b, i, j)),
                  pl.BlockSpec((None, 1, tn), lambda b, i, j: (b, 0, j)),
                  pl.BlockSpec((None, tm, D), row),
                  pl.BlockSpec((D, tn), lambda b, i, j: (0, j))],
        out_specs=pl.BlockSpec((None, tm, tn), lambda b, i, j: (b, i, j)),
        out_shape=jax.ShapeDtypeStruct((bsz, n, D), F32),
        compiler_params=_cparams(("parallel", "parallel", "parallel")),
        name="merge_out",
    )(x, gate1, mix, w_out)


def _route_kernel(aff_ref, posm_ref, pos_ref, *, cap, rpe):
    rows = N_EXPERTS * rpe
    bits = pltpu.bitcast(aff_ref[...], I32)
    ones = jnp.ones((LANES, LANES), BF16)
    r = lax.broadcasted_iota(I32, (rows, rows), 0)
    c = lax.broadcasted_iota(I32, (rows, rows), 1)
    same = (r // rpe) == (c // rpe)
    grp = same.astype(BF16)
    grp_before = jnp.logical_and(same, c < r).astype(BF16)
    lr = lax.broadcasted_iota(I32, (LANES, LANES), 0)
    lc = lax.broadcasted_iota(I32, (LANES, LANES), 1)
    before = (lr < lc).astype(BF16)

    def count(mask):
        per_row = jnp.dot(mask.astype(BF16), ones, preferred_element_type=F32)
        return jnp.dot(grp, per_row.astype(BF16), preferred_element_type=F32)

    def excl_prefix(mask):
        mb = mask.astype(BF16)
        within = jnp.dot(mb, before, preferred_element_type=F32)
        per_row = jnp.dot(mb, ones, preferred_element_type=F32)
        return within + jnp.dot(grp_before, per_row.astype(BF16), preferred_element_type=F32)

    def body(i, thr):
        cand = thr | (jnp.int32(1) << (30 - i))
        return jnp.where(count(bits >= cand) >= cap, cand, thr)

    thr = lax.fori_loop(0, 31, body, jnp.zeros((rows, LANES), I32))
    gt = bits > thr
    eq = bits == thr
    need = cap - count(gt)
    sel = jnp.logical_or(gt, jnp.logical_and(eq, excl_prefix(eq) < need))
    pos = excl_prefix(sel).astype(I32)
    pos_ref[...] = pos
    posm_ref[...] = jnp.where(sel, pos, -1)


def _route(aff, cap):
    bsz, n, _ = aff.shape
    rpe = n // LANES
    rows = N_EXPERTS * rpe
    aff_t = jnp.swapaxes(aff, 1, 2).reshape(bsz, rows, LANES)
    spec = pl.BlockSpec((None, rows, LANES), lambda b: (b, 0, 0))
    posm, pos = pl.pallas_call(
        functools.partial(_route_kernel, cap=cap, rpe=rpe),
        grid=(bsz,),
        in_specs=[spec], out_specs=[spec, spec],
        out_shape=[jax.ShapeDtypeStruct((bsz, rows, LANES), I32)] * 2,
        compiler_params=_cparams(("parallel",)),
        name="route",
    )(aff_t)
    return posm.reshape(bsz, N_EXPERTS, n), pos.reshape(bsz, N_EXPERTS, n)


def _gather_kernel(base_ref, base_small_ref, small_ref, h_ref, posm_ref, xg_ref, *, win, win_small, nblk):
    b = pl.program_id(0)
    xg_ref[...] = jnp.zeros_like(xg_ref)

    def scatter_block(j, bases_ref, w):
        t0 = pl.multiple_of(j * TOK_BLK, TOK_BLK)
        slot = lax.broadcasted_iota(I32, (w, TOK_BLK), 0)
        bases, onehots = [], []
        for e in range(N_EXPERTS):
            base = pl.multiple_of(bases_ref[(b * N_EXPERTS + e) * nblk + j], SLOT_ALIGN)
            rel = posm_ref[e:e + 1, pl.ds(t0, TOK_BLK)] - base
            bases.append(base)
            onehots.append(jnp.where(slot == rel, 1.0, 0.0).astype(BF16))
        rows = jnp.dot(jnp.concatenate(onehots, axis=0), h_ref[pl.ds(t0, TOK_BLK), :],
                       preferred_element_type=F32).astype(BF16)
        for e in range(N_EXPERTS):
            xg_ref[e, pl.ds(bases[e], w), :] += rows[e * w:(e + 1) * w]

    def body(j, carry):
        small = small_ref[b * nblk + j] != 0

        @pl.when(small)
        def _():
            scatter_block(j, base_small_ref, win_small)

        @pl.when(jnp.logical_not(small))
        def _():
            scatter_block(j, base_ref, win)

        return carry

    lax.fori_loop(0, nblk, body, 0)


def _expert_kernel(xg_ref, w1_ref, w3_ref, w2_ref, y_ref, acc_ref):
    f = pl.program_id(2)

    @pl.when(f == 0)
    def _():
        acc_ref[...] = jnp.zeros_like(acc_ref)

    xg = xg_ref[...]
    a = jnp.dot(xg, w1_ref[...].astype(BF16), preferred_element_type=F32)
    g = jnp.dot(xg, w3_ref[...].astype(BF16), preferred_element_type=F32)
    acc_ref[...] += jnp.dot((_silu(a) * g).astype(BF16), w2_ref[...].astype(BF16), preferred_element_type=F32)

    @pl.when(f == pl.num_programs(2) - 1)
    def _():
        y_ref[...] = acc_ref[...].astype(y_ref.dtype)


def _combine_kernel(base_ref, x_ref, g2_ref, aff_ref, posm_ref, *rest, win, win_tile, cap, nblk, nsub, final):
    y_refs = rest[:EXPERT_GROUP]
    if final:
        fw_ref, o_ref, acc_ref = rest[EXPERT_GROUP:]
    else:
        o_ref, acc_ref = rest[EXPERT_GROUP:]
    b, jt, eg = pl.program_id(0), pl.program_id(1), pl.program_id(2)

    @pl.when(eg == 0)
    def _():
        acc_ref[...] = jnp.zeros_like(acc_ref)

    lane = lax.broadcasted_iota(I32, (TOK_BLK, N_EXPERTS), 1)
    slot = lax.broadcasted_iota(I32, (TOK_BLK, win), 1)
    for sb in range(nsub):
        rows = slice(sb * TOK_BLK, (sb + 1) * TOK_BLK)
        total = None
        for k, y_ref in enumerate(y_refs):
            e = eg * EXPERT_GROUP + k
            first = (b * N_EXPERTS + e) * nblk + jt * nsub
            tile_base = jnp.minimum(base_ref[first], cap - win_tile)
            base = base_ref[first + sb]
            mine = lane == e
            rel = jnp.sum(jnp.where(mine, posm_ref[rows, :], 0), axis=-1, keepdims=True) - base
            val = jnp.sum(jnp.where(mine, aff_ref[rows, :], 0.0), axis=-1, keepdims=True)
            onehot = jnp.where(slot == rel, 1.0, 0.0).astype(BF16)
            off = pl.multiple_of(base - tile_base, SLOT_ALIGN)
            part = val * jnp.dot(onehot, y_ref[pl.ds(off, win), :], preferred_element_type=F32)
            total = part if total is None else total + part
        acc_ref[rows, :] += total

    @pl.when(eg == pl.num_programs(2) - 1)
    def _():
        out = x_ref[...] + g2_ref[...] * acc_ref[...]
        if final:
            ms = jnp.mean(out * out, axis=-1, keepdims=True)
            out = out * lax.rsqrt(ms + EPS) * fw_ref[...]
        o_ref[...] = out


def _ec_moe(token_sets, w1, w3, w2, layer):
    bsz = token_sets[0][0].shape[0]
    plans, xgs = [], []
    for x, gate2, h2, aff, final_w in token_sets:
        n = x.shape[1]
        cap = EC_CAPACITY * n // N_EXPERTS
        nblk = n // TOK_BLK
        win = min(TOK_BLK + SLOT_ALIGN, cap)
        posm, pos = _route(aff, cap)
        start = pos[:, :, ::TOK_BLK]
        aligned = (start // SLOT_ALIGN) * SLOT_ALIGN
        base = jnp.minimum(aligned, cap - win).astype(I32).reshape(-1)
        win_small = min(GATHER_WIN_SMALL, win)
        base_small = jnp.minimum(aligned, cap - win_small).astype(I32).reshape(-1)
        count = jnp.concatenate([start[:, :, 1:], jnp.full_like(start[:, :, :1], cap)], axis=2) - start
        small = jnp.all(count <= win_small - SLOT_ALIGN, axis=1).astype(I32).reshape(-1)
        dq = GATHER_COLS
        xgs.append(pl.pallas_call(
            functools.partial(_gather_kernel, win=win, win_small=win_small, nblk=nblk),
            grid_spec=pltpu.PrefetchScalarGridSpec(
                num_scalar_prefetch=3,
                grid=(bsz, D // dq),
                in_specs=[pl.BlockSpec((None, n, dq), lambda b, c, *_: (b, 0, c)),
                          pl.BlockSpec((None, N_EXPERTS, n), lambda b, c, *_: (b, 0, 0))],
                out_specs=pl.BlockSpec((None, N_EXPERTS, cap, dq), lambda b, c, *_: (b, 0, 0, c))),
            out_shape=jax.ShapeDtypeStruct((bsz, N_EXPERTS, cap, D), BF16),
            compiler_params=_cparams(("parallel", "parallel")),
            name="moe_gather",
        )(base, base_small, small, h2, posm))
        plans.append((n, cap, nblk, win, posm, base))

    xg = xgs[0] if len(xgs) == 1 else jnp.concatenate(xgs, axis=2)
    cap_all = xg.shape[2]
    tf = 256
    y = pl.pallas_call(
        _expert_kernel,
        grid=(N_EXPERTS, bsz, EXPERT_FF // tf),
        in_specs=[pl.BlockSpec((None, None, cap_all, D), lambda e, b, f: (b, e, 0, 0)),
                  pl.BlockSpec((None, None, D, tf), lambda e, b, f: (layer, e, 0, f)),
                  pl.BlockSpec((None, None, D, tf), lambda e, b, f: (layer, e, 0, f)),
                  pl.BlockSpec((None, None, tf, D), lambda e, b, f: (layer, e, f, 0))],
        out_specs=pl.BlockSpec((None, None, cap_all, D), lambda e, b, f: (b, e, 0, 0)),
        out_shape=jax.ShapeDtypeStruct((bsz, N_EXPERTS, cap_all, D), BF16),
        scratch_shapes=[pltpu.VMEM((cap_all, D), F32)],
        compiler_params=_cparams(("parallel", "parallel", "arbitrary")),
        name="moe_expert",
    )(xg, w1, w3, w2)
    y2d = y.reshape(bsz * N_EXPERTS * cap_all, D)

    outs = []
    row0 = 0
    for (x, gate2, h2, aff, final_w), (n, cap, nblk, win, posm, base) in zip(token_sets, plans):
        tile = min(COMBINE_TILE, n)
        nsub = tile // TOK_BLK
        win_tile = min(tile + SLOT_ALIGN, cap)
        final = final_w is not None

        def y_window(b, j, eg, base, k, cap=cap, nblk=nblk, nsub=nsub, win_tile=win_tile, row0=row0):
            be = b * N_EXPERTS + eg * EXPERT_GROUP + k
            start = jnp.minimum(base[be * nblk + j * nsub], cap - win_tile)
            return pl.multiple_of(be * cap_all + row0 + start, SLOT_ALIGN), 0

        in_specs = [pl.BlockSpec((None, tile, D), lambda b, j, e, base: (b, j, 0)),
                    pl.BlockSpec((None, 1, D), lambda b, j, e, base: (b, 0, 0)),
                    pl.BlockSpec((None, tile, N_EXPERTS), lambda b, j, e, base: (b, j, 0)),
                    pl.BlockSpec((None, tile, N_EXPERTS), lambda b, j, e, base: (b, j, 0))]
        in_specs += [pl.BlockSpec((pl.Element(win_tile), pl.Element(D)), functools.partial(y_window, k=k))
                     for k in range(EXPERT_GROUP)]
        args = [base, x, gate2, aff, jnp.swapaxes(posm, 1, 2)] + [y2d] * EXPERT_GROUP
        if final:
            in_specs.append(pl.BlockSpec((1, D), lambda b, j, e, base: (0, 0)))
            args.append(final_w.reshape(1, D))
        outs.append(pl.pallas_call(
            functools.partial(_combine_kernel, win=win, win_tile=win_tile, cap=cap, nblk=nblk, nsub=nsub, final=final),
            grid_spec=pltpu.PrefetchScalarGridSpec(
                num_scalar_prefetch=1,
                grid=(bsz, n // tile, N_EXPERTS // EXPERT_GROUP),
                in_specs=in_specs,
                out_specs=pl.BlockSpec((None, tile, D), lambda b, j, e, base: (b, j, 0)),
                scratch_shapes=[pltpu.VMEM((tile, D), F32)]),
            out_shape=jax.ShapeDtypeStruct((bsz, n, D), F32),
            compiler_params=_cparams(("parallel", "parallel", "arbitrary")),
            name="moe_combine",
        )(*args))
        row0 += cap
    return outs


def _rope_tables(n_tokens):
    rows = n_tokens // GRID_W
    row = jnp.repeat(jnp.arange(rows, dtype=F32), GRID_W)
    col = (jnp.arange(rows * GRID_W, dtype=I32) % GRID_W).astype(F32)
    inv = ROPE_BASE ** (-jnp.arange(ROPE_PAIRS, dtype=F32) / ROPE_PAIRS)
    ang = jnp.concatenate([row[:, None] * inv, col[:, None] * inv], axis=-1)
    cos, sin = jnp.cos(ang), jnp.sin(ang)
    reps = LANES // DA_HEAD_DIM
    cos_t = jnp.tile(jnp.concatenate([cos, cos], axis=-1), (1, reps))
    sin_t = jnp.tile(jnp.concatenate([-sin, sin], axis=-1), (1, reps))
    return cos_t, sin_t


def _lower_bounds(p):
    cum = jnp.cumsum(jax.nn.softmax(p.astype(F32), axis=0), axis=0)
    return cum - cum[0]


def _pad128(v):
    return jnp.pad(v.reshape(1, -1).astype(F32), ((0, 0), (0, LANES - v.size)))


def kernel(x, c, ctx, c_ctx, ada_w, ada_b, norm1_w, norm2_w, w_in, ssd_conv_w, ssd_conv_b, ssd_dt_bias,
           ssd_a_log, ssd_d, ssd_norm_w, hg_lb, hg_norm_w, da_lambda, da_norm_w, w_up, w_gate, b_gate,
           w_out, moe_router, moe_w1, moe_w3, moe_w2, final_norm_w):
    bsz, n_lat, _ = x.shape
    n_ctx = ctx.shape[1]
    depth = ada_w.shape[0]
    rope = _rope_tables(n_lat)
    lb_all = jnp.stack([_lower_bounds(hg_lb[0]), _lower_bounds(hg_lb[1])], axis=1)

    mod = _adaln(jnp.concatenate([c, c_ctx[None]], axis=0), ada_w, ada_b)
    mod = mod.reshape(depth, 8, 6, 1, D)

    xl, xc = x, ctx
    for l in range(depth):
        need_ctx = l < depth - 1
        lam_init = 0.8 - 0.6 * math.exp(-0.3 * l)
        mod_l = [mod[l, :bsz, k] for k in range(6)]
        mod_c = [jnp.broadcast_to(mod[l, bsz:bsz + 1, k], (bsz, 1, D)) for k in range(6)]

        wl = w_in[l]
        w_a = jnp.concatenate([wl[:, OFF_SSD_Z:OFF_SSD_XBC], wl[:, OFF_HG_Q:OFF_HG_F], wl[:, OFF_HG_I:OFF_DA_Q],
                               wl[:, OFF_DA_V:N_IN], wl[:, OFF_SSD_XBC:OFF_SSD_DT]], axis=1).astype(BF16)
        w_f = jnp.concatenate([wl[:, OFF_HG_F:OFF_HG_I], wl[:, OFF_SSD_DT:OFF_HG_Q],
                               jnp.zeros((D, PF_N - PF_DT - 2 * SSD_HEADS), F32)], axis=1).astype(BF16)
        w_qk = wl[:, OFF_DA_Q:OFF_DA_V].astype(BF16)
        alog128 = _pad128(ssd_a_log[l])
        bias128 = _pad128(ssd_dt_bias[l])
        d_full = jnp.repeat(ssd_d[l].astype(F32), SSD_D_INNER // SSD_HEADS).reshape(1, SSD_D_INNER)
        ssd_nw = ssd_norm_w[l].reshape(1, SSD_D_INNER).astype(F32)
        hg_nw = hg_norm_w[l].reshape(1, HG_HEAD_DIM).astype(F32)
        wg_b = w_gate[l].astype(BF16)
        bg = b_gate[l].reshape(1, N_BRANCH * D)
        wu_b = w_up[l].astype(BF16)
        wo_b = w_out[l].astype(BF16)

        def project(xs, m, use_rope):
            n = xs.shape[1]
            h = _modulate(xs, norm1_w[l], m[0], m[1])
            h2d = h.reshape(bsz * n, D)
            tm = min(1024, n)
            pa = _matmul(h2d, w_a, BF16, tm, PA_N // 4).reshape(bsz, n, PA_N)
            pf = _matmul(h2d, w_f, F32, tm, PF_N).reshape(bsz, n, PF_N)
            pqk = _matmul(h2d, w_qk, BF16, tm, 1024, rope_tables=rope if use_rope else None).reshape(bsz, n, PQK_N)
            return h, pa, pf, pqk

        hl, pa_l, pf_l, pqk_l = project(xl, mod_l, True)
        hc, pa_c, pf_c, pqk_c = project(xc, mod_c, False)

        xa_c = _ssd_conv(pa_c, ssd_conv_w[l], ssd_conv_b[l])
        xa_l = _ssd_conv(pa_l, ssd_conv_w[l], ssd_conv_b[l])
        yc_f, s_f = _ssd_scan(xa_c, pf_c, alog128, bias128, rev=False)
        yl_f, _ = _ssd_scan(xa_l, pf_l, alog128, bias128, rev=False, init=s_f)
        ya_c, s_b = _ssd_scan(xa_c, pf_c, alog128, bias128, rev=True, readout=(pa_c, yc_f, d_full, ssd_nw))
        ya_l, _ = _ssd_scan(xa_l, pf_l, alog128, bias128, rev=True, init=s_b, readout=(pa_l, yl_f, d_full, ssd_nw))

        lb_f, lb_b = lb_all[l, 0].reshape(1, HG_WIDTH), lb_all[l, 1].reshape(1, HG_WIDTH)
        oc_f, t_f = _hg_scan(pa_c, pf_c, lb_f, rev=False)
        ol_f, _ = _hg_scan(pa_l, pf_l, lb_f, rev=False, init=t_f)
        yb_c, t_b = _hg_scan(pa_c, pf_c, lb_b, rev=True, readout=(oc_f, hg_nw))
        yb_l, _ = _hg_scan(pa_l, pf_l, lb_b, rev=True, init=t_b, readout=(ol_f, hg_nw))

        k_all = jnp.concatenate([pqk_c[..., DA_WIDTH:], pqk_l[..., DA_WIDTH:]], axis=1)
        v_all = jnp.concatenate([pa_c[..., PA_DV:PA_DV + DA_WIDTH], pa_l[..., PA_DV:PA_DV + DA_WIDTH]], axis=1)
        yd_l = _diff_attn(pqk_l, [(k_all, 0, v_all, 0)], da_lambda[l], da_norm_w[l], lam_init, 512, ATTN_TK)

        xl = _merge(xl, mod_l[2], hl, ya_l, yb_l, yd_l, wg_b, bg, wu_b, wo_b)
        h2l, aff_l = _modulate(xl, norm2_w[l], mod_l[3], mod_l[4], w_router=moe_router[l])
        token_sets = [(xl, mod_l[5], h2l, aff_l, final_norm_w if l == depth - 1 else None)]
        if need_ctx:
            yd_c = _diff_attn(pqk_c, [(pqk_c, DA_WIDTH, pa_c, PA_DV)], da_lambda[l], da_norm_w[l], lam_init,
                              n_ctx, ATTN_TK)
            xc = _merge(xc, mod_c[2], hc, ya_c, yb_c, yd_c, wg_b, bg, wu_b, wo_b)
            h2c, aff_c = _modulate(xc, norm2_w[l], mod_c[3], mod_c[4], w_router=moe_router[l])
            token_sets.append((xc, mod_c[5], h2c, aff_c, None))
        outs = _ec_moe(token_sets, moe_w1, moe_w3, moe_w2, l)
        xl = outs[0]
        if need_ctx:
            xc = outs[1]
    return xl
```

```python
import functools
import math

import jax
import jax.numpy as jnp
from jax import lax
from jax.experimental import pallas as pl
from jax.experimental.pallas import tpu as pltpu

F32 = jnp.float32
BF16 = jnp.bfloat16
I32 = jnp.int32
HIGHEST = lax.Precision.HIGHEST
LOG2E = 1.4426950408889634

D = 2048
EPS = 1e-6
GRID_W = 64
SSD_D_INNER = 1024
SSD_HEADS = 16
SSD_GROUPS = 2
SSD_STATE = 128
SSD_XBC = 1536
SSD_CONV_W = 5
HG_WIDTH = 1024
HG_HEADS = 8
HG_HEAD_DIM = 128
DA_HEADS = 8
DA_HEAD_DIM = 64
DA_WIDTH = 1024
ROPE_BASE = 10000.0
ROPE_PAIRS = 16
N_BRANCH = 3
N_EXPERTS = 16
EXPERT_FF = 2048
EC_CAPACITY = 2

OFF_SSD_Z = 0
OFF_SSD_XBC = 1024
OFF_SSD_DT = 2560
OFF_HG_Q = 2592
OFF_HG_F = OFF_HG_Q + 1024
OFF_HG_I = OFF_HG_F + 2048
OFF_HG_G = OFF_HG_I + 1024
OFF_DA_Q = OFF_HG_G + 1024
OFF_DA_K = OFF_DA_Q + 1024
OFF_DA_V = OFF_DA_K + 1024
N_IN = OFF_DA_V + 1024

LANES = 128
PA_Z, PA_HQ, PA_HI, PA_HG, PA_DV, PA_XBC = 0, 1024, 2048, 3072, 4096, 5120
PA_N = 6656
PF_F, PF_DT = 0, 2048
PF_N = 2176
PQK_N = 2048

SCAN_Q = 256
ATTN_TK = 1408
HALO = 16
TOK_BLK = 128
COMBINE_TILE = 256
GATHER_COLS = 512
EXPERT_GROUP = 16
GATHER_WIN_SMALL = 48
SLOT_ALIGN = 16
VMEM_LIMIT = 56 * 1024 * 1024


def _cparams(sem, vmem=VMEM_LIMIT):
    return pltpu.CompilerParams(dimension_semantics=sem, vmem_limit_bytes=vmem)


def _silu(x):
    return x * jax.nn.sigmoid(x)


def _adaln_kernel(cb_ref, w_ref, b_ref, o_ref, *, n_rows, tn):
    nct = tn // LANES

    def body(k, accs):
        r0 = pl.multiple_of(k * 8, 8)
        ws = [w_ref[pl.ds(r0, 8), c * LANES:(c + 1) * LANES] for c in range(nct)]
        out = []
        for r in range(n_rows):
            s = _silu(cb_ref[r, pl.ds(r0, 8), :])
            out.append(tuple(accs[r][c] + s * ws[c] for c in range(nct)))
        return tuple(out)

    zero = jnp.zeros((8, LANES), F32)
    accs = lax.fori_loop(0, D // 8, body, tuple(tuple(zero for _ in range(nct)) for _ in range(n_rows)),
                         unroll=8)
    rows = []
    for r in range(n_rows):
        rows.append(jnp.concatenate([jnp.sum(accs[r][c], axis=0, keepdims=True) for c in range(nct)], axis=1))
    rows.append(jnp.zeros((8 - n_rows, tn), F32))
    o_ref[...] = jnp.concatenate(rows, axis=0) + b_ref[...]


def _adaln(cvec, ada_w, ada_b):
    n_rows = cvec.shape[0]
    depth, _, n6 = ada_w.shape
    tn = 512
    cb = jnp.broadcast_to(cvec[:, :, None], (n_rows, D, LANES))
    return pl.pallas_call(
        functools.partial(_adaln_kernel, n_rows=n_rows, tn=tn),
        grid=(depth, n6 // tn),
        in_specs=[pl.BlockSpec((n_rows, D, LANES), lambda l, j: (0, 0, 0)),
                  pl.BlockSpec((None, D, tn), lambda l, j: (l, 0, j)),
                  pl.BlockSpec((None, 1, tn), lambda l, j: (l, 0, j))],
        out_specs=pl.BlockSpec((None, 8, tn), lambda l, j: (l, 0, j)),
        out_shape=jax.ShapeDtypeStruct((depth, 8, n6), F32),
        compiler_params=_cparams(("parallel", "parallel")),
        name="adaln",
    )(cb, ada_w, ada_b.reshape(depth, 1, n6))


def _modulate_kernel(x_ref, nw_ref, sh_ref, sc_ref, *rest, with_router):
    x = x_ref[...]
    ms = jnp.mean(x * x, axis=-1, keepdims=True)
    h = (x * lax.rsqrt(ms + EPS) * nw_ref[...]) * (1.0 + sc_ref[...]) + sh_ref[...]
    if with_router:
        wr_ref, h_ref, aff_ref = rest
        logits = jnp.dot(h, wr_ref[...], preferred_element_type=F32, precision=HIGHEST)
        m = jnp.max(logits, axis=-1, keepdims=True)
        e = jnp.exp(logits - m)
        aff_ref[...] = e / jnp.sum(e, axis=-1, keepdims=True)
    else:
        (h_ref,) = rest
    h_ref[...] = h.astype(BF16)


def _modulate(x, nw, shift, scale, w_router=None):
    bsz, n, _ = x.shape
    tm = min(512, n)
    with_router = w_router is not None
    in_specs = [pl.BlockSpec((None, tm, D), lambda b, i: (b, i, 0)),
                pl.BlockSpec((1, D), lambda b, i: (0, 0)),
                pl.BlockSpec((None, 1, D), lambda b, i: (b, 0, 0)),
                pl.BlockSpec((None, 1, D), lambda b, i: (b, 0, 0))]
    args = [x, nw.reshape(1, D), shift, scale]
    out_specs = [pl.BlockSpec((None, tm, D), lambda b, i: (b, i, 0))]
    out_shape = [jax.ShapeDtypeStruct((bsz, n, D), BF16)]
    if with_router:
        in_specs.append(pl.BlockSpec((D, N_EXPERTS), lambda b, i: (0, 0)))
        args.append(w_router)
        out_specs.append(pl.BlockSpec((None, tm, N_EXPERTS), lambda b, i: (b, i, 0)))
        out_shape.append(jax.ShapeDtypeStruct((bsz, n, N_EXPERTS), F32))
    res = pl.pallas_call(
        functools.partial(_modulate_kernel, with_router=with_router),
        grid=(bsz, n // tm),
        in_specs=in_specs, out_specs=out_specs, out_shape=out_shape,
        compiler_params=_cparams(("parallel", "parallel")),
        name="modulate_router" if with_router else "modulate",
    )(*args)
    return res if with_router else res[0]


def _mm_kernel(a_ref, w_ref, *rest, rope, tn):
    acc = jnp.dot(a_ref[...], w_ref[...], preferred_element_type=F32)
    if rope:
        cos_ref, sin_ref, o_ref = rest
        cos = cos_ref[...]
        sin = sin_ref[...]
        lane = lax.broadcasted_iota(I32, cos.shape, 1)
        first_half = (lane % DA_HEAD_DIM) < (DA_HEAD_DIM // 2)
        for c in range(tn // LANES):
            xs = acc[:, c * LANES:(c + 1) * LANES]
            partner = jnp.where(first_half,
                                pltpu.roll(xs, LANES - DA_HEAD_DIM // 2, 1),
                                pltpu.roll(xs, DA_HEAD_DIM // 2, 1))
            o_ref[:, c * LANES:(c + 1) * LANES] = (xs * cos + partner * sin).astype(o_ref.dtype)
    else:
        (o_ref,) = rest
        o_ref[...] = acc.astype(o_ref.dtype)


def _matmul(a, w, out_dtype, tm, tn, rope_tables=None):
    m, k = a.shape
    n = w.shape[1]
    rope = rope_tables is not None
    in_specs = [pl.BlockSpec((tm, k), lambda i, j: (i, 0)),
                pl.BlockSpec((k, tn), lambda i, j: (0, j))]
    args = [a, w]
    if rope:
        cos, sin = rope_tables
        nt = cos.shape[0] // tm
        in_specs += [pl.BlockSpec((tm, LANES), lambda i, j: (i % nt, 0)),
                     pl.BlockSpec((tm, LANES), lambda i, j: (i % nt, 0))]
        args += [cos, sin]
    return pl.pallas_call(
        functools.partial(_mm_kernel, rope=rope, tn=tn),
        grid=(m // tm, n // tn),
        in_specs=in_specs,
        out_specs=pl.BlockSpec((tm, tn), lambda i, j: (i, j)),
        out_shape=jax.ShapeDtypeStruct((m, n), out_dtype),
        compiler_params=_cparams(("parallel", "parallel")),
        name="proj_rope" if rope else "proj",
    )(*args)


def _conv_kernel(prev_ref, cur_ref, next_ref, w_ref, b_ref, o_ref, *, tc):
    i = pl.program_id(1)
    last = pl.num_programs(1) - 1
    pad = (SSD_CONV_W - 1) // 2
    prev = jnp.where(i > 0, prev_ref[...].astype(F32), 0.0)
    nxt = jnp.where(i < last, next_ref[...].astype(F32), 0.0)
    ext = jnp.concatenate([prev, cur_ref[...].astype(F32), nxt], axis=0)
    w = w_ref[...]
    y = b_ref[...] + w[0:1] * ext[HALO - pad:HALO - pad + tc]
    for k in range(1, SSD_CONV_W):
        y = y + w[k:k + 1] * ext[HALO - pad + k:HALO - pad + k + tc]
    o_ref[...] = _silu(y).astype(o_ref.dtype)


def _ssd_conv(pa, conv_w, conv_b):
    bsz, n, _ = pa.shape
    tc = min(512, n)
    tw = 512
    c0 = PA_XBC // tw
    nbh = n // HALO
    return pl.pallas_call(
        functools.partial(_conv_kernel, tc=tc),
        grid=(bsz, n // tc, SSD_XBC // tw),
        in_specs=[pl.BlockSpec((None, HALO, tw), lambda b, i, j: (b, jnp.maximum(i * (tc // HALO) - 1, 0), c0 + j)),
                  pl.BlockSpec((None, tc, tw), lambda b, i, j: (b, i, c0 + j)),
                  pl.BlockSpec((None, HALO, tw),
                               lambda b, i, j: (b, jnp.minimum((i + 1) * (tc // HALO), nbh - 1), c0 + j)),
                  pl.BlockSpec((SSD_CONV_W, tw), lambda b, i, j: (0, j)),
                  pl.BlockSpec((1, tw), lambda b, i, j: (0, j))],
        out_specs=pl.BlockSpec((None, tc, tw), lambda b, i, j: (b, i, j)),
        out_shape=jax.ShapeDtypeStruct((bsz, n, SSD_XBC), BF16),
        compiler_params=_cparams(("parallel", "parallel", "parallel")),
        name="ssd_conv",
    )(pa, pa, pa, conv_w, conv_b.reshape(1, SSD_XBC))


def _tri(q, rev):
    r = lax.broadcasted_iota(I32, (q, q), 0)
    c = lax.broadcasted_iota(I32, (q, q), 1)
    return (c >= r) if rev else (r >= c)


def _split3(x):
    hi = x.astype(BF16)
    r = x - hi.astype(F32)
    mid = r.astype(BF16)
    lo = (r - mid.astype(F32)).astype(BF16)
    return hi, mid, lo


def _select_rows_dot(sel01, x):
    hi, mid, lo = _split3(x)
    return (jnp.dot(sel01, hi, preferred_element_type=F32) + jnp.dot(sel01, mid, preferred_element_type=F32)
            + jnp.dot(sel01, lo, preferred_element_type=F32))


def _select_cols_dot(x, sel01):
    hi, mid, lo = _split3(x)
    return (jnp.dot(hi, sel01, preferred_element_type=F32) + jnp.dot(mid, sel01, preferred_element_type=F32)
            + jnp.dot(lo, sel01, preferred_element_type=F32))


def _ssd_scan_kernel(*refs, rev, has_init, readout):
    it = iter(refs)
    xact_ref, dt_ref, alog_ref, bias_ref = next(it), next(it), next(it), next(it)
    init_ref = next(it) if has_init else None
    if readout:
        z_ref, yprev_ref, d_ref, nw_ref = next(it), next(it), next(it), next(it)
    y_ref, st_ref = next(it), next(it)
    ybuf_ref = next(it) if readout else None

    q = SCAN_Q
    hd = SSD_D_INNER // SSD_HEADS
    hg = SSD_HEADS // SSD_GROUPS
    gw = hg * hd
    d_off = SSD_HEADS if rev else 0

    @pl.when(pl.program_id(1) == 0)
    def _():
        st_ref[...] = init_ref[...] if has_init else jnp.zeros_like(st_ref)

    tri = _tri(q, rev)
    dtv = jax.nn.softplus(dt_ref[...] + bias_ref[...])
    da = dtv * (-LOG2E * jnp.exp(alog_ref[...]))
    cs = _select_rows_dot(jnp.where(tri, 1.0, 0.0).astype(BF16), da)
    cs_t = cs.T
    erow = lax.broadcasted_iota(I32, (LANES, SSD_D_INNER), 0)
    ecol = lax.broadcasted_iota(I32, (LANES, SSD_D_INNER), 1)
    expand = jnp.where(erow == d_off + ecol // hd, 1.0, 0.0).astype(BF16)
    csx = _select_cols_dot(cs, expand)
    dtx = _select_cols_dot(dtv, expand)
    x = xact_ref[:, :SSD_D_INNER].astype(F32)
    xdt = x * dtx
    xdt_b = xdt.astype(BF16)
    cs_end = csx[0:1] if rev else csx[q - 1:q]
    ecs = jnp.exp2(csx)
    xw = (xdt * jnp.exp2(cs_end - csx)).astype(BF16)
    ecs_end = jnp.exp2(cs_end)
    lane = lax.broadcasted_iota(I32, (q, LANES), 1)
    out_ref = ybuf_ref if readout else y_ref

    for g in range(SSD_GROUPS):
        bg = xact_ref[:, SSD_D_INNER + g * SSD_STATE:SSD_D_INNER + (g + 1) * SSD_STATE]
        cg = xact_ref[:, SSD_D_INNER + (SSD_GROUPS + g) * SSD_STATE:SSD_D_INNER + (SSD_GROUPS + g + 1) * SSD_STATE]
        cb = lax.dot_general(cg, bg, (((1,), (1,)), ((), ())), preferred_element_type=F32)
        st_g = st_ref[:, g * gw:(g + 1) * gw]
        y_inter = jnp.dot(cg, st_g.astype(BF16), preferred_element_type=F32) * ecs[:, g * gw:(g + 1) * gw]
        for jj in range(hg // 2):
            col0 = g * gw + jj * LANES
            xp = xdt_b[:, col0:col0 + LANES]
            acc = y_inter[:, jj * LANES:(jj + 1) * LANES]
            for s in range(2):
                j = d_off + g * hg + 2 * jj + s
                e = cs[:, j:j + 1] - cs_t[j:j + 1, :]
                seg = jnp.exp2(jnp.where(tri, e, -jnp.inf))
                m = (cb * seg).astype(BF16)
                xm = jnp.where((lane >= hd) if s else (lane < hd), xp, jnp.zeros_like(xp))
                acc = acc + jnp.dot(m, xm, preferred_element_type=F32)
            out_ref[:, col0:col0 + LANES] = acc
        upd = lax.dot_general(bg, xw[:, g * gw:(g + 1) * gw], (((0,), (0,)), ((), ())),
                              preferred_element_type=F32)
        st_ref[:, g * gw:(g + 1) * gw] = st_g * ecs_end[:, g * gw:(g + 1) * gw] + upd

    if readout:
        z = z_ref[...].astype(F32)
        yy = (yprev_ref[...] + ybuf_ref[...] + d_ref[...] * x) * _silu(z)
        for g in range(SSD_GROUPS):
            seg = yy[:, g * gw:(g + 1) * gw]
            ms = jnp.mean(seg * seg, axis=-1, keepdims=True)
            y_ref[:, g * gw:(g + 1) * gw] = (seg * lax.rsqrt(ms + EPS) * nw_ref[:, g * gw:(g + 1) * gw]).astype(y_ref.dtype)


def _ssd_scan(xact, pf, alog128, bias128, *, rev, init=None, readout=None):
    bsz, n, _ = xact.shape
    nc = n // SCAN_Q
    cidx = (lambda c: nc - 1 - c) if rev else (lambda c: c)
    in_specs = [pl.BlockSpec((None, SCAN_Q, SSD_XBC), lambda b, c: (b, cidx(c), 0)),
                pl.BlockSpec((None, SCAN_Q, LANES), lambda b, c: (b, cidx(c), PF_DT // LANES)),
                pl.BlockSpec((1, LANES), lambda b, c: (0, 0)),
                pl.BlockSpec((1, LANES), lambda b, c: (0, 0))]
    args = [xact, pf, alog128, bias128]
    if init is not None:
        in_specs.append(pl.BlockSpec((None, SSD_STATE, SSD_D_INNER), lambda b, c: (b, 0, 0)))
        args.append(init)
    scratch = []
    if readout is not None:
        pa, yprev, d_full, norm_w = readout
        in_specs += [pl.BlockSpec((None, SCAN_Q, SSD_D_INNER), lambda b, c: (b, cidx(c), PA_Z // SSD_D_INNER)),
                     pl.BlockSpec((None, SCAN_Q, SSD_D_INNER), lambda b, c: (b, cidx(c), 0)),
                     pl.BlockSpec((1, SSD_D_INNER), lambda b, c: (0, 0)),
                     pl.BlockSpec((1, SSD_D_INNER), lambda b, c: (0, 0))]
        args += [pa, yprev, d_full, norm_w]
        scratch = [pltpu.VMEM((SCAN_Q, SSD_D_INNER), F32)]
    y_dtype = BF16 if readout is not None else F32
    y, st = pl.pallas_call(
        functools.partial(_ssd_scan_kernel, rev=rev, has_init=init is not None, readout=readout is not None),
        grid=(bsz, nc),
        in_specs=in_specs,
        out_specs=[pl.BlockSpec((None, SCAN_Q, SSD_D_INNER), lambda b, c: (b, cidx(c), 0)),
                   pl.BlockSpec((None, SSD_STATE, SSD_D_INNER), lambda b, c: (b, 0, 0))],
        out_shape=[jax.ShapeDtypeStruct((bsz, n, SSD_D_INNER), y_dtype),
                   jax.ShapeDtypeStruct((bsz, SSD_STATE, SSD_D_INNER), F32)],
        scratch_shapes=scratch,
        compiler_params=_cparams(("parallel", "arbitrary")),
        name="ssd_scan_bwd" if rev else "ssd_scan_fwd",
    )(*args)
    return y, st


def _ref_rows(bb, s, rev):
    q, w = bb.shape
    rl = s if rev else s - 1
    if 2 * s >= 8:
        b3 = bb.reshape(q // (2 * s), 2 * s, w)
        return jnp.broadcast_to(b3[:, rl:rl + 1, :], b3.shape).reshape(q, w)
    off = lax.broadcasted_iota(I32, (q, 1), 0) % (2 * s)
    out = bb
    for ov in range(2 * s):
        if ov != rl:
            out = jnp.where(off == ov, pltpu.roll(bb, (ov - rl) % q, 0), out)
    return out


def _level_operand(bb, qq, kk, s, rev):
    q = bb.shape[0]
    if s < 8:
        upper = (lax.broadcasted_iota(I32, (q, 1), 0) & s) != 0
        q_side = jnp.logical_not(upper) if rev else upper
        ex = jnp.exp2(-jnp.abs(bb - _ref_rows(bb, s, rev)))
        return (jnp.where(q_side, qq, kk) * ex).astype(BF16)
    pieces = []
    for a in range(0, q, 2 * s):
        lo, hi = slice(a, a + s), slice(a + s, a + 2 * s)
        if rev:
            r = bb[a + s:a + s + 1]
            pieces += [qq[lo] * jnp.exp2(bb[lo] - r), kk[hi] * jnp.exp2(r - bb[hi])]
        else:
            r = bb[a + s - 1:a + s]
            pieces += [kk[lo] * jnp.exp2(r - bb[lo]), qq[hi] * jnp.exp2(bb[hi] - r)]
    return jnp.concatenate(pieces, axis=0).astype(BF16)


def _hg_scan_kernel(*refs, rev, has_init, readout):
    it = iter(refs)
    q_ref, f_ref, v_ref, lb_ref = next(it), next(it), next(it), next(it)
    init_ref = next(it) if has_init else None
    if readout:
        g_ref, oprev_ref, nw_ref = next(it), next(it), next(it)
    o_ref, st_ref, att_ref = next(it), next(it), next(it)
    obuf_ref = next(it) if readout else None

    q = SCAN_Q
    hd = HG_HEAD_DIM

    @pl.when(pl.program_id(1) == 0)
    def _():
        st_ref[...] = init_ref[...] if has_init else jnp.zeros_like(st_ref)

    lb = lb_ref[...]
    qq = _silu(q_ref[...].astype(F32))
    f = lb + (1.0 - lb) * jax.nn.sigmoid(f_ref[...])
    kk = 1.0 - f
    lf = jnp.log(f) * LOG2E
    tri = _tri(q, rev)
    bb = jnp.dot(jnp.where(tri, 1.0, 0.0), lf, preferred_element_type=F32, precision=HIGHEST)

    half = q // 2
    ri = lax.broadcasted_iota(I32, (half, half), 0)
    ci = lax.broadcasted_iota(I32, (half, half), 1)
    xr = ri ^ ci
    causal = (ci > ri) if rev else (ri > ci)
    nt = (((1,), (1,)), ((), ()))
    s = half
    while s >= 1:
        u = _level_operand(bb, qq, kk, s, rev)
        if s < half:
            level = jnp.logical_and((xr >> int(math.log2(s))) == 1, causal)
        for h in range(HG_HEADS):
            u_lo, u_hi = u[:half, h * hd:(h + 1) * hd], u[half:, h * hd:(h + 1) * hd]
            if s == half:
                qs_, ks_ = (u_lo, u_hi) if rev else (u_hi, u_lo)
                att_ref[h, 2] = lax.dot_general(qs_, ks_, nt, preferred_element_type=F32)
            else:
                for blk, ub in enumerate((u_lo, u_hi)):
                    p = lax.dot_general(ub, ub, nt, preferred_element_type=F32)
                    old = att_ref[h, blk] if 2 * s < half else 0.0
                    att_ref[h, blk] = jnp.where(level, p, old)
        s //= 2

    b_end = bb[0:1] if rev else bb[q - 1:q]
    qe = (qq * jnp.exp2(bb)).astype(BF16)
    kh = (kk * jnp.exp2(b_end - bb)).astype(BF16)
    qk = (qq * kk).astype(BF16)
    e_end = jnp.exp2(b_end)
    ones = jnp.ones((hd, hd), BF16)
    out_ref = obuf_ref if readout else o_ref
    for h in range(HG_HEADS):
        sl = slice(h * hd, (h + 1) * hd)
        vh = v_ref[:, sl]
        st_h = st_ref[sl, :]
        v_lo, v_hi = vh[:half], vh[half:]
        d0, d1, off = (att_ref[h, k].astype(BF16) for k in range(3))
        o_lo = jnp.dot(d0, v_lo, preferred_element_type=F32)
        o_hi = jnp.dot(d1, v_hi, preferred_element_type=F32)
        if rev:
            o_lo = o_lo + jnp.dot(off, v_hi, preferred_element_type=F32)
        else:
            o_hi = o_hi + jnp.dot(off, v_lo, preferred_element_type=F32)
        o = jnp.concatenate([o_lo, o_hi], axis=0)
        o = o + jnp.dot(qk[:, sl], ones, preferred_element_type=F32) * vh.astype(F32)
        o = o + lax.dot_general(qe[:, sl], st_h.astype(BF16), (((1,), (1,)), ((), ())), preferred_element_type=F32)
        out_ref[:, sl] = o
        upd = lax.dot_general(vh, kh[:, sl], (((0,), (0,)), ((), ())), preferred_element_type=F32)
        st_ref[sl, :] = st_h * e_end[:, sl] + upd

    if readout:
        for h in range(HG_HEADS):
            sl = slice(h * hd, (h + 1) * hd)
            o = oprev_ref[:, sl] + obuf_ref[:, sl]
            ms = jnp.mean(o * o, axis=-1, keepdims=True)
            gate = _silu(g_ref[:, sl].astype(F32))
            o_ref[:, sl] = (o * lax.rsqrt(ms + EPS) * nw_ref[...] * gate).astype(o_ref.dtype)


def _hg_scan(pa, pf, lb_row, *, rev, init=None, readout=None):
    bsz, n, _ = pa.shape
    nc = n // SCAN_Q
    cidx = (lambda c: nc - 1 - c) if rev else (lambda c: c)
    fcol = (PF_F // HG_WIDTH) + (1 if rev else 0)
    in_specs = [pl.BlockSpec((None, SCAN_Q, HG_WIDTH), lambda b, c: (b, cidx(c), PA_HQ // HG_WIDTH)),
                pl.BlockSpec((None, SCAN_Q, HG_WIDTH), lambda b, c: (b, cidx(c), fcol)),
                pl.BlockSpec((None, SCAN_Q, HG_WIDTH), lambda b, c: (b, cidx(c), PA_HI // HG_WIDTH)),
                pl.BlockSpec((1, HG_WIDTH), lambda b, c: (0, 0))]
    args = [pa, pf, pa, lb_row]
    if init is not None:
        in_specs.append(pl.BlockSpec((None, HG_WIDTH, HG_HEAD_DIM), lambda b, c: (b, 0, 0)))
        args.append(init)
    scratch = [pltpu.VMEM((HG_HEADS, 3, SCAN_Q // 2, SCAN_Q // 2), F32)]
    if readout is not None:
        oprev, nw128 = readout
        in_specs += [pl.BlockSpec((None, SCAN_Q, HG_WIDTH), lambda b, c: (b, cidx(c), PA_HG // HG_WIDTH)),
                     pl.BlockSpec((None, SCAN_Q, HG_WIDTH), lambda b, c: (b, cidx(c), 0)),
                     pl.BlockSpec((1, HG_HEAD_DIM), lambda b, c: (0, 0))]
        args += [pa, oprev, nw128]
        scratch.append(pltpu.VMEM((SCAN_Q, HG_WIDTH), F32))
    o_dtype = BF16 if readout is not None else F32
    o, st = pl.pallas_call(
        functools.partial(_hg_scan_kernel, rev=rev, has_init=init is not None, readout=readout is not None),
        grid=(bsz, nc),
        in_specs=in_specs,
        out_specs=[pl.BlockSpec((None, SCAN_Q, HG_WIDTH), lambda b, c: (b, cidx(c), 0)),
                   pl.BlockSpec((None, HG_WIDTH, HG_HEAD_DIM), lambda b, c: (b, 0, 0))],
        out_shape=[jax.ShapeDtypeStruct((bsz, n, HG_WIDTH), o_dtype),
                   jax.ShapeDtypeStruct((bsz, HG_WIDTH, HG_HEAD_DIM), F32)],
        scratch_shapes=scratch,
        compiler_params=_cparams(("parallel", "arbitrary")),
        name="hg_scan_bwd" if rev else "hg_scan_fwd",
    )(*args)
    return o, st


def _attn_kernel(*refs, seg_lens, tk, lam_init):
    lam_ref, nw_ref, q_ref = refs[0], refs[1], refs[2]
    nseg = len(seg_lens)
    kv_refs = refs[3:3 + 2 * nseg]
    o_ref, qs_ref, sa_ref, sb_ref, m_ref, mn_ref, acc_ref = refs[3 + 2 * nseg:10 + 2 * nseg]
    vaug_refs = refs[10 + 2 * nseg:]
    tq = q_ref.shape[0]

    @pl.when(pl.program_id(2) == 0)
    def _():
        for si in range(nseg):
            v_ref, vaug_ref = kv_refs[2 * si + 1], vaug_refs[si]
            vaug_ref[:, :2 * DA_HEAD_DIM] = v_ref[...]
            vaug_ref[:, 2 * DA_HEAD_DIM:] = jnp.ones(v_ref.shape, BF16)

    qv = q_ref[...] * jnp.asarray(DA_HEAD_DIM ** -0.5, BF16)
    lane = lax.broadcasted_iota(I32, qv.shape, 1)
    zero = jnp.zeros_like(qv)
    qs_ref[:tq] = jnp.where(lane < DA_HEAD_DIM, qv, zero)
    qs_ref[tq:] = jnp.where(lane >= DA_HEAD_DIM, qv, zero)
    m_ref[...] = jnp.full_like(m_ref, -jnp.inf)
    acc_ref[...] = jnp.zeros_like(acc_ref)

    def scores(kc):
        return lax.dot_general(qs_ref[...], kc, (((1,), (1,)), ((), ())), preferred_element_type=F32)

    def lane_tiles(a):
        return [a[:, c * LANES:(c + 1) * LANES] for c in range(a.shape[1] // LANES)]

    def row_max(s, m_prev):
        tiles = lane_tiles(s)
        m = tiles[0]
        for tile in tiles[1:]:
            m = jnp.maximum(m, tile)
        return jnp.maximum(m_prev, jnp.broadcast_to(jnp.max(m, axis=-1, keepdims=True), m_prev.shape))

    def online_step(s, vc, m_prev, m_cur):
        alpha = jnp.exp(m_prev - m_cur)
        ps = [jnp.exp(tile - m_cur) for tile in lane_tiles(s)]
        pv = jnp.dot(jnp.concatenate(ps, axis=1).astype(BF16), vc, preferred_element_type=F32)
        acc_ref[...] = jnp.concatenate([alpha, alpha], axis=1) * acc_ref[...] + pv

    for si, n in enumerate(seg_lens):
        k_ref, v_ref = kv_refs[2 * si], vaug_refs[si]
        t = min(tk, n)
        nc = n // t
        if nc == 1:
            s = scores(k_ref[...])
            m_prev = m_ref[...]
            m_cur = row_max(s, m_prev)
            online_step(s, v_ref[...], m_prev, m_cur)
            m_ref[...] = m_cur
        else:
            s0 = scores(k_ref[pl.ds(0, t), :])
            sa_ref[...] = s0
            mn_ref[...] = row_max(s0, m_ref[...])

            def pipe_step(c, cur_ref, nxt_ref, k_ref=k_ref, v_ref=v_ref, t=t):
                r0 = pl.multiple_of(c * t, t)
                r1 = pl.multiple_of(r0 + t, t)
                m_prev, m_cur = m_ref[...], mn_ref[...]
                s_next = scores(k_ref[pl.ds(r1, t), :])
                nxt_ref[...] = s_next
                mn_ref[...] = row_max(s_next, m_cur)
                m_ref[...] = m_cur
                online_step(cur_ref[...], v_ref[pl.ds(r0, t), :], m_prev, m_cur)

            def body(i, carry):
                pipe_step(2 * i, sa_ref, sb_ref)
                pipe_step(2 * i + 1, sb_ref, sa_ref)
                return carry

            lax.fori_loop(0, (nc - 1) // 2, body, 0)
            last_ref = sa_ref
            if nc % 2 == 0:
                pipe_step(nc - 2, sa_ref, sb_ref)
                last_ref = sb_ref
            m_prev, m_cur = m_ref[...], mn_ref[...]
            online_step(last_ref[...], v_ref[pl.ds((nc - 1) * t, t), :], m_prev, m_cur)
            m_ref[...] = m_cur
    hw = 2 * DA_HEAD_DIM
    acc = acc_ref[:, :hw]
    l = acc_ref[:, hw:]
    lam = lam_ref[...]
    lmbda = (jnp.exp(jnp.sum(lam[0:1] * lam[1:2], axis=-1, keepdims=True))
             - jnp.exp(jnp.sum(lam[2:3] * lam[3:4], axis=-1, keepdims=True)) + lam_init)
    o = acc[:tq] / l[:tq] - lmbda * (acc[tq:] / l[tq:])
    ms = jnp.mean(o * o, axis=-1, keepdims=True)
    o_ref[...] = (o * lax.rsqrt(ms + EPS) * nw_ref[...] * (1.0 - lam_init)).astype(o_ref.dtype)


def _diff_attn(pqk_q, kv_segs, lam, nw, lam_init, tq, tk):
    bsz, nq, _ = pqk_q.shape
    hw = 2 * DA_HEAD_DIM
    in_specs = [pl.BlockSpec((4, DA_HEAD_DIM), lambda b, h, i: (0, 0)),
                pl.BlockSpec((1, hw), lambda b, h, i: (0, 0)),
                pl.BlockSpec((None, tq, hw), lambda b, h, i: (b, i, h))]
    args = [lam, nw.reshape(1, hw), pqk_q]
    seg_lens = []
    for k_arr, k_col, v_arr, v_col in kv_segs:
        nk = k_arr.shape[1]
        assert nk % min(tk, nk) == 0
        seg_lens.append(nk)
        in_specs += [pl.BlockSpec((None, nk, hw), functools.partial(lambda b, h, i, c: (b, 0, c + h), c=k_col // hw)),
                     pl.BlockSpec((None, nk, hw), functools.partial(lambda b, h, i, c: (b, 0, c + h), c=v_col // hw))]
        args += [k_arr, v_arr]
    return pl.pallas_call(
        functools.partial(_attn_kernel, seg_lens=tuple(seg_lens), tk=tk, lam_init=lam_init),
        grid=(bsz, DA_HEADS, nq // tq),
        in_specs=in_specs,
        out_specs=pl.BlockSpec((None, tq, hw), lambda b, h, i: (b, i, h)),
        out_shape=jax.ShapeDtypeStruct((bsz, nq, DA_WIDTH), BF16),
        scratch_shapes=[pltpu.VMEM((2 * tq, hw), BF16),
                        pltpu.VMEM((2 * tq, tk), F32),
                        pltpu.VMEM((2 * tq, tk), F32),
                        pltpu.VMEM((2 * tq, LANES), F32),
                        pltpu.VMEM((2 * tq, LANES), F32),
                        pltpu.VMEM((2 * tq, 2 * hw), F32)] + [pltpu.VMEM((nk, 2 * hw), BF16) for nk in seg_lens],
        compiler_params=_cparams(("parallel", "parallel", "arbitrary")),
        name="diff_attn",
    )(*args)


def _mix_kernel(h_ref, ya_ref, yb_ref, yd_ref, wg0, wg1, wg2, bg0, bg1, bg2, wu0, wu1, wu2, o_ref):
    h = h_ref[...]
    mix = None
    for y_ref, wg, bg, wu in ((ya_ref, wg0, bg0, wu0), (yb_ref, wg1, bg1, wu1), (yd_ref, wg2, bg2, wu2)):
        gate = jax.nn.sigmoid(jnp.dot(h, wg[...], preferred_element_type=F32) + bg[...])
        up = jnp.dot(y_ref[...], wu[...], preferred_element_type=F32)
        mix = gate * up if mix is None else mix + gate * up
    o_ref[...] = mix.astype(o_ref.dtype)


def _residual_out_kernel(x_ref, g1_ref, m_ref, wo_ref, o_ref):
    o_ref[...] = x_ref[...] + g1_ref[...] * jnp.dot(m_ref[...], wo_ref[...], preferred_element_type=F32)


def _merge(x, gate1, h, ya, yb, yd, w_gate, b_gate, w_up, w_out):
    bsz, n, _ = x.shape
    tm = min(1024, n)
    tn = 512
    nn = D // tn
    row = lambda b, i, j: (b, i, 0)
    in_specs = [pl.BlockSpec((None, tm, D), row),
                pl.BlockSpec((None, tm, 1024), row),
                pl.BlockSpec((None, tm, 1024), row),
                pl.BlockSpec((None, tm, 1024), row)]
    in_specs += [pl.BlockSpec((D, tn), functools.partial(lambda b, i, j, k: (0, k * nn + j), k=k)) for k in range(3)]
    in_specs += [pl.BlockSpec((1, tn), functools.partial(lambda b, i, j, k: (0, k * nn + j), k=k)) for k in range(3)]
    in_specs += [pl.BlockSpec((None, 1024, tn), functools.partial(lambda b, i, j, k: (k, 0, j), k=k)) for k in range(3)]
    mix = pl.pallas_call(
        _mix_kernel,
        grid=(bsz, n // tm, nn),
        in_specs=in_specs,
        out_specs=pl.BlockSpec((None, tm, tn), lambda b, i, j: (b, i, j)),
        out_shape=jax.ShapeDtypeStruct((bsz, n, D), BF16),
        compiler_params=_cparams(("parallel", "parallel", "parallel")),
        name="merge_mix",
    )(h, ya, yb, yd, w_gate, w_gate, w_gate, b_gate, b_gate, b_gate, w_up, w_up, w_up)
    return pl.pallas_call(
        _residual_out_kernel,
        grid=(bsz, n // tm, nn),
        in_specs=[pl.BlockSpec((None, tm, tn), lambda b, i, j: (b, i, j)),
                  pl.BlockSpec((None, 1, tn), lambda b, i, j: (b, 0, j)),
                  pl.BlockSpec((None, tm, D), row),
                  pl.BlockSpec((D, tn), lambda b, i, j: (0, j))],
        out_specs=pl.BlockSpec((None, tm, tn), lambda b, i, j: (b, i, j)),
        out_shape=jax.ShapeDtypeStruct((bsz, n, D), F32),
        compiler_params=_cparams(("parallel", "parallel", "parallel")),
        name="merge_out",
    )(x, gate1, mix, w_out)


def _route_kernel(aff_ref, posm_ref, pos_ref, *, cap, rpe):
    rows = N_EXPERTS * rpe
    bits = pltpu.bitcast(aff_ref[...], I32)
    ones = jnp.ones((LANES, LANES), BF16)
    r = lax.broadcasted_iota(I32, (rows, rows), 0)
    c = lax.broadcasted_iota(I32, (rows, rows), 1)
    same = (r // rpe) == (c // rpe)
    grp = same.astype(BF16)
    grp_before = jnp.logical_and(same, c < r).astype(BF16)
    lr = lax.broadcasted_iota(I32, (LANES, LANES), 0)
    lc = lax.broadcasted_iota(I32, (LANES, LANES), 1)
    before = (lr < lc).astype(BF16)

    def count(mask):
        per_row = jnp.dot(mask.astype(BF16), ones, preferred_element_type=F32)
        return jnp.dot(grp, per_row.astype(BF16), preferred_element_type=F32)

    def excl_prefix(mask):
        mb = mask.astype(BF16)
        within = jnp.dot(mb, before, preferred_element_type=F32)
        per_row = jnp.dot(mb, ones, preferred_element_type=F32)
        return within + jnp.dot(grp_before, per_row.astype(BF16), preferred_element_type=F32)

    def body(i, thr):
        cand = thr | (jnp.int32(1) << (30 - i))
        return jnp.where(count(bits >= cand) >= cap, cand, thr)

    thr = lax.fori_loop(0, 31, body, jnp.zeros((rows, LANES), I32))
    gt = bits > thr
    eq = bits == thr
    need = cap - count(gt)
    sel = jnp.logical_or(gt, jnp.logical_and(eq, excl_prefix(eq) < need))
    pos = excl_prefix(sel).astype(I32)
    pos_ref[...] = pos
    posm_ref[...] = jnp.where(sel, pos, -1)


def _route(aff, cap):
    bsz, n, _ = aff.shape
    rpe = n // LANES
    rows = N_EXPERTS * rpe
    aff_t = jnp.swapaxes(aff, 1, 2).reshape(bsz, rows, LANES)
    spec = pl.BlockSpec((None, rows, LANES), lambda b: (b, 0, 0))
    posm, pos = pl.pallas_call(
        functools.partial(_route_kernel, cap=cap, rpe=rpe),
        grid=(bsz,),
        in_specs=[spec], out_specs=[spec, spec],
        out_shape=[jax.ShapeDtypeStruct((bsz, rows, LANES), I32)] * 2,
        compiler_params=_cparams(("parallel",)),
        name="route",
    )(aff_t)
    return posm.reshape(bsz, N_EXPERTS, n), pos.reshape(bsz, N_EXPERTS, n)


def _gather_kernel(base_ref, base_small_ref, small_ref, h_ref, posm_ref, xg_ref, *, win, win_small, nblk):
    b = pl.program_id(0)
    xg_ref[...] = jnp.zeros_like(xg_ref)

    def scatter_block(j, bases_ref, w):
        t0 = pl.multiple_of(j * TOK_BLK, TOK_BLK)
        slot = lax.broadcasted_iota(I32, (w, TOK_BLK), 0)
        bases, onehots = [], []
        for e in range(N_EXPERTS):
            base = pl.multiple_of(bases_ref[(b * N_EXPERTS + e) * nblk + j], SLOT_ALIGN)
            rel = posm_ref[e:e + 1, pl.ds(t0, TOK_BLK)] - base
            bases.append(base)
            onehots.append(jnp.where(slot == rel, 1.0, 0.0).astype(BF16))
        rows = jnp.dot(jnp.concatenate(onehots, axis=0), h_ref[pl.ds(t0, TOK_BLK), :],
                       preferred_element_type=F32).astype(BF16)
        for e in range(N_EXPERTS):
            xg_ref[e, pl.ds(bases[e], w), :] += rows[e * w:(e + 1) * w]

    def body(j, carry):
        small = small_ref[b * nblk + j] != 0

        @pl.when(small)
        def _():
            scatter_block(j, base_small_ref, win_small)

        @pl.when(jnp.logical_not(small))
        def _():
            scatter_block(j, base_ref, win)

        return carry

    lax.fori_loop(0, nblk, body, 0)


def _expert_kernel(xg_ref, w1_ref, w3_ref, w2_ref, y_ref, acc_ref):
    f = pl.program_id(2)

    @pl.when(f == 0)
    def _():
        acc_ref[...] = jnp.zeros_like(acc_ref)

    xg = xg_ref[...]
    a = jnp.dot(xg, w1_ref[...].astype(BF16), preferred_element_type=F32)
    g = jnp.dot(xg, w3_ref[...].astype(BF16), preferred_element_type=F32)
    acc_ref[...] += jnp.dot((_silu(a) * g).astype(BF16), w2_ref[...].astype(BF16), preferred_element_type=F32)

    @pl.when(f == pl.num_programs(2) - 1)
    def _():
        y_ref[...] = acc_ref[...].astype(y_ref.dtype)


def _combine_kernel(base_ref, x_ref, g2_ref, aff_ref, posm_ref, *rest, win, win_tile, cap, nblk, nsub, final):
    y_refs = rest[:EXPERT_GROUP]
    if final:
        fw_ref, o_ref, acc_ref = rest[EXPERT_GROUP:]
    else:
        o_ref, acc_ref = rest[EXPERT_GROUP:]
    b, jt, eg = pl.program_id(0), pl.program_id(1), pl.program_id(2)

    @pl.when(eg == 0)
    def _():
        acc_ref[...] = jnp.zeros_like(acc_ref)

    lane = lax.broadcasted_iota(I32, (TOK_BLK, N_EXPERTS), 1)
    slot = lax.broadcasted_iota(I32, (TOK_BLK, win), 1)
    for sb in range(nsub):
        rows = slice(sb * TOK_BLK, (sb + 1) * TOK_BLK)
        total = None
        for k, y_ref in enumerate(y_refs):
            e = eg * EXPERT_GROUP + k
            first = (b * N_EXPERTS + e) * nblk + jt * nsub
            tile_base = jnp.minimum(base_ref[first], cap - win_tile)
            base = base_ref[first + sb]
            mine = lane == e
            rel = jnp.sum(jnp.where(mine, posm_ref[rows, :], 0), axis=-1, keepdims=True) - base
            val = jnp.sum(jnp.where(mine, aff_ref[rows, :], 0.0), axis=-1, keepdims=True)
            onehot = jnp.where(slot == rel, 1.0, 0.0).astype(BF16)
            off = pl.multiple_of(base - tile_base, SLOT_ALIGN)
            part = val * jnp.dot(onehot, y_ref[pl.ds(off, win), :], preferred_element_type=F32)
            total = part if total is None else total + part
        acc_ref[rows, :] += total

    @pl.when(eg == pl.num_programs(2) - 1)
    def _():
        out = x_ref[...] + g2_ref[...] * acc_ref[...]
        if final:
            ms = jnp.mean(out * out, axis=-1, keepdims=True)
            out = out * lax.rsqrt(ms + EPS) * fw_ref[...]
        o_ref[...] = out


def _ec_moe(token_sets, w1, w3, w2, layer):
    bsz = token_sets[0][0].shape[0]
    plans, xgs = [], []
    for x, gate2, h2, aff, final_w in token_sets:
        n = x.shape[1]
        cap = EC_CAPACITY * n // N_EXPERTS
        nblk = n // TOK_BLK
        win = min(TOK_BLK + SLOT_ALIGN, cap)
        posm, pos = _route(aff, cap)
        start = pos[:, :, ::TOK_BLK]
        aligned = (start // SLOT_ALIGN) * SLOT_ALIGN
        base = jnp.minimum(aligned, cap - win).astype(I32).reshape(-1)
        win_small = min(GATHER_WIN_SMALL, win)
        base_small = jnp.minimum(aligned, cap - win_small).astype(I32).reshape(-1)
        count = jnp.concatenate([start[:, :, 1:], jnp.full_like(start[:, :, :1], cap)], axis=2) - start
        small = jnp.all(count <= win_small - SLOT_ALIGN, axis=1).astype(I32).reshape(-1)
        dq = GATHER_COLS
        xgs.append(pl.pallas_call(
            functools.partial(_gather_kernel, win=win, win_small=win_small, nblk=nblk),
            grid_spec=pltpu.PrefetchScalarGridSpec(
                num_scalar_prefetch=3,
                grid=(bsz, D // dq),
                in_specs=[pl.BlockSpec((None, n, dq), lambda b, c, *_: (b, 0, c)),
                          pl.BlockSpec((None, N_EXPERTS, n), lambda b, c, *_: (b, 0, 0))],
                out_specs=pl.BlockSpec((None, N_EXPERTS, cap, dq), lambda b, c, *_: (b, 0, 0, c))),
            out_shape=jax.ShapeDtypeStruct((bsz, N_EXPERTS, cap, D), BF16),
            compiler_params=_cparams(("parallel", "parallel")),
            name="moe_gather",
        )(base, base_small, small, h2, posm))
        plans.append((n, cap, nblk, win, posm, base))

    xg = xgs[0] if len(xgs) == 1 else jnp.concatenate(xgs, axis=2)
    cap_all = xg.shape[2]
    tf = 256
    y = pl.pallas_call(
        _expert_kernel,
        grid=(N_EXPERTS, bsz, EXPERT_FF // tf),
        in_specs=[pl.BlockSpec((None, None, cap_all, D), lambda e, b, f: (b, e, 0, 0)),
                  pl.BlockSpec((None, None, D, tf), lambda e, b, f: (layer, e, 0, f)),
                  pl.BlockSpec((None, None, D, tf), lambda e, b, f: (layer, e, 0, f)),
                  pl.BlockSpec((None, None, tf, D), lambda e, b, f: (layer, e, f, 0))],
        out_specs=pl.BlockSpec((None, None, cap_all, D), lambda e, b, f: (b, e, 0, 0)),
        out_shape=jax.ShapeDtypeStruct((bsz, N_EXPERTS, cap_all, D), BF16),
        scratch_shapes=[pltpu.VMEM((cap_all, D), F32)],
        compiler_params=_cparams(("parallel", "parallel", "arbitrary")),
        name="moe_expert",
    )(xg, w1, w3, w2)
    y2d = y.reshape(bsz * N_EXPERTS * cap_all, D)

    outs = []
    row0 = 0
    for (x, gate2, h2, aff, final_w), (n, cap, nblk, win, posm, base) in zip(token_sets, plans):
        tile = min(COMBINE_TILE, n)
        nsub = tile // TOK_BLK
        win_tile = min(tile + SLOT_ALIGN, cap)
        final = final_w is not None

        def y_window(b, j, eg, base, k, cap=cap, nblk=nblk, nsub=nsub, win_tile=win_tile, row0=row0):
            be = b * N_EXPERTS + eg * EXPERT_GROUP + k
            start = jnp.minimum(base[be * nblk + j * nsub], cap - win_tile)
            return pl.multiple_of(be * cap_all + row0 + start, SLOT_ALIGN), 0

        in_specs = [pl.BlockSpec((None, tile, D), lambda b, j, e, base: (b, j, 0)),
                    pl.BlockSpec((None, 1, D), lambda b, j, e, base: (b, 0, 0)),
                    pl.BlockSpec((None, tile, N_EXPERTS), lambda b, j, e, base: (b, j, 0)),
                    pl.BlockSpec((None, tile, N_EXPERTS), lambda b, j, e, base: (b, j, 0))]
        in_specs += [pl.BlockSpec((pl.Element(win_tile), pl.Element(D)), functools.partial(y_window, k=k))
                     for k in range(EXPERT_GROUP)]
        args = [base, x, gate2, aff, jnp.swapaxes(posm, 1, 2)] + [y2d] * EXPERT_GROUP
        if final:
            in_specs.append(pl.BlockSpec((1, D), lambda b, j, e, base: (0, 0)))
            args.append(final_w.reshape(1, D))
        outs.append(pl.pallas_call(
            functools.partial(_combine_kernel, win=win, win_tile=win_tile, cap=cap, nblk=nblk, nsub=nsub, final=final),
            grid_spec=pltpu.PrefetchScalarGridSpec(
                num_scalar_prefetch=1,
                grid=(bsz, n // tile, N_EXPERTS // EXPERT_GROUP),
                in_specs=in_specs,
                out_specs=pl.BlockSpec((None, tile, D), lambda b, j, e, base: (b, j, 0)),
                scratch_shapes=[pltpu.VMEM((tile, D), F32)]),
            out_shape=jax.ShapeDtypeStruct((bsz, n, D), F32),
            compiler_params=_cparams(("parallel", "parallel", "arbitrary")),
            name="moe_combine",
        )(*args))
        row0 += cap
    return outs


def _rope_tables(n_tokens):
    rows = n_tokens // GRID_W
    row = jnp.repeat(jnp.arange(rows, dtype=F32), GRID_W)
    col = (jnp.arange(rows * GRID_W, dtype=I32) % GRID_W).astype(F32)
    inv = ROPE_BASE ** (-jnp.arange(ROPE_PAIRS, dtype=F32) / ROPE_PAIRS)
    ang = jnp.concatenate([row[:, None] * inv, col[:, None] * inv], axis=-1)
    cos, sin = jnp.cos(ang), jnp.sin(ang)
    reps = LANES // DA_HEAD_DIM
    cos_t = jnp.tile(jnp.concatenate([cos, cos], axis=-1), (1, reps))
    sin_t = jnp.tile(jnp.concatenate([-sin, sin], axis=-1), (1, reps))
    return cos_t, sin_t


def _lower_bounds(p):
    cum = jnp.cumsum(jax.nn.softmax(p.astype(F32), axis=0), axis=0)
    return cum - cum[0]


def _pad128(v):
    return jnp.pad(v.reshape(1, -1).astype(F32), ((0, 0), (0, LANES - v.size)))


def kernel(x, c, ctx, c_ctx, ada_w, ada_b, norm1_w, norm2_w, w_in, ssd_conv_w, ssd_conv_b, ssd_dt_bias,
           ssd_a_log, ssd_d, ssd_norm_w, hg_lb, hg_norm_w, da_lambda, da_norm_w, w_up, w_gate, b_gate,
           w_out, moe_router, moe_w1, moe_w3, moe_w2, final_norm_w):
    bsz, n_lat, _ = x.shape
    n_ctx = ctx.shape[1]
    depth = ada_w.shape[0]
    rope = _rope_tables(n_lat)
    lb_all = jnp.stack([_lower_bounds(hg_lb[0]), _lower_bounds(hg_lb[1])], axis=1)

    mod = _adaln(jnp.concatenate([c, c_ctx[None]], axis=0), ada_w, ada_b)
    mod = mod.reshape(depth, 8, 6, 1, D)

    xl, xc = x, ctx
    for l in range(depth):
        need_ctx = l < depth - 1
        lam_init = 0.8 - 0.6 * math.exp(-0.3 * l)
        mod_l = [mod[l, :bsz, k] for k in range(6)]
        mod_c = [jnp.broadcast_to(mod[l, bsz:bsz + 1, k], (bsz, 1, D)) for k in range(6)]

        wl = w_in[l]
        w_a = jnp.concatenate([wl[:, OFF_SSD_Z:OFF_SSD_XBC], wl[:, OFF_HG_Q:OFF_HG_F], wl[:, OFF_HG_I:OFF_DA_Q],
                               wl[:, OFF_DA_V:N_IN], wl[:, OFF_SSD_XBC:OFF_SSD_DT]], axis=1).astype(BF16)
        w_f = jnp.concatenate([wl[:, OFF_HG_F:OFF_HG_I], wl[:, OFF_SSD_DT:OFF_HG_Q],
                               jnp.zeros((D, PF_N - PF_DT - 2 * SSD_HEADS), F32)], axis=1).astype(BF16)
        w_qk = wl[:, OFF_DA_Q:OFF_DA_V].astype(BF16)
        alog128 = _pad128(ssd_a_log[l])
        bias128 = _pad128(ssd_dt_bias[l])
        d_full = jnp.repeat(ssd_d[l].astype(F32), SSD_D_INNER // SSD_HEADS).reshape(1, SSD_D_INNER)
        ssd_nw = ssd_norm_w[l].reshape(1, SSD_D_INNER).astype(F32)
        hg_nw = hg_norm_w[l].reshape(1, HG_HEAD_DIM).astype(F32)
        wg_b = w_gate[l].astype(BF16)
        bg = b_gate[l].reshape(1, N_BRANCH * D)
        wu_b = w_up[l].astype(BF16)
        wo_b = w_out[l].astype(BF16)

        def project(xs, m, use_rope):
            n = xs.shape[1]
            h = _modulate(xs, norm1_w[l], m[0], m[1])
            h2d = h.reshape(bsz * n, D)
            tm = min(1024, n)
            pa = _matmul(h2d, w_a, BF16, tm, PA_N // 4).reshape(bsz, n, PA_N)
            pf = _matmul(h2d, w_f, F32, tm, PF_N).reshape(bsz, n, PF_N)
            pqk = _matmul(h2d, w_qk, BF16, tm, 1024, rope_tables=rope if use_rope else None).reshape(bsz, n, PQK_N)
            return h, pa, pf, pqk

        hl, pa_l, pf_l, pqk_l = project(xl, mod_l, True)
        hc, pa_c, pf_c, pqk_c = project(xc, mod_c, False)

        xa_c = _ssd_conv(pa_c, ssd_conv_w[l], ssd_conv_b[l])
        xa_l = _ssd_conv(pa_l, ssd_conv_w[l], ssd_conv_b[l])
        yc_f, s_f = _ssd_scan(xa_c, pf_c, alog128, bias128, rev=False)
        yl_f, _ = _ssd_scan(xa_l, pf_l, alog128, bias128, rev=False, init=s_f)
        ya_c, s_b = _ssd_scan(xa_c, pf_c, alog128, bias128, rev=True, readout=(pa_c, yc_f, d_full, ssd_nw))
        ya_l, _ = _ssd_scan(xa_l, pf_l, alog128, bias128, rev=True, init=s_b, readout=(pa_l, yl_f, d_full, ssd_nw))

        lb_f, lb_b = lb_all[l, 0].reshape(1, HG_WIDTH), lb_all[l, 1].reshape(1, HG_WIDTH)
        oc_f, t_f = _hg_scan(pa_c, pf_c, lb_f, rev=False)
        ol_f, _ = _hg_scan(pa_l, pf_l, lb_f, rev=False, init=t_f)
        yb_c, t_b = _hg_scan(pa_c, pf_c, lb_b, rev=True, readout=(oc_f, hg_nw))
        yb_l, _ = _hg_scan(pa_l, pf_l, lb_b, rev=True, init=t_b, readout=(ol_f, hg_nw))

        k_all = jnp.concatenate([pqk_c[..., DA_WIDTH:], pqk_l[..., DA_WIDTH:]], axis=1)
        v_all = jnp.concatenate([pa_c[..., PA_DV:PA_DV + DA_WIDTH], pa_l[..., PA_DV:PA_DV + DA_WIDTH]], axis=1)
        yd_l = _diff_attn(pqk_l, [(k_all, 0, v_all, 0)], da_lambda[l], da_norm_w[l], lam_init, 512, ATTN_TK)

        xl = _merge(xl, mod_l[2], hl, ya_l, yb_l, yd_l, wg_b, bg, wu_b, wo_b)
        h2l, aff_l = _modulate(xl, norm2_w[l], mod_l[3], mod_l[4], w_router=moe_router[l])
        token_sets = [(xl, mod_l[5], h2l, aff_l, final_norm_w if l == depth - 1 else None)]
        if need_ctx:
            yd_c = _diff_attn(pqk_c, [(pqk_c, DA_WIDTH, pa_c, PA_DV)], da_lambda[l], da_norm_w[l], lam_init,
                              n_ctx, ATTN_TK)
            xc = _merge(xc, mod_c[2], hc, ya_c, yb_c, yd_c, wg_b, bg, wu_b, wo_b)
            h2c, aff_c = _modulate(xc, norm2_w[l], mod_c[3], mod_c[4], w_router=moe_router[l])
            token_sets.append((xc, mod_c[5], h2c, aff_c, None))
        outs = _ec_moe(token_sets, moe_w1, moe_w3, moe_w2, l)
        xl = outs[0]
        if need_ctx:
            xc = outs[1]
    return xl
```

```python
import functools
import math

import jax
import jax.numpy as jnp
from jax import lax
from jax.experimental import pallas as pl
from jax.experimental.pallas import tpu as pltpu

F32 = jnp.float32
BF16 = jnp.bfloat16
I32 = jnp.int32
HIGHEST = lax.Precision.HIGHEST
LOG2E = 1.4426950408889634

D = 2048
EPS = 1e-6
GRID_W = 64
SSD_D_INNER = 1024
SSD_HEADS = 16
SSD_GROUPS = 2
SSD_STATE = 128
SSD_XBC = 1536
SSD_CONV_W = 5
HG_WIDTH = 1024
HG_HEADS = 8
HG_HEAD_DIM = 128
DA_HEADS = 8
DA_HEAD_DIM = 64
DA_WIDTH = 1024
ROPE_BASE = 10000.0
ROPE_PAIRS = 16
N_BRANCH = 3
N_EXPERTS = 16
EXPERT_FF = 2048
EC_CAPACITY = 2

OFF_SSD_Z = 0
OFF_SSD_XBC = 1024
OFF_SSD_DT = 2560
OFF_HG_Q = 2592
OFF_HG_F = OFF_HG_Q + 1024
OFF_HG_I = OFF_HG_F + 2048
OFF_HG_G = OFF_HG_I + 1024
OFF_DA_Q = OFF_HG_G + 1024
OFF_DA_K = OFF_DA_Q + 1024
OFF_DA_V = OFF_DA_K + 1024
N_IN = OFF_DA_V + 1024

LANES = 128
PA_Z, PA_HQ, PA_HI, PA_HG, PA_DV, PA_XBC = 0, 1024, 2048, 3072, 4096, 5120
PA_N = 6656
PF_F, PF_DT = 0, 2048
PF_N = 2176
PQK_N = 2048

SCAN_Q = 256
ATTN_TK = 1408
HALO = 16
TOK_BLK = 128
COMBINE_TILE = 256
GATHER_COLS = 512
EXPERT_GROUP = 16
GATHER_WIN_SMALL = 48
SLOT_ALIGN = 16
VMEM_LIMIT = 56 * 1024 * 1024


def _cparams(sem, vmem=VMEM_LIMIT):
    return pltpu.CompilerParams(dimension_semantics=sem, vmem_limit_bytes=vmem)


def _silu(x):
    return x * jax.nn.sigmoid(x)


def _adaln_kernel(cb_ref, w_ref, b_ref, o_ref, *, n_rows, tn):
    nct = tn // LANES

    def body(k, accs):
        r0 = pl.multiple_of(k * 8, 8)
        ws = [w_ref[pl.ds(r0, 8), c * LANES:(c + 1) * LANES] for c in range(nct)]
        out = []
        for r in range(n_rows):
            s = _silu(cb_ref[r, pl.ds(r0, 8), :])
            out.append(tuple(accs[r][c] + s * ws[c] for c in range(nct)))
        return tuple(out)

    zero = jnp.zeros((8, LANES), F32)
    accs = lax.fori_loop(0, D // 8, body, tuple(tuple(zero for _ in range(nct)) for _ in range(n_rows)),
                         unroll=8)
    rows = []
    for r in range(n_rows):
        rows.append(jnp.concatenate([jnp.sum(accs[r][c], axis=0, keepdims=True) for c in range(nct)], axis=1))
    rows.append(jnp.zeros((8 - n_rows, tn), F32))
    o_ref[...] = jnp.concatenate(rows, axis=0) + b_ref[...]


def _adaln(cvec, ada_w, ada_b):
    n_rows = cvec.shape[0]
    depth, _, n6 = ada_w.shape
    tn = 512
    cb = jnp.broadcast_to(cvec[:, :, None], (n_rows, D, LANES))
    return pl.pallas_call(
        functools.partial(_adaln_kernel, n_rows=n_rows, tn=tn),
        grid=(depth, n6 // tn),
        in_specs=[pl.BlockSpec((n_rows, D, LANES), lambda l, j: (0, 0, 0)),
                  pl.BlockSpec((None, D, tn), lambda l, j: (l, 0, j)),
                  pl.BlockSpec((None, 1, tn), lambda l, j: (l, 0, j))],
        out_specs=pl.BlockSpec((None, 8, tn), lambda l, j: (l, 0, j)),
        out_shape=jax.ShapeDtypeStruct((depth, 8, n6), F32),
        compiler_params=_cparams(("parallel", "parallel")),
        name="adaln",
    )(cb, ada_w, ada_b.reshape(depth, 1, n6))


def _modulate_kernel(x_ref, nw_ref, sh_ref, sc_ref, *rest, with_router):
    x = x_ref[...]
    ms = jnp.mean(x * x, axis=-1, keepdims=True)
    h = (x * lax.rsqrt(ms + EPS) * nw_ref[...]) * (1.0 + sc_ref[...]) + sh_ref[...]
    if with_router:
        wr_ref, h_ref, aff_ref = rest
        logits = jnp.dot(h, wr_ref[...], preferred_element_type=F32, precision=HIGHEST)
        m = jnp.max(logits, axis=-1, keepdims=True)
        e = jnp.exp(logits - m)
        aff_ref[...] = e / jnp.sum(e, axis=-1, keepdims=True)
    else:
        (h_ref,) = rest
    h_ref[...] = h.astype(BF16)


def _modulate(x, nw, shift, scale, w_router=None):
    bsz, n, _ = x.shape
    tm = min(512, n)
    with_router = w_router is not None
    in_specs = [pl.BlockSpec((None, tm, D), lambda b, i: (b, i, 0)),
                pl.BlockSpec((1, D), lambda b, i: (0, 0)),
                pl.BlockSpec((None, 1, D), lambda b, i: (b, 0, 0)),
                pl.BlockSpec((None, 1, D), lambda b, i: (b, 0, 0))]
    args = [x, nw.reshape(1, D), shift, scale]
    out_specs = [pl.BlockSpec((None, tm, D), lambda b, i: (b, i, 0))]
    out_shape = [jax.ShapeDtypeStruct((bsz, n, D), BF16)]
    if with_router:
        in_specs.append(pl.BlockSpec((D, N_EXPERTS), lambda b, i: (0, 0)))
        args.append(w_router)
        out_specs.append(pl.BlockSpec((None, tm, N_EXPERTS), lambda b, i: (b, i, 0)))
        out_shape.append(jax.ShapeDtypeStruct((bsz, n, N_EXPERTS), F32))
    res = pl.pallas_call(
        functools.partial(_modulate_kernel, with_router=with_router),
        grid=(bsz, n // tm),
        in_specs=in_specs, out_specs=out_specs, out_shape=out_shape,
        compiler_params=_cparams(("parallel", "parallel")),
        name="modulate_router" if with_router else "modulate",
    )(*args)
    return res if with_router else res[0]


def _mm_kernel(a_ref, w_ref, *rest, rope, tn):
    acc = jnp.dot(a_ref[...], w_ref[...], preferred_element_type=F32)
    if rope:
        cos_ref, sin_ref, o_ref = rest
        cos = cos_ref[...]
        sin = sin_ref[...]
        lane = lax.broadcasted_iota(I32, cos.shape, 1)
        first_half = (lane % DA_HEAD_DIM) < (DA_HEAD_DIM // 2)
        for c in range(tn // LANES):
            xs = acc[:, c * LANES:(c + 1) * LANES]
            partner = jnp.where(first_half,
                                pltpu.roll(xs, LANES - DA_HEAD_DIM // 2, 1),
                                pltpu.roll(xs, DA_HEAD_DIM // 2, 1))
            o_ref[:, c * LANES:(c + 1) * LANES] = (xs * cos + partner * sin).astype(o_ref.dtype)
    else:
        (o_ref,) = rest
        o_ref[...] = acc.astype(o_ref.dtype)


def _matmul(a, w, out_dtype, tm, tn, rope_tables=None):
    m, k = a.shape
    n = w.shape[1]
    rope = rope_tables is not None
    in_specs = [pl.BlockSpec((tm, k), lambda i, j: (i, 0)),
                pl.BlockSpec((k, tn), lambda i, j: (0, j))]
    args = [a, w]
    if rope:
        cos, sin = rope_tables
        nt = cos.shape[0] // tm
        in_specs += [pl.BlockSpec((tm, LANES), lambda i, j: (i % nt, 0)),
                     pl.BlockSpec((tm, LANES), lambda i, j: (i % nt, 0))]
        args += [cos, sin]
    return pl.pallas_call(
        functools.partial(_mm_kernel, rope=rope, tn=tn),
        grid=(m // tm, n // tn),
        in_specs=in_specs,
        out_specs=pl.BlockSpec((tm, tn), lambda i, j: (i, j)),
        out_shape=jax.ShapeDtypeStruct((m, n), out_dtype),
        compiler_params=_cparams(("parallel", "parallel")),
        name="proj_rope" if rope else "proj",
    )(*args)


def _conv_kernel(prev_ref, cur_ref, next_ref, w_ref, b_ref, o_ref, *, tc):
    i = pl.program_id(1)
    last = pl.num_programs(1) - 1
    pad = (SSD_CONV_W - 1) // 2
    prev = jnp.where(i > 0, prev_ref[...].astype(F32), 0.0)
    nxt = jnp.where(i < last, next_ref[...].astype(F32), 0.0)
    ext = jnp.concatenate([prev, cur_ref[...].astype(F32), nxt], axis=0)
    w = w_ref[...]
    y = b_ref[...] + w[0:1] * ext[HALO - pad:HALO - pad + tc]
    for k in range(1, SSD_CONV_W):
        y = y + w[k:k + 1] * ext[HALO - pad + k:HALO - pad + k + tc]
    o_ref[...] = _silu(y).astype(o_ref.dtype)


def _ssd_conv(pa, conv_w, conv_b):
    bsz, n, _ = pa.shape
    tc = min(512, n)
    tw = 512
    c0 = PA_XBC // tw
    nbh = n // HALO
    return pl.pallas_call(
        functools.partial(_conv_kernel, tc=tc),
        grid=(bsz, n // tc, SSD_XBC // tw),
        in_specs=[pl.BlockSpec((None, HALO, tw), lambda b, i, j: (b, jnp.maximum(i * (tc // HALO) - 1, 0), c0 + j)),
                  pl.BlockSpec((None, tc, tw), lambda b, i, j: (b, i, c0 + j)),
                  pl.BlockSpec((None, HALO, tw),
                               lambda b, i, j: (b, jnp.minimum((i + 1) * (tc // HALO), nbh - 1), c0 + j)),
                  pl.BlockSpec((SSD_CONV_W, tw), lambda b, i, j: (0, j)),
                  pl.BlockSpec((1, tw), lambda b, i, j: (0, j))],
        out_specs=pl.BlockSpec((None, tc, tw), lambda b, i, j: (b, i, j)),
        out_shape=jax.ShapeDtypeStruct((bsz, n, SSD_XBC), BF16),
        compiler_params=_cparams(("parallel", "parallel", "parallel")),
        name="ssd_conv",
    )(pa, pa, pa, conv_w, conv_b.reshape(1, SSD_XBC))


def _tri(q, rev):
    r = lax.broadcasted_iota(I32, (q, q), 0)
    c = lax.broadcasted_iota(I32, (q, q), 1)
    return (c >= r) if rev else (r >= c)


def _split3(x):
    hi = x.astype(BF16)
    r = x - hi.astype(F32)
    mid = r.astype(BF16)
    lo = (r - mid.astype(F32)).astype(BF16)
    return hi, mid, lo


def _select_rows_dot(sel01, x):
    hi, mid, lo = _split3(x)
    return (jnp.dot(sel01, hi, preferred_element_type=F32) + jnp.dot(sel01, mid, preferred_element_type=F32)
            + jnp.dot(sel01, lo, preferred_element_type=F32))


def _select_cols_dot(x, sel01):
    hi, mid, lo = _split3(x)
    return (jnp.dot(hi, sel01, preferred_element_type=F32) + jnp.dot(mid, sel01, preferred_element_type=F32)
            + jnp.dot(lo, sel01, preferred_element_type=F32))


def _ssd_scan_kernel(*refs, rev, has_init, readout):
    it = iter(refs)
    xact_ref, dt_ref, alog_ref, bias_ref = next(it), next(it), next(it), next(it)
    init_ref = next(it) if has_init else None
    if readout:
        z_ref, yprev_ref, d_ref, nw_ref = next(it), next(it), next(it), next(it)
    y_ref, st_ref = next(it), next(it)
    ybuf_ref = next(it) if readout else None

    q = SCAN_Q
    hd = SSD_D_INNER // SSD_HEADS
    hg = SSD_HEADS // SSD_GROUPS
    gw = hg * hd
    d_off = SSD_HEADS if rev else 0

    @pl.when(pl.program_id(1) == 0)
    def _():
        st_ref[...] = init_ref[...] if has_init else jnp.zeros_like(st_ref)

    tri = _tri(q, rev)
    dtv = jax.nn.softplus(dt_ref[...] + bias_ref[...])
    da = dtv * (-LOG2E * jnp.exp(alog_ref[...]))
    cs = _select_rows_dot(jnp.where(tri, 1.0, 0.0).astype(BF16), da)
    cs_t = cs.T
    erow = lax.broadcasted_iota(I32, (LANES, SSD_D_INNER), 0)
    ecol = lax.broadcasted_iota(I32, (LANES, SSD_D_INNER), 1)
    expand = jnp.where(erow == d_off + ecol // hd, 1.0, 0.0).astype(BF16)
    csx = _select_cols_dot(cs, expand)
    dtx = _select_cols_dot(dtv, expand)
    x = xact_ref[:, :SSD_D_INNER].astype(F32)
    xdt = x * dtx
    xdt_b = xdt.astype(BF16)
    cs_end = csx[0:1] if rev else csx[q - 1:q]
    ecs = jnp.exp2(csx)
    xw = (xdt * jnp.exp2(cs_end - csx)).astype(BF16)
    ecs_end = jnp.exp2(cs_end)
    lane = lax.broadcasted_iota(I32, (q, LANES), 1)
    out_ref = ybuf_ref if readout else y_ref

    for g in range(SSD_GROUPS):
        bg = xact_ref[:, SSD_D_INNER + g * SSD_STATE:SSD_D_INNER + (g + 1) * SSD_STATE]
        cg = xact_ref[:, SSD_D_INNER + (SSD_GROUPS + g) * SSD_STATE:SSD_D_INNER + (SSD_GROUPS + g + 1) * SSD_STATE]
        cb = lax.dot_general(cg, bg, (((1,), (1,)), ((), ())), preferred_element_type=F32)
        st_g = st_ref[:, g * gw:(g + 1) * gw]
        y_inter = jnp.dot(cg, st_g.astype(BF16), preferred_element_type=F32) * ecs[:, g * gw:(g + 1) * gw]
        for jj in range(hg // 2):
            col0 = g * gw + jj * LANES
            xp = xdt_b[:, col0:col0 + LANES]
            acc = y_inter[:, jj * LANES:(jj + 1) * LANES]
            for s in range(2):
                j = d_off + g * hg + 2 * jj + s
                e = cs[:, j:j + 1] - cs_t[j:j + 1, :]
                seg = jnp.exp2(jnp.where(tri, e, -jnp.inf))
                m = (cb * seg).astype(BF16)
                xm = jnp.where((lane >= hd) if s else (lane < hd), xp, jnp.zeros_like(xp))
                acc = acc + jnp.dot(m, xm, preferred_element_type=F32)
            out_ref[:, col0:col0 + LANES] = acc
        upd = lax.dot_general(bg, xw[:, g * gw:(g + 1) * gw], (((0,), (0,)), ((), ())),
                              preferred_element_type=F32)
        st_ref[:, g * gw:(g + 1) * gw] = st_g * ecs_end[:, g * gw:(g + 1) * gw] + upd

    if readout:
        z = z_ref[...].astype(F32)
        yy = (yprev_ref[...] + ybuf_ref[...] + d_ref[...] * x) * _silu(z)
        for g in range(SSD_GROUPS):
            seg = yy[:, g * gw:(g + 1) * gw]
            ms = jnp.mean(seg * seg, axis=-1, keepdims=True)
            y_ref[:, g * gw:(g + 1) * gw] = (seg * lax.rsqrt(ms + EPS) * nw_ref[:, g * gw:(g + 1) * gw]).astype(y_ref.dtype)


def _ssd_scan(xact, pf, alog128, bias128, *, rev, init=None, readout=None):
    bsz, n, _ = xact.shape
    nc = n // SCAN_Q
    cidx = (lambda c: nc - 1 - c) if rev else (lambda c: c)
    in_specs = [pl.BlockSpec((None, SCAN_Q, SSD_XBC), lambda b, c: (b, cidx(c), 0)),
                pl.BlockSpec((None, SCAN_Q, LANES), lambda b, c: (b, cidx(c), PF_DT // LANES)),
                pl.BlockSpec((1, LANES), lambda b, c: (0, 0)),
                pl.BlockSpec((1, LANES), lambda b, c: (0, 0))]
    args = [xact, pf, alog128, bias128]
    if init is not None:
        in_specs.append(pl.BlockSpec((None, SSD_STATE, SSD_D_INNER), lambda b, c: (b, 0, 0)))
        args.append(init)
    scratch = []
    if readout is not None:
        pa, yprev, d_full, norm_w = readout
        in_specs += [pl.BlockSpec((None, SCAN_Q, SSD_D_INNER), lambda b, c: (b, cidx(c), PA_Z // SSD_D_INNER)),
                     pl.BlockSpec((None, SCAN_Q, SSD_D_INNER), lambda b, c: (b, cidx(c), 0)),
                     pl.BlockSpec((1, SSD_D_INNER), lambda b, c: (0, 0)),
                     pl.BlockSpec((1, SSD_D_INNER), lambda b, c: (0, 0))]
        args += [pa, yprev, d_full, norm_w]
        scratch = [pltpu.VMEM((SCAN_Q, SSD_D_INNER), F32)]
    y_dtype = BF16 if readout is not None else F32
    y, st = pl.pallas_call(
        functools.partial(_ssd_scan_kernel, rev=rev, has_init=init is not None, readout=readout is not None),
        grid=(bsz, nc),
        in_specs=in_specs,
        out_specs=[pl.BlockSpec((None, SCAN_Q, SSD_D_INNER), lambda b, c: (b, cidx(c), 0)),
                   pl.BlockSpec((None, SSD_STATE, SSD_D_INNER), lambda b, c: (b, 0, 0))],
        out_shape=[jax.ShapeDtypeStruct((bsz, n, SSD_D_INNER), y_dtype),
                   jax.ShapeDtypeStruct((bsz, SSD_STATE, SSD_D_INNER), F32)],
        scratch_shapes=scratch,
        compiler_params=_cparams(("parallel", "arbitrary")),
        name="ssd_scan_bwd" if rev else "ssd_scan_fwd",
    )(*args)
    return y, st


def _ref_rows(bb, s, rev):
    q, w = bb.shape
    rl = s if rev else s - 1
    if 2 * s >= 8:
        b3 = bb.reshape(q // (2 * s), 2 * s, w)
        return jnp.broadcast_to(b3[:, rl:rl + 1, :], b3.shape).reshape(q, w)
    off = lax.broadcasted_iota(I32, (q, 1), 0) % (2 * s)
    out = bb
    for ov in range(2 * s):
        if ov != rl:
            out = jnp.where(off == ov, pltpu.roll(bb, (ov - rl) % q, 0), out)
    return out


def _level_operand(bb, qq, kk, s, rev):
    q = bb.shape[0]
    if s < 8:
        upper = (lax.broadcasted_iota(I32, (q, 1), 0) & s) != 0
        q_side = jnp.logical_not(upper) if rev else upper
        ex = jnp.exp2(-jnp.abs(bb - _ref_rows(bb, s, rev)))
        return (jnp.where(q_side, qq, kk) * ex).astype(BF16)
    pieces = []
    for a in range(0, q, 2 * s):
        lo, hi = slice(a, a + s), slice(a + s, a + 2 * s)
        if rev:
            r = bb[a + s:a + s + 1]
            pieces += [qq[lo] * jnp.exp2(bb[lo] - r), kk[hi] * jnp.exp2(r - bb[hi])]
        else:
            r = bb[a + s - 1:a + s]
            pieces += [kk[lo] * jnp.exp2(r - bb[lo]), qq[hi] * jnp.exp2(bb[hi] - r)]
    return jnp.concatenate(pieces, axis=0).astype(BF16)


def _hg_scan_kernel(*refs, rev, has_init, readout):
    it = iter(refs)
    q_ref, f_ref, v_ref, lb_ref = next(it), next(it), next(it), next(it)
    init_ref = next(it) if has_init else None
    if readout:
        g_ref, oprev_ref, nw_ref = next(it), next(it), next(it)
    o_ref, st_ref, att_ref = next(it), next(it), next(it)
    obuf_ref = next(it) if readout else None

    q = SCAN_Q
    hd = HG_HEAD_DIM

    @pl.when(pl.program_id(1) == 0)
    def _():
        st_ref[...] = init_ref[...] if has_init else jnp.zeros_like(st_ref)

    lb = lb_ref[...]
    qq = _silu(q_ref[...].astype(F32))
    f = lb + (1.0 - lb) * jax.nn.sigmoid(f_ref[...])
    kk = 1.0 - f
    lf = jnp.log(f) * LOG2E
    tri = _tri(q, rev)
    bb = jnp.dot(jnp.where(tri, 1.0, 0.0), lf, preferred_element_type=F32, precision=HIGHEST)

    half = q // 2
    ri = lax.broadcasted_iota(I32, (half, half), 0)
    ci = lax.broadcasted_iota(I32, (half, half), 1)
    xr = ri ^ ci
    causal = (ci > ri) if rev else (ri > ci)
    nt = (((1,), (1,)), ((), ()))
    s = half
    while s >= 1:
        u = _level_operand(bb, qq, kk, s, rev)
        if s < half:
            level = jnp.logical_and((xr >> int(math.log2(s))) == 1, causal)
        for h in range(HG_HEADS):
            u_lo, u_hi = u[:half, h * hd:(h + 1) * hd], u[half:, h * hd:(h + 1) * hd]
            if s == half:
                qs_, ks_ = (u_lo, u_hi) if rev else (u_hi, u_lo)
                att_ref[h, 2] = lax.dot_general(qs_, ks_, nt, preferred_element_type=F32)
            else:
                for blk, ub in enumerate((u_lo, u_hi)):
                    p = lax.dot_general(ub, ub, nt, preferred_element_type=F32)
                    old = att_ref[h, blk] if 2 * s < half else 0.0
                    att_ref[h, blk] = jnp.where(level, p, old)
        s //= 2

    b_end = bb[0:1] if rev else bb[q - 1:q]
    qe = (qq * jnp.exp2(bb)).astype(BF16)
    kh = (kk * jnp.exp2(b_end - bb)).astype(BF16)
    qk = (qq * kk).astype(BF16)
    e_end = jnp.exp2(b_end)
    ones = jnp.ones((hd, hd), BF16)
    out_ref = obuf_ref if readout else o_ref
    for h in range(HG_HEADS):
        sl = slice(h * hd, (h + 1) * hd)
        vh = v_ref[:, sl]
        st_h = st_ref[sl, :]
        v_lo, v_hi = vh[:half], vh[half:]
        d0, d1, off = (att_ref[h, k].astype(BF16) for k in range(3))
        o_lo = jnp.dot(d0, v_lo, preferred_element_type=F32)
        o_hi = jnp.dot(d1, v_hi, preferred_element_type=F32)
        if rev:
            o_lo = o_lo + jnp.dot(off, v_hi, preferred_element_type=F32)
        else:
            o_hi = o_hi + jnp.dot(off, v_lo, preferred_element_type=F32)
        o = jnp.concatenate([o_lo, o_hi], axis=0)
        o = o + jnp.dot(qk[:, sl], ones, preferred_element_type=F32) * vh.astype(F32)
        o = o + lax.dot_general(qe[:, sl], st_h.astype(BF16), (((1,), (1,)), ((), ())), preferred_element_type=F32)
        out_ref[:, sl] = o
        upd = lax.dot_general(vh, kh[:, sl], (((0,), (0,)), ((), ())), preferred_element_type=F32)
        st_ref[sl, :] = st_h * e_end[:, sl] + upd

    if readout:
        for h in range(HG_HEADS):
            sl = slice(h * hd, (h + 1) * hd)
            o = oprev_ref[:, sl] + obuf_ref[:, sl]
            ms = jnp.mean(o * o, axis=-1, keepdims=True)
            gate = _silu(g_ref[:, sl].astype(F32))
            o_ref[:, sl] = (o * lax.rsqrt(ms + EPS) * nw_ref[...] * gate).astype(o_ref.dtype)


def _hg_scan(pa, pf, lb_row, *, rev, init=None, readout=None):
    bsz, n, _ = pa.shape
    nc = n // SCAN_Q
    cidx = (lambda c: nc - 1 - c) if rev else (lambda c: c)
    fcol = (PF_F // HG_WIDTH) + (1 if rev else 0)
    in_specs = [pl.BlockSpec((None, SCAN_Q, HG_WIDTH), lambda b, c: (b, cidx(c), PA_HQ // HG_WIDTH)),
                pl.BlockSpec((None, SCAN_Q, HG_WIDTH), lambda b, c: (b, cidx(c), fcol)),
                pl.BlockSpec((None, SCAN_Q, HG_WIDTH), lambda b, c: (b, cidx(c), PA_HI // HG_WIDTH)),
                pl.BlockSpec((1, HG_WIDTH), lambda b, c: (0, 0))]
    args = [pa, pf, pa, lb_row]
    if init is not None:
        in_specs.append(pl.BlockSpec((None, HG_WIDTH, HG_HEAD_DIM), lambda b, c: (b, 0, 0)))
        args.append(init)
    scratch = [pltpu.VMEM((HG_HEADS, 3, SCAN_Q // 2, SCAN_Q // 2), F32)]
    if readout is not None:
        oprev, nw128 = readout
        in_specs += [pl.BlockSpec((None, SCAN_Q, HG_WIDTH), lambda b, c: (b, cidx(c), PA_HG // HG_WIDTH)),
                     pl.BlockSpec((None, SCAN_Q, HG_WIDTH), lambda b, c: (b, cidx(c), 0)),
                     pl.BlockSpec((1, HG_HEAD_DIM), lambda b, c: (0, 0))]
        args += [pa, oprev, nw128]
        scratch.append(pltpu.VMEM((SCAN_Q, HG_WIDTH), F32))
    o_dtype = BF16 if readout is not None else F32
    o, st = pl.pallas_call(
        functools.partial(_hg_scan_kernel, rev=rev, has_init=init is not None, readout=readout is not None),
        grid=(bsz, nc),
        in_specs=in_specs,
        out_specs=[pl.BlockSpec((None, SCAN_Q, HG_WIDTH), lambda b, c: (b, cidx(c), 0)),
                   pl.BlockSpec((None, HG_WIDTH, HG_HEAD_DIM), lambda b, c: (b, 0, 0))],
        out_shape=[jax.ShapeDtypeStruct((bsz, n, HG_WIDTH), o_dtype),
                   jax.ShapeDtypeStruct((bsz, HG_WIDTH, HG_HEAD_DIM), F32)],
        scratch_shapes=scratch,
        compiler_params=_cparams(("parallel", "arbitrary")),
        name="hg_scan_bwd" if rev else "hg_scan_fwd",
    )(*args)
    return o, st


def _attn_kernel(*refs, seg_lens, tk, lam_init):
    lam_ref, nw_ref, q_ref = refs[0], refs[1], refs[2]
    nseg = len(seg_lens)
    kv_refs = refs[3:3 + 2 * nseg]
    o_ref, qs_ref, sa_ref, sb_ref, m_ref, mn_ref, acc_ref, kall_ref, vaug_ref = refs[3 + 2 * nseg:]
    tq = q_ref.shape[0]

    @pl.when(pl.program_id(2) == 0)
    def _():
        r0 = 0
        for si, n in enumerate(seg_lens):
            kall_ref[r0:r0 + n, :] = kv_refs[2 * si][...]
            vaug_ref[r0:r0 + n, :2 * DA_HEAD_DIM] = kv_refs[2 * si + 1][...]
            r0 += n
        vaug_ref[:, 2 * DA_HEAD_DIM:] = jnp.ones((r0, 2 * DA_HEAD_DIM), BF16)

    qv = q_ref[...] * jnp.asarray(DA_HEAD_DIM ** -0.5, BF16)
    lane = lax.broadcasted_iota(I32, qv.shape, 1)
    zero = jnp.zeros_like(qv)
    qs_ref[:tq] = jnp.where(lane < DA_HEAD_DIM, qv, zero)
    qs_ref[tq:] = jnp.where(lane >= DA_HEAD_DIM, qv, zero)
    m_ref[...] = jnp.full_like(m_ref, -jnp.inf)
    acc_ref[...] = jnp.zeros_like(acc_ref)

    def scores(kc):
        return lax.dot_general(qs_ref[...], kc, (((1,), (1,)), ((), ())), preferred_element_type=F32)

    def lane_tiles(a):
        return [a[:, c * LANES:(c + 1) * LANES] for c in range(a.shape[1] // LANES)]

    def row_max(s, m_prev):
        tiles = lane_tiles(s)
        m = tiles[0]
        for tile in tiles[1:]:
            m = jnp.maximum(m, tile)
        return jnp.maximum(m_prev, jnp.broadcast_to(jnp.max(m, axis=-1, keepdims=True), m_prev.shape))

    def online_step(s, vc, m_prev, m_cur):
        alpha = jnp.exp(m_prev - m_cur)
        ps = [jnp.exp(tile - m_cur) for tile in lane_tiles(s)]
        pv = jnp.dot(jnp.concatenate(ps, axis=1).astype(BF16), vc, preferred_element_type=F32)
        acc_ref[...] = jnp.concatenate([alpha, alpha], axis=1) * acc_ref[...] + pv

    for n in (sum(seg_lens),):
        k_ref, v_ref = kall_ref, vaug_ref
        t = min(tk, n)
        nc = n // t
        if nc == 1:
            s = scores(k_ref[...])
            m_prev = m_ref[...]
            m_cur = row_max(s, m_prev)
            online_step(s, v_ref[...], m_prev, m_cur)
            m_ref[...] = m_cur
        else:
            s0 = scores(k_ref[pl.ds(0, t), :])
            sa_ref[...] = s0
            mn_ref[...] = row_max(s0, m_ref[...])

            def pipe_step(c, cur_ref, nxt_ref, k_ref=k_ref, v_ref=v_ref, t=t):
                r0 = pl.multiple_of(c * t, t)
                r1 = pl.multiple_of(r0 + t, t)
                m_prev, m_cur = m_ref[...], mn_ref[...]
                s_next = scores(k_ref[pl.ds(r1, t), :])
                nxt_ref[...] = s_next
                mn_ref[...] = row_max(s_next, m_cur)
                m_ref[...] = m_cur
                online_step(cur_ref[...], v_ref[pl.ds(r0, t), :], m_prev, m_cur)

            def body(i, carry):
                pipe_step(2 * i, sa_ref, sb_ref)
                pipe_step(2 * i + 1, sb_ref, sa_ref)
                return carry

            lax.fori_loop(0, (nc - 1) // 2, body, 0)
            last_ref = sa_ref
            if nc % 2 == 0:
                pipe_step(nc - 2, sa_ref, sb_ref)
                last_ref = sb_ref
            m_prev, m_cur = m_ref[...], mn_ref[...]
            online_step(last_ref[...], v_ref[pl.ds((nc - 1) * t, t), :], m_prev, m_cur)
            m_ref[...] = m_cur
    hw = 2 * DA_HEAD_DIM
    acc = acc_ref[:, :hw]
    l = acc_ref[:, hw:]
    lam = lam_ref[...]
    lmbda = (jnp.exp(jnp.sum(lam[0:1] * lam[1:2], axis=-1, keepdims=True))
             - jnp.exp(jnp.sum(lam[2:3] * lam[3:4], axis=-1, keepdims=True)) + lam_init)
    o = acc[:tq] / l[:tq] - lmbda * (acc[tq:] / l[tq:])
    ms = jnp.mean(o * o, axis=-1, keepdims=True)
    o_ref[...] = (o * lax.rsqrt(ms + EPS) * nw_ref[...] * (1.0 - lam_init)).astype(o_ref.dtype)


def _diff_attn(pqk_q, kv_segs, lam, nw, lam_init, tq, tk):
    bsz, nq, _ = pqk_q.shape
    hw = 2 * DA_HEAD_DIM
    in_specs = [pl.BlockSpec((4, DA_HEAD_DIM), lambda b, h, i: (0, 0)),
                pl.BlockSpec((1, hw), lambda b, h, i: (0, 0)),
                pl.BlockSpec((None, tq, hw), lambda b, h, i: (b, i, h))]
    args = [lam, nw.reshape(1, hw), pqk_q]
    seg_lens = []
    for k_arr, k_col, v_arr, v_col in kv_segs:
        nk = k_arr.shape[1]
        seg_lens.append(nk)
        in_specs += [pl.BlockSpec((None, nk, hw), functools.partial(lambda b, h, i, c: (b, 0, c + h), c=k_col // hw)),
                     pl.BlockSpec((None, nk, hw), functools.partial(lambda b, h, i, c: (b, 0, c + h), c=v_col // hw))]
        args += [k_arr, v_arr]
    n_keys = sum(seg_lens)
    assert n_keys % min(tk, n_keys) == 0
    return pl.pallas_call(
        functools.partial(_attn_kernel, seg_lens=tuple(seg_lens), tk=tk, lam_init=lam_init),
        grid=(bsz, DA_HEADS, nq // tq),
        in_specs=in_specs,
        out_specs=pl.BlockSpec((None, tq, hw), lambda b, h, i: (b, i, h)),
        out_shape=jax.ShapeDtypeStruct((bsz, nq, DA_WIDTH), BF16),
        scratch_shapes=[pltpu.VMEM((2 * tq, hw), BF16),
                        pltpu.VMEM((2 * tq, tk), F32),
                        pltpu.VMEM((2 * tq, tk), F32),
                        pltpu.VMEM((2 * tq, LANES), F32),
                        pltpu.VMEM((2 * tq, LANES), F32),
                        pltpu.VMEM((2 * tq, 2 * hw), F32),
                        pltpu.VMEM((n_keys, hw), BF16),
                        pltpu.VMEM((n_keys, 2 * hw), BF16)],
        compiler_params=_cparams(("parallel", "parallel", "arbitrary")),
        name="diff_attn",
    )(*args)


def _mix_kernel(h_ref, ya_ref, yb_ref, yd_ref, wg0, wg1, wg2, bg0, bg1, bg2, wu0, wu1, wu2, o_ref):
    h = h_ref[...]
    mix = None
    for y_ref, wg, bg, wu in ((ya_ref, wg0, bg0, wu0), (yb_ref, wg1, bg1, wu1), (yd_ref, wg2, bg2, wu2)):
        gate = jax.nn.sigmoid(jnp.dot(h, wg[...], preferred_element_type=F32) + bg[...])
        up = jnp.dot(y_ref[...], wu[...], preferred_element_type=F32)
        mix = gate * up if mix is None else mix + gate * up
    o_ref[...] = mix.astype(o_ref.dtype)


def _residual_out_kernel(x_ref, g1_ref, m_ref, wo_ref, o_ref):
    o_ref[...] = x_ref[...] + g1_ref[...] * jnp.dot(m_ref[...], wo_ref[...], preferred_element_type=F32)


def _merge(x, gate1, h, ya, yb, yd, w_gate, b_gate, w_up, w_out):
    bsz, n, _ = x.shape
    tm = min(1024, n)
    tn = 512
    nn = D // tn
    row = lambda b, i, j: (b, i, 0)
    in_specs = [pl.BlockSpec((None, tm, D), row),
                pl.BlockSpec((None, tm, 1024), row),
                pl.BlockSpec((None, tm, 1024), row),
                pl.BlockSpec((None, tm, 1024), row)]
    in_specs += [pl.BlockSpec((D, tn), functools.partial(lambda b, i, j, k: (0, k * nn + j), k=k)) for k in range(3)]
    in_specs += [pl.BlockSpec((1, tn), functools.partial(lambda b, i, j, k: (0, k * nn + j), k=k)) for k in range(3)]
    in_specs += [pl.BlockSpec((None, 1024, tn), functools.partial(lambda b, i, j, k: (k, 0, j), k=k)) for k in range(3)]
    mix = pl.pallas_call(
        _mix_kernel,
        grid=(bsz, n // tm, nn),
        in_specs=in_specs,
        out_specs=pl.BlockSpec((None, tm, tn), lambda b, i, j: (b, i, j)),
        out_shape=jax.ShapeDtypeStruct((bsz, n, D), BF16),
        compiler_params=_cparams(("parallel", "parallel", "parallel")),
        name="merge_mix",
    )(h, ya, yb, yd, w_gate, w_gate, w_gate, b_gate, b_gate, b_gate, w_up, w_up, w_up)
    return pl.pallas_call(
        _residual_out_kernel,
        grid=(bsz, n // tm, nn),
        in_specs=[pl.BlockSpec((None, tm, tn), lambda b, i, j: (b, i, j)),
                  pl.BlockSpec((None, 1, tn), lambda b, i, j: (b, 0, j)),
                  pl.BlockSpec((None, tm, D), row),
                  pl.BlockSpec((D, tn), lambda b, i, j: (0, j))],
        out_specs=pl.BlockSpec((None, tm, tn), lambda b, i, j: (b, i, j)),
        out_shape=jax.ShapeDtypeStruct((bsz, n, D), F32),
        compiler_params=_cparams(("parallel", "parallel", "parallel")),
        name="merge_out",
    )(x, gate1, mix, w_out)


def _route_kernel(aff_ref, posm_ref, pos_ref, *, cap, rpe):
    rows = N_EXPERTS * rpe
    bits = pltpu.bitcast(aff_ref[...], I32)
    ones = jnp.ones((LANES, LANES), BF16)
    r = lax.broadcasted_iota(I32, (rows, rows), 0)
    c = lax.broadcasted_iota(I32, (rows, rows), 1)
    same = (r // rpe) == (c // rpe)
    grp = same.astype(BF16)
    grp_before = jnp.logical_and(same, c < r).astype(BF16)
    lr = lax.broadcasted_iota(I32, (LANES, LANES), 0)
    lc = lax.broadcasted_iota(I32, (LANES, LANES), 1)
    before = (lr < lc).astype(BF16)

    def count(mask):
        per_row = jnp.dot(mask.astype(BF16), ones, preferred_element_type=F32)
        return jnp.dot(grp, per_row.astype(BF16), preferred_element_type=F32)

    def excl_prefix(mask):
        mb = mask.astype(BF16)
        within = jnp.dot(mb, before, preferred_element_type=F32)
        per_row = jnp.dot(mb, ones, preferred_element_type=F32)
        return within + jnp.dot(grp_before, per_row.astype(BF16), preferred_element_type=F32)

    def body(i, thr):
        cand = thr | (jnp.int32(1) << (30 - i))
        return jnp.where(count(bits >= cand) >= cap, cand, thr)

    thr = lax.fori_loop(0, 31, body, jnp.zeros((rows, LANES), I32))
    gt = bits > thr
    eq = bits == thr
    need = cap - count(gt)
    sel = jnp.logical_or(gt, jnp.logical_and(eq, excl_prefix(eq) < need))
    pos = excl_prefix(sel).astype(I32)
    pos_ref[...] = pos
    posm_ref[...] = jnp.where(sel, pos, -1)


def _route(aff, cap):
    bsz, n, _ = aff.shape
    rpe = n // LANES
    rows = N_EXPERTS * rpe
    aff_t = jnp.swapaxes(aff, 1, 2).reshape(bsz, rows, LANES)
    spec = pl.BlockSpec((None, rows, LANES), lambda b: (b, 0, 0))
    posm, pos = pl.pallas_call(
        functools.partial(_route_kernel, cap=cap, rpe=rpe),
        grid=(bsz,),
        in_specs=[spec], out_specs=[spec, spec],
        out_shape=[jax.ShapeDtypeStruct((bsz, rows, LANES), I32)] * 2,
        compiler_params=_cparams(("parallel",)),
        name="route",
    )(aff_t)
    return posm.reshape(bsz, N_EXPERTS, n), pos.reshape(bsz, N_EXPERTS, n)


def _gather_kernel(base_ref, base_small_ref, small_ref, h_ref, posm_ref, xg_ref, *, win, win_small, nblk):
    b = pl.program_id(0)
    xg_ref[...] = jnp.zeros_like(xg_ref)

    def scatter_block(j, bases_ref, w):
        t0 = pl.multiple_of(j * TOK_BLK, TOK_BLK)
        slot = lax.broadcasted_iota(I32, (w, TOK_BLK), 0)
        bases, onehots = [], []
        for e in range(N_EXPERTS):
            base = pl.multiple_of(bases_ref[(b * N_EXPERTS + e) * nblk + j], SLOT_ALIGN)
            rel = posm_ref[e:e + 1, pl.ds(t0, TOK_BLK)] - base
            bases.append(base)
            onehots.append(jnp.where(slot == rel, 1.0, 0.0).astype(BF16))
        rows = jnp.dot(jnp.concatenate(onehots, axis=0), h_ref[pl.ds(t0, TOK_BLK), :],
                       preferred_element_type=F32).astype(BF16)
        for e in range(N_EXPERTS):
            xg_ref[e, pl.ds(bases[e], w), :] += rows[e * w:(e + 1) * w]

    def body(j, carry):
        small = small_ref[b * nblk + j] != 0

        @pl.when(small)
        def _():
            scatter_block(j, base_small_ref, win_small)

        @pl.when(jnp.logical_not(small))
        def _():
            scatter_block(j, base_ref, win)

        return carry

    lax.fori_loop(0, nblk, body, 0)


def _expert_kernel(xg_ref, w1_ref, w3_ref, w2_ref, y_ref, acc_ref):
    f = pl.program_id(2)

    @pl.when(f == 0)
    def _():
        acc_ref[...] = jnp.zeros_like(acc_ref)

    xg = xg_ref[...]
    a = jnp.dot(xg, w1_ref[...].astype(BF16), preferred_element_type=F32)
    g = jnp.dot(xg, w3_ref[...].astype(BF16), preferred_element_type=F32)
    acc_ref[...] += jnp.dot((_silu(a) * g).astype(BF16), w2_ref[...].astype(BF16), preferred_element_type=F32)

    @pl.when(f == pl.num_programs(2) - 1)
    def _():
        y_ref[...] = acc_ref[...].astype(y_ref.dtype)


def _combine_kernel(base_ref, x_ref, g2_ref, aff_ref, posm_ref, *rest, win, win_tile, cap, nblk, nsub, final):
    y_refs = rest[:EXPERT_GROUP]
    if final:
        fw_ref, o_ref, acc_ref = rest[EXPERT_GROUP:]
    else:
        o_ref, acc_ref = rest[EXPERT_GROUP:]
    b, jt, eg = pl.program_id(0), pl.program_id(1), pl.program_id(2)

    @pl.when(eg == 0)
    def _():
        acc_ref[...] = jnp.zeros_like(acc_ref)

    lane = lax.broadcasted_iota(I32, (TOK_BLK, N_EXPERTS), 1)
    slot = lax.broadcasted_iota(I32, (TOK_BLK, win), 1)
    for sb in range(nsub):
        rows = slice(sb * TOK_BLK, (sb + 1) * TOK_BLK)
        total = None
        for k, y_ref in enumerate(y_refs):
            e = eg * EXPERT_GROUP + k
            first = (b * N_EXPERTS + e) * nblk + jt * nsub
            tile_base = jnp.minimum(base_ref[first], cap - win_tile)
            base = base_ref[first + sb]
            mine = lane == e
            rel = jnp.sum(jnp.where(mine, posm_ref[rows, :], 0), axis=-1, keepdims=True) - base
            val = jnp.sum(jnp.where(mine, aff_ref[rows, :], 0.0), axis=-1, keepdims=True)
            onehot = jnp.where(slot == rel, 1.0, 0.0).astype(BF16)
            off = pl.multiple_of(base - tile_base, SLOT_ALIGN)
            part = val * jnp.dot(onehot, y_ref[pl.ds(off, win), :], preferred_element_type=F32)
            total = part if total is None else total + part
        acc_ref[rows, :] += total

    @pl.when(eg == pl.num_programs(2) - 1)
    def _():
        out = x_ref[...] + g2_ref[...] * acc_ref[...]
        if final:
            ms = jnp.mean(out * out, axis=-1, keepdims=True)
            out = out * lax.rsqrt(ms + EPS) * fw_ref[...]
        o_ref[...] = out


def _ec_moe(token_sets, w1, w3, w2, layer):
    bsz = token_sets[0][0].shape[0]
    plans, xgs = [], []
    for x, gate2, h2, aff, final_w in token_sets:
        n = x.shape[1]
        cap = EC_CAPACITY * n // N_EXPERTS
        nblk = n // TOK_BLK
        win = min(TOK_BLK + SLOT_ALIGN, cap)
        posm, pos = _route(aff, cap)
        start = pos[:, :, ::TOK_BLK]
        aligned = (start // SLOT_ALIGN) * SLOT_ALIGN
        base = jnp.minimum(aligned, cap - win).astype(I32).reshape(-1)
        win_small = min(GATHER_WIN_SMALL, win)
        base_small = jnp.minimum(aligned, cap - win_small).astype(I32).reshape(-1)
        count = jnp.concatenate([start[:, :, 1:], jnp.full_like(start[:, :, :1], cap)], axis=2) - start
        small = jnp.all(count <= win_small - SLOT_ALIGN, axis=1).astype(I32).reshape(-1)
        dq = GATHER_COLS
        xgs.append(pl.pallas_call(
            functools.partial(_gather_kernel, win=win, win_small=win_small, nblk=nblk),
            grid_spec=pltpu.PrefetchScalarGridSpec(
                num_scalar_prefetch=3,
                grid=(bsz, D // dq),
                in_specs=[pl.BlockSpec((None, n, dq), lambda b, c, *_: (b, 0, c)),
                          pl.BlockSpec((None, N_EXPERTS, n), lambda b, c, *_: (b, 0, 0))],
                out_specs=pl.BlockSpec((None, N_EXPERTS, cap, dq), lambda b, c, *_: (b, 0, 0, c))),
            out_shape=jax.ShapeDtypeStruct((bsz, N_EXPERTS, cap, D), BF16),
            compiler_params=_cparams(("parallel", "parallel")),
            name="moe_gather",
        )(base, base_small, small, h2, posm))
        plans.append((n, cap, nblk, win, posm, base))

    xg = xgs[0] if len(xgs) == 1 else jnp.concatenate(xgs, axis=2)
    cap_all = xg.shape[2]
    tf = 256
    y = pl.pallas_call(
        _expert_kernel,
        grid=(N_EXPERTS, bsz, EXPERT_FF // tf),
        in_specs=[pl.BlockSpec((None, None, cap_all, D), lambda e, b, f: (b, e, 0, 0)),
                  pl.BlockSpec((None, None, D, tf), lambda e, b, f: (layer, e, 0, f)),
                  pl.BlockSpec((None, None, D, tf), lambda e, b, f: (layer, e, 0, f)),
                  pl.BlockSpec((None, None, tf, D), lambda e, b, f: (layer, e, f, 0))],
        out_specs=pl.BlockSpec((None, None, cap_all, D), lambda e, b, f: (b, e, 0, 0)),
        out_shape=jax.ShapeDtypeStruct((bsz, N_EXPERTS, cap_all, D), BF16),
        scratch_shapes=[pltpu.VMEM((cap_all, D), F32)],
        compiler_params=_cparams(("parallel", "parallel", "arbitrary")),
        name="moe_expert",
    )(xg, w1, w3, w2)
    y2d = y.reshape(bsz * N_EXPERTS * cap_all, D)

    outs = []
    row0 = 0
    for (x, gate2, h2, aff, final_w), (n, cap, nblk, win, posm, base) in zip(token_sets, plans):
        tile = min(COMBINE_TILE, n)
        nsub = tile // TOK_BLK
        win_tile = min(tile + SLOT_ALIGN, cap)
        final = final_w is not None

        def y_window(b, j, eg, base, k, cap=cap, nblk=nblk, nsub=nsub, win_tile=win_tile, row0=row0):
            be = b * N_EXPERTS + eg * EXPERT_GROUP + k
            start = jnp.minimum(base[be * nblk + j * nsub], cap - win_tile)
            return pl.multiple_of(be * cap_all + row0 + start, SLOT_ALIGN), 0

        in_specs = [pl.BlockSpec((None, tile, D), lambda b, j, e, base: (b, j, 0)),
                    pl.BlockSpec((None, 1, D), lambda b, j, e, base: (b, 0, 0)),
                    pl.BlockSpec((None, tile, N_EXPERTS), lambda b, j, e, base: (b, j, 0)),
                    pl.BlockSpec((None, tile, N_EXPERTS), lambda b, j, e, base: (b, j, 0))]
        in_specs += [pl.BlockSpec((pl.Element(win_tile), pl.Element(D)), functools.partial(y_window, k=k))
                     for k in range(EXPERT_GROUP)]
        args = [base, x, gate2, aff, jnp.swapaxes(posm, 1, 2)] + [y2d] * EXPERT_GROUP
        if final:
            in_specs.append(pl.BlockSpec((1, D), lambda b, j, e, base: (0, 0)))
            args.append(final_w.reshape(1, D))
        outs.append(pl.pallas_call(
            functools.partial(_combine_kernel, win=win, win_tile=win_tile, cap=cap, nblk=nblk, nsub=nsub, final=final),
            grid_spec=pltpu.PrefetchScalarGridSpec(
                num_scalar_prefetch=1,
                grid=(bsz, n // tile, N_EXPERTS // EXPERT_GROUP),
                in_specs=in_specs,
                out_specs=pl.BlockSpec((None, tile, D), lambda b, j, e, base: (b, j, 0)),
                scratch_shapes=[pltpu.VMEM((tile, D), F32)]),
            out_shape=jax.ShapeDtypeStruct((bsz, n, D), F32),
            compiler_params=_cparams(("parallel", "parallel", "arbitrary")),
            name="moe_combine",
        )(*args))
        row0 += cap
    return outs


def _rope_tables(n_tokens):
    rows = n_tokens // GRID_W
    row = jnp.repeat(jnp.arange(rows, dtype=F32), GRID_W)
    col = (jnp.arange(rows * GRID_W, dtype=I32) % GRID_W).astype(F32)
    inv = ROPE_BASE ** (-jnp.arange(ROPE_PAIRS, dtype=F32) / ROPE_PAIRS)
    ang = jnp.concatenate([row[:, None] * inv, col[:, None] * inv], axis=-1)
    cos, sin = jnp.cos(ang), jnp.sin(ang)
    reps = LANES // DA_HEAD_DIM
    cos_t = jnp.tile(jnp.concatenate([cos, cos], axis=-1), (1, reps))
    sin_t = jnp.tile(jnp.concatenate([-sin, sin], axis=-1), (1, reps))
    return cos_t, sin_t


def _lower_bounds(p):
    cum = jnp.cumsum(jax.nn.softmax(p.astype(F32), axis=0), axis=0)
    return cum - cum[0]


def _pad128(v):
    return jnp.pad(v.reshape(1, -1).astype(F32), ((0, 0), (0, LANES - v.size)))


def kernel(x, c, ctx, c_ctx, ada_w, ada_b, norm1_w, norm2_w, w_in, ssd_conv_w, ssd_conv_b, ssd_dt_bias,
           ssd_a_log, ssd_d, ssd_norm_w, hg_lb, hg_norm_w, da_lambda, da_norm_w, w_up, w_gate, b_gate,
           w_out, moe_router, moe_w1, moe_w3, moe_w2, final_norm_w):
    bsz, n_lat, _ = x.shape
    n_ctx = ctx.shape[1]
    depth = ada_w.shape[0]
    rope = _rope_tables(n_lat)
    lb_all = jnp.stack([_lower_bounds(hg_lb[0]), _lower_bounds(hg_lb[1])], axis=1)

    mod = _adaln(jnp.concatenate([c, c_ctx[None]], axis=0), ada_w, ada_b)
    mod = mod.reshape(depth, 8, 6, 1, D)

    xl, xc = x, ctx
    for l in range(depth):
        need_ctx = l < depth - 1
        lam_init = 0.8 - 0.6 * math.exp(-0.3 * l)
        mod_l = [mod[l, :bsz, k] for k in range(6)]
        mod_c = [jnp.broadcast_to(mod[l, bsz:bsz + 1, k], (bsz, 1, D)) for k in range(6)]

        wl = w_in[l]
        w_a = jnp.concatenate([wl[:, OFF_SSD_Z:OFF_SSD_XBC], wl[:, OFF_HG_Q:OFF_HG_F], wl[:, OFF_HG_I:OFF_DA_Q],
                               wl[:, OFF_DA_V:N_IN], wl[:, OFF_SSD_XBC:OFF_SSD_DT]], axis=1).astype(BF16)
        w_f = jnp.concatenate([wl[:, OFF_HG_F:OFF_HG_I], wl[:, OFF_SSD_DT:OFF_HG_Q],
                               jnp.zeros((D, PF_N - PF_DT - 2 * SSD_HEADS), F32)], axis=1).astype(BF16)
        w_qk = wl[:, OFF_DA_Q:OFF_DA_V].astype(BF16)
        alog128 = _pad128(ssd_a_log[l])
        bias128 = _pad128(ssd_dt_bias[l])
        d_full = jnp.repeat(ssd_d[l].astype(F32), SSD_D_INNER // SSD_HEADS).reshape(1, SSD_D_INNER)
        ssd_nw = ssd_norm_w[l].reshape(1, SSD_D_INNER).astype(F32)
        hg_nw = hg_norm_w[l].reshape(1, HG_HEAD_DIM).astype(F32)
        wg_b = w_gate[l].astype(BF16)
        bg = b_gate[l].reshape(1, N_BRANCH * D)
        wu_b = w_up[l].astype(BF16)
        wo_b = w_out[l].astype(BF16)

        def project(xs, m, use_rope):
            n = xs.shape[1]
            h = _modulate(xs, norm1_w[l], m[0], m[1])
            h2d = h.reshape(bsz * n, D)
            tm = min(1024, n)
            pa = _matmul(h2d, w_a, BF16, tm, PA_N // 4).reshape(bsz, n, PA_N)
            pf = _matmul(h2d, w_f, F32, tm, PF_N).reshape(bsz, n, PF_N)
            pqk = _matmul(h2d, w_qk, BF16, tm, 1024, rope_tables=rope if use_rope else None).reshape(bsz, n, PQK_N)
            return h, pa, pf, pqk

        hl, pa_l, pf_l, pqk_l = project(xl, mod_l, True)
        hc, pa_c, pf_c, pqk_c = project(xc, mod_c, False)

        xa_c = _ssd_conv(pa_c, ssd_conv_w[l], ssd_conv_b[l])
        xa_l = _ssd_conv(pa_l, ssd_conv_w[l], ssd_conv_b[l])
        yc_f, s_f = _ssd_scan(xa_c, pf_c, alog128, bias128, rev=False)
        yl_f, _ = _ssd_scan(xa_l, pf_l, alog128, bias128, rev=False, init=s_f)
        ya_c, s_b = _ssd_scan(xa_c, pf_c, alog128, bias128, rev=True, readout=(pa_c, yc_f, d_full, ssd_nw))
        ya_l, _ = _ssd_scan(xa_l, pf_l, alog128, bias128, rev=True, init=s_b, readout=(pa_l, yl_f, d_full, ssd_nw))

        lb_f, lb_b = lb_all[l, 0].reshape(1, HG_WIDTH), lb_all[l, 1].reshape(1, HG_WIDTH)
        oc_f, t_f = _hg_scan(pa_c, pf_c, lb_f, rev=False)
        ol_f, _ = _hg_scan(pa_l, pf_l, lb_f, rev=False, init=t_f)
        yb_c, t_b = _hg_scan(pa_c, pf_c, lb_b, rev=True, readout=(oc_f, hg_nw))
        yb_l, _ = _hg_scan(pa_l, pf_l, lb_b, rev=True, init=t_b, readout=(ol_f, hg_nw))

        yd_l = _diff_attn(pqk_l, [(pqk_c, DA_WIDTH, pa_c, PA_DV), (pqk_l, DA_WIDTH, pa_l, PA_DV)],
                          da_lambda[l], da_norm_w[l], lam_init, 512, ATTN_TK)

        xl = _merge(xl, mod_l[2], hl, ya_l, yb_l, yd_l, wg_b, bg, wu_b, wo_b)
        h2l, aff_l = _modulate(xl, norm2_w[l], mod_l[3], mod_l[4], w_router=moe_router[l])
        token_sets = [(xl, mod_l[5], h2l, aff_l, final_norm_w if l == depth - 1 else None)]
        if need_ctx:
            yd_c = _diff_attn(pqk_c, [(pqk_c, DA_WIDTH, pa_c, PA_DV)], da_lambda[l], da_norm_w[l], lam_init,
                              n_ctx, ATTN_TK)
            xc = _merge(xc, mod_c[2], hc, ya_c, yb_c, yd_c, wg_b, bg, wu_b, wo_b)
            h2c, aff_c = _modulate(xc, norm2_w[l], mod_c[3], mod_c[4], w_router=moe_router[l])
            token_sets.append((xc, mod_c[5], h2c, aff_c, None))
        outs = _ec_moe(token_sets, moe_w1, moe_w3, moe_w2, l)
        xl = outs[0]
        if need_ctx:
            xc = outs[1]
    return xl
```

```python
import functools
import math

import jax
import jax.numpy as jnp
from jax import lax
from jax.experimental import pallas as pl
from jax.experimental.pallas import tpu as pltpu

F32 = jnp.float32
BF16 = jnp.bfloat16
I32 = jnp.int32
HIGHEST = lax.Precision.HIGHEST
LOG2E = 1.4426950408889634

D = 2048
EPS = 1e-6
GRID_W = 64
SSD_D_INNER = 1024
SSD_HEADS = 16
SSD_GROUPS = 2
SSD_STATE = 128
SSD_XBC = 1536
SSD_CONV_W = 5
HG_WIDTH = 1024
HG_HEADS = 8
HG_HEAD_DIM = 128
DA_HEADS = 8
DA_HEAD_DIM = 64
DA_WIDTH = 1024
ROPE_BASE = 10000.0
ROPE_PAIRS = 16
N_BRANCH = 3
N_EXPERTS = 16
EXPERT_FF = 2048
EC_CAPACITY = 2

OFF_SSD_Z = 0
OFF_SSD_XBC = 1024
OFF_SSD_DT = 2560
OFF_HG_Q = 2592
OFF_HG_F = OFF_HG_Q + 1024
OFF_HG_I = OFF_HG_F + 2048
OFF_HG_G = OFF_HG_I + 1024
OFF_DA_Q = OFF_HG_G + 1024
OFF_DA_K = OFF_DA_Q + 1024
OFF_DA_V = OFF_DA_K + 1024
N_IN = OFF_DA_V + 1024

LANES = 128
PA_Z, PA_HQ, PA_HI, PA_HG, PA_DV, PA_XBC = 0, 1024, 2048, 3072, 4096, 5120
PA_N = 6656
PF_F, PF_DT = 0, 2048
PF_N = 2176
PQK_N = 2048

SCAN_Q = 256
ATTN_TK = 2816
HALO = 16
TOK_BLK = 128
COMBINE_TILE = 256
GATHER_COLS = 512
EXPERT_GROUP = 16
GATHER_WIN_SMALL = 48
SLOT_ALIGN = 16
VMEM_LIMIT = 56 * 1024 * 1024


def _cparams(sem, vmem=VMEM_LIMIT):
    return pltpu.CompilerParams(dimension_semantics=sem, vmem_limit_bytes=vmem)


def _silu(x):
    return x * jax.nn.sigmoid(x)


def _adaln_kernel(cb_ref, w_ref, b_ref, o_ref, *, n_rows, tn):
    nct = tn // LANES

    def body(k, accs):
        r0 = pl.multiple_of(k * 8, 8)
        ws = [w_ref[pl.ds(r0, 8), c * LANES:(c + 1) * LANES] for c in range(nct)]
        out = []
        for r in range(n_rows):
            s = _silu(cb_ref[r, pl.ds(r0, 8), :])
            out.append(tuple(accs[r][c] + s * ws[c] for c in range(nct)))
        return tuple(out)

    zero = jnp.zeros((8, LANES), F32)
    accs = lax.fori_loop(0, D // 8, body, tuple(tuple(zero for _ in range(nct)) for _ in range(n_rows)),
                         unroll=8)
    rows = []
    for r in range(n_rows):
        rows.append(jnp.concatenate([jnp.sum(accs[r][c], axis=0, keepdims=True) for c in range(nct)], axis=1))
    rows.append(jnp.zeros((8 - n_rows, tn), F32))
    o_ref[...] = jnp.concatenate(rows, axis=0) + b_ref[...]


def _adaln(cvec, ada_w, ada_b):
    n_rows = cvec.shape[0]
    depth, _, n6 = ada_w.shape
    tn = 512
    cb = jnp.broadcast_to(cvec[:, :, None], (n_rows, D, LANES))
    return pl.pallas_call(
        functools.partial(_adaln_kernel, n_rows=n_rows, tn=tn),
        grid=(depth, n6 // tn),
        in_specs=[pl.BlockSpec((n_rows, D, LANES), lambda l, j: (0, 0, 0)),
                  pl.BlockSpec((None, D, tn), lambda l, j: (l, 0, j)),
                  pl.BlockSpec((None, 1, tn), lambda l, j: (l, 0, j))],
        out_specs=pl.BlockSpec((None, 8, tn), lambda l, j: (l, 0, j)),
        out_shape=jax.ShapeDtypeStruct((depth, 8, n6), F32),
        compiler_params=_cparams(("parallel", "parallel")),
        name="adaln",
    )(cb, ada_w, ada_b.reshape(depth, 1, n6))


def _modulate_kernel(x_ref, nw_ref, sh_ref, sc_ref, *rest, with_router):
    x = x_ref[...]
    ms = jnp.mean(x * x, axis=-1, keepdims=True)
    h = (x * lax.rsqrt(ms + EPS) * nw_ref[...]) * (1.0 + sc_ref[...]) + sh_ref[...]
    if with_router:
        wr_ref, h_ref, aff_ref = rest
        logits = jnp.dot(h, wr_ref[...], preferred_element_type=F32, precision=HIGHEST)
        m = jnp.max(logits, axis=-1, keepdims=True)
        e = jnp.exp(logits - m)
        aff_ref[...] = e / jnp.sum(e, axis=-1, keepdims=True)
    else:
        (h_ref,) = rest
    h_ref[...] = h.astype(BF16)


def _modulate(x, nw, shift, scale, w_router=None):
    bsz, n, _ = x.shape
    tm = min(512, n)
    with_router = w_router is not None
    in_specs = [pl.BlockSpec((None, tm, D), lambda b, i: (b, i, 0)),
                pl.BlockSpec((1, D), lambda b, i: (0, 0)),
                pl.BlockSpec((None, 1, D), lambda b, i: (b, 0, 0)),
                pl.BlockSpec((None, 1, D), lambda b, i: (b, 0, 0))]
    args = [x, nw.reshape(1, D), shift, scale]
    out_specs = [pl.BlockSpec((None, tm, D), lambda b, i: (b, i, 0))]
    out_shape = [jax.ShapeDtypeStruct((bsz, n, D), BF16)]
    if with_router:
        in_specs.append(pl.BlockSpec((D, N_EXPERTS), lambda b, i: (0, 0)))
        args.append(w_router)
        out_specs.append(pl.BlockSpec((None, tm, N_EXPERTS), lambda b, i: (b, i, 0)))
        out_shape.append(jax.ShapeDtypeStruct((bsz, n, N_EXPERTS), F32))
    res = pl.pallas_call(
        functools.partial(_modulate_kernel, with_router=with_router),
        grid=(bsz, n // tm),
        in_specs=in_specs, out_specs=out_specs, out_shape=out_shape,
        compiler_params=_cparams(("parallel", "parallel")),
        name="modulate_router" if with_router else "modulate",
    )(*args)
    return res if with_router else res[0]


def _mm_kernel(a_ref, w_ref, *rest, rope, tn):
    acc = jnp.dot(a_ref[...], w_ref[...], preferred_element_type=F32)
    if rope:
        cos_ref, sin_ref, o_ref = rest
        cos = cos_ref[...]
        sin = sin_ref[...]
        lane = lax.broadcasted_iota(I32, cos.shape, 1)
        first_half = (lane % DA_HEAD_DIM) < (DA_HEAD_DIM // 2)
        for c in range(tn // LANES):
            xs = acc[:, c * LANES:(c + 1) * LANES]
            partner = jnp.where(first_half,
                                pltpu.roll(xs, LANES - DA_HEAD_DIM // 2, 1),
                                pltpu.roll(xs, DA_HEAD_DIM // 2, 1))
            o_ref[:, c * LANES:(c + 1) * LANES] = (xs * cos + partner * sin).astype(o_ref.dtype)
    else:
        (o_ref,) = rest
        o_ref[...] = acc.astype(o_ref.dtype)


def _matmul(a, w, out_dtype, tm, tn, rope_tables=None):
    m, k = a.shape
    n = w.shape[1]
    rope = rope_tables is not None
    in_specs = [pl.BlockSpec((tm, k), lambda i, j: (i, 0)),
                pl.BlockSpec((k, tn), lambda i, j: (0, j))]
    args = [a, w]
    if rope:
        cos, sin = rope_tables
        nt = cos.shape[0] // tm
        in_specs += [pl.BlockSpec((tm, LANES), lambda i, j: (i % nt, 0)),
                     pl.BlockSpec((tm, LANES), lambda i, j: (i % nt, 0))]
        args += [cos, sin]
    return pl.pallas_call(
        functools.partial(_mm_kernel, rope=rope, tn=tn),
        grid=(m // tm, n // tn),
        in_specs=in_specs,
        out_specs=pl.BlockSpec((tm, tn), lambda i, j: (i, j)),
        out_shape=jax.ShapeDtypeStruct((m, n), out_dtype),
        compiler_params=_cparams(("parallel", "parallel")),
        name="proj_rope" if rope else "proj",
    )(*args)


def _conv_kernel(prev_ref, cur_ref, next_ref, w_ref, b_ref, o_ref, *, tc):
    i = pl.program_id(1)
    last = pl.num_programs(1) - 1
    pad = (SSD_CONV_W - 1) // 2
    prev = jnp.where(i > 0, prev_ref[...].astype(F32), 0.0)
    nxt = jnp.where(i < last, next_ref[...].astype(F32), 0.0)
    ext = jnp.concatenate([prev, cur_ref[...].astype(F32), nxt], axis=0)
    w = w_ref[...]
    y = b_ref[...] + w[0:1] * ext[HALO - pad:HALO - pad + tc]
    for k in range(1, SSD_CONV_W):
        y = y + w[k:k + 1] * ext[HALO - pad + k:HALO - pad + k + tc]
    o_ref[...] = _silu(y).astype(o_ref.dtype)


def _ssd_conv(pa, conv_w, conv_b):
    bsz, n, _ = pa.shape
    tc = min(512, n)
    tw = 512
    c0 = PA_XBC // tw
    nbh = n // HALO
    return pl.pallas_call(
        functools.partial(_conv_kernel, tc=tc),
        grid=(bsz, n // tc, SSD_XBC // tw),
        in_specs=[pl.BlockSpec((None, HALO, tw), lambda b, i, j: (b, jnp.maximum(i * (tc // HALO) - 1, 0), c0 + j)),
                  pl.BlockSpec((None, tc, tw), lambda b, i, j: (b, i, c0 + j)),
                  pl.BlockSpec((None, HALO, tw),
                               lambda b, i, j: (b, jnp.minimum((i + 1) * (tc // HALO), nbh - 1), c0 + j)),
                  pl.BlockSpec((SSD_CONV_W, tw), lambda b, i, j: (0, j)),
                  pl.BlockSpec((1, tw), lambda b, i, j: (0, j))],
        out_specs=pl.BlockSpec((None, tc, tw), lambda b, i, j: (b, i, j)),
        out_shape=jax.ShapeDtypeStruct((bsz, n, SSD_XBC), BF16),
        compiler_params=_cparams(("parallel", "parallel", "parallel")),
        name="ssd_conv",
    )(pa, pa, pa, conv_w, conv_b.reshape(1, SSD_XBC))


def _tri(q, rev):
    r = lax.broadcasted_iota(I32, (q, q), 0)
    c = lax.broadcasted_iota(I32, (q, q), 1)
    return (c >= r) if rev else (r >= c)


def _split3(x):
    hi = x.astype(BF16)
    r = x - hi.astype(F32)
    mid = r.astype(BF16)
    lo = (r - mid.astype(F32)).astype(BF16)
    return hi, mid, lo


def _select_rows_dot(sel01, x):
    hi, mid, lo = _split3(x)
    return (jnp.dot(sel01, hi, preferred_element_type=F32) + jnp.dot(sel01, mid, preferred_element_type=F32)
            + jnp.dot(sel01, lo, preferred_element_type=F32))


def _select_cols_dot(x, sel01):
    hi, mid, lo = _split3(x)
    return (jnp.dot(hi, sel01, preferred_element_type=F32) + jnp.dot(mid, sel01, preferred_element_type=F32)
            + jnp.dot(lo, sel01, preferred_element_type=F32))


def _ssd_scan_kernel(*refs, rev, has_init, readout):
    it = iter(refs)
    xact_ref, dt_ref, alog_ref, bias_ref = next(it), next(it), next(it), next(it)
    init_ref = next(it) if has_init else None
    if readout:
        z_ref, yprev_ref, d_ref, nw_ref = next(it), next(it), next(it), next(it)
    y_ref, st_ref = next(it), next(it)
    ybuf_ref = next(it) if readout else None

    q = SCAN_Q
    hd = SSD_D_INNER // SSD_HEADS
    hg = SSD_HEADS // SSD_GROUPS
    gw = hg * hd
    d_off = SSD_HEADS if rev else 0

    @pl.when(pl.program_id(1) == 0)
    def _():
        st_ref[...] = init_ref[...] if has_init else jnp.zeros_like(st_ref)

    tri = _tri(q, rev)
    dtv = jax.nn.softplus(dt_ref[...] + bias_ref[...])
    da = dtv * (-LOG2E * jnp.exp(alog_ref[...]))
    cs = _select_rows_dot(jnp.where(tri, 1.0, 0.0).astype(BF16), da)
    cs_t = cs.T
    erow = lax.broadcasted_iota(I32, (LANES, SSD_D_INNER), 0)
    ecol = lax.broadcasted_iota(I32, (LANES, SSD_D_INNER), 1)
    expand = jnp.where(erow == d_off + ecol // hd, 1.0, 0.0).astype(BF16)
    csx = _select_cols_dot(cs, expand)
    dtx = _select_cols_dot(dtv, expand)
    x = xact_ref[:, :SSD_D_INNER].astype(F32)
    xdt = x * dtx
    xdt_b = xdt.astype(BF16)
    cs_end = csx[0:1] if rev else csx[q - 1:q]
    ecs = jnp.exp2(csx)
    xw = (xdt * jnp.exp2(cs_end - csx)).astype(BF16)
    ecs_end = jnp.exp2(cs_end)
    lane = lax.broadcasted_iota(I32, (q, LANES), 1)
    out_ref = ybuf_ref if readout else y_ref

    for g in range(SSD_GROUPS):
        bg = xact_ref[:, SSD_D_INNER + g * SSD_STATE:SSD_D_INNER + (g + 1) * SSD_STATE]
        cg = xact_ref[:, SSD_D_INNER + (SSD_GROUPS + g) * SSD_STATE:SSD_D_INNER + (SSD_GROUPS + g + 1) * SSD_STATE]
        cb = lax.dot_general(cg, bg, (((1,), (1,)), ((), ())), preferred_element_type=F32)
        st_g = st_ref[:, g * gw:(g + 1) * gw]
        y_inter = jnp.dot(cg, st_g.astype(BF16), preferred_element_type=F32) * ecs[:, g * gw:(g + 1) * gw]
        for jj in range(hg // 2):
            col0 = g * gw + jj * LANES
            xp = xdt_b[:, col0:col0 + LANES]
            acc = y_inter[:, jj * LANES:(jj + 1) * LANES]
            for s in range(2):
                j = d_off + g * hg + 2 * jj + s
                e = cs[:, j:j + 1] - cs_t[j:j + 1, :]
                seg = jnp.exp2(jnp.where(tri, e, -jnp.inf))
                m = (cb * seg).astype(BF16)
                xm = jnp.where((lane >= hd) if s else (lane < hd), xp, jnp.zeros_like(xp))
                acc = acc + jnp.dot(m, xm, preferred_element_type=F32)
            out_ref[:, col0:col0 + LANES] = acc
        upd = lax.dot_general(bg, xw[:, g * gw:(g + 1) * gw], (((0,), (0,)), ((), ())),
                              preferred_element_type=F32)
        st_ref[:, g * gw:(g + 1) * gw] = st_g * ecs_end[:, g * gw:(g + 1) * gw] + upd

    if readout:
        z = z_ref[...].astype(F32)
        yy = (yprev_ref[...] + ybuf_ref[...] + d_ref[...] * x) * _silu(z)
        for g in range(SSD_GROUPS):
            seg = yy[:, g * gw:(g + 1) * gw]
            ms = jnp.mean(seg * seg, axis=-1, keepdims=True)
            y_ref[:, g * gw:(g + 1) * gw] = (seg * lax.rsqrt(ms + EPS) * nw_ref[:, g * gw:(g + 1) * gw]).astype(y_ref.dtype)


def _ssd_scan(xact, pf, alog128, bias128, *, rev, init=None, readout=None):
    bsz, n, _ = xact.shape
    nc = n // SCAN_Q
    cidx = (lambda c: nc - 1 - c) if rev else (lambda c: c)
    in_specs = [pl.BlockSpec((None, SCAN_Q, SSD_XBC), lambda b, c: (b, cidx(c), 0)),
                pl.BlockSpec((None, SCAN_Q, LANES), lambda b, c: (b, cidx(c), PF_DT // LANES)),
                pl.BlockSpec((1, LANES), lambda b, c: (0, 0)),
                pl.BlockSpec((1, LANES), lambda b, c: (0, 0))]
    args = [xact, pf, alog128, bias128]
    if init is not None:
        in_specs.append(pl.BlockSpec((None, SSD_STATE, SSD_D_INNER), lambda b, c: (b, 0, 0)))
        args.append(init)
    scratch = []
    if readout is not None:
        pa, yprev, d_full, norm_w = readout
        in_specs += [pl.BlockSpec((None, SCAN_Q, SSD_D_INNER), lambda b, c: (b, cidx(c), PA_Z // SSD_D_INNER)),
                     pl.BlockSpec((None, SCAN_Q, SSD_D_INNER), lambda b, c: (b, cidx(c), 0)),
                     pl.BlockSpec((1, SSD_D_INNER), lambda b, c: (0, 0)),
                     pl.BlockSpec((1, SSD_D_INNER), lambda b, c: (0, 0))]
        args += [pa, yprev, d_full, norm_w]
        scratch = [pltpu.VMEM((SCAN_Q, SSD_D_INNER), F32)]
    y_dtype = BF16 if readout is not None else F32
    y, st = pl.pallas_call(
        functools.partial(_ssd_scan_kernel, rev=rev, has_init=init is not None, readout=readout is not None),
        grid=(bsz, nc),
        in_specs=in_specs,
        out_specs=[pl.BlockSpec((None, SCAN_Q, SSD_D_INNER), lambda b, c: (b, cidx(c), 0)),
                   pl.BlockSpec((None, SSD_STATE, SSD_D_INNER), lambda b, c: (b, 0, 0))],
        out_shape=[jax.ShapeDtypeStruct((bsz, n, SSD_D_INNER), y_dtype),
                   jax.ShapeDtypeStruct((bsz, SSD_STATE, SSD_D_INNER), F32)],
        scratch_shapes=scratch,
        compiler_params=_cparams(("parallel", "arbitrary")),
        name="ssd_scan_bwd" if rev else "ssd_scan_fwd",
    )(*args)
    return y, st


def _ref_rows(bb, s, rev):
    q, w = bb.shape
    rl = s if rev else s - 1
    if 2 * s >= 8:
        b3 = bb.reshape(q // (2 * s), 2 * s, w)
        return jnp.broadcast_to(b3[:, rl:rl + 1, :], b3.shape).reshape(q, w)
    off = lax.broadcasted_iota(I32, (q, 1), 0) % (2 * s)
    out = bb
    for ov in range(2 * s):
        if ov != rl:
            out = jnp.where(off == ov, pltpu.roll(bb, (ov - rl) % q, 0), out)
    return out


def _level_operand(bb, qq, kk, s, rev):
    q = bb.shape[0]
    if s < 8:
        upper = (lax.broadcasted_iota(I32, (q, 1), 0) & s) != 0
        q_side = jnp.logical_not(upper) if rev else upper
        ex = jnp.exp2(-jnp.abs(bb - _ref_rows(bb, s, rev)))
        return (jnp.where(q_side, qq, kk) * ex).astype(BF16)
    pieces = []
    for a in range(0, q, 2 * s):
        lo, hi = slice(a, a + s), slice(a + s, a + 2 * s)
        if rev:
            r = bb[a + s:a + s + 1]
            pieces += [qq[lo] * jnp.exp2(bb[lo] - r), kk[hi] * jnp.exp2(r - bb[hi])]
        else:
            r = bb[a + s - 1:a + s]
            pieces += [kk[lo] * jnp.exp2(r - bb[lo]), qq[hi] * jnp.exp2(bb[hi] - r)]
    return jnp.concatenate(pieces, axis=0).astype(BF16)


def _hg_scan_kernel(*refs, rev, has_init, readout):
    it = iter(refs)
    q_ref, f_ref, v_ref, lb_ref = next(it), next(it), next(it), next(it)
    init_ref = next(it) if has_init else None
    if readout:
        g_ref, oprev_ref, nw_ref = next(it), next(it), next(it)
    o_ref, st_ref, att_ref = next(it), next(it), next(it)
    obuf_ref = next(it) if readout else None

    q = SCAN_Q
    hd = HG_HEAD_DIM

    @pl.when(pl.program_id(1) == 0)
    def _():
        st_ref[...] = init_ref[...] if has_init else jnp.zeros_like(st_ref)

    lb = lb_ref[...]
    qq = _silu(q_ref[...].astype(F32))
    f = lb + (1.0 - lb) * jax.nn.sigmoid(f_ref[...])
    kk = 1.0 - f
    lf = jnp.log(f) * LOG2E
    tri = _tri(q, rev)
    bb = jnp.dot(jnp.where(tri, 1.0, 0.0), lf, preferred_element_type=F32, precision=HIGHEST)

    half = q // 2
    ri = lax.broadcasted_iota(I32, (half, half), 0)
    ci = lax.broadcasted_iota(I32, (half, half), 1)
    xr = ri ^ ci
    causal = (ci > ri) if rev else (ri > ci)
    nt = (((1,), (1,)), ((), ()))
    s = half
    while s >= 1:
        u = _level_operand(bb, qq, kk, s, rev)
        if s < half:
            level = jnp.logical_and((xr >> int(math.log2(s))) == 1, causal)
        for h in range(HG_HEADS):
            u_lo, u_hi = u[:half, h * hd:(h + 1) * hd], u[half:, h * hd:(h + 1) * hd]
            if s == half:
                qs_, ks_ = (u_lo, u_hi) if rev else (u_hi, u_lo)
                att_ref[h, 2] = lax.dot_general(qs_, ks_, nt, preferred_element_type=F32)
            else:
                for blk, ub in enumerate((u_lo, u_hi)):
                    p = lax.dot_general(ub, ub, nt, preferred_element_type=F32)
                    old = att_ref[h, blk] if 2 * s < half else 0.0
                    att_ref[h, blk] = jnp.where(level, p, old)
        s //= 2

    b_end = bb[0:1] if rev else bb[q - 1:q]
    qe = (qq * jnp.exp2(bb)).astype(BF16)
    kh = (kk * jnp.exp2(b_end - bb)).astype(BF16)
    qk = (qq * kk).astype(BF16)
    e_end = jnp.exp2(b_end)
    ones = jnp.ones((hd, hd), BF16)
    out_ref = obuf_ref if readout else o_ref
    for h in range(HG_HEADS):
        sl = slice(h * hd, (h + 1) * hd)
        vh = v_ref[:, sl]
        st_h = st_ref[sl, :]
        v_lo, v_hi = vh[:half], vh[half:]
        d0, d1, off = (att_ref[h, k].astype(BF16) for k in range(3))
        o_lo = jnp.dot(d0, v_lo, preferred_element_type=F32)
        o_hi = jnp.dot(d1, v_hi, preferred_element_type=F32)
        if rev:
            o_lo = o_lo + jnp.dot(off, v_hi, preferred_element_type=F32)
        else:
            o_hi = o_hi + jnp.dot(off, v_lo, preferred_element_type=F32)
        o = jnp.concatenate([o_lo, o_hi], axis=0)
        o = o + jnp.dot(qk[:, sl], ones, preferred_element_type=F32) * vh.astype(F32)
        o = o + lax.dot_general(qe[:, sl], st_h.astype(BF16), (((1,), (1,)), ((), ())), preferred_element_type=F32)
        out_ref[:, sl] = o
        upd = lax.dot_general(vh, kh[:, sl], (((0,), (0,)), ((), ())), preferred_element_type=F32)
        st_ref[sl, :] = st_h * e_end[:, sl] + upd

    if readout:
        for h in range(HG_HEADS):
            sl = slice(h * hd, (h + 1) * hd)
            o = oprev_ref[:, sl] + obuf_ref[:, sl]
            ms = jnp.mean(o * o, axis=-1, keepdims=True)
            gate = _silu(g_ref[:, sl].astype(F32))
            o_ref[:, sl] = (o * lax.rsqrt(ms + EPS) * nw_ref[...] * gate).astype(o_ref.dtype)


def _hg_scan(pa, pf, lb_row, *, rev, init=None, readout=None):
    bsz, n, _ = pa.shape
    nc = n // SCAN_Q
    cidx = (lambda c: nc - 1 - c) if rev else (lambda c: c)
    fcol = (PF_F // HG_WIDTH) + (1 if rev else 0)
    in_specs = [pl.BlockSpec((None, SCAN_Q, HG_WIDTH), lambda b, c: (b, cidx(c), PA_HQ // HG_WIDTH)),
                pl.BlockSpec((None, SCAN_Q, HG_WIDTH), lambda b, c: (b, cidx(c), fcol)),
                pl.BlockSpec((None, SCAN_Q, HG_WIDTH), lambda b, c: (b, cidx(c), PA_HI // HG_WIDTH)),
                pl.BlockSpec((1, HG_WIDTH), lambda b, c: (0, 0))]
    args = [pa, pf, pa, lb_row]
    if init is not None:
        in_specs.append(pl.BlockSpec((None, HG_WIDTH, HG_HEAD_DIM), lambda b, c: (b, 0, 0)))
        args.append(init)
    scratch = [pltpu.VMEM((HG_HEADS, 3, SCAN_Q // 2, SCAN_Q // 2), F32)]
    if readout is not None:
        oprev, nw128 = readout
        in_specs += [pl.BlockSpec((None, SCAN_Q, HG_WIDTH), lambda b, c: (b, cidx(c), PA_HG // HG_WIDTH)),
                     pl.BlockSpec((None, SCAN_Q, HG_WIDTH), lambda b, c: (b, cidx(c), 0)),
                     pl.BlockSpec((1, HG_HEAD_DIM), lambda b, c: (0, 0))]
        args += [pa, oprev, nw128]
        scratch.append(pltpu.VMEM((SCAN_Q, HG_WIDTH), F32))
    o_dtype = BF16 if readout is not None else F32
    o, st = pl.pallas_call(
        functools.partial(_hg_scan_kernel, rev=rev, has_init=init is not None, readout=readout is not None),
        grid=(bsz, nc),
        in_specs=in_specs,
        out_specs=[pl.BlockSpec((None, SCAN_Q, HG_WIDTH), lambda b, c: (b, cidx(c), 0)),
                   pl.BlockSpec((None, HG_WIDTH, HG_HEAD_DIM), lambda b, c: (b, 0, 0))],
        out_shape=[jax.ShapeDtypeStruct((bsz, n, HG_WIDTH), o_dtype),
                   jax.ShapeDtypeStruct((bsz, HG_WIDTH, HG_HEAD_DIM), F32)],
        scratch_shapes=scratch,
        compiler_params=_cparams(("parallel", "arbitrary")),
        name="hg_scan_bwd" if rev else "hg_scan_fwd",
    )(*args)
    return o, st


def _attn_kernel(*refs, seg_lens, tk, lam_init):
    lam_ref, nw_ref, q_ref = refs[0], refs[1], refs[2]
    nseg = len(seg_lens)
    kv_refs = refs[3:3 + 2 * nseg]
    o_ref, qs_ref, sa_ref, sb_ref, m_ref, mn_ref, acc_ref, kall_ref, vaug_ref = refs[3 + 2 * nseg:]
    tq = q_ref.shape[0]

    @pl.when(pl.program_id(2) == 0)
    def _():
        r0 = 0
        for si, n in enumerate(seg_lens):
            kall_ref[r0:r0 + n, :] = kv_refs[2 * si][...]
            vaug_ref[r0:r0 + n, :2 * DA_HEAD_DIM] = kv_refs[2 * si + 1][...]
            r0 += n
        vaug_ref[:, 2 * DA_HEAD_DIM:] = jnp.ones((r0, 2 * DA_HEAD_DIM), BF16)

    qv = q_ref[...] * jnp.asarray(DA_HEAD_DIM ** -0.5, BF16)
    lane = lax.broadcasted_iota(I32, qv.shape, 1)
    zero = jnp.zeros_like(qv)
    qs_ref[:tq] = jnp.where(lane < DA_HEAD_DIM, qv, zero)
    qs_ref[tq:] = jnp.where(lane >= DA_HEAD_DIM, qv, zero)
    m_ref[...] = jnp.full_like(m_ref, -jnp.inf)
    acc_ref[...] = jnp.zeros_like(acc_ref)

    def scores(kc):
        return lax.dot_general(qs_ref[...], kc, (((1,), (1,)), ((), ())), preferred_element_type=F32)

    def lane_tiles(a):
        return [a[:, c * LANES:(c + 1) * LANES] for c in range(a.shape[1] // LANES)]

    def row_max(s, m_prev):
        tiles = lane_tiles(s)
        m = tiles[0]
        for tile in tiles[1:]:
            m = jnp.maximum(m, tile)
        return jnp.maximum(m_prev, jnp.broadcast_to(jnp.max(m, axis=-1, keepdims=True), m_prev.shape))

    def online_step(s, vc, m_prev, m_cur):
        alpha = jnp.exp(m_prev - m_cur)
        ps = [jnp.exp(tile - m_cur) for tile in lane_tiles(s)]
        pv = jnp.dot(jnp.concatenate(ps, axis=1).astype(BF16), vc, preferred_element_type=F32)
        acc_ref[...] = jnp.concatenate([alpha, alpha], axis=1) * acc_ref[...] + pv

    for n in (sum(seg_lens),):
        k_ref, v_ref = kall_ref, vaug_ref
        t = min(tk, n)
        nc = n // t
        if nc == 1:
            s = scores(k_ref[...])
            m_prev = m_ref[...]
            m_cur = row_max(s, m_prev)
            online_step(s, v_ref[...], m_prev, m_cur)
            m_ref[...] = m_cur
        else:
            s0 = scores(k_ref[pl.ds(0, t), :])
            sa_ref[...] = s0
            mn_ref[...] = row_max(s0, m_ref[...])

            def pipe_step(c, cur_ref, nxt_ref, k_ref=k_ref, v_ref=v_ref, t=t):
                r0 = pl.multiple_of(c * t, t)
                r1 = pl.multiple_of(r0 + t, t)
                m_prev, m_cur = m_ref[...], mn_ref[...]
                s_next = scores(k_ref[pl.ds(r1, t), :])
                nxt_ref[...] = s_next
                mn_ref[...] = row_max(s_next, m_cur)
                m_ref[...] = m_cur
                online_step(cur_ref[...], v_ref[pl.ds(r0, t), :], m_prev, m_cur)

            def body(i, carry):
                pipe_step(2 * i, sa_ref, sb_ref)
                pipe_step(2 * i + 1, sb_ref, sa_ref)
                return carry

            lax.fori_loop(0, (nc - 1) // 2, body, 0)
            last_ref = sa_ref
            if nc % 2 == 0:
                pipe_step(nc - 2, sa_ref, sb_ref)
                last_ref = sb_ref
            m_prev, m_cur = m_ref[...], mn_ref[...]
            online_step(last_ref[...], v_ref[pl.ds((nc - 1) * t, t), :], m_prev, m_cur)
            m_ref[...] = m_cur
    hw = 2 * DA_HEAD_DIM
    acc = acc_ref[:, :hw]
    l = acc_ref[:, hw:]
    lam = lam_ref[...]
    lmbda = (jnp.exp(jnp.sum(lam[0:1] * lam[1:2], axis=-1, keepdims=True))
             - jnp.exp(jnp.sum(lam[2:3] * lam[3:4], axis=-1, keepdims=True)) + lam_init)
    o = acc[:tq] / l[:tq] - lmbda * (acc[tq:] / l[tq:])
    ms = jnp.mean(o * o, axis=-1, keepdims=True)
    o_ref[...] = (o * lax.rsqrt(ms + EPS) * nw_ref[...] * (1.0 - lam_init)).astype(o_ref.dtype)


def _diff_attn(pqk_q, kv_segs, lam, nw, lam_init, tq, tk):
    bsz, nq, _ = pqk_q.shape
    hw = 2 * DA_HEAD_DIM
    in_specs = [pl.BlockSpec((4, DA_HEAD_DIM), lambda b, h, i: (0, 0)),
                pl.BlockSpec((1, hw), lambda b, h, i: (0, 0)),
                pl.BlockSpec((None, tq, hw), lambda b, h, i: (b, i, h))]
    args = [lam, nw.reshape(1, hw), pqk_q]
    seg_lens = []
    for k_arr, k_col, v_arr, v_col in kv_segs:
        nk = k_arr.shape[1]
        seg_lens.append(nk)
        in_specs += [pl.BlockSpec((None, nk, hw), functools.partial(lambda b, h, i, c: (b, 0, c + h), c=k_col // hw)),
                     pl.BlockSpec((None, nk, hw), functools.partial(lambda b, h, i, c: (b, 0, c + h), c=v_col // hw))]
        args += [k_arr, v_arr]
    n_keys = sum(seg_lens)
    assert n_keys % min(tk, n_keys) == 0
    return pl.pallas_call(
        functools.partial(_attn_kernel, seg_lens=tuple(seg_lens), tk=tk, lam_init=lam_init),
        grid=(bsz, DA_HEADS, nq // tq),
        in_specs=in_specs,
        out_specs=pl.BlockSpec((None, tq, hw), lambda b, h, i: (b, i, h)),
        out_shape=jax.ShapeDtypeStruct((bsz, nq, DA_WIDTH), BF16),
        scratch_shapes=[pltpu.VMEM((2 * tq, hw), BF16),
                        pltpu.VMEM((2 * tq, tk), F32),
                        pltpu.VMEM((2 * tq, tk), F32),
                        pltpu.VMEM((2 * tq, LANES), F32),
                        pltpu.VMEM((2 * tq, LANES), F32),
                        pltpu.VMEM((2 * tq, 2 * hw), F32),
                        pltpu.VMEM((n_keys, hw), BF16),
                        pltpu.VMEM((n_keys, 2 * hw), BF16)],
        compiler_params=_cparams(("parallel", "parallel", "arbitrary")),
        name="diff_attn",
    )(*args)


def _mix_kernel(h_ref, ya_ref, yb_ref, yd_ref, wg0, wg1, wg2, bg0, bg1, bg2, wu0, wu1, wu2, o_ref):
    h = h_ref[...]
    mix = None
    for y_ref, wg, bg, wu in ((ya_ref, wg0, bg0, wu0), (yb_ref, wg1, bg1, wu1), (yd_ref, wg2, bg2, wu2)):
        gate = jax.nn.sigmoid(jnp.dot(h, wg[...], preferred_element_type=F32) + bg[...])
        up = jnp.dot(y_ref[...], wu[...], preferred_element_type=F32)
        mix = gate * up if mix is None else mix + gate * up
    o_ref[...] = mix.astype(o_ref.dtype)


def _residual_out_kernel(x_ref, g1_ref, m_ref, wo_ref, o_ref):
    o_ref[...] = x_ref[...] + g1_ref[...] * jnp.dot(m_ref[...], wo_ref[...], preferred_element_type=F32)


def _merge(x, gate1, h, ya, yb, yd, w_gate, b_gate, w_up, w_out):
    bsz, n, _ = x.shape
    tm = min(1024, n)
    tn = 512
    nn = D // tn
    row = lambda b, i, j: (b, i, 0)
    in_specs = [pl.BlockSpec((None, tm, D), row),
                pl.BlockSpec((None, tm, 1024), row),
                pl.BlockSpec((None, tm, 1024), row),
                pl.BlockSpec((None, tm, 1024), row)]
    in_specs += [pl.BlockSpec((D, tn), functools.partial(lambda b, i, j, k: (0, k * nn + j), k=k)) for k in range(3)]
    in_specs += [pl.BlockSpec((1, tn), functools.partial(lambda b, i, j, k: (0, k * nn + j), k=k)) for k in range(3)]
    in_specs += [pl.BlockSpec((None, 1024, tn), functools.partial(lambda b, i, j, k: (k, 0, j), k=k)) for k in range(3)]
    mix = pl.pallas_call(
        _mix_kernel,
        grid=(bsz, n // tm, nn),
        in_specs=in_specs,
        out_specs=pl.BlockSpec((None, tm, tn), lambda b, i, j: (b, i, j)),
        out_shape=jax.ShapeDtypeStruct((bsz, n, D), BF16),
        compiler_params=_cparams(("parallel", "parallel", "parallel")),
        name="merge_mix",
    )(h, ya, yb, yd, w_gate, w_gate, w_gate, b_gate, b_gate, b_gate, w_up, w_up, w_up)
    return pl.pallas_call(
        _residual_out_kernel,
        grid=(bsz, n // tm, nn),
        in_specs=[pl.BlockSpec((None, tm, tn), lambda b, i, j: (b, i, j)),
                  pl.BlockSpec((None, 1, tn), lambda b, i, j: (b, 0, j)),
                  pl.BlockSpec((None, tm, D), row),
                  pl.BlockSpec((D, tn), lambda b, i, j: (0, j))],
        out_specs=pl.BlockSpec((None, tm, tn), lambda b, i, j: (b, i, j)),
        out_shape=jax.ShapeDtypeStruct((bsz, n, D), F32),
        compiler_params=_cparams(("parallel", "parallel", "parallel")),
        name="merge_out",
    )(x, gate1, mix, w_out)


def _route_kernel(aff_ref, posm_ref, pos_ref, *, cap, rpe):
    rows = N_EXPERTS * rpe
    bits = pltpu.bitcast(aff_ref[...], I32)
    ones = jnp.ones((LANES, LANES), BF16)
    r = lax.broadcasted_iota(I32, (rows, rows), 0)
    c = lax.broadcasted_iota(I32, (rows, rows), 1)
    same = (r // rpe) == (c // rpe)
    grp = same.astype(BF16)
    grp_before = jnp.logical_and(same, c < r).astype(BF16)
    lr = lax.broadcasted_iota(I32, (LANES, LANES), 0)
    lc = lax.broadcasted_iota(I32, (LANES, LANES), 1)
    before = (lr < lc).astype(BF16)

    def count(mask):
        per_row = jnp.dot(mask.astype(BF16), ones, preferred_element_type=F32)
        return jnp.dot(grp, per_row.astype(BF16), preferred_element_type=F32)

    def excl_prefix(mask):
        mb = mask.astype(BF16)
        within = jnp.dot(mb, before, preferred_element_type=F32)
        per_row = jnp.dot(mb, ones, preferred_element_type=F32)
        return within + jnp.dot(grp_before, per_row.astype(BF16), preferred_element_type=F32)

    def body(i, thr):
        cand = thr | (jnp.int32(1) << (30 - i))
        return jnp.where(count(bits >= cand) >= cap, cand, thr)

    thr = lax.fori_loop(0, 31, body, jnp.zeros((rows, LANES), I32))
    gt = bits > thr
    eq = bits == thr
    need = cap - count(gt)
    sel = jnp.logical_or(gt, jnp.logical_and(eq, excl_prefix(eq) < need))
    pos = excl_prefix(sel).astype(I32)
    pos_ref[...] = pos
    posm_ref[...] = jnp.where(sel, pos, -1)


def _route(aff, cap):
    bsz, n, _ = aff.shape
    rpe = n // LANES
    rows = N_EXPERTS * rpe
    aff_t = jnp.swapaxes(aff, 1, 2).reshape(bsz, rows, LANES)
    spec = pl.BlockSpec((None, rows, LANES), lambda b: (b, 0, 0))
    posm, pos = pl.pallas_call(
        functools.partial(_route_kernel, cap=cap, rpe=rpe),
        grid=(bsz,),
        in_specs=[spec], out_specs=[spec, spec],
        out_shape=[jax.ShapeDtypeStruct((bsz, rows, LANES), I32)] * 2,
        compiler_params=_cparams(("parallel",)),
        name="route",
    )(aff_t)
    return posm.reshape(bsz, N_EXPERTS, n), pos.reshape(bsz, N_EXPERTS, n)


def _gather_kernel(base_ref, base_small_ref, small_ref, h_ref, posm_ref, xg_ref, *, win, win_small, nblk):
    b = pl.program_id(0)
    xg_ref[...] = jnp.zeros_like(xg_ref)

    def scatter_block(j, bases_ref, w):
        t0 = pl.multiple_of(j * TOK_BLK, TOK_BLK)
        slot = lax.broadcasted_iota(I32, (w, TOK_BLK), 0)
        bases, onehots = [], []
        for e in range(N_EXPERTS):
            base = pl.multiple_of(bases_ref[(b * N_EXPERTS + e) * nblk + j], SLOT_ALIGN)
            rel = posm_ref[e:e + 1, pl.ds(t0, TOK_BLK)] - base
            bases.append(base)
            onehots.append(jnp.where(slot == rel, 1.0, 0.0).astype(BF16))
        rows = jnp.dot(jnp.concatenate(onehots, axis=0), h_ref[pl.ds(t0, TOK_BLK), :],
                       preferred_element_type=F32).astype(BF16)
        for e in range(N_EXPERTS):
            xg_ref[e, pl.ds(bases[e], w), :] += rows[e * w:(e + 1) * w]

    def body(j, carry):
        small = small_ref[b * nblk + j] != 0

        @pl.when(small)
        def _():
            scatter_block(j, base_small_ref, win_small)

        @pl.when(jnp.logical_not(small))
        def _():
            scatter_block(j, base_ref, win)

        return carry

    lax.fori_loop(0, nblk, body, 0)


def _expert_kernel(xg_ref, w1_ref, w3_ref, w2_ref, y_ref, acc_ref):
    f = pl.program_id(2)

    @pl.when(f == 0)
    def _():
        acc_ref[...] = jnp.zeros_like(acc_ref)

    xg = xg_ref[...]
    a = jnp.dot(xg, w1_ref[...].astype(BF16), preferred_element_type=F32)
    g = jnp.dot(xg, w3_ref[...].astype(BF16), preferred_element_type=F32)
    acc_ref[...] += jnp.dot((_silu(a) * g).astype(BF16), w2_ref[...].astype(BF16), preferred_element_type=F32)

    @pl.when(f == pl.num_programs(2) - 1)
    def _():
        y_ref[...] = acc_ref[...].astype(y_ref.dtype)


def _combine_kernel(base_ref, x_ref, g2_ref, aff_ref, posm_ref, *rest, win, win_tile, cap, nblk, nsub, final):
    y_refs = rest[:EXPERT_GROUP]
    if final:
        fw_ref, o_ref, acc_ref = rest[EXPERT_GROUP:]
    else:
        o_ref, acc_ref = rest[EXPERT_GROUP:]
    b, jt, eg = pl.program_id(0), pl.program_id(1), pl.program_id(2)

    @pl.when(eg == 0)
    def _():
        acc_ref[...] = jnp.zeros_like(acc_ref)

    lane = lax.broadcasted_iota(I32, (TOK_BLK, N_EXPERTS), 1)
    slot = lax.broadcasted_iota(I32, (TOK_BLK, win), 1)
    for sb in range(nsub):
        rows = slice(sb * TOK_BLK, (sb + 1) * TOK_BLK)
        total = None
        for k, y_ref in enumerate(y_refs):
            e = eg * EXPERT_GROUP + k
            first = (b * N_EXPERTS + e) * nblk + jt * nsub
            tile_base = jnp.minimum(base_ref[first], cap - win_tile)
            base = base_ref[first + sb]
            mine = lane == e
            rel = jnp.sum(jnp.where(mine, posm_ref[rows, :], 0), axis=-1, keepdims=True) - base
            val = jnp.sum(jnp.where(mine, aff_ref[rows, :], 0.0), axis=-1, keepdims=True)
            onehot = jnp.where(slot == rel, 1.0, 0.0).astype(BF16)
            off = pl.multiple_of(base - tile_base, SLOT_ALIGN)
            part = val * jnp.dot(onehot, y_ref[pl.ds(off, win), :], preferred_element_type=F32)
            total = part if total is None else total + part
        acc_ref[rows, :] += total

    @pl.when(eg == pl.num_programs(2) - 1)
    def _():
        out = x_ref[...] + g2_ref[...] * acc_ref[...]
        if final:
            ms = jnp.mean(out * out, axis=-1, keepdims=True)
            out = out * lax.rsqrt(ms + EPS) * fw_ref[...]
        o_ref[...] = out


def _ec_moe(token_sets, w1, w3, w2, layer):
    bsz = token_sets[0][0].shape[0]
    plans, xgs = [], []
    for x, gate2, h2, aff, final_w in token_sets:
        n = x.shape[1]
        cap = EC_CAPACITY * n // N_EXPERTS
        nblk = n // TOK_BLK
        win = min(TOK_BLK + SLOT_ALIGN, cap)
        posm, pos = _route(aff, cap)
        start = pos[:, :, ::TOK_BLK]
        aligned = (start // SLOT_ALIGN) * SLOT_ALIGN
        base = jnp.minimum(aligned, cap - win).astype(I32).reshape(-1)
        win_small = min(GATHER_WIN_SMALL, win)
        base_small = jnp.minimum(aligned, cap - win_small).astype(I32).reshape(-1)
        count = jnp.concatenate([start[:, :, 1:], jnp.full_like(start[:, :, :1], cap)], axis=2) - start
        small = jnp.all(count <= win_small - SLOT_ALIGN, axis=1).astype(I32).reshape(-1)
        dq = GATHER_COLS
        xgs.append(pl.pallas_call(
            functools.partial(_gather_kernel, win=win, win_small=win_small, nblk=nblk),
            grid_spec=pltpu.PrefetchScalarGridSpec(
                num_scalar_prefetch=3,
                grid=(bsz, D // dq),
                in_specs=[pl.BlockSpec((None, n, dq), lambda b, c, *_: (b, 0, c)),
                          pl.BlockSpec((None, N_EXPERTS, n), lambda b, c, *_: (b, 0, 0))],
                out_specs=pl.BlockSpec((None, N_EXPERTS, cap, dq), lambda b, c, *_: (b, 0, 0, c))),
            out_shape=jax.ShapeDtypeStruct((bsz, N_EXPERTS, cap, D), BF16),
            compiler_params=_cparams(("parallel", "parallel")),
            name="moe_gather",
        )(base, base_small, small, h2, posm))
        plans.append((n, cap, nblk, win, posm, base))

    xg = xgs[0] if len(xgs) == 1 else jnp.concatenate(xgs, axis=2)
    cap_all = xg.shape[2]
    tf = 256
    y = pl.pallas_call(
        _expert_kernel,
        grid=(N_EXPERTS, bsz, EXPERT_FF // tf),
        in_specs=[pl.BlockSpec((None, None, cap_all, D), lambda e, b, f: (b, e, 0, 0)),
                  pl.BlockSpec((None, None, D, tf), lambda e, b, f: (layer, e, 0, f)),
                  pl.BlockSpec((None, None, D, tf), lambda e, b, f: (layer, e, 0, f)),
                  pl.BlockSpec((None, None, tf, D), lambda e, b, f: (layer, e, f, 0))],
        out_specs=pl.BlockSpec((None, None, cap_all, D), lambda e, b, f: (b, e, 0, 0)),
        out_shape=jax.ShapeDtypeStruct((bsz, N_EXPERTS, cap_all, D), BF16),
        scratch_shapes=[pltpu.VMEM((cap_all, D), F32)],
        compiler_params=_cparams(("parallel", "parallel", "arbitrary")),
        name="moe_expert",
    )(xg, w1, w3, w2)
    y2d = y.reshape(bsz * N_EXPERTS * cap_all, D)

    outs = []
    row0 = 0
    for (x, gate2, h2, aff, final_w), (n, cap, nblk, win, posm, base) in zip(token_sets, plans):
        tile = min(COMBINE_TILE, n)
        nsub = tile // TOK_BLK
        win_tile = min(tile + SLOT_ALIGN, cap)
        final = final_w is not None

        def y_window(b, j, eg, base, k, cap=cap, nblk=nblk, nsub=nsub, win_tile=win_tile, row0=row0):
            be = b * N_EXPERTS + eg * EXPERT_GROUP + k
            start = jnp.minimum(base[be * nblk + j * nsub], cap - win_tile)
            return pl.multiple_of(be * cap_all + row0 + start, SLOT_ALIGN), 0

        in_specs = [pl.BlockSpec((None, tile, D), lambda b, j, e, base: (b, j, 0)),
                    pl.BlockSpec((None, 1, D), lambda b, j, e, base: (b, 0, 0)),
                    pl.BlockSpec((None, tile, N_EXPERTS), lambda b, j, e, base: (b, j, 0)),
                    pl.BlockSpec((None, tile, N_EXPERTS), lambda b, j, e, base: (b, j, 0))]
        in_specs += [pl.BlockSpec((pl.Element(win_tile), pl.Element(D)), functools.partial(y_window, k=k))
                     for k in range(EXPERT_GROUP)]
        args = [base, x, gate2, aff, jnp.swapaxes(posm, 1, 2)] + [y2d] * EXPERT_GROUP
        if final:
            in_specs.append(pl.BlockSpec((1, D), lambda b, j, e, base: (0, 0)))
            args.append(final_w.reshape(1, D))
        outs.append(pl.pallas_call(
            functools.partial(_combine_kernel, win=win, win_tile=win_tile, cap=cap, nblk=nblk, nsub=nsub, final=final),
            grid_spec=pltpu.PrefetchScalarGridSpec(
                num_scalar_prefetch=1,
                grid=(bsz, n // tile, N_EXPERTS // EXPERT_GROUP),
                in_specs=in_specs,
                out_specs=pl.BlockSpec((None, tile, D), lambda b, j, e, base: (b, j, 0)),
                scratch_shapes=[pltpu.VMEM((tile, D), F32)]),
            out_shape=jax.ShapeDtypeStruct((bsz, n, D), F32),
            compiler_params=_cparams(("parallel", "parallel", "arbitrary")),
            name="moe_combine",
        )(*args))
        row0 += cap
    return outs


def _rope_tables(n_tokens):
    rows = n_tokens // GRID_W
    row = jnp.repeat(jnp.arange(rows, dtype=F32), GRID_W)
    col = (jnp.arange(rows * GRID_W, dtype=I32) % GRID_W).astype(F32)
    inv = ROPE_BASE ** (-jnp.arange(ROPE_PAIRS, dtype=F32) / ROPE_PAIRS)
    ang = jnp.concatenate([row[:, None] * inv, col[:, None] * inv], axis=-1)
    cos, sin = jnp.cos(ang), jnp.sin(ang)
    reps = LANES // DA_HEAD_DIM
    cos_t = jnp.tile(jnp.concatenate([cos, cos], axis=-1), (1, reps))
    sin_t = jnp.tile(jnp.concatenate([-sin, sin], axis=-1), (1, reps))
    return cos_t, sin_t


def _lower_bounds(p):
    cum = jnp.cumsum(jax.nn.softmax(p.astype(F32), axis=0), axis=0)
    return cum - cum[0]


def _pad128(v):
    return jnp.pad(v.reshape(1, -1).astype(F32), ((0, 0), (0, LANES - v.size)))


def kernel(x, c, ctx, c_ctx, ada_w, ada_b, norm1_w, norm2_w, w_in, ssd_conv_w, ssd_conv_b, ssd_dt_bias,
           ssd_a_log, ssd_d, ssd_norm_w, hg_lb, hg_norm_w, da_lambda, da_norm_w, w_up, w_gate, b_gate,
           w_out, moe_router, moe_w1, moe_w3, moe_w2, final_norm_w):
    bsz, n_lat, _ = x.shape
    n_ctx = ctx.shape[1]
    depth = ada_w.shape[0]
    rope = _rope_tables(n_lat)
    lb_all = jnp.stack([_lower_bounds(hg_lb[0]), _lower_bounds(hg_lb[1])], axis=1)

    mod = _adaln(jnp.concatenate([c, c_ctx[None]], axis=0), ada_w, ada_b)
    mod = mod.reshape(depth, 8, 6, 1, D)

    xl, xc = x, ctx
    for l in range(depth):
        need_ctx = l < depth - 1
        lam_init = 0.8 - 0.6 * math.exp(-0.3 * l)
        mod_l = [mod[l, :bsz, k] for k in range(6)]
        mod_c = [jnp.broadcast_to(mod[l, bsz:bsz + 1, k], (bsz, 1, D)) for k in range(6)]

        wl = w_in[l]
        w_a = jnp.concatenate([wl[:, OFF_SSD_Z:OFF_SSD_XBC], wl[:, OFF_HG_Q:OFF_HG_F], wl[:, OFF_HG_I:OFF_DA_Q],
                               wl[:, OFF_DA_V:N_IN], wl[:, OFF_SSD_XBC:OFF_SSD_DT]], axis=1).astype(BF16)
        w_f = jnp.concatenate([wl[:, OFF_HG_F:OFF_HG_I], wl[:, OFF_SSD_DT:OFF_HG_Q],
                               jnp.zeros((D, PF_N - PF_DT - 2 * SSD_HEADS), F32)], axis=1).astype(BF16)
        w_qk = wl[:, OFF_DA_Q:OFF_DA_V].astype(BF16)
        alog128 = _pad128(ssd_a_log[l])
        bias128 = _pad128(ssd_dt_bias[l])
        d_full = jnp.repeat(ssd_d[l].astype(F32), SSD_D_INNER // SSD_HEADS).reshape(1, SSD_D_INNER)
        ssd_nw = ssd_norm_w[l].reshape(1, SSD_D_INNER).astype(F32)
        hg_nw = hg_norm_w[l].reshape(1, HG_HEAD_DIM).astype(F32)
        wg_b = w_gate[l].astype(BF16)
        bg = b_gate[l].reshape(1, N_BRANCH * D)
        wu_b = w_up[l].astype(BF16)
        wo_b = w_out[l].astype(BF16)

        def project(xs, m, use_rope):
            n = xs.shape[1]
            h = _modulate(xs, norm1_w[l], m[0], m[1])
            h2d = h.reshape(bsz * n, D)
            tm = min(1024, n)
            pa = _matmul(h2d, w_a, BF16, tm, PA_N // 4).reshape(bsz, n, PA_N)
            pf = _matmul(h2d, w_f, F32, tm, PF_N).reshape(bsz, n, PF_N)
            pqk = _matmul(h2d, w_qk, BF16, tm, 1024, rope_tables=rope if use_rope else None).reshape(bsz, n, PQK_N)
            return h, pa, pf, pqk

        hl, pa_l, pf_l, pqk_l = project(xl, mod_l, True)
        hc, pa_c, pf_c, pqk_c = project(xc, mod_c, False)

        xa_c = _ssd_conv(pa_c, ssd_conv_w[l], ssd_conv_b[l])
        xa_l = _ssd_conv(pa_l, ssd_conv_w[l], ssd_conv_b[l])
        yc_f, s_f = _ssd_scan(xa_c, pf_c, alog128, bias128, rev=False)
        yl_f, _ = _ssd_scan(xa_l, pf_l, alog128, bias128, rev=False, init=s_f)
        ya_c, s_b = _ssd_scan(xa_c, pf_c, alog128, bias128, rev=True, readout=(pa_c, yc_f, d_full, ssd_nw))
        ya_l, _ = _ssd_scan(xa_l, pf_l, alog128, bias128, rev=True, init=s_b, readout=(pa_l, yl_f, d_full, ssd_nw))

        lb_f, lb_b = lb_all[l, 0].reshape(1, HG_WIDTH), lb_all[l, 1].reshape(1, HG_WIDTH)
        oc_f, t_f = _hg_scan(pa_c, pf_c, lb_f, rev=False)
        ol_f, _ = _hg_scan(pa_l, pf_l, lb_f, rev=False, init=t_f)
        yb_c, t_b = _hg_scan(pa_c, pf_c, lb_b, rev=True, readout=(oc_f, hg_nw))
        yb_l, _ = _hg_scan(pa_l, pf_l, lb_b, rev=True, init=t_b, readout=(ol_f, hg_nw))

        yd_l = _diff_attn(pqk_l, [(pqk_c, DA_WIDTH, pa_c, PA_DV), (pqk_l, DA_WIDTH, pa_l, PA_DV)],
                          da_lambda[l], da_norm_w[l], lam_init, 512, ATTN_TK)

        xl = _merge(xl, mod_l[2], hl, ya_l, yb_l, yd_l, wg_b, bg, wu_b, wo_b)
        h2l, aff_l = _modulate(xl, norm2_w[l], mod_l[3], mod_l[4], w_router=moe_router[l])
        token_sets = [(xl, mod_l[5], h2l, aff_l, final_norm_w if l == depth - 1 else None)]
        if need_ctx:
            yd_c = _diff_attn(pqk_c, [(pqk_c, DA_WIDTH, pa_c, PA_DV)], da_lambda[l], da_norm_w[l], lam_init,
                              n_ctx, ATTN_TK)
            xc = _merge(xc, mod_c[2], hc, ya_c, yb_c, yd_c, wg_b, bg, wu_b, wo_b)
            h2c, aff_c = _modulate(xc, norm2_w[l], mod_c[3], mod_c[4], w_router=moe_router[l])
            token_sets.append((xc, mod_c[5], h2c, aff_c, None))
        outs = _ec_moe(token_sets, moe_w1, moe_w3, moe_w2, l)
        xl = outs[0]
        if need_ctx:
            xc = outs[1]
    return xl
```

```python
import functools
import math

import jax
import jax.numpy as jnp
from jax import lax
from jax.experimental import pallas as pl
from jax.experimental.pallas import tpu as pltpu

F32 = jnp.float32
BF16 = jnp.bfloat16
I32 = jnp.int32
HIGHEST = lax.Precision.HIGHEST
LOG2E = 1.4426950408889634

D = 2048
EPS = 1e-6
GRID_W = 64
SSD_D_INNER = 1024
SSD_HEADS = 16
SSD_GROUPS = 2
SSD_STATE = 128
SSD_XBC = 1536
SSD_CONV_W = 5
HG_WIDTH = 1024
HG_HEADS = 8
HG_HEAD_DIM = 128
DA_HEADS = 8
DA_HEAD_DIM = 64
DA_WIDTH = 1024
ROPE_BASE = 10000.0
ROPE_PAIRS = 16
N_BRANCH = 3
N_EXPERTS = 16
EXPERT_FF = 2048
EC_CAPACITY = 2

OFF_SSD_Z = 0
OFF_SSD_XBC = 1024
OFF_SSD_DT = 2560
OFF_HG_Q = 2592
OFF_HG_F = OFF_HG_Q + 1024
OFF_HG_I = OFF_HG_F + 2048
OFF_HG_G = OFF_HG_I + 1024
OFF_DA_Q = OFF_HG_G + 1024
OFF_DA_K = OFF_DA_Q + 1024
OFF_DA_V = OFF_DA_K + 1024
N_IN = OFF_DA_V + 1024

LANES = 128
PA_Z, PA_HQ, PA_HI, PA_HG, PA_DV, PA_XBC = 0, 1024, 2048, 3072, 4096, 5120
PA_N = 6656
PF_F, PF_DT = 0, 2048
PF_N = 2176
PQK_N = 2048

SCAN_Q = 256
ATTN_TK = 2816
HALO = 16
TOK_BLK = 128
COMBINE_TILE = 256
GATHER_COLS = 512
EXPERT_GROUP = 16
GATHER_WIN_SMALL = 48
SLOT_ALIGN = 16
VMEM_LIMIT = 56 * 1024 * 1024


def _cparams(sem, vmem=VMEM_LIMIT):
    return pltpu.CompilerParams(dimension_semantics=sem, vmem_limit_bytes=vmem)


def _silu(x):
    return x * jax.nn.sigmoid(x)


def _adaln_kernel(cb_ref, w_ref, b_ref, o_ref, *, n_rows, tn):
    nct = tn // LANES

    def body(k, accs):
        r0 = pl.multiple_of(k * 8, 8)
        ws = [w_ref[pl.ds(r0, 8), c * LANES:(c + 1) * LANES] for c in range(nct)]
        out = []
        for r in range(n_rows):
            s = _silu(cb_ref[r, pl.ds(r0, 8), :])
            out.append(tuple(accs[r][c] + s * ws[c] for c in range(nct)))
        return tuple(out)

    zero = jnp.zeros((8, LANES), F32)
    accs = lax.fori_loop(0, D // 8, body, tuple(tuple(zero for _ in range(nct)) for _ in range(n_rows)),
                         unroll=8)
    rows = []
    for r in range(n_rows):
        rows.append(jnp.concatenate([jnp.sum(accs[r][c], axis=0, keepdims=True) for c in range(nct)], axis=1))
    rows.append(jnp.zeros((8 - n_rows, tn), F32))
    o_ref[...] = jnp.concatenate(rows, axis=0) + b_ref[...]


def _adaln(cvec, ada_w, ada_b):
    n_rows = cvec.shape[0]
    depth, _, n6 = ada_w.shape
    tn = 512
    cb = jnp.broadcast_to(cvec[:, :, None], (n_rows, D, LANES))
    return pl.pallas_call(
        functools.partial(_adaln_kernel, n_rows=n_rows, tn=tn),
        grid=(depth, n6 // tn),
        in_specs=[pl.BlockSpec((n_rows, D, LANES), lambda l, j: (0, 0, 0)),
                  pl.BlockSpec((None, D, tn), lambda l, j: (l, 0, j)),
                  pl.BlockSpec((None, 1, tn), lambda l, j: (l, 0, j))],
        out_specs=pl.BlockSpec((None, 8, tn), lambda l, j: (l, 0, j)),
        out_shape=jax.ShapeDtypeStruct((depth, 8, n6), F32),
        compiler_params=_cparams(("parallel", "parallel")),
        name="adaln",
    )(cb, ada_w, ada_b.reshape(depth, 1, n6))


def _modulate_kernel(x_ref, nw_ref, sh_ref, sc_ref, *rest, with_router):
    x = x_ref[...]
    ms = jnp.mean(x * x, axis=-1, keepdims=True)
    h = (x * lax.rsqrt(ms + EPS) * nw_ref[...]) * (1.0 + sc_ref[...]) + sh_ref[...]
    if with_router:
        wr_ref, h_ref, aff_ref = rest
        logits = jnp.dot(h, wr_ref[...], preferred_element_type=F32, precision=HIGHEST)
        m = jnp.max(logits, axis=-1, keepdims=True)
        e = jnp.exp(logits - m)
        aff_ref[...] = e / jnp.sum(e, axis=-1, keepdims=True)
    else:
        (h_ref,) = rest
    h_ref[...] = h.astype(BF16)


def _modulate(x, nw, shift, scale, w_router=None):
    bsz, n, _ = x.shape
    tm = min(512, n)
    with_router = w_router is not None
    in_specs = [pl.BlockSpec((None, tm, D), lambda b, i: (b, i, 0)),
                pl.BlockSpec((1, D), lambda b, i: (0, 0)),
                pl.BlockSpec((None, 1, D), lambda b, i: (b, 0, 0)),
                pl.BlockSpec((None, 1, D), lambda b, i: (b, 0, 0))]
    args = [x, nw.reshape(1, D), shift, scale]
    out_specs = [pl.BlockSpec((None, tm, D), lambda b, i: (b, i, 0))]
    out_shape = [jax.ShapeDtypeStruct((bsz, n, D), BF16)]
    if with_router:
        in_specs.append(pl.BlockSpec((D, N_EXPERTS), lambda b, i: (0, 0)))
        args.append(w_router)
        out_specs.append(pl.BlockSpec((None, tm, N_EXPERTS), lambda b, i: (b, i, 0)))
        out_shape.append(jax.ShapeDtypeStruct((bsz, n, N_EXPERTS), F32))
    res = pl.pallas_call(
        functools.partial(_modulate_kernel, with_router=with_router),
        grid=(bsz, n // tm),
        in_specs=in_specs, out_specs=out_specs, out_shape=out_shape,
        compiler_params=_cparams(("parallel", "parallel")),
        name="modulate_router" if with_router else "modulate",
    )(*args)
    return res if with_router else res[0]


def _mm_kernel(a_ref, w_ref, *rest, rope, tn):
    acc = jnp.dot(a_ref[...], w_ref[...], preferred_element_type=F32)
    if rope:
        cos_ref, sin_ref, o_ref = rest
        cos = cos_ref[...]
        sin = sin_ref[...]
        lane = lax.broadcasted_iota(I32, cos.shape, 1)
        first_half = (lane % DA_HEAD_DIM) < (DA_HEAD_DIM // 2)
        for c in range(tn // LANES):
            xs = acc[:, c * LANES:(c + 1) * LANES]
            partner = jnp.where(first_half,
                                pltpu.roll(xs, LANES - DA_HEAD_DIM // 2, 1),
                                pltpu.roll(xs, DA_HEAD_DIM // 2, 1))
            o_ref[:, c * LANES:(c + 1) * LANES] = (xs * cos + partner * sin).astype(o_ref.dtype)
    else:
        (o_ref,) = rest
        o_ref[...] = acc.astype(o_ref.dtype)


def _matmul(a, w, out_dtype, tm, tn, rope_tables=None):
    m, k = a.shape
    n = w.shape[1]
    rope = rope_tables is not None
    in_specs = [pl.BlockSpec((tm, k), lambda i, j: (i, 0)),
                pl.BlockSpec((k, tn), lambda i, j: (0, j))]
    args = [a, w]
    if rope:
        cos, sin = rope_tables
        nt = cos.shape[0] // tm
        in_specs += [pl.BlockSpec((tm, LANES), lambda i, j: (i % nt, 0)),
                     pl.BlockSpec((tm, LANES), lambda i, j: (i % nt, 0))]
        args += [cos, sin]
    return pl.pallas_call(
        functools.partial(_mm_kernel, rope=rope, tn=tn),
        grid=(m // tm, n // tn),
        in_specs=in_specs,
        out_specs=pl.BlockSpec((tm, tn), lambda i, j: (i, j)),
        out_shape=jax.ShapeDtypeStruct((m, n), out_dtype),
        compiler_params=_cparams(("parallel", "parallel")),
        name="proj_rope" if rope else "proj",
    )(*args)


def _conv_kernel(prev_ref, cur_ref, next_ref, w_ref, b_ref, o_ref, *, tc):
    i = pl.program_id(1)
    last = pl.num_programs(1) - 1
    pad = (SSD_CONV_W - 1) // 2
    prev = jnp.where(i > 0, prev_ref[...].astype(F32), 0.0)
    nxt = jnp.where(i < last, next_ref[...].astype(F32), 0.0)
    ext = jnp.concatenate([prev, cur_ref[...].astype(F32), nxt], axis=0)
    w = w_ref[...]
    y = b_ref[...] + w[0:1] * ext[HALO - pad:HALO - pad + tc]
    for k in range(1, SSD_CONV_W):
        y = y + w[k:k + 1] * ext[HALO - pad + k:HALO - pad + k + tc]
    o_ref[...] = _silu(y).astype(o_ref.dtype)


def _ssd_conv(pa, conv_w, conv_b):
    bsz, n, _ = pa.shape
    tc = min(512, n)
    tw = 512
    c0 = PA_XBC // tw
    nbh = n // HALO
    return pl.pallas_call(
        functools.partial(_conv_kernel, tc=tc),
        grid=(bsz, n // tc, SSD_XBC // tw),
        in_specs=[pl.BlockSpec((None, HALO, tw), lambda b, i, j: (b, jnp.maximum(i * (tc // HALO) - 1, 0), c0 + j)),
                  pl.BlockSpec((None, tc, tw), lambda b, i, j: (b, i, c0 + j)),
                  pl.BlockSpec((None, HALO, tw),
                               lambda b, i, j: (b, jnp.minimum((i + 1) * (tc // HALO), nbh - 1), c0 + j)),
                  pl.BlockSpec((SSD_CONV_W, tw), lambda b, i, j: (0, j)),
                  pl.BlockSpec((1, tw), lambda b, i, j: (0, j))],
        out_specs=pl.BlockSpec((None, tc, tw), lambda b, i, j: (b, i, j)),
        out_shape=jax.ShapeDtypeStruct((bsz, n, SSD_XBC), BF16),
        compiler_params=_cparams(("parallel", "parallel", "parallel")),
        name="ssd_conv",
    )(pa, pa, pa, conv_w, conv_b.reshape(1, SSD_XBC))


def _tri(q, rev):
    r = lax.broadcasted_iota(I32, (q, q), 0)
    c = lax.broadcasted_iota(I32, (q, q), 1)
    return (c >= r) if rev else (r >= c)


def _split3(x):
    hi = x.astype(BF16)
    r = x - hi.astype(F32)
    mid = r.astype(BF16)
    lo = (r - mid.astype(F32)).astype(BF16)
    return hi, mid, lo


def _select_rows_dot(sel01, x):
    hi, mid, lo = _split3(x)
    return (jnp.dot(sel01, hi, preferred_element_type=F32) + jnp.dot(sel01, mid, preferred_element_type=F32)
            + jnp.dot(sel01, lo, preferred_element_type=F32))


def _select_cols_dot(x, sel01):
    hi, mid, lo = _split3(x)
    return (jnp.dot(hi, sel01, preferred_element_type=F32) + jnp.dot(mid, sel01, preferred_element_type=F32)
            + jnp.dot(lo, sel01, preferred_element_type=F32))


def _ssd_scan_kernel(*refs, rev, has_init, readout):
    it = iter(refs)
    xact_ref, dt_ref, alog_ref, bias_ref = next(it), next(it), next(it), next(it)
    init_ref = next(it) if has_init else None
    if readout:
        z_ref, yprev_ref, d_ref, nw_ref = next(it), next(it), next(it), next(it)
    y_ref, st_ref = next(it), next(it)
    ybuf_ref = next(it) if readout else None

    q = SCAN_Q
    hd = SSD_D_INNER // SSD_HEADS
    hg = SSD_HEADS // SSD_GROUPS
    gw = hg * hd
    d_off = SSD_HEADS if rev else 0

    @pl.when(pl.program_id(1) == 0)
    def _():
        st_ref[...] = init_ref[...] if has_init else jnp.zeros_like(st_ref)

    tri = _tri(q, rev)
    dtv = jax.nn.softplus(dt_ref[...] + bias_ref[...])
    da = dtv * (-LOG2E * jnp.exp(alog_ref[...]))
    cs = _select_rows_dot(jnp.where(tri, 1.0, 0.0).astype(BF16), da)
    cs_t = cs.T
    erow = lax.broadcasted_iota(I32, (LANES, SSD_D_INNER), 0)
    ecol = lax.broadcasted_iota(I32, (LANES, SSD_D_INNER), 1)
    expand = jnp.where(erow == d_off + ecol // hd, 1.0, 0.0).astype(BF16)
    csx = _select_cols_dot(cs, expand)
    dtx = _select_cols_dot(dtv, expand)
    x = xact_ref[:, :SSD_D_INNER].astype(F32)
    xdt = x * dtx
    xdt_b = xdt.astype(BF16)
    cs_end = csx[0:1] if rev else csx[q - 1:q]
    ecs = jnp.exp2(csx)
    xw = (xdt * jnp.exp2(cs_end - csx)).astype(BF16)
    ecs_end = jnp.exp2(cs_end)
    lane = lax.broadcasted_iota(I32, (q, LANES), 1)
    out_ref = ybuf_ref if readout else y_ref

    for g in range(SSD_GROUPS):
        bg = xact_ref[:, SSD_D_INNER + g * SSD_STATE:SSD_D_INNER + (g + 1) * SSD_STATE]
        cg = xact_ref[:, SSD_D_INNER + (SSD_GROUPS + g) * SSD_STATE:SSD_D_INNER + (SSD_GROUPS + g + 1) * SSD_STATE]
        cb = lax.dot_general(cg, bg, (((1,), (1,)), ((), ())), preferred_element_type=F32)
        st_g = st_ref[:, g * gw:(g + 1) * gw]
        y_inter = jnp.dot(cg, st_g.astype(BF16), preferred_element_type=F32) * ecs[:, g * gw:(g + 1) * gw]
        for jj in range(hg // 2):
            col0 = g * gw + jj * LANES
            xp = xdt_b[:, col0:col0 + LANES]
            acc = y_inter[:, jj * LANES:(jj + 1) * LANES]
            for s in range(2):
                j = d_off + g * hg + 2 * jj + s
                e = cs[:, j:j + 1] - cs_t[j:j + 1, :]
                seg = jnp.exp2(jnp.where(tri, e, -jnp.inf))
                m = (cb * seg).astype(BF16)
                xm = jnp.where((lane >= hd) if s else (lane < hd), xp, jnp.zeros_like(xp))
                acc = acc + jnp.dot(m, xm, preferred_element_type=F32)
            out_ref[:, col0:col0 + LANES] = acc
        upd = lax.dot_general(bg, xw[:, g * gw:(g + 1) * gw], (((0,), (0,)), ((), ())),
                              preferred_element_type=F32)
        st_ref[:, g * gw:(g + 1) * gw] = st_g * ecs_end[:, g * gw:(g + 1) * gw] + upd

    if readout:
        z = z_ref[...].astype(F32)
        yy = (yprev_ref[...] + ybuf_ref[...] + d_ref[...] * x) * _silu(z)
        for g in range(SSD_GROUPS):
            seg = yy[:, g * gw:(g + 1) * gw]
            ms = jnp.mean(seg * seg, axis=-1, keepdims=True)
            y_ref[:, g * gw:(g + 1) * gw] = (seg * lax.rsqrt(ms + EPS) * nw_ref[:, g * gw:(g + 1) * gw]).astype(y_ref.dtype)


def _ssd_scan(xact, pf, alog128, bias128, *, rev, init=None, readout=None):
    bsz, n, _ = xact.shape
    nc = n // SCAN_Q
    cidx = (lambda c: nc - 1 - c) if rev else (lambda c: c)
    in_specs = [pl.BlockSpec((None, SCAN_Q, SSD_XBC), lambda b, c: (b, cidx(c), 0)),
                pl.BlockSpec((None, SCAN_Q, LANES), lambda b, c: (b, cidx(c), PF_DT // LANES)),
                pl.BlockSpec((1, LANES), lambda b, c: (0, 0)),
                pl.BlockSpec((1, LANES), lambda b, c: (0, 0))]
    args = [xact, pf, alog128, bias128]
    if init is not None:
        in_specs.append(pl.BlockSpec((None, SSD_STATE, SSD_D_INNER), lambda b, c: (b, 0, 0)))
        args.append(init)
    scratch = []
    if readout is not None:
        pa, yprev, d_full, norm_w = readout
        in_specs += [pl.BlockSpec((None, SCAN_Q, SSD_D_INNER), lambda b, c: (b, cidx(c), PA_Z // SSD_D_INNER)),
                     pl.BlockSpec((None, SCAN_Q, SSD_D_INNER), lambda b, c: (b, cidx(c), 0)),
                     pl.BlockSpec((1, SSD_D_INNER), lambda b, c: (0, 0)),
                     pl.BlockSpec((1, SSD_D_INNER), lambda b, c: (0, 0))]
        args += [pa, yprev, d_full, norm_w]
        scratch = [pltpu.VMEM((SCAN_Q, SSD_D_INNER), F32)]
    y_dtype = BF16 if readout is not None else F32
    y, st = pl.pallas_call(
        functools.partial(_ssd_scan_kernel, rev=rev, has_init=init is not None, readout=readout is not None),
        grid=(bsz, nc),
        in_specs=in_specs,
        out_specs=[pl.BlockSpec((None, SCAN_Q, SSD_D_INNER), lambda b, c: (b, cidx(c), 0)),
                   pl.BlockSpec((None, SSD_STATE, SSD_D_INNER), lambda b, c: (b, 0, 0))],
        out_shape=[jax.ShapeDtypeStruct((bsz, n, SSD_D_INNER), y_dtype),
                   jax.ShapeDtypeStruct((bsz, SSD_STATE, SSD_D_INNER), F32)],
        scratch_shapes=scratch,
        compiler_params=_cparams(("parallel", "arbitrary")),
        name="ssd_scan_bwd" if rev else "ssd_scan_fwd",
    )(*args)
    return y, st


def _ref_rows(bb, s, rev):
    q, w = bb.shape
    rl = s if rev else s - 1
    if 2 * s >= 8:
        b3 = bb.reshape(q // (2 * s), 2 * s, w)
        return jnp.broadcast_to(b3[:, rl:rl + 1, :], b3.shape).reshape(q, w)
    off = lax.broadcasted_iota(I32, (q, 1), 0) % (2 * s)
    out = bb
    for ov in range(2 * s):
        if ov != rl:
            out = jnp.where(off == ov, pltpu.roll(bb, (ov - rl) % q, 0), out)
    return out


def _level_operand(bb, qq, kk, s, rev):
    q = bb.shape[0]
    if s < 8:
        upper = (lax.broadcasted_iota(I32, (q, 1), 0) & s) != 0
        q_side = jnp.logical_not(upper) if rev else upper
        ex = jnp.exp2(-jnp.abs(bb - _ref_rows(bb, s, rev)))
        return (jnp.where(q_side, qq, kk) * ex).astype(BF16)
    pieces = []
    for a in range(0, q, 2 * s):
        lo, hi = slice(a, a + s), slice(a + s, a + 2 * s)
        if rev:
            r = bb[a + s:a + s + 1]
            pieces += [qq[lo] * jnp.exp2(bb[lo] - r), kk[hi] * jnp.exp2(r - bb[hi])]
        else:
            r = bb[a + s - 1:a + s]
            pieces += [kk[lo] * jnp.exp2(r - bb[lo]), qq[hi] * jnp.exp2(bb[hi] - r)]
    return jnp.concatenate(pieces, axis=0).astype(BF16)


def _hg_scan_kernel(*refs, rev, has_init, readout):
    it = iter(refs)
    q_ref, f_ref, v_ref, lb_ref = next(it), next(it), next(it), next(it)
    init_ref = next(it) if has_init else None
    if readout:
        g_ref, oprev_ref, nw_ref = next(it), next(it), next(it)
    o_ref, st_ref, att_ref = next(it), next(it), next(it)
    obuf_ref = next(it) if readout else None

    q = SCAN_Q
    hd = HG_HEAD_DIM

    @pl.when(pl.program_id(1) == 0)
    def _():
        st_ref[...] = init_ref[...] if has_init else jnp.zeros_like(st_ref)

    lb = lb_ref[...]
    qq = _silu(q_ref[...].astype(F32))
    f = lb + (1.0 - lb) * jax.nn.sigmoid(f_ref[...])
    kk = 1.0 - f
    lf = jnp.log(f) * LOG2E
    tri = _tri(q, rev)
    bb = jnp.dot(jnp.where(tri, 1.0, 0.0), lf, preferred_element_type=F32, precision=HIGHEST)

    half = q // 2
    ri = lax.broadcasted_iota(I32, (half, half), 0)
    ci = lax.broadcasted_iota(I32, (half, half), 1)
    xr = ri ^ ci
    causal = (ci > ri) if rev else (ri > ci)
    nt = (((1,), (1,)), ((), ()))
    s = half
    while s >= 1:
        u = _level_operand(bb, qq, kk, s, rev)
        if s < half:
            level = jnp.logical_and((xr >> int(math.log2(s))) == 1, causal)
        for h in range(HG_HEADS):
            u_lo, u_hi = u[:half, h * hd:(h + 1) * hd], u[half:, h * hd:(h + 1) * hd]
            if s == half:
                qs_, ks_ = (u_lo, u_hi) if rev else (u_hi, u_lo)
                att_ref[h, 2] = lax.dot_general(qs_, ks_, nt, preferred_element_type=F32)
            else:
                for blk, ub in enumerate((u_lo, u_hi)):
                    p = lax.dot_general(ub, ub, nt, preferred_element_type=F32)
                    old = att_ref[h, blk] if 2 * s < half else 0.0
                    att_ref[h, blk] = jnp.where(level, p, old)
        s //= 2

    b_end = bb[0:1] if rev else bb[q - 1:q]
    qe = (qq * jnp.exp2(bb)).astype(BF16)
    kh = (kk * jnp.exp2(b_end - bb)).astype(BF16)
    qk = (qq * kk).astype(BF16)
    e_end = jnp.exp2(b_end)
    ones = jnp.ones((hd, hd), BF16)
    out_ref = obuf_ref if readout else o_ref
    for h in range(HG_HEADS):
        sl = slice(h * hd, (h + 1) * hd)
        vh = v_ref[:, sl]
        st_h = st_ref[sl, :]
        v_lo, v_hi = vh[:half], vh[half:]
        d0, d1, off = (att_ref[h, k].astype(BF16) for k in range(3))
        o_lo = jnp.dot(d0, v_lo, preferred_element_type=F32)
        o_hi = jnp.dot(d1, v_hi, preferred_element_type=F32)
        if rev:
            o_lo = o_lo + jnp.dot(off, v_hi, preferred_element_type=F32)
        else:
            o_hi = o_hi + jnp.dot(off, v_lo, preferred_element_type=F32)
        o = jnp.concatenate([o_lo, o_hi], axis=0)
        o = o + jnp.dot(qk[:, sl], ones, preferred_element_type=F32) * vh.astype(F32)
        o = o + lax.dot_general(qe[:, sl], st_h.astype(BF16), (((1,), (1,)), ((), ())), preferred_element_type=F32)
        out_ref[:, sl] = o
        upd = lax.dot_general(vh, kh[:, sl], (((0,), (0,)), ((), ())), preferred_element_type=F32)
        st_ref[sl, :] = st_h * e_end[:, sl] + upd

    if readout:
        for h in range(HG_HEADS):
            sl = slice(h * hd, (h + 1) * hd)
            o = oprev_ref[:, sl] + obuf_ref[:, sl]
            ms = jnp.mean(o * o, axis=-1, keepdims=True)
            gate = _silu(g_ref[:, sl].astype(F32))
            o_ref[:, sl] = (o * lax.rsqrt(ms + EPS) * nw_ref[...] * gate).astype(o_ref.dtype)


def _hg_scan(pa, pf, lb_row, *, rev, init=None, readout=None):
    bsz, n, _ = pa.shape
    nc = n // SCAN_Q
    cidx = (lambda c: nc - 1 - c) if rev else (lambda c: c)
    fcol = (PF_F // HG_WIDTH) + (1 if rev else 0)
    in_specs = [pl.BlockSpec((None, SCAN_Q, HG_WIDTH), lambda b, c: (b, cidx(c), PA_HQ // HG_WIDTH)),
                pl.BlockSpec((None, SCAN_Q, HG_WIDTH), lambda b, c: (b, cidx(c), fcol)),
                pl.BlockSpec((None, SCAN_Q, HG_WIDTH), lambda b, c: (b, cidx(c), PA_HI // HG_WIDTH)),
                pl.BlockSpec((1, HG_WIDTH), lambda b, c: (0, 0))]
    args = [pa, pf, pa, lb_row]
    if init is not None:
        in_specs.append(pl.BlockSpec((None, HG_WIDTH, HG_HEAD_DIM), lambda b, c: (b, 0, 0)))
        args.append(init)
    scratch = [pltpu.VMEM((HG_HEADS, 3, SCAN_Q // 2, SCAN_Q // 2), F32)]
    if readout is not None:
        oprev, nw128 = readout
        in_specs += [pl.BlockSpec((None, SCAN_Q, HG_WIDTH), lambda b, c: (b, cidx(c), PA_HG // HG_WIDTH)),
                     pl.BlockSpec((None, SCAN_Q, HG_WIDTH), lambda b, c: (b, cidx(c), 0)),
                     pl.BlockSpec((1, HG_HEAD_DIM), lambda b, c: (0, 0))]
        args += [pa, oprev, nw128]
        scratch.append(pltpu.VMEM((SCAN_Q, HG_WIDTH), F32))
    o_dtype = BF16 if readout is not None else F32
    o, st = pl.pallas_call(
        functools.partial(_hg_scan_kernel, rev=rev, has_init=init is not None, readout=readout is not None),
        grid=(bsz, nc),
        in_specs=in_specs,
        out_specs=[pl.BlockSpec((None, SCAN_Q, HG_WIDTH), lambda b, c: (b, cidx(c), 0)),
                   pl.BlockSpec((None, HG_WIDTH, HG_HEAD_DIM), lambda b, c: (b, 0, 0))],
        out_shape=[jax.ShapeDtypeStruct((bsz, n, HG_WIDTH), o_dtype),
                   jax.ShapeDtypeStruct((bsz, HG_WIDTH, HG_HEAD_DIM), F32)],
        scratch_shapes=scratch,
        compiler_params=_cparams(("parallel", "arbitrary")),
        name="hg_scan_bwd" if rev else "hg_scan_fwd",
    )(*args)
    return o, st


def _attn_kernel(*refs, seg_lens, tk, lam_init):
    lam_ref, nw_ref, q_ref = refs[0], refs[1], refs[2]
    nseg = len(seg_lens)
    kv_refs = refs[3:3 + 2 * nseg]
    o_ref, qs_ref, sa_ref, sb_ref, m_ref, mn_ref, acc_ref, kall_ref, vaug_ref = refs[3 + 2 * nseg:]
    tq = q_ref.shape[0]

    @pl.when(pl.program_id(2) == 0)
    def _():
        r0 = 0
        for si, n in enumerate(seg_lens):
            kall_ref[r0:r0 + n, :] = kv_refs[2 * si][...]
            vaug_ref[r0:r0 + n, :2 * DA_HEAD_DIM] = kv_refs[2 * si + 1][...]
            r0 += n
        vaug_ref[:, 2 * DA_HEAD_DIM:] = jnp.ones((r0, 2 * DA_HEAD_DIM), BF16)

    qv = q_ref[...] * jnp.asarray(DA_HEAD_DIM ** -0.5, BF16)
    lane = lax.broadcasted_iota(I32, qv.shape, 1)
    zero = jnp.zeros_like(qv)
    qs_ref[:tq] = jnp.where(lane < DA_HEAD_DIM, qv, zero)
    qs_ref[tq:] = jnp.where(lane >= DA_HEAD_DIM, qv, zero)
    m_ref[...] = jnp.full_like(m_ref, -jnp.inf)
    acc_ref[...] = jnp.zeros_like(acc_ref)

    def scores(kc):
        return lax.dot_general(qs_ref[...], kc, (((1,), (1,)), ((), ())), preferred_element_type=F32)

    def lane_tiles(a):
        return [a[:, c * LANES:(c + 1) * LANES] for c in range(a.shape[1] // LANES)]

    def row_max(s, m_prev):
        tiles = lane_tiles(s)
        m = tiles[0]
        for tile in tiles[1:]:
            m = jnp.maximum(m, tile)
        return jnp.maximum(m_prev, jnp.broadcast_to(jnp.max(m, axis=-1, keepdims=True), m_prev.shape))

    def online_step(s, vc, m_prev, m_cur):
        alpha = jnp.exp(m_prev - m_cur)
        ps = [jnp.exp(tile - m_cur) for tile in lane_tiles(s)]
        pv = jnp.dot(jnp.concatenate(ps, axis=1).astype(BF16), vc, preferred_element_type=F32)
        acc_ref[...] = jnp.concatenate([alpha, alpha], axis=1) * acc_ref[...] + pv

    for n in (sum(seg_lens),):
        k_ref, v_ref = kall_ref, vaug_ref
        t = min(tk, n)
        nc = n // t
        if nc == 1:
            s = scores(k_ref[...])
            m_prev = m_ref[...]
            m_cur = row_max(s, m_prev)
            online_step(s, v_ref[...], m_prev, m_cur)
            m_ref[...] = m_cur
        else:
            s0 = scores(k_ref[pl.ds(0, t), :])
            sa_ref[...] = s0
            mn_ref[...] = row_max(s0, m_ref[...])

            def pipe_step(c, cur_ref, nxt_ref, k_ref=k_ref, v_ref=v_ref, t=t):
                r0 = pl.multiple_of(c * t, t)
                r1 = pl.multiple_of(r0 + t, t)
                m_prev, m_cur = m_ref[...], mn_ref[...]
                s_next = scores(k_ref[pl.ds(r1, t), :])
                nxt_ref[...] = s_next
                mn_ref[...] = row_max(s_next, m_cur)
                m_ref[...] = m_cur
                online_step(cur_ref[...], v_ref[pl.ds(r0, t), :], m_prev, m_cur)

            def body(i, carry):
                pipe_step(2 * i, sa_ref, sb_ref)
                pipe_step(2 * i + 1, sb_ref, sa_ref)
                return carry

            lax.fori_loop(0, (nc - 1) // 2, body, 0)
            last_ref = sa_ref
            if nc % 2 == 0:
                pipe_step(nc - 2, sa_ref, sb_ref)
                last_ref = sb_ref
            m_prev, m_cur = m_ref[...], mn_ref[...]
            online_step(last_ref[...], v_ref[pl.ds((nc - 1) * t, t), :], m_prev, m_cur)
            m_ref[...] = m_cur
    hw = 2 * DA_HEAD_DIM
    acc = acc_ref[:, :hw]
    l = acc_ref[:, hw:]
    lam = lam_ref[...]
    lmbda = (jnp.exp(jnp.sum(lam[0:1] * lam[1:2], axis=-1, keepdims=True))
             - jnp.exp(jnp.sum(lam[2:3] * lam[3:4], axis=-1, keepdims=True)) + lam_init)
    o = acc[:tq] / l[:tq] - lmbda * (acc[tq:] / l[tq:])
    ms = jnp.mean(o * o, axis=-1, keepdims=True)
    o_ref[...] = (o * lax.rsqrt(ms + EPS) * nw_ref[...] * (1.0 - lam_init)).astype(o_ref.dtype)


def _diff_attn(pqk_q, kv_segs, lam, nw, lam_init, tq, tk):
    bsz, nq, _ = pqk_q.shape
    hw = 2 * DA_HEAD_DIM
    in_specs = [pl.BlockSpec((4, DA_HEAD_DIM), lambda b, h, i: (0, 0)),
                pl.BlockSpec((1, hw), lambda b, h, i: (0, 0)),
                pl.BlockSpec((None, tq, hw), lambda b, h, i: (b, i, h))]
    args = [lam, nw.reshape(1, hw), pqk_q]
    seg_lens = []
    for k_arr, k_col, v_arr, v_col in kv_segs:
        nk = k_arr.shape[1]
        seg_lens.append(nk)
        in_specs += [pl.BlockSpec((None, nk, hw), functools.partial(lambda b, h, i, c: (b, 0, c + h), c=k_col // hw)),
                     pl.BlockSpec((None, nk, hw), functools.partial(lambda b, h, i, c: (b, 0, c + h), c=v_col // hw))]
        args += [k_arr, v_arr]
    n_keys = sum(seg_lens)
    assert n_keys % min(tk, n_keys) == 0
    return pl.pallas_call(
        functools.partial(_attn_kernel, seg_lens=tuple(seg_lens), tk=tk, lam_init=lam_init),
        grid=(bsz, DA_HEADS, nq // tq),
        in_specs=in_specs,
        out_specs=pl.BlockSpec((None, tq, hw), lambda b, h, i: (b, i, h)),
        out_shape=jax.ShapeDtypeStruct((bsz, nq, DA_WIDTH), BF16),
        scratch_shapes=[pltpu.VMEM((2 * tq, hw), BF16),
                        pltpu.VMEM((2 * tq, tk), F32),
                        pltpu.VMEM((2 * tq, tk), F32),
                        pltpu.VMEM((2 * tq, LANES), F32),
                        pltpu.VMEM((2 * tq, LANES), F32),
                        pltpu.VMEM((2 * tq, 2 * hw), F32),
                        pltpu.VMEM((n_keys, hw), BF16),
                        pltpu.VMEM((n_keys, 2 * hw), BF16)],
        compiler_params=_cparams(("parallel", "parallel", "arbitrary")),
        name="diff_attn",
    )(*args)


def _mix_kernel(h_ref, ya_ref, yb_ref, yd_ref, wg0, wg1, wg2, bg0, bg1, bg2, wu0, wu1, wu2, o_ref):
    h = h_ref[...]
    mix = None
    for y_ref, wg, bg, wu in ((ya_ref, wg0, bg0, wu0), (yb_ref, wg1, bg1, wu1), (yd_ref, wg2, bg2, wu2)):
        gate = jax.nn.sigmoid(jnp.dot(h, wg[...], preferred_element_type=F32) + bg[...])
        up = jnp.dot(y_ref[...], wu[...], preferred_element_type=F32)
        mix = gate * up if mix is None else mix + gate * up
    o_ref[...] = mix.astype(o_ref.dtype)


def _residual_out_kernel(x_ref, g1_ref, m_ref, wo_ref, o_ref):
    o_ref[...] = x_ref[...] + g1_ref[...] * jnp.dot(m_ref[...], wo_ref[...], preferred_element_type=F32)


def _merge(x, gate1, h, ya, yb, yd, w_gate, b_gate, w_up, w_out):
    bsz, n, _ = x.shape
    tm = min(1024, n)
    tn = 512
    nn = D // tn
    row = lambda b, i, j: (b, i, 0)
    in_specs = [pl.BlockSpec((None, tm, D), row),
                pl.BlockSpec((None, tm, 1024), row),
                pl.BlockSpec((None, tm, 1024), row),
                pl.BlockSpec((None, tm, 1024), row)]
    in_specs += [pl.BlockSpec((D, tn), functools.partial(lambda b, i, j, k: (0, k * nn + j), k=k)) for k in range(3)]
    in_specs += [pl.BlockSpec((1, tn), functools.partial(lambda b, i, j, k: (0, k * nn + j), k=k)) for k in range(3)]
    in_specs += [pl.BlockSpec((None, 1024, tn), functools.partial(lambda b, i, j, k: (k, 0, j), k=k)) for k in range(3)]
    mix = pl.pallas_call(
        _mix_kernel,
        grid=(bsz, n // tm, nn),
        in_specs=in_specs,
        out_specs=pl.BlockSpec((None, tm, tn), lambda b, i, j: (b, i, j)),
        out_shape=jax.ShapeDtypeStruct((bsz, n, D), BF16),
        compiler_params=_cparams(("parallel", "parallel", "parallel")),
        name="merge_mix",
    )(h, ya, yb, yd, w_gate, w_gate, w_gate, b_gate, b_gate, b_gate, w_up, w_up, w_up)
    return pl.pallas_call(
        _residual_out_kernel,
        grid=(bsz, n // tm, nn),
        in_specs=[pl.BlockSpec((None, tm, tn), lambda b, i, j: (b, i, j)),
                  pl.BlockSpec((None, 1, tn), lambda b, i, j: (b, 0, j)),
                  pl.BlockSpec((None, tm, D), row),
                  pl.BlockSpec((D, tn), lambda b, i, j: (0, j))],
        out_specs=pl.BlockSpec((None, tm, tn), lambda b, i, j: (b, i, j)),
        out_shape=jax.ShapeDtypeStruct((bsz, n, D), F32),
        compiler_params=_cparams(("parallel", "parallel", "parallel")),
        name="merge_out",
    )(x, gate1, mix, w_out)


def _route_kernel(aff_ref, posm_ref, pos_ref, *, cap, rpe):
    rows = N_EXPERTS * rpe
    bits = pltpu.bitcast(aff_ref[...], I32)
    ones = jnp.ones((LANES, LANES), BF16)
    r = lax.broadcasted_iota(I32, (rows, rows), 0)
    c = lax.broadcasted_iota(I32, (rows, rows), 1)
    same = (r // rpe) == (c // rpe)
    grp = same.astype(BF16)
    grp_before = jnp.logical_and(same, c < r).astype(BF16)
    lr = lax.broadcasted_iota(I32, (LANES, LANES), 0)
    lc = lax.broadcasted_iota(I32, (LANES, LANES), 1)
    before = (lr < lc).astype(BF16)

    def count(mask):
        per_row = jnp.dot(mask.astype(BF16), ones, preferred_element_type=F32)
        return jnp.dot(grp, per_row.astype(BF16), preferred_element_type=F32)

    def excl_prefix(mask):
        mb = mask.astype(BF16)
        within = jnp.dot(mb, before, preferred_element_type=F32)
        per_row = jnp.dot(mb, ones, preferred_element_type=F32)
        return within + jnp.dot(grp_before, per_row.astype(BF16), preferred_element_type=F32)

    def body(i, thr):
        cand = thr | (jnp.int32(1) << (30 - i))
        return jnp.where(count(bits >= cand) >= cap, cand, thr)

    thr = lax.fori_loop(0, 31, body, jnp.zeros((rows, LANES), I32))
    gt = bits > thr
    eq = bits == thr
    need = cap - count(gt)
    sel = jnp.logical_or(gt, jnp.logical_and(eq, excl_prefix(eq) < need))
    pos = excl_prefix(sel).astype(I32)
    pos_ref[...] = pos
    posm_ref[...] = jnp.where(sel, pos, -1)


def _route(aff, cap):
    bsz, n, _ = aff.shape
    rpe = n // LANES
    rows = N_EXPERTS * rpe
    aff_t = jnp.swapaxes(aff, 1, 2).reshape(bsz, rows, LANES)
    spec = pl.BlockSpec((None, rows, LANES), lambda b: (b, 0, 0))
    posm, pos = pl.pallas_call(
        functools.partial(_route_kernel, cap=cap, rpe=rpe),
        grid=(bsz,),
        in_specs=[spec], out_specs=[spec, spec],
        out_shape=[jax.ShapeDtypeStruct((bsz, rows, LANES), I32)] * 2,
        compiler_params=_cparams(("parallel",)),
        name="route",
    )(aff_t)
    return posm.reshape(bsz, N_EXPERTS, n), pos.reshape(bsz, N_EXPERTS, n)


def _gather_kernel(base_ref, base_small_ref, small_ref, h_ref, posm_ref, xg_ref, *, win, win_small, nblk):
    b = pl.program_id(0)
    xg_ref[...] = jnp.zeros_like(xg_ref)

    def scatter_block(j, bases_ref, w):
        t0 = pl.multiple_of(j * TOK_BLK, TOK_BLK)
        slot = lax.broadcasted_iota(I32, (w, TOK_BLK), 0)
        bases, onehots = [], []
        for e in range(N_EXPERTS):
            base = pl.multiple_of(bases_ref[(b * N_EXPERTS + e) * nblk + j], SLOT_ALIGN)
            rel = posm_ref[e:e + 1, pl.ds(t0, TOK_BLK)] - base
            bases.append(base)
            onehots.append(jnp.where(slot == rel, 1.0, 0.0).astype(BF16))
        rows = jnp.dot(jnp.concatenate(onehots, axis=0), h_ref[pl.ds(t0, TOK_BLK), :],
                       preferred_element_type=F32).astype(BF16)
        for e in range(N_EXPERTS):
            xg_ref[e, pl.ds(bases[e], w), :] += rows[e * w:(e + 1) * w]

    def body(j, carry):
        small = small_ref[b * nblk + j] != 0

        @pl.when(small)
        def _():
            scatter_block(j, base_small_ref, win_small)

        @pl.when(jnp.logical_not(small))
        def _():
            scatter_block(j, base_ref, win)

        return carry

    lax.fori_loop(0, nblk, body, 0)


def _expert_kernel(xg_ref, w1_ref, w3_ref, w2_ref, y_ref, hid_ref, *, nf, tf):
    s = pl.program_id(2)

    @pl.when(s < nf)
    def _():
        xg = xg_ref[...]
        a = jnp.dot(xg, w1_ref[...].astype(BF16), preferred_element_type=F32)
        g = jnp.dot(xg, w3_ref[...].astype(BF16), preferred_element_type=F32)
        hid_ref[:, pl.ds(pl.multiple_of(s * tf, tf), tf)] = (_silu(a) * g).astype(BF16)

    @pl.when(s >= nf)
    def _():
        y_ref[...] = jnp.dot(hid_ref[...], w2_ref[...].astype(BF16), preferred_element_type=F32).astype(y_ref.dtype)


def _combine_kernel(base_ref, x_ref, g2_ref, aff_ref, posm_ref, *rest, win, win_tile, cap, nblk, nsub, final):
    y_refs = rest[:EXPERT_GROUP]
    if final:
        fw_ref, o_ref, acc_ref = rest[EXPERT_GROUP:]
    else:
        o_ref, acc_ref = rest[EXPERT_GROUP:]
    b, jt, eg = pl.program_id(0), pl.program_id(1), pl.program_id(2)

    @pl.when(eg == 0)
    def _():
        acc_ref[...] = jnp.zeros_like(acc_ref)

    lane = lax.broadcasted_iota(I32, (TOK_BLK, N_EXPERTS), 1)
    slot = lax.broadcasted_iota(I32, (TOK_BLK, win), 1)
    for sb in range(nsub):
        rows = slice(sb * TOK_BLK, (sb + 1) * TOK_BLK)
        total = None
        for k, y_ref in enumerate(y_refs):
            e = eg * EXPERT_GROUP + k
            first = (b * N_EXPERTS + e) * nblk + jt * nsub
            tile_base = jnp.minimum(base_ref[first], cap - win_tile)
            base = base_ref[first + sb]
            mine = lane == e
            rel = jnp.sum(jnp.where(mine, posm_ref[rows, :], 0), axis=-1, keepdims=True) - base
            val = jnp.sum(jnp.where(mine, aff_ref[rows, :], 0.0), axis=-1, keepdims=True)
            onehot = jnp.where(slot == rel, 1.0, 0.0).astype(BF16)
            off = pl.multiple_of(base - tile_base, SLOT_ALIGN)
            part = val * jnp.dot(onehot, y_ref[pl.ds(off, win), :], preferred_element_type=F32)
            total = part if total is None else total + part
        acc_ref[rows, :] += total

    @pl.when(eg == pl.num_programs(2) - 1)
    def _():
        out = x_ref[...] + g2_ref[...] * acc_ref[...]
        if final:
            ms = jnp.mean(out * out, axis=-1, keepdims=True)
            out = out * lax.rsqrt(ms + EPS) * fw_ref[...]
        o_ref[...] = out


def _ec_moe(token_sets, w1, w3, w2, layer):
    bsz = token_sets[0][0].shape[0]
    plans, xgs = [], []
    for x, gate2, h2, aff, final_w in token_sets:
        n = x.shape[1]
        cap = EC_CAPACITY * n // N_EXPERTS
        nblk = n // TOK_BLK
        win = min(TOK_BLK + SLOT_ALIGN, cap)
        posm, pos = _route(aff, cap)
        start = pos[:, :, ::TOK_BLK]
        aligned = (start // SLOT_ALIGN) * SLOT_ALIGN
        base = jnp.minimum(aligned, cap - win).astype(I32).reshape(-1)
        win_small = min(GATHER_WIN_SMALL, win)
        base_small = jnp.minimum(aligned, cap - win_small).astype(I32).reshape(-1)
        count = jnp.concatenate([start[:, :, 1:], jnp.full_like(start[:, :, :1], cap)], axis=2) - start
        small = jnp.all(count <= win_small - SLOT_ALIGN, axis=1).astype(I32).reshape(-1)
        dq = GATHER_COLS
        xgs.append(pl.pallas_call(
            functools.partial(_gather_kernel, win=win, win_small=win_small, nblk=nblk),
            grid_spec=pltpu.PrefetchScalarGridSpec(
                num_scalar_prefetch=3,
                grid=(bsz, D // dq),
                in_specs=[pl.BlockSpec((None, n, dq), lambda b, c, *_: (b, 0, c)),
                          pl.BlockSpec((None, N_EXPERTS, n), lambda b, c, *_: (b, 0, 0))],
                out_specs=pl.BlockSpec((None, N_EXPERTS, cap, dq), lambda b, c, *_: (b, 0, 0, c))),
            out_shape=jax.ShapeDtypeStruct((bsz, N_EXPERTS, cap, D), BF16),
            compiler_params=_cparams(("parallel", "parallel")),
            name="moe_gather",
        )(base, base_small, small, h2, posm))
        plans.append((n, cap, nblk, win, posm, base))

    xg = xgs[0] if len(xgs) == 1 else jnp.concatenate(xgs, axis=2)
    cap_all = xg.shape[2]
    tf = 512
    nf = EXPERT_FF // tf
    y = pl.pallas_call(
        functools.partial(_expert_kernel, nf=nf, tf=tf),
        grid=(N_EXPERTS, bsz, 2 * nf),
        in_specs=[pl.BlockSpec((None, None, cap_all, D), lambda e, b, s: (b, e, 0, 0)),
                  pl.BlockSpec((None, None, D, tf), lambda e, b, s: (layer, e, 0, jnp.minimum(s, nf - 1))),
                  pl.BlockSpec((None, None, D, tf), lambda e, b, s: (layer, e, 0, jnp.minimum(s, nf - 1))),
                  pl.BlockSpec((None, None, EXPERT_FF, tf), lambda e, b, s: (layer, e, 0, jnp.maximum(s - nf, 0)))],
        out_specs=pl.BlockSpec((None, None, cap_all, tf), lambda e, b, s: (b, e, 0, jnp.maximum(s - nf, 0))),
        out_shape=jax.ShapeDtypeStruct((bsz, N_EXPERTS, cap_all, D), BF16),
        scratch_shapes=[pltpu.VMEM((cap_all, EXPERT_FF), BF16)],
        compiler_params=_cparams(("parallel", "parallel", "arbitrary")),
        name="moe_expert",
    )(xg, w1, w3, w2)
    y2d = y.reshape(bsz * N_EXPERTS * cap_all, D)

    outs = []
    row0 = 0
    for (x, gate2, h2, aff, final_w), (n, cap, nblk, win, posm, base) in zip(token_sets, plans):
        tile = min(COMBINE_TILE, n)
        nsub = tile // TOK_BLK
        win_tile = min(tile + SLOT_ALIGN, cap)
        final = final_w is not None

        def y_window(b, j, eg, base, k, cap=cap, nblk=nblk, nsub=nsub, win_tile=win_tile, row0=row0):
            be = b * N_EXPERTS + eg * EXPERT_GROUP + k
            start = jnp.minimum(base[be * nblk + j * nsub], cap - win_tile)
            return pl.multiple_of(be * cap_all + row0 + start, SLOT_ALIGN), 0

        in_specs = [pl.BlockSpec((None, tile, D), lambda b, j, e, base: (b, j, 0)),
                    pl.BlockSpec((None, 1, D), lambda b, j, e, base: (b, 0, 0)),
                    pl.BlockSpec((None, tile, N_EXPERTS), lambda b, j, e, base: (b, j, 0)),
                    pl.BlockSpec((None, tile, N_EXPERTS), lambda b, j, e, base: (b, j, 0))]
        in_specs += [pl.BlockSpec((pl.Element(win_tile), pl.Element(D)), functools.partial(y_window, k=k))
                     for k in range(EXPERT_GROUP)]
        args = [base, x, gate2, aff, jnp.swapaxes(posm, 1, 2)] + [y2d] * EXPERT_GROUP
        if final:
            in_specs.append(pl.BlockSpec((1, D), lambda b, j, e, base: (0, 0)))
            args.append(final_w.reshape(1, D))
        outs.append(pl.pallas_call(
            functools.partial(_combine_kernel, win=win, win_tile=win_tile, cap=cap, nblk=nblk, nsub=nsub, final=final),
            grid_spec=pltpu.PrefetchScalarGridSpec(
                num_scalar_prefetch=1,
                grid=(bsz, n // tile, N_EXPERTS // EXPERT_GROUP),
                in_specs=in_specs,
                out_specs=pl.BlockSpec((None, tile, D), lambda b, j, e, base: (b, j, 0)),
                scratch_shapes=[pltpu.VMEM((tile, D), F32)]),
            out_shape=jax.ShapeDtypeStruct((bsz, n, D), F32),
            compiler_params=_cparams(("parallel", "parallel", "arbitrary")),
            name="moe_combine",
        )(*args))
        row0 += cap
    return outs


def _rope_tables(n_tokens):
    rows = n_tokens // GRID_W
    row = jnp.repeat(jnp.arange(rows, dtype=F32), GRID_W)
    col = (jnp.arange(rows * GRID_W, dtype=I32) % GRID_W).astype(F32)
    inv = ROPE_BASE ** (-jnp.arange(ROPE_PAIRS, dtype=F32) / ROPE_PAIRS)
    ang = jnp.concatenate([row[:, None] * inv, col[:, None] * inv], axis=-1)
    cos, sin = jnp.cos(ang), jnp.sin(ang)
    reps = LANES // DA_HEAD_DIM
    cos_t = jnp.tile(jnp.concatenate([cos, cos], axis=-1), (1, reps))
    sin_t = jnp.tile(jnp.concatenate([-sin, sin], axis=-1), (1, reps))
    return cos_t, sin_t


def _lower_bounds(p):
    cum = jnp.cumsum(jax.nn.softmax(p.astype(F32), axis=0), axis=0)
    return cum - cum[0]


def _pad128(v):
    return jnp.pad(v.reshape(1, -1).astype(F32), ((0, 0), (0, LANES - v.size)))


def kernel(x, c, ctx, c_ctx, ada_w, ada_b, norm1_w, norm2_w, w_in, ssd_conv_w, ssd_conv_b, ssd_dt_bias,
           ssd_a_log, ssd_d, ssd_norm_w, hg_lb, hg_norm_w, da_lambda, da_norm_w, w_up, w_gate, b_gate,
           w_out, moe_router, moe_w1, moe_w3, moe_w2, final_norm_w):
    bsz, n_lat, _ = x.shape
    n_ctx = ctx.shape[1]
    depth = ada_w.shape[0]
    rope = _rope_tables(n_lat)
    lb_all = jnp.stack([_lower_bounds(hg_lb[0]), _lower_bounds(hg_lb[1])], axis=1)

    mod = _adaln(jnp.concatenate([c, c_ctx[None]], axis=0), ada_w, ada_b)
    mod = mod.reshape(depth, 8, 6, 1, D)

    xl, xc = x, ctx
    for l in range(depth):
        need_ctx = l < depth - 1
        lam_init = 0.8 - 0.6 * math.exp(-0.3 * l)
        mod_l = [mod[l, :bsz, k] for k in range(6)]
        mod_c = [jnp.broadcast_to(mod[l, bsz:bsz + 1, k], (bsz, 1, D)) for k in range(6)]

        wl = w_in[l]
        w_a = jnp.concatenate([wl[:, OFF_SSD_Z:OFF_SSD_XBC], wl[:, OFF_HG_Q:OFF_HG_F], wl[:, OFF_HG_I:OFF_DA_Q],
                               wl[:, OFF_DA_V:N_IN], wl[:, OFF_SSD_XBC:OFF_SSD_DT]], axis=1).astype(BF16)
        w_f = jnp.concatenate([wl[:, OFF_HG_F:OFF_HG_I], wl[:, OFF_SSD_DT:OFF_HG_Q],
                               jnp.zeros((D, PF_N - PF_DT - 2 * SSD_HEADS), F32)], axis=1).astype(BF16)
        w_qk = wl[:, OFF_DA_Q:OFF_DA_V].astype(BF16)
        alog128 = _pad128(ssd_a_log[l])
        bias128 = _pad128(ssd_dt_bias[l])
        d_full = jnp.repeat(ssd_d[l].astype(F32), SSD_D_INNER // SSD_HEADS).reshape(1, SSD_D_INNER)
        ssd_nw = ssd_norm_w[l].reshape(1, SSD_D_INNER).astype(F32)
        hg_nw = hg_norm_w[l].reshape(1, HG_HEAD_DIM).astype(F32)
        wg_b = w_gate[l].astype(BF16)
        bg = b_gate[l].reshape(1, N_BRANCH * D)
        wu_b = w_up[l].astype(BF16)
        wo_b = w_out[l].astype(BF16)

        def project(xs, m, use_rope):
            n = xs.shape[1]
            h = _modulate(xs, norm1_w[l], m[0], m[1])
            h2d = h.reshape(bsz * n, D)
            tm = min(1024, n)
            pa = _matmul(h2d, w_a, BF16, tm, PA_N // 4).reshape(bsz, n, PA_N)
            pf = _matmul(h2d, w_f, F32, tm, PF_N).reshape(bsz, n, PF_N)
            pqk = _matmul(h2d, w_qk, BF16, tm, 1024, rope_tables=rope if use_rope else None).reshape(bsz, n, PQK_N)
            return h, pa, pf, pqk

        hl, pa_l, pf_l, pqk_l = project(xl, mod_l, True)
        hc, pa_c, pf_c, pqk_c = project(xc, mod_c, False)

        xa_c = _ssd_conv(pa_c, ssd_conv_w[l], ssd_conv_b[l])
        xa_l = _ssd_conv(pa_l, ssd_conv_w[l], ssd_conv_b[l])
        yc_f, s_f = _ssd_scan(xa_c, pf_c, alog128, bias128, rev=False)
        yl_f, _ = _ssd_scan(xa_l, pf_l, alog128, bias128, rev=False, init=s_f)
        ya_c, s_b = _ssd_scan(xa_c, pf_c, alog128, bias128, rev=True, readout=(pa_c, yc_f, d_full, ssd_nw))
        ya_l, _ = _ssd_scan(xa_l, pf_l, alog128, bias128, rev=True, init=s_b, readout=(pa_l, yl_f, d_full, ssd_nw))

        lb_f, lb_b = lb_all[l, 0].reshape(1, HG_WIDTH), lb_all[l, 1].reshape(1, HG_WIDTH)
        oc_f, t_f = _hg_scan(pa_c, pf_c, lb_f, rev=False)
        ol_f, _ = _hg_scan(pa_l, pf_l, lb_f, rev=False, init=t_f)
        yb_c, t_b = _hg_scan(pa_c, pf_c, lb_b, rev=True, readout=(oc_f, hg_nw))
        yb_l, _ = _hg_scan(pa_l, pf_l, lb_b, rev=True, init=t_b, readout=(ol_f, hg_nw))

        yd_l = _diff_attn(pqk_l, [(pqk_c, DA_WIDTH, pa_c, PA_DV), (pqk_l, DA_WIDTH, pa_l, PA_DV)],
                          da_lambda[l], da_norm_w[l], lam_init, 512, ATTN_TK)

        xl = _merge(xl, mod_l[2], hl, ya_l, yb_l, yd_l, wg_b, bg, wu_b, wo_b)
        h2l, aff_l = _modulate(xl, norm2_w[l], mod_l[3], mod_l[4], w_router=moe_router[l])
        token_sets = [(xl, mod_l[5], h2l, aff_l, final_norm_w if l == depth - 1 else None)]
        if need_ctx:
            yd_c = _diff_attn(pqk_c, [(pqk_c, DA_WIDTH, pa_c, PA_DV)], da_lambda[l], da_norm_w[l], lam_init,
                              n_ctx, ATTN_TK)
            xc = _merge(xc, mod_c[2], hc, ya_c, yb_c, yd_c, wg_b, bg, wu_b, wo_b)
            h2c, aff_c = _modulate(xc, norm2_w[l], mod_c[3], mod_c[4], w_router=moe_router[l])
            token_sets.append((xc, mod_c[5], h2c, aff_c, None))
        outs = _ec_moe(token_sets, moe_w1, moe_w3, moe_w2, l)
        xl = outs[0]
        if need_ctx:
            xc = outs[1]
    return xl
```
